```python
import math
import jax, jax.numpy as jnp
from jax import lax
import numpy as np

D_MODEL = 1024
BATCH = 16
SEQ = 256
DEPTH = 1
DEC_BATCH = 8
DEC_SEQ = 2048
PAST_LEN = 256

GRID_W = 64
CONV_WIDTH = 512
CONV_TAPS = 31
REC_HEADS = 4
REC_DK = 128
REC_DV = 128
REC_WIDTH = REC_HEADS * REC_DK
MIX_WIDTH = CONV_WIDTH + REC_WIDTH
IN_COLS = 2 * CONV_WIDTH + 5 * REC_WIDTH
CHUNK = 64
N_EXPERTS = 32
TOP_K = 4
D_FF = D_MODEL
SWIGLU_LIMIT = 7.0
SWIGLU_ALPHA = 1.702
LN_EPS = 1e-5
RMS_EPS = 1e-6
DEEPNORM_ALPHA = (2.0 * DEPTH) ** 0.25
DEEPNORM_BETA = (8.0 * DEPTH) ** -0.25

kernel_name = "hybrid_conformer_hgrn2_moe_diffusion_step"


def layer_norm(x, g=None, b=None):
    xf = x.astype(jnp.float32)
    mu = jnp.mean(xf, -1, keepdims=True)
    var = jnp.mean(jnp.square(xf - mu), -1, keepdims=True)
    y = (xf - mu) * lax.rsqrt(var + LN_EPS)
    if g is not None:
        y = y * g.astype(jnp.float32) + b.astype(jnp.float32)
    return y.astype(x.dtype)


def lower_bounds(lb_logits):
    p = jax.nn.softmax(lb_logits.astype(jnp.float32), axis=0)
    return jnp.cumsum(p, axis=0)[:DEPTH]


def hgrn2_gates(z, lb):
    log_f = jnp.log(lb + (1.0 - lb) * jax.nn.sigmoid(z))
    k = (1.0 - lb) * jax.nn.sigmoid(-z)
    return log_f, k


def hgrn2_scan(q, log_f, k, v, s0):
    B, T = q.shape[0], q.shape[1]
    n = T // CHUNK

    def to_chunks(a):
        return a.reshape(B, n, CHUNK, REC_HEADS, a.shape[-1]).transpose(1, 0, 3, 2, 4)

    causal = jnp.tril(jnp.ones((CHUNK, CHUNK), bool))[:, :, None]

    def step(s, inp):
        qc, lfc, kc, vc = inp
        b = jnp.cumsum(lfc, axis=2)
        o_inter = jnp.einsum('bhtk,bhkv->bhtv', qc * jnp.exp(b), s)
        rel = b[:, :, :, None, :] - b[:, :, None, :, :]
        decay = jnp.exp(jnp.where(causal, rel, -jnp.inf))
        a = jnp.einsum('bhtk,bhsk,bhtsk->bhts', qc, kc, decay)
        o_intra = jnp.einsum('bhts,bhsv->bhtv', a, vc)
        b_last = b[:, :, -1:, :]
        s_new = jnp.exp(b_last[:, :, 0, :])[..., None] * s + jnp.einsum(
            'bhsk,bhsv->bhkv', kc * jnp.exp(b_last - b), vc)
        return s_new, o_inter + o_intra

    s_final, o = lax.scan(step, s0, (to_chunks(q), to_chunks(log_f), to_chunks(k), to_chunks(v)))
    o = o.transpose(1, 0, 3, 2, 4).reshape(B, T, REC_HEADS, REC_DV)
    return o, s_final


def mixer(h, rows, s0_f, s0_b, lb, w_in, conv_w, conv_b, conv_ln_g, conv_ln_b, rec_norm_g, w_out):
    B, T, _ = h.shape
    f32 = jnp.float32
    C, R = CONV_WIDTH, REC_WIDTH
    proj = h @ w_in
    ga, gb, q, zf, zb, iv, g = jnp.split(
        proj, [C, 2 * C, 2 * C + R, 2 * C + 2 * R, 2 * C + 3 * R, 2 * C + 4 * R], axis=-1)

    u = (ga * jax.nn.sigmoid(gb)).reshape(B * rows, T // rows, C)
    u = lax.conv_general_dilated(
        u, conv_w[:, None, :], window_strides=(1,),
        padding=[(CONV_TAPS // 2, CONV_TAPS // 2)],
        dimension_numbers=('NWC', 'WIO', 'NWC'), feature_group_count=C) + conv_b
    conv_out = jax.nn.silu(layer_norm(u.reshape(B, T, C), conv_ln_g, conv_ln_b))

    def heads(a):
        return a.astype(f32).reshape(B, T, REC_HEADS, -1)
    qh, vh = heads(q), heads(iv)
    lf_f, k_f = hgrn2_gates(heads(zf), lb[0].reshape(REC_HEADS, REC_DK))
    lf_b, k_b = hgrn2_gates(heads(zb), lb[1].reshape(REC_HEADS, REC_DK))
    o_f, s_f = hgrn2_scan(qh, lf_f, k_f, vh, s0_f.astype(f32))
    flip = lambda a: a[:, ::-1]
    o_b, s_b = hgrn2_scan(flip(qh), flip(lf_b), flip(k_b), flip(vh), s0_b.astype(f32))
    o = o_f + flip(o_b)
    o = o * lax.rsqrt(jnp.mean(o * o, -1, keepdims=True) + RMS_EPS)
    o = o.reshape(B, T, REC_WIDTH) * rec_norm_g.astype(f32)
    rec_out = (o * jax.nn.silu(g.astype(f32))).astype(h.dtype)

    out = jnp.concatenate([conv_out, rec_out], axis=-1) @ w_out
    return out, s_f, s_b


def moe(h, router_w, router_b, w_gate_up, b_gate_up, w_down, b_down):
    B, T, D = h.shape
    t = h.reshape(B * T, D)
    logits = (t @ router_w + router_b).astype(jnp.float32)
    top_v, top_i = lax.top_k(logits, TOP_K)
    wts = jax.nn.softmax(top_v, axis=-1)
    gates = jnp.sum(jax.nn.one_hot(top_i, N_EXPERTS, dtype=jnp.float32) * wts[..., None], axis=1)
    gates = gates.astype(h.dtype)
    y = jnp.zeros_like(t)
    for e in range(N_EXPERTS):
        gu = t @ w_gate_up[e] + b_gate_up[e]
        gate = jnp.minimum(gu[:, :D_FF], SWIGLU_LIMIT)
        up = jnp.clip(gu[:, D_FF:], -SWIGLU_LIMIT, SWIGLU_LIMIT)
        act = (up + 1.0) * gate * jax.nn.sigmoid(SWIGLU_ALPHA * gate)
        y = y + gates[:, e:e + 1] * (act @ w_down[e] + b_down[e])
    return y.reshape(B, T, D)


def trunk_layer(x, cond, rows, s0_f, s0_b, lb, w_ada, b_ada, w_in, conv_w, conv_b, conv_ln_g,
                conv_ln_b, rec_norm_g, w_out, ln1_g, ln1_b, ln2_g, ln2_b, router_w, router_b,
                w_gate_up, b_gate_up, w_down, b_down):
    mod = (jax.nn.silu(cond) @ w_ada + b_ada)[:, None, :]
    sh1, sc1, g1, sh2, sc2, g2 = jnp.split(mod, 6, axis=-1)
    h = layer_norm(x) * (1.0 + sc1) + sh1
    mix, s_f, s_b = mixer(h, rows, s0_f, s0_b, lb, w_in, conv_w, conv_b, conv_ln_g, conv_ln_b,
                          rec_norm_g, w_out)
    x = layer_norm(DEEPNORM_ALPHA * x + g1 * mix, ln1_g, ln1_b)
    h = layer_norm(x) * (1.0 + sc2) + sh2
    ff = moe(h, router_w, router_b, w_gate_up, b_gate_up, w_down, b_down)
    x = layer_norm(DEEPNORM_ALPHA * x + g2 * ff, ln2_g, ln2_b)
    return x, s_f, s_b


def setup_inputs(seed: int = 0) -> dict:
    key = jax.random.key(seed)
    ks = jax.random.split(key, 26)
    nrm = lambda k, shape, s=1.0: jax.random.normal(k, shape, jnp.float32) * s
    st_shape = (DEC_BATCH, DEPTH, REC_HEADS, REC_DK, REC_DV)
    return {
        "x_prompt": nrm(ks[0], (BATCH, SEQ, D_MODEL)),
        "x_sample": nrm(ks[1], (DEC_BATCH, DEC_SEQ, D_MODEL)),
        "c": nrm(ks[2], (DEC_BATCH, D_MODEL)),
        "state_fwd": nrm(ks[3], st_shape),
        "state_bwd": nrm(ks[4], st_shape),
        "c_ctx": nrm(ks[5], (D_MODEL,)),
        "w_ada": nrm(ks[6], (DEPTH, D_MODEL, 6 * D_MODEL), 0.5 * D_MODEL ** -0.5),
        "b_ada": nrm(ks[7], (DEPTH, 6 * D_MODEL), 0.01),
        "w_in": nrm(ks[8], (DEPTH, D_MODEL, IN_COLS), D_MODEL ** -0.5),
        "conv_w": nrm(ks[9], (DEPTH, CONV_TAPS, CONV_WIDTH), CONV_TAPS ** -0.5),
        "conv_b": nrm(ks[10], (DEPTH, CONV_WIDTH), 0.01),
        "conv_ln_g": 1.0 + nrm(ks[11], (DEPTH, CONV_WIDTH), 0.01),
        "conv_ln_b": nrm(ks[12], (DEPTH, CONV_WIDTH), 0.01),
        "lb_logits": nrm(ks[13], (DEPTH + 1, 2, REC_WIDTH), 0.5),
        "rec_norm_g": 1.0 + nrm(ks[14], (DEPTH, REC_WIDTH), 0.01),
        "w_out": nrm(ks[15], (DEPTH, MIX_WIDTH, D_MODEL), DEEPNORM_BETA * MIX_WIDTH ** -0.5),
        "ln1_g": 1.0 + nrm(ks[16], (DEPTH, D_MODEL), 0.01),
        "ln1_b": nrm(ks[17], (DEPTH, D_MODEL), 0.01),
        "ln2_g": 1.0 + nrm(ks[18], (DEPTH, D_MODEL), 0.01),
        "ln2_b": nrm(ks[19], (DEPTH, D_MODEL), 0.01),
        "router_w": nrm(ks[20], (DEPTH, D_MODEL, N_EXPERTS), D_MODEL ** -0.5),
        "router_b": nrm(ks[21], (DEPTH, N_EXPERTS), 0.01),
        "w_gate_up": nrm(ks[22], (DEPTH, N_EXPERTS, D_MODEL, 2 * D_FF), D_MODEL ** -0.5),
        "b_gate_up": nrm(ks[23], (DEPTH, N_EXPERTS, 2 * D_FF), 0.01),
        "w_down": nrm(ks[24], (DEPTH, N_EXPERTS, D_FF, D_MODEL), DEEPNORM_BETA * D_FF ** -0.5),
        "b_down": nrm(ks[25], (DEPTH, N_EXPERTS, D_MODEL), 0.01),
    }


def reference(x_prompt, x_sample, c, state_fwd, state_bwd, c_ctx, w_ada, b_ada, w_in, conv_w,
              conv_b, conv_ln_g, conv_ln_b, lb_logits, rec_norm_g, w_out, ln1_g, ln1_b, ln2_g,
              ln2_b, router_w, router_b, w_gate_up, b_gate_up, w_down, b_down):
    lbs = lower_bounds(lb_logits)

    def layer(l, x, cond, rows, s0_f, s0_b):
        return trunk_layer(x, cond, rows, s0_f, s0_b, lbs[l], w_ada[l], b_ada[l], w_in[l],
                           conv_w[l], conv_b[l], conv_ln_g[l], conv_ln_b[l], rec_norm_g[l],
                           w_out[l], ln1_g[l], ln1_b[l], ln2_g[l], ln2_b[l], router_w[l],
                           router_b[l], w_gate_up[l], b_gate_up[l], w_down[l], b_down[l])

    nb = x_prompt.shape[0]
    cond_p = jnp.broadcast_to(c_ctx, (nb, D_MODEL))
    zero_state = jnp.zeros((nb, REC_HEADS, REC_DK, REC_DV), jnp.float32)
    xp = x_prompt
    new_f, new_b = [], []
    for l in range(DEPTH):
        xp, s_f, s_b = layer(l, xp, cond_p, 1, zero_state, zero_state)
        new_f.append(s_f.astype(x_prompt.dtype))
        new_b.append(s_b.astype(x_prompt.dtype))
    new_state_fwd = jnp.stack(new_f, axis=1)
    new_state_bwd = jnp.stack(new_b, axis=1)

    rows = x_sample.shape[1] // GRID_W
    xs = x_sample
    for l in range(DEPTH):
        xs, _, _ = layer(l, xs, c, rows, state_fwd[:, l], state_bwd[:, l])

    return (xp, xs, new_state_fwd, new_state_bwd)
```

```python
import functools

import numpy as np
import jax
import jax.numpy as jnp
from jax import lax
from jax.experimental import pallas as pl
from jax.experimental.pallas import tpu as pltpu

F32 = jnp.float32
BF16 = jnp.bfloat16

D_MODEL = 1024
CONV_WIDTH = 512
CONV_TAPS = 31
REC_HEADS = 4
REC_DK = 128
REC_WIDTH = REC_HEADS * REC_DK
REC_COLS = 5 * REC_WIDTH
CHUNK = 64
N_LEVELS = 6
N_EXPERTS = 32
TOP_K = 4
D_FF = 1024
SWIGLU_LIMIT = 7.0
SWIGLU_ALPHA = 1.702
LN_EPS = 1e-5
RMS_EPS = 1e-6
GRID_W = 64

LANES = 128
TOK_TILE = 256
EXP_TILE = 256
VMEM_LIMIT = 56 * 1024 * 1024


def _sigmoid(x):
    return 1.0 / (1.0 + jnp.exp(-x))


def _layer_norm(x):
    mu = jnp.mean(x, axis=-1, keepdims=True)
    xc = x - mu
    var = jnp.mean(xc * xc, axis=-1, keepdims=True)
    return xc * lax.rsqrt(var + LN_EPS)


def _dot(a, b):
    return jnp.dot(a, b, preferred_element_type=F32)


def _dot_nt(a, b):
    return lax.dot_general(a, b, (((1,), (1,)), ((), ())), preferred_element_type=F32)


def _split3(x):
    hi = x.astype(BF16)
    r1 = x - hi.astype(F32)
    mid = r1.astype(BF16)
    lo = (r1 - mid.astype(F32)).astype(BF16)
    return hi, mid, lo


def _ada_kernel(c_ref, w_ref, b_ref, o_ref):
    c = c_ref[...]
    s = (c * _sigmoid(c)).astype(BF16)
    o_ref[...] = _dot(s, w_ref[...].astype(BF16)) + b_ref[...]


def _ada(cond, w_ada, b_ada):
    rows = cond.shape[0]
    ncol = w_ada.shape[1]
    tn = 1024
    return pl.pallas_call(
        _ada_kernel,
        out_shape=jax.ShapeDtypeStruct((rows, ncol), F32),
        grid=(ncol // tn,),
        in_specs=[pl.BlockSpec((rows, D_MODEL), lambda j: (0, 0)),
                  pl.BlockSpec((D_MODEL, tn), lambda j: (0, j)),
                  pl.BlockSpec((1, tn), lambda j: (0, j))],
        out_specs=pl.BlockSpec((rows, tn), lambda j: (0, j)),
        compiler_params=pltpu.CompilerParams(dimension_semantics=("parallel",),
                                             vmem_limit_bytes=VMEM_LIMIT),
        name="ada_mod",
    )(cond, w_ada, b_ada.reshape(1, ncol))


def _inproj_kernel(n_prompt_tiles, prompt_seg, xp_ref, xs_ref, mod_ref, win_ref, cw_ref, cb_ref,
                   cg_ref, cbeta_ref, conv_ref, rec_ref):
    i = pl.program_id(0)
    is_prompt = i < n_prompt_tiles
    x = jnp.where(is_prompt, xp_ref[...], xs_ref[...])
    mod = mod_ref[0]
    sh1 = mod[:, 0:D_MODEL]
    sc1 = mod[:, D_MODEL:2 * D_MODEL]
    h = (_layer_norm(x) * (1.0 + sc1) + sh1).astype(BF16)

    glu = _dot(h, win_ref[:, 0:2 * CONV_WIDTH])
    rec_ref[...] = _dot(h, win_ref[:, 2 * CONV_WIDTH:])
    u = glu[:, :CONV_WIDTH] * _sigmoid(glu[:, CONV_WIDTH:])

    seg_len = jnp.where(is_prompt, prompt_seg, GRID_W)
    pos = lax.broadcasted_iota(jnp.int32, u.shape, 0) & (seg_len - 1)
    half = CONV_TAPS // 2
    acc = u * cw_ref[half:half + 1, :] + cb_ref[...]
    for j in range(CONV_TAPS):
        d = j - half
        if d == 0:
            continue
        shifted = pltpu.roll(u, (-d) % TOK_TILE, axis=0)
        valid = (pos >= -d) if d < 0 else (pos < seg_len - d)
        acc = acc + jnp.where(valid, shifted, 0.0) * cw_ref[j:j + 1, :]
    y = _layer_norm(acc) * cg_ref[...] + cbeta_ref[...]
    conv_ref[...] = (y * _sigmoid(y)).astype(BF16)


def _inproj(xp, xs, mod3, w_in_bf, conv_w, conv_b, conv_g, conv_beta, prompt_seq, sample_seq):
    n_p, n_s = xp.shape[0], xs.shape[0]
    n_tok = n_p + n_s
    npt = n_p // TOK_TILE
    tiles_per_sample_seq = sample_seq // TOK_TILE
    assert prompt_seq == TOK_TILE and sample_seq % TOK_TILE == 0

    def mod_row(i):
        return jnp.where(i < npt, 0, 1 + (i - npt) // tiles_per_sample_seq)

    const = lambda i: (0, 0)
    return pl.pallas_call(
        functools.partial(_inproj_kernel, npt, prompt_seq),
        out_shape=(jax.ShapeDtypeStruct((n_tok, CONV_WIDTH), BF16),
                   jax.ShapeDtypeStruct((n_tok, REC_COLS), F32)),
        grid=(n_tok // TOK_TILE,),
        in_specs=[pl.BlockSpec((TOK_TILE, D_MODEL), lambda i: (jnp.minimum(i, npt - 1), 0)),
                  pl.BlockSpec((TOK_TILE, D_MODEL), lambda i: (jnp.maximum(i - npt, 0), 0)),
                  pl.BlockSpec((1, 1, 6 * D_MODEL), lambda i: (mod_row(i), 0, 0)),
                  pl.BlockSpec(w_in_bf.shape, const),
                  pl.BlockSpec(conv_w.shape, const),
                  pl.BlockSpec((1, CONV_WIDTH), const),
                  pl.BlockSpec((1, CONV_WIDTH), const),
                  pl.BlockSpec((1, CONV_WIDTH), const)],
        out_specs=(pl.BlockSpec((TOK_TILE, CONV_WIDTH), lambda i: (i, 0)),
                   pl.BlockSpec((TOK_TILE, REC_COLS), lambda i: (i, 0))),
        compiler_params=pltpu.CompilerParams(dimension_semantics=("parallel",),
                                             vmem_limit_bytes=VMEM_LIMIT),
        name="inproj_conv",
    )(xp, xs, mod3, w_in_bf, conv_w, conv_b, conv_g, conv_beta)


def _scan_constants():
    c = CHUNK
    w = np.zeros((N_LEVELS + 2, c, c), np.float32)
    m = np.zeros((N_LEVELS + 1, c, c), np.float32)
    t = np.arange(c)
    for j in range(N_LEVELS):
        half = 1 << j
        for ti in range(c):
            ref = (ti & ~(2 * half - 1)) + half - 1
            if ti & half:
                w[j, ti, ref + 1:ti + 1] = 1.0
            else:
                w[j, ti, ti + 1:ref + 1] = 1.0
        upper = (t[:, None] & half) != 0
        lower = (t[None, :] & half) == 0
        same = (t[:, None] >> (j + 1)) == (t[None, :] >> (j + 1))
        m[j] = (upper & lower & same).astype(np.float32)
    w[N_LEVELS] = (t[None, :] <= t[:, None]).astype(np.float32)
    w[N_LEVELS + 1] = (t[None, :] > t[:, None]).astype(np.float32)
    m[N_LEVELS] = np.eye(c, dtype=np.float32)
    w_f = w.reshape(-1, c)
    w_b = w[:, ::-1, ::-1].reshape(-1, c)
    m_b = m[:, ::-1, ::-1]
    return (jnp.asarray(w_f, BF16), jnp.asarray(np.ascontiguousarray(w_b), BF16),
            jnp.asarray(m), jnp.asarray(np.ascontiguousarray(m_b)))


def _scan_chain(q, z, v, lb, w_ref, m_ref, st_ref, head, last_row):
    e = jnp.exp(-jnp.abs(z))
    r = 1.0 / (1.0 + e)
    a = e * r
    sig = jnp.where(z >= 0, r, a)
    sig_neg = jnp.where(z >= 0, a, r)
    one_m_lb = 1.0 - lb
    log_f = jnp.log(lb + one_m_lb * sig)
    k = one_m_lb * sig_neg

    hi, mid, lo = _split3(log_f)
    w = w_ref[...]
    ex = jnp.exp(_dot(w, hi) + _dot(w, mid) + _dot(w, lo))

    c = CHUNK
    a_mat = m_ref[N_LEVELS] * _dot_nt(q.astype(BF16), k.astype(BF16))
    for j in range(N_LEVELS):
        xj = ex[j * c:(j + 1) * c]
        a_mat = a_mat + m_ref[j] * _dot_nt((q * xj).astype(BF16), (k * xj).astype(BF16))
    ex_b = ex[N_LEVELS * c:(N_LEVELS + 1) * c]
    ex_l = ex[(N_LEVELS + 1) * c:(N_LEVELS + 2) * c]
    st = st_ref[0, head]
    o = _dot(a_mat.astype(BF16), v.astype(BF16)) + _dot_nt((q * ex_b).astype(BF16), st.astype(BF16))
    decay_all = ex_b[last_row:last_row + 1, :]
    st_ref[0, head] = st * decay_all + _dot(v.T.astype(BF16), (k * ex_l).astype(BF16))
    return o


def _scan_kernel(fblk, bblk, first, seq, qf_ref, zf_ref, vf_ref, qb_ref, zb_ref, vb_ref, lb_ref,
                 wf_ref, wb_ref, mf_ref, mb_ref, s0f_ref, s0b_ref, of_ref, ob_ref, sf_ref, sb_ref):
    s = pl.program_id(0)

    @pl.when(first[s] == 1)
    def _():
        sf_ref[...] = s0f_ref[...]
        sb_ref[...] = s0b_ref[...]

    for h in range(REC_HEADS):
        cols = slice(h * REC_DK, (h + 1) * REC_DK)
        of_ref[:, cols] = _scan_chain(qf_ref[:, cols], zf_ref[:, cols], vf_ref[:, cols],
                                      lb_ref[0:1, cols], wf_ref, mf_ref, sf_ref, h, CHUNK - 1)
        ob_ref[:, cols] = _scan_chain(qb_ref[:, cols], zb_ref[:, cols], vb_ref[:, cols],
                                      lb_ref[1:2, cols], wb_ref, mb_ref, sb_ref, h, 0)


def _scan(rec, lb, s0f_t, s0b_t, seq_lens):
    n_tok = rec.shape[0]
    fblk, bblk, first, seq = [], [], [], []
    base = 0
    for si, ln in enumerate(seq_lens):
        n = ln // CHUNK
        for ci in range(n):
            fblk.append(base + ci)
            bblk.append(base + n - 1 - ci)
            first.append(1 if ci == 0 else 0)
            seq.append(si)
        base += n
    steps = len(fblk)
    as_i32 = lambda a: jnp.asarray(np.asarray(a, np.int32))
    w_f, w_b, m_f, m_b = _scan_constants()
    n_seq = len(seq_lens)

    def col(block_of, c):
        return pl.BlockSpec((CHUNK, REC_WIDTH), lambda s, fb, bb, fi, sq: (block_of(fb, bb)[s], c))

    fwd = lambda fb, bb: fb
    bwd = lambda fb, bb: bb
    const2 = lambda s, fb, bb, fi, sq: (0, 0)
    const3 = lambda s, fb, bb, fi, sq: (0, 0, 0)
    state_spec = pl.BlockSpec((1, REC_HEADS, REC_DK, REC_DK), lambda s, fb, bb, fi, sq: (sq[s], 0, 0, 0))
    grid_spec = pltpu.PrefetchScalarGridSpec(
        num_scalar_prefetch=4,
        grid=(steps,),
        in_specs=[col(fwd, 0), col(fwd, 1), col(fwd, 3), col(bwd, 0), col(bwd, 2), col(bwd, 3),
                  pl.BlockSpec((2, REC_WIDTH), const2),
                  pl.BlockSpec(w_f.shape, const2), pl.BlockSpec(w_b.shape, const2),
                  pl.BlockSpec(m_f.shape, const3), pl.BlockSpec(m_b.shape, const3),
                  state_spec, state_spec],
        out_specs=(pl.BlockSpec((CHUNK, REC_WIDTH), lambda s, fb, bb, fi, sq: (fb[s], 0)),
                   pl.BlockSpec((CHUNK, REC_WIDTH), lambda s, fb, bb, fi, sq: (bb[s], 0)),
                   state_spec, state_spec),
    )
    st_shape = jax.ShapeDtypeStruct((n_seq, REC_HEADS, REC_DK, REC_DK), F32)
    return pl.pallas_call(
        _scan_kernel,
        out_shape=(jax.ShapeDtypeStruct((n_tok, REC_WIDTH), F32),
                   jax.ShapeDtypeStruct((n_tok, REC_WIDTH), F32), st_shape, st_shape),
        grid_spec=grid_spec,
        compiler_params=pltpu.CompilerParams(dimension_semantics=("arbitrary",),
                                             vmem_limit_bytes=VMEM_LIMIT),
        name="hgrn2_scan",
    )(as_i32(fblk), as_i32(bblk), as_i32(first), as_i32(seq),
      rec, rec, rec, rec, rec, rec, lb, w_f, w_b, m_f, m_b, s0f_t, s0b_t)


def _post_kernel(n_prompt_tiles, alpha, of_ref, ob_ref, g_ref, conv_ref, xp_ref, xs_ref, mod_ref,
                 wout_ref, rg_ref, l1g_ref, l1b_ref, rwh_ref, rwl_ref, rb_ref, tri_ref,
                 x1_ref, h2_ref, topi_ref, rank_ref, w4_ref, cnt_ref, run_ref):
    i = pl.program_id(0)

    @pl.when(i == 0)
    def _():
        run_ref[...] = jnp.zeros_like(run_ref)

    x = jnp.where(i < n_prompt_tiles, xp_ref[...], xs_ref[...])
    mod = mod_ref[0]
    g1 = mod[:, 2 * D_MODEL:3 * D_MODEL]
    sh2 = mod[:, 3 * D_MODEL:4 * D_MODEL]
    sc2 = mod[:, 4 * D_MODEL:5 * D_MODEL]

    g = g_ref[...]
    silu_g = g * _sigmoid(g)
    mix = _dot(conv_ref[...], wout_ref[0:CONV_WIDTH, :])
    for h in range(REC_HEADS):
        cols = slice(h * REC_DK, (h + 1) * REC_DK)
        o = of_ref[:, cols] + ob_ref[:, cols]
        o = o * lax.rsqrt(jnp.mean(o * o, axis=-1, keepdims=True) + RMS_EPS)
        rec_out = (o * rg_ref[:, cols] * silu_g[:, cols]).astype(BF16)
        mix = mix + _dot(rec_out, wout_ref[CONV_WIDTH + h * REC_DK:CONV_WIDTH + (h + 1) * REC_DK, :])

    x1 = _layer_norm(alpha * x + g1 * mix) * l1g_ref[...] + l1b_ref[...]
    x1_ref[...] = x1
    h2 = _layer_norm(x1) * (1.0 + sc2) + sh2
    h2_ref[...] = h2

    h_hi = h2.astype(BF16)
    h_lo = (h2 - h_hi.astype(F32)).astype(BF16)
    logits = (_dot(h_hi, rwh_ref[...]) + _dot(h_lo, rwh_ref[...]) + _dot(h_hi, rwl_ref[...])
              + rb_ref[...])
    lane = lax.broadcasted_iota(jnp.int32, logits.shape, 1)
    lane_f = lane.astype(F32)
    neg_inf = jnp.float32(-jnp.inf)
    work = jnp.where(lane < N_EXPERTS, logits, neg_inf)
    vals, sels, idxs = [], [], []
    for _ in range(TOP_K):
        m = jnp.max(work, axis=-1, keepdims=True)
        idx = jnp.min(jnp.where(work == m, lane_f, float(LANES)), axis=-1, keepdims=True)
        sel = lane_f == idx
        vals.append(m)
        idxs.append(idx)
        sels.append(sel)
        work = jnp.where(sel, neg_inf, work)
    exps = [jnp.exp(v - vals[0]) for v in vals]
    denom = exps[0] + exps[1] + exps[2] + exps[3]

    sel_any = jnp.zeros(logits.shape, F32)
    for sel in sels:
        sel_any = jnp.where(sel, 1.0, sel_any)
    prior = _dot(tri_ref[...], sel_any.astype(BF16)) + run_ref[...]
    run_new = run_ref[...] + jnp.sum(sel_any, axis=0, keepdims=True)
    run_ref[...] = run_new
    cnt_ref[...] = run_new

    topi = jnp.zeros(logits.shape, F32)
    rank = jnp.zeros(logits.shape, F32)
    w4 = jnp.zeros(logits.shape, F32)
    for j in range(TOP_K):
        rj = jnp.sum(jnp.where(sels[j], prior, 0.0), axis=-1, keepdims=True)
        topi = jnp.where(lane == j, idxs[j], topi)
        rank = jnp.where(lane == j, rj, rank)
        w4 = jnp.where(lane == j, exps[j] / denom, w4)
    topi_ref[...] = topi.astype(jnp.int32)
    rank_ref[...] = rank.astype(jnp.int32)
    w4_ref[...] = w4


def _post(o_f, o_b, rec, conv, xp, xs, mod3, w_out_bf, rec_g, ln1_g, ln1_b, rw_hi, rw_lo, rb,
          alpha, sample_seq):
    n_p, n_s = xp.shape[0], xs.shape[0]
    n_tok = n_p + n_s
    npt = n_p // TOK_TILE
    tiles_per_sample_seq = sample_seq // TOK_TILE
    tri = jnp.asarray(np.tril(np.ones((TOK_TILE, TOK_TILE), np.float32), -1), BF16)

    def mod_row(i):
        return jnp.where(i < npt, 0, 1 + (i - npt) // tiles_per_sample_seq)

    const = lambda i: (0, 0)
    tok = lambda w: pl.BlockSpec((TOK_TILE, w), lambda i: (i, 0))
    return pl.pallas_call(
        functools.partial(_post_kernel, npt, alpha),
        out_shape=(jax.ShapeDtypeStruct((n_tok, D_MODEL), F32),
                   jax.ShapeDtypeStruct((n_tok, D_MODEL), F32),
                   jax.ShapeDtypeStruct((n_tok, LANES), jnp.int32),
                   jax.ShapeDtypeStruct((n_tok, LANES), jnp.int32),
                   jax.ShapeDtypeStruct((n_tok, LANES), F32),
                   jax.ShapeDtypeStruct((1, LANES), F32)),
        grid=(n_tok // TOK_TILE,),
        in_specs=[tok(REC_WIDTH), tok(REC_WIDTH),
                  pl.BlockSpec((TOK_TILE, REC_WIDTH), lambda i: (i, 4)),
                  tok(CONV_WIDTH),
                  pl.BlockSpec((TOK_TILE, D_MODEL), lambda i: (jnp.minimum(i, npt - 1), 0)),
                  pl.BlockSpec((TOK_TILE, D_MODEL), lambda i: (jnp.maximum(i - npt, 0), 0)),
                  pl.BlockSpec((1, 1, 6 * D_MODEL), lambda i: (mod_row(i), 0, 0)),
                  pl.BlockSpec(w_out_bf.shape, const),
                  pl.BlockSpec((1, REC_WIDTH), const),
                  pl.BlockSpec((1, D_MODEL), const), pl.BlockSpec((1, D_MODEL), const),
                  pl.BlockSpec((D_MODEL, LANES), const), pl.BlockSpec((D_MODEL, LANES), const),
                  pl.BlockSpec((1, LANES), const),
                  pl.BlockSpec((TOK_TILE, TOK_TILE), const)],
        out_specs=(tok(D_MODEL), tok(D_MODEL), tok(LANES), tok(LANES), tok(LANES),
                   pl.BlockSpec((1, LANES), const)),
        scratch_shapes=[pltpu.VMEM((1, LANES), F32)],
        compiler_params=pltpu.CompilerParams(dimension_semantics=("arbitrary",),
                                             vmem_limit_bytes=VMEM_LIMIT),
        name="post_mixer_router",
    )(o_f, o_b, rec, conv, xp, xs, mod3, w_out_bf, rec_g, ln1_g, ln1_b, rw_hi, rw_lo, rb, tri)


def _dispatch_kernel(gend, gpad, dest_ref, h_ref, xs_ref, zero_ref, sem):
    i = pl.program_id(0)

    @pl.when(i == 0)
    def _():
        zero_ref[...] = jnp.zeros_like(zero_ref)
        n_tiles = xs_ref.shape[0] // EXP_TILE

        def zero_tile(start):
            start = pl.multiple_of(start, EXP_TILE)
            return pltpu.make_async_copy(zero_ref, xs_ref.at[pl.ds(start, EXP_TILE)], sem)

        def group_tail(action):
            def body(e, carry):
                @pl.when(gpad[e] > 0)
                def _():
                    action(zero_tile(gend[e] - EXP_TILE))
                return carry
            lax.fori_loop(0, N_EXPERTS, body, 0)

        def unused_tiles(action):
            def body(tile, carry):
                action(zero_tile(tile * EXP_TILE))
                return carry
            lax.fori_loop(gend[N_EXPERTS - 1] // EXP_TILE, n_tiles, body, 0)

        group_tail(lambda cp: cp.start())
        unused_tiles(lambda cp: cp.start())
        group_tail(lambda cp: cp.wait())
        unused_tiles(lambda cp: cp.wait())

    def issue(t, carry):
        for j in range(TOP_K):
            d = dest_ref[0, 0, t * TOP_K + j]
            pltpu.make_async_copy(h_ref.at[pl.ds(t, 1)], xs_ref.at[pl.ds(d, 1)], sem).start()
        return carry

    lax.fori_loop(0, TOK_TILE, issue, 0)
    for _ in range(TOP_K):
        pltpu.make_async_copy(h_ref, xs_ref.at[pl.ds(0, TOK_TILE)], sem).wait()


def _dispatch(h2, dest3, gend, gpad, n_rows):
    n_tok = h2.shape[0]
    grid_spec = pltpu.PrefetchScalarGridSpec(
        num_scalar_prefetch=2,
        grid=(n_tok // TOK_TILE,),
        in_specs=[pl.BlockSpec((1, 1, TOK_TILE * TOP_K), lambda i, ge, gp: (i, 0, 0),
                               memory_space=pltpu.SMEM),
                  pl.BlockSpec((TOK_TILE, D_MODEL), lambda i, ge, gp: (i, 0))],
        out_specs=pl.BlockSpec(memory_space=pl.ANY),
        scratch_shapes=[pltpu.VMEM((EXP_TILE, D_MODEL), F32), pltpu.SemaphoreType.DMA(())],
    )
    return pl.pallas_call(
        _dispatch_kernel,
        out_shape=jax.ShapeDtypeStruct((n_rows, D_MODEL), F32),
        grid_spec=grid_spec,
        compiler_params=pltpu.CompilerParams(dimension_semantics=("arbitrary",),
                                             vmem_limit_bytes=VMEM_LIMIT,
                                             has_side_effects=True),
        name="moe_dispatch",
    )(gend, gpad, dest3, h2)


def _expert_kernel(tile_e, tile_blk, n_act, xs_ref, wgu_ref, bgu_ref, wd_ref, bd_ref, ys_ref,
                   wgu_bf, wd_bf):
    i = pl.program_id(0)
    prev = tile_e[jnp.maximum(i - 1, 0)]
    new_expert = jnp.logical_or(i == 0, tile_e[i] != prev)

    @pl.when(jnp.logical_and(new_expert, i < n_act[0]))
    def _():
        rows = 128

        def cast(r, carry):
            sl = pl.ds(pl.multiple_of(r * rows, rows), rows)
            wgu_bf[sl, :] = wgu_ref[0, sl, :].astype(BF16)
            wd_bf[sl, :] = wd_ref[0, sl, :].astype(BF16)
            return carry

        lax.fori_loop(0, D_MODEL // rows, cast, 0)

    @pl.when(i < n_act[0])
    def _():
        x = xs_ref[...].astype(BF16)
        gu = _dot(x, wgu_bf[...]) + bgu_ref[0]
        gate = jnp.minimum(gu[:, :D_FF], SWIGLU_LIMIT)
        up = jnp.clip(gu[:, D_FF:], -SWIGLU_LIMIT, SWIGLU_LIMIT)
        act = (up + 1.0) * gate * _sigmoid(SWIGLU_ALPHA * gate)
        ys_ref[...] = _dot(act.astype(BF16), wd_bf[...]) + bd_ref[0]

    @pl.when(i >= n_act[0])
    def _():
        ys_ref[...] = jnp.zeros_like(ys_ref)


def _experts(xs, tile_e, tile_blk, n_act, w_gate_up, b_gate_up, w_down, b_down):
    n_rows = xs.shape[0]
    n_tiles = n_rows // EXP_TILE
    grid_spec = pltpu.PrefetchScalarGridSpec(
        num_scalar_prefetch=3,
        grid=(n_tiles,),
        in_specs=[pl.BlockSpec((EXP_TILE, D_MODEL), lambda i, te, tb, na: (tb[i], 0)),
                  pl.BlockSpec((1, D_MODEL, 2 * D_FF), lambda i, te, tb, na: (te[i], 0, 0)),
                  pl.BlockSpec((1, 1, 2 * D_FF), lambda i, te, tb, na: (te[i], 0, 0)),
                  pl.BlockSpec((1, D_FF, D_MODEL), lambda i, te, tb, na: (te[i], 0, 0)),
                  pl.BlockSpec((1, 1, D_MODEL), lambda i, te, tb, na: (te[i], 0, 0))],
        out_specs=pl.BlockSpec((EXP_TILE, D_MODEL), lambda i, te, tb, na: (i, 0)),
        scratch_shapes=[pltpu.VMEM((D_MODEL, 2 * D_FF), BF16), pltpu.VMEM((D_FF, D_MODEL), BF16)],
    )
    return pl.pallas_call(
        _expert_kernel,
        out_shape=jax.ShapeDtypeStruct((n_rows, D_MODEL), F32),
        grid_spec=grid_spec,
        compiler_params=pltpu.CompilerParams(dimension_semantics=("arbitrary",),
                                             vmem_limit_bytes=VMEM_LIMIT),
        name="moe_experts",
    )(tile_e, tile_blk, n_act, xs, w_gate_up, b_gate_up.reshape(N_EXPERTS, 1, 2 * D_FF),
      w_down, b_down.reshape(N_EXPERTS, 1, D_MODEL))


def _combine_kernel(n_prompt_tiles, alpha, dest_ref, ys_ref, x1_ref, w4_ref, mod_ref, l2g_ref,
                    l2b_ref, yp_ref, ysm_ref, buf, sem):
    i = pl.program_id(0)

    def issue(t, carry):
        for j in range(TOP_K):
            d = dest_ref[0, 0, t * TOP_K + j]
            pltpu.make_async_copy(ys_ref.at[pl.ds(d, 1)], buf.at[j, pl.ds(t, 1)], sem).start()
        return carry

    lax.fori_loop(0, TOK_TILE, issue, 0)
    for j in range(TOP_K):
        pltpu.make_async_copy(ys_ref.at[pl.ds(0, TOK_TILE)], buf.at[j], sem).wait()

    w4 = w4_ref[...]
    ff = buf[0] * w4[:, 0:1]
    for j in range(1, TOP_K):
        ff = ff + buf[j] * w4[:, j:j + 1]
    g2 = mod_ref[0][:, 5 * D_MODEL:6 * D_MODEL]
    out = _layer_norm(alpha * x1_ref[...] + g2 * ff) * l2g_ref[...] + l2b_ref[...]

    @pl.when(i < n_prompt_tiles)
    def _():
        yp_ref[...] = out

    @pl.when(i >= n_prompt_tiles)
    def _():
        ysm_ref[...] = out


def _combine(ys, dest3, x1, w4, mod3, ln2_g, ln2_b, n_p, alpha, sample_seq):
    n_tok = x1.shape[0]
    n_s = n_tok - n_p
    npt = n_p // TOK_TILE
    tiles_per_sample_seq = sample_seq // TOK_TILE

    def mod_row(i):
        return jnp.where(i < npt, 0, 1 + (i - npt) // tiles_per_sample_seq)

    const = lambda i: (0, 0)
    return pl.pallas_call(
        functools.partial(_combine_kernel, npt, alpha),
        out_shape=(jax.ShapeDtypeStruct((n_p, D_MODEL), F32),
                   jax.ShapeDtypeStruct((n_s, D_MODEL), F32)),
        grid=(n_tok // TOK_TILE,),
        in_specs=[pl.BlockSpec((1, 1, TOK_TILE * TOP_K), lambda i: (i, 0, 0),
                               memory_space=pltpu.SMEM),
                  pl.BlockSpec(memory_space=pl.ANY),
                  pl.BlockSpec((TOK_TILE, D_MODEL), lambda i: (i, 0)),
                  pl.BlockSpec((TOK_TILE, LANES), lambda i: (i, 0)),
                  pl.BlockSpec((1, 1, 6 * D_MODEL), lambda i: (mod_row(i), 0, 0)),
                  pl.BlockSpec((1, D_MODEL), const), pl.BlockSpec((1, D_MODEL), const)],
        out_specs=(pl.BlockSpec((TOK_TILE, D_MODEL), lambda i: (jnp.minimum(i, npt - 1), 0)),
                   pl.BlockSpec((TOK_TILE, D_MODEL), lambda i: (jnp.maximum(i - npt, 0), 0))),
        scratch_shapes=[pltpu.VMEM((TOP_K, TOK_TILE, D_MODEL), F32), pltpu.SemaphoreType.DMA(())],
        compiler_params=pltpu.CompilerParams(dimension_semantics=("arbitrary",),
                                             vmem_limit_bytes=VMEM_LIMIT),
        name="moe_combine",
    )(dest3, ys, x1, w4, mod3, ln2_g, ln2_b)


def kernel(x_prompt, x_sample, c, state_fwd, state_bwd, c_ctx, w_ada, b_ada, w_in, conv_w, conv_b,
           conv_ln_g, conv_ln_b, lb_logits, rec_norm_g, w_out, ln1_g, ln1_b, ln2_g, ln2_b,
           router_w, router_b, w_gate_up, b_gate_up, w_down, b_down):
    depth = w_ada.shape[0]
    assert depth == 1
    alpha = (2.0 * depth) ** 0.25
    nb_p, seq_p, _ = x_prompt.shape
    nb_s, seq_s, _ = x_sample.shape
    n_p, n_s = nb_p * seq_p, nb_s * seq_s
    n_tok = n_p + n_s
    row2 = lambda a: a.reshape(1, -1)

    lb = jnp.cumsum(jax.nn.softmax(lb_logits.astype(F32), axis=0), axis=0)[0]

    cond = jnp.concatenate([c_ctx[None, :], c, jnp.zeros((16 - 1 - nb_s, D_MODEL), F32)], axis=0)
    mod = _ada(cond, w_ada[0], b_ada[0])
    mod3 = mod.reshape(16, 1, 6 * D_MODEL)

    xp = x_prompt.reshape(n_p, D_MODEL)
    xs = x_sample.reshape(n_s, D_MODEL)
    conv_w_pad = jnp.concatenate([conv_w[0], jnp.zeros((1, CONV_WIDTH), F32)], axis=0)
    conv_out, rec = _inproj(xp, xs, mod3, w_in[0].astype(BF16), conv_w_pad, row2(conv_b[0]),
                            row2(conv_ln_g[0]), row2(conv_ln_b[0]), seq_p, seq_s)

    zeros_state = jnp.zeros((nb_p, REC_HEADS, REC_DK, REC_DK), F32)
    s0f = jnp.concatenate([zeros_state, jnp.swapaxes(state_fwd[:, 0], -1, -2)], axis=0)
    s0b = jnp.concatenate([zeros_state, jnp.swapaxes(state_bwd[:, 0], -1, -2)], axis=0)
    o_f, o_b, sf_t, sb_t = _scan(rec, lb, s0f, s0b, [seq_p] * nb_p + [seq_s] * nb_s)
    new_f = jnp.swapaxes(sf_t[:nb_p], -1, -2)[:, None]
    new_b = jnp.swapaxes(sb_t[:nb_p], -1, -2)[:, None]

    rw = jnp.pad(router_w[0], ((0, 0), (0, LANES - N_EXPERTS)))
    rw_hi = rw.astype(BF16)
    rw_lo = (rw - rw_hi.astype(F32)).astype(BF16)
    rb = jnp.pad(router_b[0], (0, LANES - N_EXPERTS)).reshape(1, LANES)
    x1, h2, topi, rank, w4, counts = _post(
        o_f, o_b, rec, conv_out, xp, xs, mod3, w_out[0].astype(BF16), row2(rec_norm_g[0]),
        row2(ln1_g[0]), row2(ln1_b[0]), rw_hi, rw_lo, rb, alpha, seq_s)

    cnt = counts[0, :N_EXPERTS].astype(jnp.int32)
    gpad = ((cnt + EXP_TILE - 1) // EXP_TILE) * EXP_TILE
    gend = jnp.cumsum(gpad)
    gstart = gend - gpad
    n_rows = n_tok * TOP_K + N_EXPERTS * EXP_TILE
    n_tiles = n_rows // EXP_TILE
    n_act = gend[-1] // EXP_TILE
    tile_ids = jnp.minimum(jnp.arange(n_tiles, dtype=jnp.int32), n_act - 1)
    tile_e = jnp.sum((gend[None, :] <= tile_ids[:, None] * EXP_TILE).astype(jnp.int32), axis=1)
    dest = gstart[topi[:, :TOP_K]] + rank[:, :TOP_K]
    dest3 = dest.reshape(n_tok // TOK_TILE, 1, TOK_TILE * TOP_K).astype(jnp.int32)

    xs_sorted = _dispatch(h2, dest3, gend.astype(jnp.int32), gpad.astype(jnp.int32), n_rows)
    ys_sorted = _experts(xs_sorted, tile_e, tile_ids, n_act.reshape(1).astype(jnp.int32),
                         w_gate_up[0], b_gate_up[0], w_down[0], b_down[0])
    y_p, y_s = _combine(ys_sorted, dest3, x1, w4, mod3, row2(ln2_g[0]), row2(ln2_b[0]), n_p,
                        alpha, seq_s)
    return (y_p.reshape(nb_p, seq_p, D_MODEL), y_s.reshape(nb_s, seq_s, D_MODEL), new_f, new_b)
```

```python
import functools

import numpy as np
import jax
import jax.numpy as jnp
from jax import lax
from jax.experimental import pallas as pl
from jax.experimental.pallas import tpu as pltpu

F32 = jnp.float32
BF16 = jnp.bfloat16

D_MODEL = 1024
CONV_WIDTH = 512
CONV_TAPS = 31
REC_HEADS = 4
REC_DK = 128
REC_WIDTH = REC_HEADS * REC_DK
REC_COLS = 5 * REC_WIDTH
CHUNK = 64
N_LEVELS = 6
N_EXPERTS = 32
TOP_K = 4
D_FF = 1024
SWIGLU_LIMIT = 7.0
SWIGLU_ALPHA = 1.702
LN_EPS = 1e-5
RMS_EPS = 1e-6
GRID_W = 64

LANES = 128
TOK_TILE = 256
EXP_TILE = 512
VMEM_LIMIT = 56 * 1024 * 1024


def _sigmoid(x):
    return 1.0 / (1.0 + jnp.exp(-x))


def _layer_norm(x):
    mu = jnp.mean(x, axis=-1, keepdims=True)
    xc = x - mu
    var = jnp.mean(xc * xc, axis=-1, keepdims=True)
    return xc * lax.rsqrt(var + LN_EPS)


def _dot(a, b):
    return jnp.dot(a, b, preferred_element_type=F32)


def _dot_nt(a, b):
    return lax.dot_general(a, b, (((1,), (1,)), ((), ())), preferred_element_type=F32)


def _split3(x):
    hi = x.astype(BF16)
    r1 = x - hi.astype(F32)
    mid = r1.astype(BF16)
    lo = (r1 - mid.astype(F32)).astype(BF16)
    return hi, mid, lo


def _ada_kernel(c_ref, w_ref, b_ref, o_ref):
    c = c_ref[...]
    s = (c * _sigmoid(c)).astype(BF16)
    o_ref[...] = _dot(s, w_ref[...].astype(BF16)) + b_ref[...]


def _ada(cond, w_ada, b_ada):
    rows = cond.shape[0]
    ncol = w_ada.shape[1]
    tn = 1024
    return pl.pallas_call(
        _ada_kernel,
        out_shape=jax.ShapeDtypeStruct((rows, ncol), F32),
        grid=(ncol // tn,),
        in_specs=[pl.BlockSpec((rows, D_MODEL), lambda j: (0, 0)),
                  pl.BlockSpec((D_MODEL, tn), lambda j: (0, j)),
                  pl.BlockSpec((1, tn), lambda j: (0, j))],
        out_specs=pl.BlockSpec((rows, tn), lambda j: (0, j)),
        compiler_params=pltpu.CompilerParams(dimension_semantics=("parallel",),
                                             vmem_limit_bytes=VMEM_LIMIT),
        name="ada_mod",
    )(cond, w_ada, b_ada.reshape(1, ncol))


def _inproj_kernel(n_prompt_tiles, prompt_seg, xp_ref, xs_ref, mod_ref, win_ref, cw_ref, cb_ref,
                   cg_ref, cbeta_ref, conv_ref, rec_ref):
    i = pl.program_id(0)
    is_prompt = i < n_prompt_tiles
    x = jnp.where(is_prompt, xp_ref[...], xs_ref[...])
    mod = mod_ref[0]
    sh1 = mod[:, 0:D_MODEL]
    sc1 = mod[:, D_MODEL:2 * D_MODEL]
    h = (_layer_norm(x) * (1.0 + sc1) + sh1).astype(BF16)

    glu = _dot(h, win_ref[:, 0:2 * CONV_WIDTH])
    rec_ref[...] = _dot(h, win_ref[:, 2 * CONV_WIDTH:])
    u = glu[:, :CONV_WIDTH] * _sigmoid(glu[:, CONV_WIDTH:])

    seg_len = jnp.where(is_prompt, prompt_seg, GRID_W)
    pos = lax.broadcasted_iota(jnp.int32, u.shape, 0) & (seg_len - 1)
    half = CONV_TAPS // 2
    acc = u * cw_ref[half:half + 1, :] + cb_ref[...]
    for j in range(CONV_TAPS):
        d = j - half
        if d == 0:
            continue
        shifted = pltpu.roll(u, (-d) % TOK_TILE, axis=0)
        valid = (pos >= -d) if d < 0 else (pos < seg_len - d)
        acc = acc + jnp.where(valid, shifted, 0.0) * cw_ref[j:j + 1, :]
    y = _layer_norm(acc) * cg_ref[...] + cbeta_ref[...]
    conv_ref[...] = (y * _sigmoid(y)).astype(BF16)


def _inproj(xp, xs, mod3, w_in_bf, conv_w, conv_b, conv_g, conv_beta, prompt_seq, sample_seq):
    n_p, n_s = xp.shape[0], xs.shape[0]
    n_tok = n_p + n_s
    npt = n_p // TOK_TILE
    tiles_per_sample_seq = sample_seq // TOK_TILE
    assert prompt_seq == TOK_TILE and sample_seq % TOK_TILE == 0

    def mod_row(i):
        return jnp.where(i < npt, 0, 1 + (i - npt) // tiles_per_sample_seq)

    const = lambda i: (0, 0)
    return pl.pallas_call(
        functools.partial(_inproj_kernel, npt, prompt_seq),
        out_shape=(jax.ShapeDtypeStruct((n_tok, CONV_WIDTH), BF16),
                   jax.ShapeDtypeStruct((n_tok, REC_COLS), F32)),
        grid=(n_tok // TOK_TILE,),
        in_specs=[pl.BlockSpec((TOK_TILE, D_MODEL), lambda i: (jnp.minimum(i, npt - 1), 0)),
                  pl.BlockSpec((TOK_TILE, D_MODEL), lambda i: (jnp.maximum(i - npt, 0), 0)),
                  pl.BlockSpec((1, 1, 6 * D_MODEL), lambda i: (mod_row(i), 0, 0)),
                  pl.BlockSpec(w_in_bf.shape, const),
                  pl.BlockSpec(conv_w.shape, const),
                  pl.BlockSpec((1, CONV_WIDTH), const),
                  pl.BlockSpec((1, CONV_WIDTH), const),
                  pl.BlockSpec((1, CONV_WIDTH), const)],
        out_specs=(pl.BlockSpec((TOK_TILE, CONV_WIDTH), lambda i: (i, 0)),
                   pl.BlockSpec((TOK_TILE, REC_COLS), lambda i: (i, 0))),
        compiler_params=pltpu.CompilerParams(dimension_semantics=("parallel",),
                                             vmem_limit_bytes=VMEM_LIMIT),
        name="inproj_conv",
    )(xp, xs, mod3, w_in_bf, conv_w, conv_b, conv_g, conv_beta)


def _scan_constants():
    c = CHUNK
    w = np.zeros((N_LEVELS + 2, c, c), np.float32)
    m = np.zeros((N_LEVELS + 1, c, c), np.float32)
    t = np.arange(c)
    for j in range(N_LEVELS):
        half = 1 << j
        for ti in range(c):
            ref = (ti & ~(2 * half - 1)) + half - 1
            if ti & half:
                w[j, ti, ref + 1:ti + 1] = 1.0
            else:
                w[j, ti, ti + 1:ref + 1] = 1.0
        upper = (t[:, None] & half) != 0
        lower = (t[None, :] & half) == 0
        same = (t[:, None] >> (j + 1)) == (t[None, :] >> (j + 1))
        m[j] = (upper & lower & same).astype(np.float32)
    w[N_LEVELS] = (t[None, :] <= t[:, None]).astype(np.float32)
    w[N_LEVELS + 1] = (t[None, :] > t[:, None]).astype(np.float32)
    m[N_LEVELS] = np.eye(c, dtype=np.float32)
    w_f = w.reshape(-1, c)
    w_b = w[:, ::-1, ::-1].reshape(-1, c)
    m_b = m[:, ::-1, ::-1]
    return (jnp.asarray(w_f, BF16), jnp.asarray(np.ascontiguousarray(w_b), BF16),
            jnp.asarray(m), jnp.asarray(np.ascontiguousarray(m_b)))


def _scan_gates(z, lb, w_ref):
    e = jnp.exp(-jnp.abs(z))
    r = 1.0 / (1.0 + e)
    a = e * r
    sig = jnp.where(z >= 0, r, a)
    sig_neg = jnp.where(z >= 0, a, r)
    one_m_lb = 1.0 - lb
    log_f = jnp.log(lb + one_m_lb * sig)
    k = one_m_lb * sig_neg
    hi, mid, lo = _split3(log_f)
    w = w_ref[...]
    ex = jnp.exp(_dot(w, hi) + _dot(w, mid) + _dot(w, lo))
    return k, ex


def _scan_intra(q, k, ex, m_ref):
    c = CHUNK
    a_mat = m_ref[N_LEVELS] * _dot_nt(q.astype(BF16), k.astype(BF16))
    for j in range(N_LEVELS):
        xj = ex[j * c:(j + 1) * c]
        a_mat = a_mat + m_ref[j] * _dot_nt((q * xj).astype(BF16), (k * xj).astype(BF16))
    return a_mat


def _scan_kernel(fblk, bblk, first, seq, qf_ref, zf_ref, vf_ref, qb_ref, zb_ref, vb_ref, lb_ref,
                 wf_ref, wb_ref, mf_ref, mb_ref, s0f_ref, s0b_ref, of_ref, ob_ref, sf_ref, sb_ref):
    s = pl.program_id(0)

    @pl.when(first[s] == 1)
    def _():
        sf_ref[...] = s0f_ref[...]
        sb_ref[...] = s0b_ref[...]

    c = CHUNK
    dirs = ((qf_ref, zf_ref, vf_ref, 0, wf_ref, mf_ref, sf_ref, of_ref, c - 1),
            (qb_ref, zb_ref, vb_ref, 1, wb_ref, mb_ref, sb_ref, ob_ref, 0))
    gates = [_scan_gates(z_ref[...], lb_ref[d:d + 1, :], w_ref)
             for (_, z_ref, _, d, w_ref, _, _, _, _) in dirs]
    intra = []
    for (q_ref, _, _, _, _, m_ref, _, _, _), (k, ex) in zip(dirs, gates):
        for h in range(REC_HEADS):
            cols = slice(h * REC_DK, (h + 1) * REC_DK)
            intra.append(_scan_intra(q_ref[:, cols], k[:, cols], ex[:, cols], m_ref))
    idx = 0
    for (q_ref, _, v_ref, _, _, _, st_ref, o_ref, last), (k, ex) in zip(dirs, gates):
        for h in range(REC_HEADS):
            cols = slice(h * REC_DK, (h + 1) * REC_DK)
            q, v = q_ref[:, cols], v_ref[:, cols]
            ex_b = ex[N_LEVELS * c:(N_LEVELS + 1) * c, cols]
            ex_l = ex[(N_LEVELS + 1) * c:(N_LEVELS + 2) * c, cols]
            st = st_ref[0, h]
            o_ref[:, cols] = (_dot(intra[idx].astype(BF16), v.astype(BF16))
                              + _dot_nt((q * ex_b).astype(BF16), st.astype(BF16)))
            st_ref[0, h] = (st * ex_b[last:last + 1, :]
                            + _dot(v.T.astype(BF16), (k[:, cols] * ex_l).astype(BF16)))
            idx += 1


def _scan(rec, lb, s0f_t, s0b_t, seq_lens):
    n_tok = rec.shape[0]
    fblk, bblk, first, seq = [], [], [], []
    base = 0
    for si, ln in enumerate(seq_lens):
        n = ln // CHUNK
        for ci in range(n):
            fblk.append(base + ci)
            bblk.append(base + n - 1 - ci)
            first.append(1 if ci == 0 else 0)
            seq.append(si)
        base += n
    steps = len(fblk)
    as_i32 = lambda a: jnp.asarray(np.asarray(a, np.int32))
    w_f, w_b, m_f, m_b = _scan_constants()
    n_seq = len(seq_lens)

    def col(block_of, c):
        return pl.BlockSpec((CHUNK, REC_WIDTH), lambda s, fb, bb, fi, sq: (block_of(fb, bb)[s], c))

    fwd = lambda fb, bb: fb
    bwd = lambda fb, bb: bb
    const2 = lambda s, fb, bb, fi, sq: (0, 0)
    const3 = lambda s, fb, bb, fi, sq: (0, 0, 0)
    state_spec = pl.BlockSpec((1, REC_HEADS, REC_DK, REC_DK), lambda s, fb, bb, fi, sq: (sq[s], 0, 0, 0))
    grid_spec = pltpu.PrefetchScalarGridSpec(
        num_scalar_prefetch=4,
        grid=(steps,),
        in_specs=[col(fwd, 0), col(fwd, 1), col(fwd, 3), col(bwd, 0), col(bwd, 2), col(bwd, 3),
                  pl.BlockSpec((2, REC_WIDTH), const2),
                  pl.BlockSpec(w_f.shape, const2), pl.BlockSpec(w_b.shape, const2),
                  pl.BlockSpec(m_f.shape, const3), pl.BlockSpec(m_b.shape, const3),
                  state_spec, state_spec],
        out_specs=(pl.BlockSpec((CHUNK, REC_WIDTH), lambda s, fb, bb, fi, sq: (fb[s], 0)),
                   pl.BlockSpec((CHUNK, REC_WIDTH), lambda s, fb, bb, fi, sq: (bb[s], 0)),
                   state_spec, state_spec),
    )
    st_shape = jax.ShapeDtypeStruct((n_seq, REC_HEADS, REC_DK, REC_DK), F32)
    return pl.pallas_call(
        _scan_kernel,
        out_shape=(jax.ShapeDtypeStruct((n_tok, REC_WIDTH), F32),
                   jax.ShapeDtypeStruct((n_tok, REC_WIDTH), F32), st_shape, st_shape),
        grid_spec=grid_spec,
        compiler_params=pltpu.CompilerParams(dimension_semantics=("arbitrary",),
                                             vmem_limit_bytes=VMEM_LIMIT),
        name="hgrn2_scan",
    )(as_i32(fblk), as_i32(bblk), as_i32(first), as_i32(seq),
      rec, rec, rec, rec, rec, rec, lb, w_f, w_b, m_f, m_b, s0f_t, s0b_t)


def _post_kernel(n_prompt_tiles, alpha, of_ref, ob_ref, g_ref, conv_ref, xp_ref, xs_ref, mod_ref,
                 wout_ref, rg_ref, l1g_ref, l1b_ref, rwh_ref, rwl_ref, rb_ref, tri_ref,
                 x1_ref, h2_ref, topi_ref, rank_ref, w4_ref, cnt_ref, run_ref):
    i = pl.program_id(0)

    @pl.when(i == 0)
    def _():
        run_ref[...] = jnp.zeros_like(run_ref)

    x = jnp.where(i < n_prompt_tiles, xp_ref[...], xs_ref[...])
    mod = mod_ref[0]
    g1 = mod[:, 2 * D_MODEL:3 * D_MODEL]
    sh2 = mod[:, 3 * D_MODEL:4 * D_MODEL]
    sc2 = mod[:, 4 * D_MODEL:5 * D_MODEL]

    g = g_ref[...]
    silu_g = g * _sigmoid(g)
    mix = _dot(conv_ref[...], wout_ref[0:CONV_WIDTH, :])
    for h in range(REC_HEADS):
        cols = slice(h * REC_DK, (h + 1) * REC_DK)
        o = of_ref[:, cols] + ob_ref[:, cols]
        o = o * lax.rsqrt(jnp.mean(o * o, axis=-1, keepdims=True) + RMS_EPS)
        rec_out = (o * rg_ref[:, cols] * silu_g[:, cols]).astype(BF16)
        mix = mix + _dot(rec_out, wout_ref[CONV_WIDTH + h * REC_DK:CONV_WIDTH + (h + 1) * REC_DK, :])

    x1 = _layer_norm(alpha * x + g1 * mix) * l1g_ref[...] + l1b_ref[...]
    x1_ref[...] = x1
    h2 = _layer_norm(x1) * (1.0 + sc2) + sh2
    h2_ref[...] = h2

    h_hi = h2.astype(BF16)
    h_lo = (h2 - h_hi.astype(F32)).astype(BF16)
    logits = (_dot(h_hi, rwh_ref[...]) + _dot(h_lo, rwh_ref[...]) + _dot(h_hi, rwl_ref[...])
              + rb_ref[...])
    lane = lax.broadcasted_iota(jnp.int32, logits.shape, 1)
    lane_f = lane.astype(F32)
    neg_inf = jnp.float32(-jnp.inf)
    work = jnp.where(lane < N_EXPERTS, logits, neg_inf)
    vals, sels, idxs = [], [], []
    for _ in range(TOP_K):
        m = jnp.max(work, axis=-1, keepdims=True)
        idx = jnp.min(jnp.where(work == m, lane_f, float(LANES)), axis=-1, keepdims=True)
        sel = lane_f == idx
        vals.append(m)
        idxs.append(idx)
        sels.append(sel)
        work = jnp.where(sel, neg_inf, work)
    exps = [jnp.exp(v - vals[0]) for v in vals]
    denom = exps[0] + exps[1] + exps[2] + exps[3]

    sel_any = jnp.zeros(logits.shape, F32)
    for sel in sels:
        sel_any = jnp.where(sel, 1.0, sel_any)
    prior = _dot(tri_ref[...], sel_any.astype(BF16)) + run_ref[...]
    run_new = run_ref[...] + jnp.sum(sel_any, axis=0, keepdims=True)
    run_ref[...] = run_new
    cnt_ref[...] = run_new

    topi = jnp.zeros(logits.shape, F32)
    rank = jnp.zeros(logits.shape, F32)
    w4 = jnp.zeros(logits.shape, F32)
    for j in range(TOP_K):
        rj = jnp.sum(jnp.where(sels[j], prior, 0.0), axis=-1, keepdims=True)
        topi = jnp.where(lane == j, idxs[j], topi)
        rank = jnp.where(lane == j, rj, rank)
        w4 = jnp.where(lane == j, exps[j] / denom, w4)
    topi_ref[...] = topi.astype(jnp.int32)
    rank_ref[...] = rank.astype(jnp.int32)
    w4_ref[...] = w4


def _post(o_f, o_b, rec, conv, xp, xs, mod3, w_out_bf, rec_g, ln1_g, ln1_b, rw_hi, rw_lo, rb,
          alpha, sample_seq):
    n_p, n_s = xp.shape[0], xs.shape[0]
    n_tok = n_p + n_s
    npt = n_p // TOK_TILE
    tiles_per_sample_seq = sample_seq // TOK_TILE
    tri = jnp.asarray(np.tril(np.ones((TOK_TILE, TOK_TILE), np.float32), -1), BF16)

    def mod_row(i):
        return jnp.where(i < npt, 0, 1 + (i - npt) // tiles_per_sample_seq)

    const = lambda i: (0, 0)
    tok = lambda w: pl.BlockSpec((TOK_TILE, w), lambda i: (i, 0))
    return pl.pallas_call(
        functools.partial(_post_kernel, npt, alpha),
        out_shape=(jax.ShapeDtypeStruct((n_tok, D_MODEL), F32),
                   jax.ShapeDtypeStruct((n_tok, D_MODEL), F32),
                   jax.ShapeDtypeStruct((n_tok, LANES), jnp.int32),
                   jax.ShapeDtypeStruct((n_tok, LANES), jnp.int32),
                   jax.ShapeDtypeStruct((n_tok, LANES), F32),
                   jax.ShapeDtypeStruct((1, LANES), F32)),
        grid=(n_tok // TOK_TILE,),
        in_specs=[tok(REC_WIDTH), tok(REC_WIDTH),
                  pl.BlockSpec((TOK_TILE, REC_WIDTH), lambda i: (i, 4)),
                  tok(CONV_WIDTH),
                  pl.BlockSpec((TOK_TILE, D_MODEL), lambda i: (jnp.minimum(i, npt - 1), 0)),
                  pl.BlockSpec((TOK_TILE, D_MODEL), lambda i: (jnp.maximum(i - npt, 0), 0)),
                  pl.BlockSpec((1, 1, 6 * D_MODEL), lambda i: (mod_row(i), 0, 0)),
                  pl.BlockSpec(w_out_bf.shape, const),
                  pl.BlockSpec((1, REC_WIDTH), const),
                  pl.BlockSpec((1, D_MODEL), const), pl.BlockSpec((1, D_MODEL), const),
                  pl.BlockSpec((D_MODEL, LANES), const), pl.BlockSpec((D_MODEL, LANES), const),
                  pl.BlockSpec((1, LANES), const),
                  pl.BlockSpec((TOK_TILE, TOK_TILE), const)],
        out_specs=(tok(D_MODEL), tok(D_MODEL), tok(LANES), tok(LANES), tok(LANES),
                   pl.BlockSpec((1, LANES), const)),
        scratch_shapes=[pltpu.VMEM((1, LANES), F32)],
        compiler_params=pltpu.CompilerParams(dimension_semantics=("arbitrary",),
                                             vmem_limit_bytes=VMEM_LIMIT),
        name="post_mixer_router",
    )(o_f, o_b, rec, conv, xp, xs, mod3, w_out_bf, rec_g, ln1_g, ln1_b, rw_hi, rw_lo, rb, tri)


def _dispatch_kernel(gend, gpad, dest_ref, h_ref, xs_ref, zero_ref, sem):
    i = pl.program_id(0)

    @pl.when(i == 0)
    def _():
        zero_ref[...] = jnp.zeros_like(zero_ref)
        n_tiles = xs_ref.shape[0] // EXP_TILE

        def zero_tile(start):
            start = pl.multiple_of(start, EXP_TILE)
            return pltpu.make_async_copy(zero_ref, xs_ref.at[pl.ds(start, EXP_TILE)], sem)

        def group_tail(action):
            def body(e, carry):
                @pl.when(gpad[e] > 0)
                def _():
                    action(zero_tile(gend[e] - EXP_TILE))
                return carry
            lax.fori_loop(0, N_EXPERTS, body, 0)

        def unused_tiles(action):
            def body(tile, carry):
                action(zero_tile(tile * EXP_TILE))
                return carry
            lax.fori_loop(gend[N_EXPERTS - 1] // EXP_TILE, n_tiles, body, 0)

        group_tail(lambda cp: cp.start())
        unused_tiles(lambda cp: cp.start())
        group_tail(lambda cp: cp.wait())
        unused_tiles(lambda cp: cp.wait())

    def issue(t, carry):
        for j in range(TOP_K):
            d = dest_ref[0, 0, t * TOP_K + j]
            pltpu.make_async_copy(h_ref.at[pl.ds(t, 1)], xs_ref.at[pl.ds(d, 1)], sem).start()
        return carry

    lax.fori_loop(0, TOK_TILE, issue, 0)
    for _ in range(TOP_K):
        pltpu.make_async_copy(h_ref, xs_ref.at[pl.ds(0, TOK_TILE)], sem).wait()


def _dispatch(h2, dest3, gend, gpad, n_rows):
    n_tok = h2.shape[0]
    grid_spec = pltpu.PrefetchScalarGridSpec(
        num_scalar_prefetch=2,
        grid=(n_tok // TOK_TILE,),
        in_specs=[pl.BlockSpec((1, 1, TOK_TILE * TOP_K), lambda i, ge, gp: (i, 0, 0),
                               memory_space=pltpu.SMEM),
                  pl.BlockSpec((TOK_TILE, D_MODEL), lambda i, ge, gp: (i, 0))],
        out_specs=pl.BlockSpec(memory_space=pl.ANY),
        scratch_shapes=[pltpu.VMEM((EXP_TILE, D_MODEL), F32), pltpu.SemaphoreType.DMA(())],
    )
    return pl.pallas_call(
        _dispatch_kernel,
        out_shape=jax.ShapeDtypeStruct((n_rows, D_MODEL), F32),
        grid_spec=grid_spec,
        compiler_params=pltpu.CompilerParams(dimension_semantics=("arbitrary",),
                                             vmem_limit_bytes=VMEM_LIMIT,
                                             has_side_effects=True),
        name="moe_dispatch",
    )(gend, gpad, dest3, h2)


def _expert_kernel(tile_e, tile_blk, n_act, xs_ref, wgu_ref, bgu_ref, wd_ref, bd_ref, ys_ref,
                   wgu_bf, wd_bf):
    i = pl.program_id(0)
    prev = tile_e[jnp.maximum(i - 1, 0)]
    new_expert = jnp.logical_or(i == 0, tile_e[i] != prev)

    @pl.when(jnp.logical_and(new_expert, i < n_act[0]))
    def _():
        rows = 128

        def cast(r, carry):
            sl = pl.ds(pl.multiple_of(r * rows, rows), rows)
            wgu_bf[sl, :] = wgu_ref[0, sl, :].astype(BF16)
            wd_bf[sl, :] = wd_ref[0, sl, :].astype(BF16)
            return carry

        lax.fori_loop(0, D_MODEL // rows, cast, 0)

    @pl.when(i < n_act[0])
    def _():
        x = xs_ref[...].astype(BF16)
        gu = _dot(x, wgu_bf[...]) + bgu_ref[0]
        gate = jnp.minimum(gu[:, :D_FF], SWIGLU_LIMIT)
        up = jnp.clip(gu[:, D_FF:], -SWIGLU_LIMIT, SWIGLU_LIMIT)
        act = (up + 1.0) * gate * _sigmoid(SWIGLU_ALPHA * gate)
        ys_ref[...] = _dot(act.astype(BF16), wd_bf[...]) + bd_ref[0]

    @pl.when(i >= n_act[0])
    def _():
        ys_ref[...] = jnp.zeros_like(ys_ref)


def _experts(xs, tile_e, tile_blk, n_act, w_gate_up, b_gate_up, w_down, b_down):
    n_rows = xs.shape[0]
    n_tiles = n_rows // EXP_TILE
    grid_spec = pltpu.PrefetchScalarGridSpec(
        num_scalar_prefetch=3,
        grid=(n_tiles,),
        in_specs=[pl.BlockSpec((EXP_TILE, D_MODEL), lambda i, te, tb, na: (tb[i], 0)),
                  pl.BlockSpec((1, D_MODEL, 2 * D_FF), lambda i, te, tb, na: (te[i], 0, 0)),
                  pl.BlockSpec((1, 1, 2 * D_FF), lambda i, te, tb, na: (te[i], 0, 0)),
                  pl.BlockSpec((1, D_FF, D_MODEL), lambda i, te, tb, na: (te[i], 0, 0)),
                  pl.BlockSpec((1, 1, D_MODEL), lambda i, te, tb, na: (te[i], 0, 0))],
        out_specs=pl.BlockSpec((EXP_TILE, D_MODEL), lambda i, te, tb, na: (i, 0)),
        scratch_shapes=[pltpu.VMEM((D_MODEL, 2 * D_FF), BF16), pltpu.VMEM((D_FF, D_MODEL), BF16)],
    )
    return pl.pallas_call(
        _expert_kernel,
        out_shape=jax.ShapeDtypeStruct((n_rows, D_MODEL), F32),
        grid_spec=grid_spec,
        compiler_params=pltpu.CompilerParams(dimension_semantics=("arbitrary",),
                                             vmem_limit_bytes=VMEM_LIMIT),
        name="moe_experts",
    )(tile_e, tile_blk, n_act, xs, w_gate_up, b_gate_up.reshape(N_EXPERTS, 1, 2 * D_FF),
      w_down, b_down.reshape(N_EXPERTS, 1, D_MODEL))


def _combine_kernel(n_prompt_tiles, alpha, dest_ref, ys_ref, x1_ref, w4_ref, mod_ref, l2g_ref,
                    l2b_ref, yp_ref, ysm_ref, buf, sem):
    i = pl.program_id(0)

    def issue(t, carry):
        for j in range(TOP_K):
            d = dest_ref[0, 0, t * TOP_K + j]
            pltpu.make_async_copy(ys_ref.at[pl.ds(d, 1)], buf.at[j, pl.ds(t, 1)], sem).start()
        return carry

    lax.fori_loop(0, TOK_TILE, issue, 0)
    for j in range(TOP_K):
        pltpu.make_async_copy(ys_ref.at[pl.ds(0, TOK_TILE)], buf.at[j], sem).wait()

    w4 = w4_ref[...]
    ff = buf[0] * w4[:, 0:1]
    for j in range(1, TOP_K):
        ff = ff + buf[j] * w4[:, j:j + 1]
    g2 = mod_ref[0][:, 5 * D_MODEL:6 * D_MODEL]
    out = _layer_norm(alpha * x1_ref[...] + g2 * ff) * l2g_ref[...] + l2b_ref[...]

    @pl.when(i < n_prompt_tiles)
    def _():
        yp_ref[...] = out

    @pl.when(i >= n_prompt_tiles)
    def _():
        ysm_ref[...] = out


def _combine(ys, dest3, x1, w4, mod3, ln2_g, ln2_b, n_p, alpha, sample_seq):
    n_tok = x1.shape[0]
    n_s = n_tok - n_p
    npt = n_p // TOK_TILE
    tiles_per_sample_seq = sample_seq // TOK_TILE

    def mod_row(i):
        return jnp.where(i < npt, 0, 1 + (i - npt) // tiles_per_sample_seq)

    const = lambda i: (0, 0)
    return pl.pallas_call(
        functools.partial(_combine_kernel, npt, alpha),
        out_shape=(jax.ShapeDtypeStruct((n_p, D_MODEL), F32),
                   jax.ShapeDtypeStruct((n_s, D_MODEL), F32)),
        grid=(n_tok // TOK_TILE,),
        in_specs=[pl.BlockSpec((1, 1, TOK_TILE * TOP_K), lambda i: (i, 0, 0),
                               memory_space=pltpu.SMEM),
                  pl.BlockSpec(memory_space=pl.ANY),
                  pl.BlockSpec((TOK_TILE, D_MODEL), lambda i: (i, 0)),
                  pl.BlockSpec((TOK_TILE, LANES), lambda i: (i, 0)),
                  pl.BlockSpec((1, 1, 6 * D_MODEL), lambda i: (mod_row(i), 0, 0)),
                  pl.BlockSpec((1, D_MODEL), const), pl.BlockSpec((1, D_MODEL), const)],
        out_specs=(pl.BlockSpec((TOK_TILE, D_MODEL), lambda i: (jnp.minimum(i, npt - 1), 0)),
                   pl.BlockSpec((TOK_TILE, D_MODEL), lambda i: (jnp.maximum(i - npt, 0), 0))),
        scratch_shapes=[pltpu.VMEM((TOP_K, TOK_TILE, D_MODEL), F32), pltpu.SemaphoreType.DMA(())],
        compiler_params=pltpu.CompilerParams(dimension_semantics=("arbitrary",),
                                             vmem_limit_bytes=VMEM_LIMIT),
        name="moe_combine",
    )(dest3, ys, x1, w4, mod3, ln2_g, ln2_b)


def kernel(x_prompt, x_sample, c, state_fwd, state_bwd, c_ctx, w_ada, b_ada, w_in, conv_w, conv_b,
           conv_ln_g, conv_ln_b, lb_logits, rec_norm_g, w_out, ln1_g, ln1_b, ln2_g, ln2_b,
           router_w, router_b, w_gate_up, b_gate_up, w_down, b_down):
    depth = w_ada.shape[0]
    assert depth == 1
    alpha = (2.0 * depth) ** 0.25
    nb_p, seq_p, _ = x_prompt.shape
    nb_s, seq_s, _ = x_sample.shape
    n_p, n_s = nb_p * seq_p, nb_s * seq_s
    n_tok = n_p + n_s
    row2 = lambda a: a.reshape(1, -1)

    lb = jnp.cumsum(jax.nn.softmax(lb_logits.astype(F32), axis=0), axis=0)[0]

    cond = jnp.concatenate([c_ctx[None, :], c, jnp.zeros((16 - 1 - nb_s, D_MODEL), F32)], axis=0)
    mod = _ada(cond, w_ada[0], b_ada[0])
    mod3 = mod.reshape(16, 1, 6 * D_MODEL)

    xp = x_prompt.reshape(n_p, D_MODEL)
    xs = x_sample.reshape(n_s, D_MODEL)
    conv_w_pad = jnp.concatenate([conv_w[0], jnp.zeros((1, CONV_WIDTH), F32)], axis=0)
    conv_out, rec = _inproj(xp, xs, mod3, w_in[0].astype(BF16), conv_w_pad, row2(conv_b[0]),
                            row2(conv_ln_g[0]), row2(conv_ln_b[0]), seq_p, seq_s)

    zeros_state = jnp.zeros((nb_p, REC_HEADS, REC_DK, REC_DK), F32)
    s0f = jnp.concatenate([zeros_state, jnp.swapaxes(state_fwd[:, 0], -1, -2)], axis=0)
    s0b = jnp.concatenate([zeros_state, jnp.swapaxes(state_bwd[:, 0], -1, -2)], axis=0)
    o_f, o_b, sf_t, sb_t = _scan(rec, lb, s0f, s0b, [seq_p] * nb_p + [seq_s] * nb_s)
    new_f = jnp.swapaxes(sf_t[:nb_p], -1, -2)[:, None]
    new_b = jnp.swapaxes(sb_t[:nb_p], -1, -2)[:, None]

    rw = jnp.pad(router_w[0], ((0, 0), (0, LANES - N_EXPERTS)))
    rw_hi = rw.astype(BF16)
    rw_lo = (rw - rw_hi.astype(F32)).astype(BF16)
    rb = jnp.pad(router_b[0], (0, LANES - N_EXPERTS)).reshape(1, LANES)
    x1, h2, topi, rank, w4, counts = _post(
        o_f, o_b, rec, conv_out, xp, xs, mod3, w_out[0].astype(BF16), row2(rec_norm_g[0]),
        row2(ln1_g[0]), row2(ln1_b[0]), rw_hi, rw_lo, rb, alpha, seq_s)

    cnt = counts[0, :N_EXPERTS].astype(jnp.int32)
    gpad = ((cnt + EXP_TILE - 1) // EXP_TILE) * EXP_TILE
    gend = jnp.cumsum(gpad)
    gstart = gend - gpad
    n_rows = n_tok * TOP_K + N_EXPERTS * EXP_TILE
    n_tiles = n_rows // EXP_TILE
    n_act = gend[-1] // EXP_TILE
    tile_ids = jnp.minimum(jnp.arange(n_tiles, dtype=jnp.int32), n_act - 1)
    tile_e = jnp.sum((gend[None, :] <= tile_ids[:, None] * EXP_TILE).astype(jnp.int32), axis=1)
    dest = gstart[topi[:, :TOP_K]] + rank[:, :TOP_K]
    dest3 = dest.reshape(n_tok // TOK_TILE, 1, TOK_TILE * TOP_K).astype(jnp.int32)

    xs_sorted = _dispatch(h2, dest3, gend.astype(jnp.int32), gpad.astype(jnp.int32), n_rows)
    ys_sorted = _experts(xs_sorted, tile_e, tile_ids, n_act.reshape(1).astype(jnp.int32),
                         w_gate_up[0], b_gate_up[0], w_down[0], b_down[0])
    y_p, y_s = _combine(ys_sorted, dest3, x1, w4, mod3, row2(ln2_g[0]), row2(ln2_b[0]), n_p,
                        alpha, seq_s)
    return (y_p.reshape(nb_p, seq_p, D_MODEL), y_s.reshape(nb_s, seq_s, D_MODEL), new_f, new_b)
```

```python
import functools

import numpy as np
import jax
import jax.numpy as jnp
from jax import lax
from jax.experimental import pallas as pl
from jax.experimental.pallas import tpu as pltpu

F32 = jnp.float32
BF16 = jnp.bfloat16

D_MODEL = 1024
CONV_WIDTH = 512
CONV_TAPS = 31
REC_HEADS = 4
REC_DK = 128
REC_WIDTH = REC_HEADS * REC_DK
REC_COLS = 5 * REC_WIDTH
CHUNK = 64
N_LEVELS = 6
N_EXPERTS = 32
TOP_K = 4
D_FF = 1024
SWIGLU_LIMIT = 7.0
SWIGLU_ALPHA = 1.702
LN_EPS = 1e-5
RMS_EPS = 1e-6
GRID_W = 64

LANES = 128
TOK_TILE = 256
EXP_TILE = 512
VMEM_LIMIT = 56 * 1024 * 1024


def _sigmoid(x):
    return 1.0 / (1.0 + jnp.exp(-x))


def _layer_norm(x):
    mu = jnp.mean(x, axis=-1, keepdims=True)
    xc = x - mu
    var = jnp.mean(xc * xc, axis=-1, keepdims=True)
    return xc * lax.rsqrt(var + LN_EPS)


def _dot(a, b):
    return jnp.dot(a, b, preferred_element_type=F32)


def _dot_nt(a, b):
    return lax.dot_general(a, b, (((1,), (1,)), ((), ())), preferred_element_type=F32)


SUBLANES = 8
assert D_MODEL == SUBLANES * LANES


def _store_row_tiles(ref, value):
    for s in range(SUBLANES):
        ref[pl.ds(s, value.shape[0], stride=SUBLANES), :] = value[:, s * LANES:(s + 1) * LANES]


def _load_row_tiles(ref, n_rows):
    return jnp.concatenate(
        [ref[pl.ds(s, n_rows, stride=SUBLANES), :] for s in range(SUBLANES)], axis=-1)


def _split3(x):
    hi = x.astype(BF16)
    r1 = x - hi.astype(F32)
    mid = r1.astype(BF16)
    lo = (r1 - mid.astype(F32)).astype(BF16)
    return hi, mid, lo


def _ada_kernel(c_ref, w_ref, b_ref, o_ref):
    c = c_ref[...]
    s = (c * _sigmoid(c)).astype(BF16)
    o_ref[...] = _dot(s, w_ref[...].astype(BF16)) + b_ref[...]


def _ada(cond, w_ada, b_ada):
    rows = cond.shape[0]
    ncol = w_ada.shape[1]
    tn = 1024
    return pl.pallas_call(
        _ada_kernel,
        out_shape=jax.ShapeDtypeStruct((rows, ncol), F32),
        grid=(ncol // tn,),
        in_specs=[pl.BlockSpec((rows, D_MODEL), lambda j: (0, 0)),
                  pl.BlockSpec((D_MODEL, tn), lambda j: (0, j)),
                  pl.BlockSpec((1, tn), lambda j: (0, j))],
        out_specs=pl.BlockSpec((rows, tn), lambda j: (0, j)),
        compiler_params=pltpu.CompilerParams(dimension_semantics=("parallel",),
                                             vmem_limit_bytes=VMEM_LIMIT),
        name="ada_mod",
    )(cond, w_ada, b_ada.reshape(1, ncol))


def _inproj_kernel(n_prompt_tiles, prompt_seg, xp_ref, xs_ref, mod_ref, win_ref, cw_ref, cb_ref,
                   cg_ref, cbeta_ref, conv_ref, rec_ref):
    i = pl.program_id(0)
    is_prompt = i < n_prompt_tiles
    x = jnp.where(is_prompt, xp_ref[...], xs_ref[...])
    mod = mod_ref[0]
    sh1 = mod[:, 0:D_MODEL]
    sc1 = mod[:, D_MODEL:2 * D_MODEL]
    h = (_layer_norm(x) * (1.0 + sc1) + sh1).astype(BF16)

    glu = _dot(h, win_ref[:, 0:2 * CONV_WIDTH])
    rec_ref[...] = _dot(h, win_ref[:, 2 * CONV_WIDTH:])
    u = glu[:, :CONV_WIDTH] * _sigmoid(glu[:, CONV_WIDTH:])

    seg_len = jnp.where(is_prompt, prompt_seg, GRID_W)
    pos = lax.broadcasted_iota(jnp.int32, u.shape, 0) & (seg_len - 1)
    half = CONV_TAPS // 2
    acc = u * cw_ref[half:half + 1, :] + cb_ref[...]
    for j in range(CONV_TAPS):
        d = j - half
        if d == 0:
            continue
        shifted = pltpu.roll(u, (-d) % TOK_TILE, axis=0)
        valid = (pos >= -d) if d < 0 else (pos < seg_len - d)
        acc = acc + jnp.where(valid, shifted, 0.0) * cw_ref[j:j + 1, :]
    y = _layer_norm(acc) * cg_ref[...] + cbeta_ref[...]
    conv_ref[...] = (y * _sigmoid(y)).astype(BF16)


def _inproj(xp, xs, mod3, w_in_bf, conv_w, conv_b, conv_g, conv_beta, prompt_seq, sample_seq):
    n_p, n_s = xp.shape[0], xs.shape[0]
    n_tok = n_p + n_s
    npt = n_p // TOK_TILE
    tiles_per_sample_seq = sample_seq // TOK_TILE
    assert prompt_seq == TOK_TILE and sample_seq % TOK_TILE == 0

    def mod_row(i):
        return jnp.where(i < npt, 0, 1 + (i - npt) // tiles_per_sample_seq)

    const = lambda i: (0, 0)
    return pl.pallas_call(
        functools.partial(_inproj_kernel, npt, prompt_seq),
        out_shape=(jax.ShapeDtypeStruct((n_tok, CONV_WIDTH), BF16),
                   jax.ShapeDtypeStruct((n_tok, REC_COLS), F32)),
        grid=(n_tok // TOK_TILE,),
        in_specs=[pl.BlockSpec((TOK_TILE, D_MODEL), lambda i: (jnp.minimum(i, npt - 1), 0)),
                  pl.BlockSpec((TOK_TILE, D_MODEL), lambda i: (jnp.maximum(i - npt, 0), 0)),
                  pl.BlockSpec((1, 1, 6 * D_MODEL), lambda i: (mod_row(i), 0, 0)),
                  pl.BlockSpec(w_in_bf.shape, const),
                  pl.BlockSpec(conv_w.shape, const),
                  pl.BlockSpec((1, CONV_WIDTH), const),
                  pl.BlockSpec((1, CONV_WIDTH), const),
                  pl.BlockSpec((1, CONV_WIDTH), const)],
        out_specs=(pl.BlockSpec((TOK_TILE, CONV_WIDTH), lambda i: (i, 0)),
                   pl.BlockSpec((TOK_TILE, REC_COLS), lambda i: (i, 0))),
        compiler_params=pltpu.CompilerParams(dimension_semantics=("parallel",),
                                             vmem_limit_bytes=VMEM_LIMIT),
        name="inproj_conv",
    )(xp, xs, mod3, w_in_bf, conv_w, conv_b, conv_g, conv_beta)


def _scan_constants():
    c = CHUNK
    w = np.zeros((N_LEVELS + 2, c, c), np.float32)
    m = np.zeros((N_LEVELS + 1, c, c), np.float32)
    t = np.arange(c)
    for j in range(N_LEVELS):
        half = 1 << j
        for ti in range(c):
            ref = (ti & ~(2 * half - 1)) + half - 1
            if ti & half:
                w[j, ti, ref + 1:ti + 1] = 1.0
            else:
                w[j, ti, ti + 1:ref + 1] = 1.0
        upper = (t[:, None] & half) != 0
        lower = (t[None, :] & half) == 0
        same = (t[:, None] >> (j + 1)) == (t[None, :] >> (j + 1))
        m[j] = (upper & lower & same).astype(np.float32)
    w[N_LEVELS] = (t[None, :] <= t[:, None]).astype(np.float32)
    w[N_LEVELS + 1] = (t[None, :] > t[:, None]).astype(np.float32)
    m[N_LEVELS] = np.eye(c, dtype=np.float32)
    w_f = w.reshape(-1, c)
    w_b = w[:, ::-1, ::-1].reshape(-1, c)
    m_b = m[:, ::-1, ::-1]
    return (jnp.asarray(w_f, BF16), jnp.asarray(np.ascontiguousarray(w_b), BF16),
            jnp.asarray(m), jnp.asarray(np.ascontiguousarray(m_b)))


def _scan_gates(z, lb, w_ref):
    e = jnp.exp(-jnp.abs(z))
    r = 1.0 / (1.0 + e)
    a = e * r
    sig = jnp.where(z >= 0, r, a)
    sig_neg = jnp.where(z >= 0, a, r)
    one_m_lb = 1.0 - lb
    log_f = jnp.log(lb + one_m_lb * sig)
    k = one_m_lb * sig_neg
    hi, mid, lo = _split3(log_f)
    w = w_ref[...]
    ex = jnp.exp(_dot(w, hi) + _dot(w, mid) + _dot(w, lo))
    return k, ex


def _scan_intra(q, k, ex, m_ref):
    c = CHUNK
    a_mat = m_ref[N_LEVELS] * _dot_nt(q.astype(BF16), k.astype(BF16))
    for j in range(N_LEVELS):
        xj = ex[j * c:(j + 1) * c]
        a_mat = a_mat + m_ref[j] * _dot_nt((q * xj).astype(BF16), (k * xj).astype(BF16))
    return a_mat


def _scan_kernel(fblk, bblk, first, seq, qf_ref, zf_ref, vf_ref, qb_ref, zb_ref, vb_ref, lb_ref,
                 wf_ref, wb_ref, mf_ref, mb_ref, s0f_ref, s0b_ref, of_ref, ob_ref, sf_ref, sb_ref):
    s = pl.program_id(0)

    @pl.when(first[s] == 1)
    def _():
        sf_ref[...] = s0f_ref[...]
        sb_ref[...] = s0b_ref[...]

    c = CHUNK
    dirs = ((qf_ref, zf_ref, vf_ref, 0, wf_ref, mf_ref, sf_ref, of_ref, c - 1),
            (qb_ref, zb_ref, vb_ref, 1, wb_ref, mb_ref, sb_ref, ob_ref, 0))
    gates = [_scan_gates(z_ref[...], lb_ref[d:d + 1, :], w_ref)
             for (_, z_ref, _, d, w_ref, _, _, _, _) in dirs]
    intra = []
    for (q_ref, _, _, _, _, m_ref, _, _, _), (k, ex) in zip(dirs, gates):
        for h in range(REC_HEADS):
            cols = slice(h * REC_DK, (h + 1) * REC_DK)
            intra.append(_scan_intra(q_ref[:, cols], k[:, cols], ex[:, cols], m_ref))
    idx = 0
    for (q_ref, _, v_ref, _, _, _, st_ref, o_ref, last), (k, ex) in zip(dirs, gates):
        for h in range(REC_HEADS):
            cols = slice(h * REC_DK, (h + 1) * REC_DK)
            q, v = q_ref[:, cols], v_ref[:, cols]
            ex_b = ex[N_LEVELS * c:(N_LEVELS + 1) * c, cols]
            ex_l = ex[(N_LEVELS + 1) * c:(N_LEVELS + 2) * c, cols]
            st = st_ref[0, h]
            o_ref[:, cols] = (_dot(intra[idx].astype(BF16), v.astype(BF16))
                              + _dot_nt((q * ex_b).astype(BF16), st.astype(BF16)))
            st_ref[0, h] = (st * ex_b[last:last + 1, :]
                            + _dot(v.T.astype(BF16), (k[:, cols] * ex_l).astype(BF16)))
            idx += 1


def _scan(rec, lb, s0f_t, s0b_t, seq_lens):
    n_tok = rec.shape[0]
    fblk, bblk, first, seq = [], [], [], []
    base = 0
    for si, ln in enumerate(seq_lens):
        n = ln // CHUNK
        for ci in range(n):
            fblk.append(base + ci)
            bblk.append(base + n - 1 - ci)
            first.append(1 if ci == 0 else 0)
            seq.append(si)
        base += n
    steps = len(fblk)
    as_i32 = lambda a: jnp.asarray(np.asarray(a, np.int32))
    w_f, w_b, m_f, m_b = _scan_constants()
    n_seq = len(seq_lens)

    def col(block_of, c):
        return pl.BlockSpec((CHUNK, REC_WIDTH), lambda s, fb, bb, fi, sq: (block_of(fb, bb)[s], c))

    fwd = lambda fb, bb: fb
    bwd = lambda fb, bb: bb
    const2 = lambda s, fb, bb, fi, sq: (0, 0)
    const3 = lambda s, fb, bb, fi, sq: (0, 0, 0)
    state_spec = pl.BlockSpec((1, REC_HEADS, REC_DK, REC_DK), lambda s, fb, bb, fi, sq: (sq[s], 0, 0, 0))
    grid_spec = pltpu.PrefetchScalarGridSpec(
        num_scalar_prefetch=4,
        grid=(steps,),
        in_specs=[col(fwd, 0), col(fwd, 1), col(fwd, 3), col(bwd, 0), col(bwd, 2), col(bwd, 3),
                  pl.BlockSpec((2, REC_WIDTH), const2),
                  pl.BlockSpec(w_f.shape, const2), pl.BlockSpec(w_b.shape, const2),
                  pl.BlockSpec(m_f.shape, const3), pl.BlockSpec(m_b.shape, const3),
                  state_spec, state_spec],
        out_specs=(pl.BlockSpec((CHUNK, REC_WIDTH), lambda s, fb, bb, fi, sq: (fb[s], 0)),
                   pl.BlockSpec((CHUNK, REC_WIDTH), lambda s, fb, bb, fi, sq: (bb[s], 0)),
                   state_spec, state_spec),
    )
    st_shape = jax.ShapeDtypeStruct((n_seq, REC_HEADS, REC_DK, REC_DK), F32)
    return pl.pallas_call(
        _scan_kernel,
        out_shape=(jax.ShapeDtypeStruct((n_tok, REC_WIDTH), F32),
                   jax.ShapeDtypeStruct((n_tok, REC_WIDTH), F32), st_shape, st_shape),
        grid_spec=grid_spec,
        compiler_params=pltpu.CompilerParams(dimension_semantics=("arbitrary",),
                                             vmem_limit_bytes=VMEM_LIMIT),
        name="hgrn2_scan",
    )(as_i32(fblk), as_i32(bblk), as_i32(first), as_i32(seq),
      rec, rec, rec, rec, rec, rec, lb, w_f, w_b, m_f, m_b, s0f_t, s0b_t)


def _post_kernel(n_prompt_tiles, alpha, of_ref, ob_ref, g_ref, conv_ref, xp_ref, xs_ref, mod_ref,
                 wout_ref, rg_ref, l1g_ref, l1b_ref, rwh_ref, rwl_ref, rb_ref, tri_ref,
                 x1_ref, h2_ref, topi_ref, rank_ref, w4_ref, cnt_ref, run_ref):
    i = pl.program_id(0)

    @pl.when(i == 0)
    def _():
        run_ref[...] = jnp.zeros_like(run_ref)

    x = jnp.where(i < n_prompt_tiles, xp_ref[...], xs_ref[...])
    mod = mod_ref[0]
    g1 = mod[:, 2 * D_MODEL:3 * D_MODEL]
    sh2 = mod[:, 3 * D_MODEL:4 * D_MODEL]
    sc2 = mod[:, 4 * D_MODEL:5 * D_MODEL]

    g = g_ref[...]
    silu_g = g * _sigmoid(g)
    mix = _dot(conv_ref[...], wout_ref[0:CONV_WIDTH, :])
    for h in range(REC_HEADS):
        cols = slice(h * REC_DK, (h + 1) * REC_DK)
        o = of_ref[:, cols] + ob_ref[:, cols]
        o = o * lax.rsqrt(jnp.mean(o * o, axis=-1, keepdims=True) + RMS_EPS)
        rec_out = (o * rg_ref[:, cols] * silu_g[:, cols]).astype(BF16)
        mix = mix + _dot(rec_out, wout_ref[CONV_WIDTH + h * REC_DK:CONV_WIDTH + (h + 1) * REC_DK, :])

    x1 = _layer_norm(alpha * x + g1 * mix) * l1g_ref[...] + l1b_ref[...]
    x1_ref[...] = x1
    h2 = _layer_norm(x1) * (1.0 + sc2) + sh2
    _store_row_tiles(h2_ref, h2)

    h_hi = h2.astype(BF16)
    h_lo = (h2 - h_hi.astype(F32)).astype(BF16)
    logits = (_dot(h_hi, rwh_ref[...]) + _dot(h_lo, rwh_ref[...]) + _dot(h_hi, rwl_ref[...])
              + rb_ref[...])
    lane = lax.broadcasted_iota(jnp.int32, logits.shape, 1)
    lane_f = lane.astype(F32)
    neg_inf = jnp.float32(-jnp.inf)
    work = jnp.where(lane < N_EXPERTS, logits, neg_inf)
    vals, sels, idxs = [], [], []
    for _ in range(TOP_K):
        m = jnp.max(work, axis=-1, keepdims=True)
        idx = jnp.min(jnp.where(work == m, lane_f, float(LANES)), axis=-1, keepdims=True)
        sel = lane_f == idx
        vals.append(m)
        idxs.append(idx)
        sels.append(sel)
        work = jnp.where(sel, neg_inf, work)
    exps = [jnp.exp(v - vals[0]) for v in vals]
    denom = exps[0] + exps[1] + exps[2] + exps[3]

    sel_any = jnp.zeros(logits.shape, F32)
    for sel in sels:
        sel_any = jnp.where(sel, 1.0, sel_any)
    prior = _dot(tri_ref[...], sel_any.astype(BF16)) + run_ref[...]
    run_new = run_ref[...] + jnp.sum(sel_any, axis=0, keepdims=True)
    run_ref[...] = run_new
    cnt_ref[...] = run_new

    topi = jnp.zeros(logits.shape, F32)
    rank = jnp.zeros(logits.shape, F32)
    w4 = jnp.zeros(logits.shape, F32)
    for j in range(TOP_K):
        rj = jnp.sum(jnp.where(sels[j], prior, 0.0), axis=-1, keepdims=True)
        topi = jnp.where(lane == j, idxs[j], topi)
        rank = jnp.where(lane == j, rj, rank)
        w4 = jnp.where(lane == j, exps[j] / denom, w4)
    topi_ref[...] = topi.astype(jnp.int32)
    rank_ref[...] = rank.astype(jnp.int32)
    w4_ref[...] = w4


def _post(o_f, o_b, rec, conv, xp, xs, mod3, w_out_bf, rec_g, ln1_g, ln1_b, rw_hi, rw_lo, rb,
          alpha, sample_seq):
    n_p, n_s = xp.shape[0], xs.shape[0]
    n_tok = n_p + n_s
    npt = n_p // TOK_TILE
    tiles_per_sample_seq = sample_seq // TOK_TILE
    tri = jnp.asarray(np.tril(np.ones((TOK_TILE, TOK_TILE), np.float32), -1), BF16)

    def mod_row(i):
        return jnp.where(i < npt, 0, 1 + (i - npt) // tiles_per_sample_seq)

    const = lambda i: (0, 0)
    tok = lambda w: pl.BlockSpec((TOK_TILE, w), lambda i: (i, 0))
    return pl.pallas_call(
        functools.partial(_post_kernel, npt, alpha),
        out_shape=(jax.ShapeDtypeStruct((n_tok, D_MODEL), F32),
                   jax.ShapeDtypeStruct((n_tok * SUBLANES, LANES), F32),
                   jax.ShapeDtypeStruct((n_tok, LANES), jnp.int32),
                   jax.ShapeDtypeStruct((n_tok, LANES), jnp.int32),
                   jax.ShapeDtypeStruct((n_tok, LANES), F32),
                   jax.ShapeDtypeStruct((1, LANES), F32)),
        grid=(n_tok // TOK_TILE,),
        in_specs=[tok(REC_WIDTH), tok(REC_WIDTH),
                  pl.BlockSpec((TOK_TILE, REC_WIDTH), lambda i: (i, 4)),
                  tok(CONV_WIDTH),
                  pl.BlockSpec((TOK_TILE, D_MODEL), lambda i: (jnp.minimum(i, npt - 1), 0)),
                  pl.BlockSpec((TOK_TILE, D_MODEL), lambda i: (jnp.maximum(i - npt, 0), 0)),
                  pl.BlockSpec((1, 1, 6 * D_MODEL), lambda i: (mod_row(i), 0, 0)),
                  pl.BlockSpec(w_out_bf.shape, const),
                  pl.BlockSpec((1, REC_WIDTH), const),
                  pl.BlockSpec((1, D_MODEL), const), pl.BlockSpec((1, D_MODEL), const),
                  pl.BlockSpec((D_MODEL, LANES), const), pl.BlockSpec((D_MODEL, LANES), const),
                  pl.BlockSpec((1, LANES), const),
                  pl.BlockSpec((TOK_TILE, TOK_TILE), const)],
        out_specs=(tok(D_MODEL), pl.BlockSpec((TOK_TILE * SUBLANES, LANES), lambda i: (i, 0)),
                   tok(LANES), tok(LANES), tok(LANES),
                   pl.BlockSpec((1, LANES), const)),
        scratch_shapes=[pltpu.VMEM((1, LANES), F32)],
        compiler_params=pltpu.CompilerParams(dimension_semantics=("arbitrary",),
                                             vmem_limit_bytes=VMEM_LIMIT),
        name="post_mixer_router",
    )(o_f, o_b, rec, conv, xp, xs, mod3, w_out_bf, rec_g, ln1_g, ln1_b, rw_hi, rw_lo, rb, tri)


N_YBUF = 3
DMA_UNROLL = 8


def _expert_kernel(tile_e, n_act, src0_ref, srcn_ref, dst_ref, h_ref, wgu_ref, bgu_ref, wd_ref,
                   bd_ref, yk_ref, xbuf, ybuf, gsem, ssem, wgu_bf, wd_bf):
    i = pl.program_id(0)
    n_steps = pl.num_programs(0)
    n_active = n_act[0]
    tile_rows = EXP_TILE * SUBLANES
    n_slot_rows = yk_ref.shape[0] - N_YBUF * tile_rows

    def row_tile(ref, start):
        return ref.at[pl.ds(pl.multiple_of(start, SUBLANES), SUBLANES)]

    def for_each_row(fn):
        def body(g, carry):
            for k in range(DMA_UNROLL):
                fn(g * DMA_UNROLL + k)
            return carry
        lax.fori_loop(0, EXP_TILE // DMA_UNROLL, body, 0)

    def on_slot(slot, n, fn):
        if isinstance(slot, int):
            fn(slot)
            return
        for s in range(n):
            pl.when(slot == s)(functools.partial(fn, s))

    def gather_rows(src_ref, slot):
        def run(s):
            def one(r):
                pltpu.make_async_copy(row_tile(h_ref, src_ref[0, 0, r]),
                                      row_tile(xbuf.at[s], r * SUBLANES), gsem.at[s]).start()
            for_each_row(one)
        on_slot(slot, 2, run)

    def scatter_rows(slot):
        def run(s):
            def one(r):
                pltpu.make_async_copy(row_tile(ybuf.at[s], r * SUBLANES),
                                      row_tile(yk_ref, dst_ref[0, 0, r]), ssem.at[s]).start()
            for_each_row(one)
        on_slot(slot, N_YBUF, run)

    def scatter_wait(slot):
        pltpu.make_async_copy(ybuf.at[slot], yk_ref.at[pl.ds(0, tile_rows)], ssem.at[slot]).wait()

    @pl.when(i == 0)
    def _():
        ybuf[0] = jnp.zeros((tile_rows, LANES), F32)
        for spare in range(N_YBUF):
            cp = pltpu.make_async_copy(
                ybuf.at[0], yk_ref.at[pl.ds(n_slot_rows + spare * tile_rows, tile_rows)],
                ssem.at[0])
            cp.start()
            cp.wait()
        gather_rows(src0_ref, 0)

    @pl.when(i + 1 < n_active)
    def _():
        gather_rows(srcn_ref, (i + 1) % 2)

    prev = tile_e[jnp.maximum(i - 1, 0)]
    new_expert = jnp.logical_or(i == 0, tile_e[i] != prev)

    @pl.when(jnp.logical_and(new_expert, i < n_active))
    def _():
        rows = 128

        def cast(r, carry):
            sl = pl.ds(pl.multiple_of(r * rows, rows), rows)
            wgu_bf[sl, :] = wgu_ref[0, sl, :].astype(BF16)
            wd_bf[sl, :] = wd_ref[0, sl, :].astype(BF16)
            return carry

        lax.fori_loop(0, D_MODEL // rows, cast, 0)

    @pl.when(i < n_active)
    def _():
        xslot = i % 2
        yslot = i % N_YBUF
        pltpu.make_async_copy(h_ref.at[pl.ds(0, tile_rows)], xbuf.at[xslot], gsem.at[xslot]).wait()

        @pl.when(i >= N_YBUF)
        def _():
            scatter_wait(yslot)

        x = _load_row_tiles(xbuf.at[xslot], EXP_TILE).astype(BF16)
        gu = _dot(x, wgu_bf[...]) + bgu_ref[0]
        gate = jnp.minimum(gu[:, :D_FF], SWIGLU_LIMIT)
        up = jnp.clip(gu[:, D_FF:], -SWIGLU_LIMIT, SWIGLU_LIMIT)
        act = (up + 1.0) * gate * _sigmoid(SWIGLU_ALPHA * gate)
        _store_row_tiles(ybuf.at[yslot], _dot(act.astype(BF16), wd_bf[...]) + bd_ref[0])
        scatter_rows(yslot)

    @pl.when(i == n_steps - 1)
    def _():
        for back in range(1, N_YBUF + 1):
            scatter_wait((n_active - back) % N_YBUF)


def _experts(h2, src3, dst3, tile_e, n_act, w_gate_up, b_gate_up, w_down, b_down):
    n_tok = h2.shape[0] // SUBLANES
    n_tiles = src3.shape[0]
    idx_block = (1, 1, EXP_TILE)
    grid_spec = pltpu.PrefetchScalarGridSpec(
        num_scalar_prefetch=2,
        grid=(n_tiles,),
        in_specs=[pl.BlockSpec(idx_block, lambda i, te, na: (0, 0, 0), memory_space=pltpu.SMEM),
                  pl.BlockSpec(idx_block, lambda i, te, na: (jnp.minimum(i + 1, n_tiles - 1), 0, 0),
                               memory_space=pltpu.SMEM),
                  pl.BlockSpec(idx_block, lambda i, te, na: (i, 0, 0), memory_space=pltpu.SMEM),
                  pl.BlockSpec(memory_space=pl.ANY),
                  pl.BlockSpec((1, D_MODEL, 2 * D_FF), lambda i, te, na: (te[i], 0, 0)),
                  pl.BlockSpec((1, 1, 2 * D_FF), lambda i, te, na: (te[i], 0, 0)),
                  pl.BlockSpec((1, D_FF, D_MODEL), lambda i, te, na: (te[i], 0, 0)),
                  pl.BlockSpec((1, 1, D_MODEL), lambda i, te, na: (te[i], 0, 0))],
        out_specs=pl.BlockSpec(memory_space=pl.ANY),
        scratch_shapes=[pltpu.VMEM((2, EXP_TILE * SUBLANES, LANES), F32),
                        pltpu.VMEM((N_YBUF, EXP_TILE * SUBLANES, LANES), F32),
                        pltpu.SemaphoreType.DMA((2,)), pltpu.SemaphoreType.DMA((N_YBUF,)),
                        pltpu.VMEM((D_MODEL, 2 * D_FF), BF16), pltpu.VMEM((D_FF, D_MODEL), BF16)],
    )
    return pl.pallas_call(
        _expert_kernel,
        out_shape=jax.ShapeDtypeStruct(((TOP_K * n_tok + N_YBUF * EXP_TILE) * SUBLANES, LANES), F32),
        grid_spec=grid_spec,
        compiler_params=pltpu.CompilerParams(dimension_semantics=("arbitrary",),
                                             vmem_limit_bytes=VMEM_LIMIT),
        name="moe_experts",
    )(tile_e, n_act, src3, src3, dst3, h2, w_gate_up, b_gate_up.reshape(N_EXPERTS, 1, 2 * D_FF),
      w_down, b_down.reshape(N_EXPERTS, 1, D_MODEL))


def _combine_kernel(n_prompt_tiles, alpha, y0_ref, y1_ref, y2_ref, y3_ref, x1_ref, w4_ref, mod_ref,
                    l2g_ref, l2b_ref, yp_ref, ysm_ref):
    i = pl.program_id(0)
    w4 = w4_ref[...]
    ff = _load_row_tiles(y0_ref, TOK_TILE) * w4[:, 0:1]
    for j, y_ref in enumerate((y1_ref, y2_ref, y3_ref), start=1):
        ff = ff + _load_row_tiles(y_ref, TOK_TILE) * w4[:, j:j + 1]
    g2 = mod_ref[0][:, 5 * D_MODEL:6 * D_MODEL]
    out = _layer_norm(alpha * x1_ref[...] + g2 * ff) * l2g_ref[...] + l2b_ref[...]

    @pl.when(i < n_prompt_tiles)
    def _():
        yp_ref[...] = out

    @pl.when(i >= n_prompt_tiles)
    def _():
        ysm_ref[...] = out


def _combine(yk, x1, w4, mod3, ln2_g, ln2_b, n_p, alpha, sample_seq):
    n_tok = x1.shape[0]
    n_s = n_tok - n_p
    npt = n_p // TOK_TILE
    n_tok_tiles = n_tok // TOK_TILE
    tiles_per_sample_seq = sample_seq // TOK_TILE

    def mod_row(i):
        return jnp.where(i < npt, 0, 1 + (i - npt) // tiles_per_sample_seq)

    def slot(j):
        return pl.BlockSpec((TOK_TILE * SUBLANES, LANES), lambda i: (j * n_tok_tiles + i, 0))

    const = lambda i: (0, 0)
    return pl.pallas_call(
        functools.partial(_combine_kernel, npt, alpha),
        out_shape=(jax.ShapeDtypeStruct((n_p, D_MODEL), F32),
                   jax.ShapeDtypeStruct((n_s, D_MODEL), F32)),
        grid=(n_tok_tiles,),
        in_specs=[slot(0), slot(1), slot(2), slot(3),
                  pl.BlockSpec((TOK_TILE, D_MODEL), lambda i: (i, 0)),
                  pl.BlockSpec((TOK_TILE, LANES), lambda i: (i, 0)),
                  pl.BlockSpec((1, 1, 6 * D_MODEL), lambda i: (mod_row(i), 0, 0)),
                  pl.BlockSpec((1, D_MODEL), const), pl.BlockSpec((1, D_MODEL), const)],
        out_specs=(pl.BlockSpec((TOK_TILE, D_MODEL), lambda i: (jnp.minimum(i, npt - 1), 0)),
                   pl.BlockSpec((TOK_TILE, D_MODEL), lambda i: (jnp.maximum(i - npt, 0), 0))),
        compiler_params=pltpu.CompilerParams(dimension_semantics=("arbitrary",),
                                             vmem_limit_bytes=VMEM_LIMIT),
        name="moe_combine",
    )(yk, yk, yk, yk, x1, w4, mod3, ln2_g, ln2_b)


def kernel(x_prompt, x_sample, c, state_fwd, state_bwd, c_ctx, w_ada, b_ada, w_in, conv_w, conv_b,
           conv_ln_g, conv_ln_b, lb_logits, rec_norm_g, w_out, ln1_g, ln1_b, ln2_g, ln2_b,
           router_w, router_b, w_gate_up, b_gate_up, w_down, b_down):
    depth = w_ada.shape[0]
    assert depth == 1
    alpha = (2.0 * depth) ** 0.25
    nb_p, seq_p, _ = x_prompt.shape
    nb_s, seq_s, _ = x_sample.shape
    n_p, n_s = nb_p * seq_p, nb_s * seq_s
    n_tok = n_p + n_s
    row2 = lambda a: a.reshape(1, -1)

    lb = jnp.cumsum(jax.nn.softmax(lb_logits.astype(F32), axis=0), axis=0)[0]

    cond = jnp.concatenate([c_ctx[None, :], c, jnp.zeros((16 - 1 - nb_s, D_MODEL), F32)], axis=0)
    mod = _ada(cond, w_ada[0], b_ada[0])
    mod3 = mod.reshape(16, 1, 6 * D_MODEL)

    xp = x_prompt.reshape(n_p, D_MODEL)
    xs = x_sample.reshape(n_s, D_MODEL)
    conv_w_pad = jnp.concatenate([conv_w[0], jnp.zeros((1, CONV_WIDTH), F32)], axis=0)
    conv_out, rec = _inproj(xp, xs, mod3, w_in[0].astype(BF16), conv_w_pad, row2(conv_b[0]),
                            row2(conv_ln_g[0]), row2(conv_ln_b[0]), seq_p, seq_s)

    zeros_state = jnp.zeros((nb_p, REC_HEADS, REC_DK, REC_DK), F32)
    s0f = jnp.concatenate([zeros_state, jnp.swapaxes(state_fwd[:, 0], -1, -2)], axis=0)
    s0b = jnp.concatenate([zeros_state, jnp.swapaxes(state_bwd[:, 0], -1, -2)], axis=0)
    o_f, o_b, sf_t, sb_t = _scan(rec, lb, s0f, s0b, [seq_p] * nb_p + [seq_s] * nb_s)
    new_f = jnp.swapaxes(sf_t[:nb_p], -1, -2)[:, None]
    new_b = jnp.swapaxes(sb_t[:nb_p], -1, -2)[:, None]

    rw = jnp.pad(router_w[0], ((0, 0), (0, LANES - N_EXPERTS)))
    rw_hi = rw.astype(BF16)
    rw_lo = (rw - rw_hi.astype(F32)).astype(BF16)
    rb = jnp.pad(router_b[0], (0, LANES - N_EXPERTS)).reshape(1, LANES)
    x1, h2, topi, rank, w4, counts = _post(
        o_f, o_b, rec, conv_out, xp, xs, mod3, w_out[0].astype(BF16), row2(rec_norm_g[0]),
        row2(ln1_g[0]), row2(ln1_b[0]), rw_hi, rw_lo, rb, alpha, seq_s)

    cnt = counts[0, :N_EXPERTS].astype(jnp.int32)
    gpad = ((cnt + EXP_TILE - 1) // EXP_TILE) * EXP_TILE
    gend = jnp.cumsum(gpad)
    gstart = gend - gpad
    n_slots = n_tok * TOP_K
    n_rows = n_slots + N_EXPERTS * EXP_TILE
    n_tiles = n_rows // EXP_TILE
    n_act = gend[-1] // EXP_TILE
    tile_ids = jnp.minimum(jnp.arange(n_tiles, dtype=jnp.int32), n_act - 1)
    tile_e = jnp.sum((gend[None, :] <= tile_ids[:, None] * EXP_TILE).astype(jnp.int32), axis=1)
    dest = (gstart[topi[:, :TOP_K]] + rank[:, :TOP_K]).astype(jnp.int32)
    code = jnp.full((n_rows,), -1, jnp.int32).at[dest.reshape(-1)].set(
        jnp.arange(n_slots, dtype=jnp.int32), unique_indices=True)
    pos = jnp.arange(n_rows, dtype=jnp.int32)
    tok = code >> 2
    src = jnp.where(code >= 0, tok, pos % n_tok)
    spare = n_slots + ((pos // EXP_TILE) % N_YBUF) * EXP_TILE + pos % EXP_TILE
    dst = jnp.where(code >= 0, (code & 3) * n_tok + tok, spare)

    as_offsets = lambda rows: (rows * SUBLANES).reshape(n_tiles, 1, EXP_TILE)
    yk = _experts(h2, as_offsets(src), as_offsets(dst), tile_e,
                  n_act.reshape(1).astype(jnp.int32), w_gate_up[0], b_gate_up[0], w_down[0],
                  b_down[0])
    y_p, y_s = _combine(yk, x1, w4, mod3, row2(ln2_g[0]), row2(ln2_b[0]), n_p, alpha, seq_s)
    return (y_p.reshape(nb_p, seq_p, D_MODEL), y_s.reshape(nb_s, seq_s, D_MODEL), new_f, new_b)
```

```python
import functools

import numpy as np
import jax
import jax.numpy as jnp
from jax import lax
from jax.experimental import pallas as pl
from jax.experimental.pallas import tpu as pltpu

F32 = jnp.float32
BF16 = jnp.bfloat16

D_MODEL = 1024
CONV_WIDTH = 512
CONV_TAPS = 31
REC_HEADS = 4
REC_DK = 128
REC_WIDTH = REC_HEADS * REC_DK
REC_COLS = 5 * REC_WIDTH
CHUNK = 64
N_LEVELS = 6
SCAN_CHUNKS = 4
N_EXPERTS = 32
TOP_K = 4
D_FF = 1024
SWIGLU_LIMIT = 7.0
SWIGLU_ALPHA = 1.702
LN_EPS = 1e-5
RMS_EPS = 1e-6
GRID_W = 64

LANES = 128
TOK_TILE = 256
EXP_TILE = 512
VMEM_LIMIT = 56 * 1024 * 1024


def _sigmoid(x):
    return 1.0 / (1.0 + jnp.exp(-x))


def _layer_norm(x):
    mu = jnp.mean(x, axis=-1, keepdims=True)
    xc = x - mu
    var = jnp.mean(xc * xc, axis=-1, keepdims=True)
    return xc * lax.rsqrt(var + LN_EPS)


def _dot(a, b):
    return jnp.dot(a, b, preferred_element_type=F32)


def _dot_nt(a, b):
    return lax.dot_general(a, b, (((1,), (1,)), ((), ())), preferred_element_type=F32)


SUBLANES = 8
assert D_MODEL == SUBLANES * LANES


def _store_row_tiles(ref, value):
    for s in range(SUBLANES):
        ref[pl.ds(s, value.shape[0], stride=SUBLANES), :] = value[:, s * LANES:(s + 1) * LANES]


def _load_row_tiles(ref, n_rows):
    return jnp.concatenate(
        [ref[pl.ds(s, n_rows, stride=SUBLANES), :] for s in range(SUBLANES)], axis=-1)


def _split3(x):
    hi = x.astype(BF16)
    r1 = x - hi.astype(F32)
    mid = r1.astype(BF16)
    lo = (r1 - mid.astype(F32)).astype(BF16)
    return hi, mid, lo


def _ada_kernel(c_ref, w_ref, b_ref, o_ref):
    c = c_ref[...]
    s = (c * _sigmoid(c)).astype(BF16)
    o_ref[...] = _dot(s, w_ref[...].astype(BF16)) + b_ref[...]


def _ada(cond, w_ada, b_ada):
    rows = cond.shape[0]
    ncol = w_ada.shape[1]
    tn = 1024
    return pl.pallas_call(
        _ada_kernel,
        out_shape=jax.ShapeDtypeStruct((rows, ncol), F32),
        grid=(ncol // tn,),
        in_specs=[pl.BlockSpec((rows, D_MODEL), lambda j: (0, 0)),
                  pl.BlockSpec((D_MODEL, tn), lambda j: (0, j)),
                  pl.BlockSpec((1, tn), lambda j: (0, j))],
        out_specs=pl.BlockSpec((rows, tn), lambda j: (0, j)),
        compiler_params=pltpu.CompilerParams(dimension_semantics=("parallel",),
                                             vmem_limit_bytes=VMEM_LIMIT),
        name="ada_mod",
    )(cond, w_ada, b_ada.reshape(1, ncol))


def _inproj_kernel(n_prompt_tiles, prompt_seg, xp_ref, xs_ref, mod_ref, win_ref, cw_ref, cb_ref,
                   cg_ref, cbeta_ref, conv_ref, rec_ref):
    i = pl.program_id(0)
    is_prompt = i < n_prompt_tiles
    x = jnp.where(is_prompt, xp_ref[...], xs_ref[...])
    mod = mod_ref[0]
    sh1 = mod[:, 0:D_MODEL]
    sc1 = mod[:, D_MODEL:2 * D_MODEL]
    h = (_layer_norm(x) * (1.0 + sc1) + sh1).astype(BF16)

    glu = _dot(h, win_ref[:, 0:2 * CONV_WIDTH])
    rec_ref[...] = _dot(h, win_ref[:, 2 * CONV_WIDTH:])
    u = glu[:, :CONV_WIDTH] * _sigmoid(glu[:, CONV_WIDTH:])

    seg_len = jnp.where(is_prompt, prompt_seg, GRID_W)
    pos = lax.broadcasted_iota(jnp.int32, u.shape, 0) & (seg_len - 1)
    half = CONV_TAPS // 2
    acc = u * cw_ref[half:half + 1, :] + cb_ref[...]
    for j in range(CONV_TAPS):
        d = j - half
        if d == 0:
            continue
        shifted = pltpu.roll(u, (-d) % TOK_TILE, axis=0)
        valid = (pos >= -d) if d < 0 else (pos < seg_len - d)
        acc = acc + jnp.where(valid, shifted, 0.0) * cw_ref[j:j + 1, :]
    y = _layer_norm(acc) * cg_ref[...] + cbeta_ref[...]
    conv_ref[...] = (y * _sigmoid(y)).astype(BF16)


def _inproj(xp, xs, mod3, w_in_bf, conv_w, conv_b, conv_g, conv_beta, prompt_seq, sample_seq):
    n_p, n_s = xp.shape[0], xs.shape[0]
    n_tok = n_p + n_s
    npt = n_p // TOK_TILE
    tiles_per_sample_seq = sample_seq // TOK_TILE
    assert prompt_seq == TOK_TILE and sample_seq % TOK_TILE == 0

    def mod_row(i):
        return jnp.where(i < npt, 0, 1 + (i - npt) // tiles_per_sample_seq)

    const = lambda i: (0, 0)
    return pl.pallas_call(
        functools.partial(_inproj_kernel, npt, prompt_seq),
        out_shape=(jax.ShapeDtypeStruct((n_tok, CONV_WIDTH), BF16),
                   jax.ShapeDtypeStruct((n_tok, REC_COLS), F32)),
        grid=(n_tok // TOK_TILE,),
        in_specs=[pl.BlockSpec((TOK_TILE, D_MODEL), lambda i: (jnp.minimum(i, npt - 1), 0)),
                  pl.BlockSpec((TOK_TILE, D_MODEL), lambda i: (jnp.maximum(i - npt, 0), 0)),
                  pl.BlockSpec((1, 1, 6 * D_MODEL), lambda i: (mod_row(i), 0, 0)),
                  pl.BlockSpec(w_in_bf.shape, const),
                  pl.BlockSpec(conv_w.shape, const),
                  pl.BlockSpec((1, CONV_WIDTH), const),
                  pl.BlockSpec((1, CONV_WIDTH), const),
                  pl.BlockSpec((1, CONV_WIDTH), const)],
        out_specs=(pl.BlockSpec((TOK_TILE, CONV_WIDTH), lambda i: (i, 0)),
                   pl.BlockSpec((TOK_TILE, REC_COLS), lambda i: (i, 0))),
        compiler_params=pltpu.CompilerParams(dimension_semantics=("parallel",),
                                             vmem_limit_bytes=VMEM_LIMIT),
        name="inproj_conv",
    )(xp, xs, mod3, w_in_bf, conv_w, conv_b, conv_g, conv_beta)


def _scan_constants():
    c = CHUNK
    w = np.zeros((N_LEVELS + 2, c, c), np.float32)
    m = np.zeros((N_LEVELS + 1, c, c), np.float32)
    t = np.arange(c)
    for j in range(N_LEVELS):
        half = 1 << j
        for ti in range(c):
            ref = (ti & ~(2 * half - 1)) + half - 1
            if ti & half:
                w[j, ti, ref + 1:ti + 1] = 1.0
            else:
                w[j, ti, ti + 1:ref + 1] = 1.0
        upper = (t[:, None] & half) != 0
        lower = (t[None, :] & half) == 0
        same = (t[:, None] >> (j + 1)) == (t[None, :] >> (j + 1))
        m[j] = (upper & lower & same).astype(np.float32)
    w[N_LEVELS] = (t[None, :] <= t[:, None]).astype(np.float32)
    w[N_LEVELS + 1] = (t[None, :] > t[:, None]).astype(np.float32)
    m[N_LEVELS] = np.eye(c, dtype=np.float32)
    w_f = w.reshape(-1, c)
    w_b = w[:, ::-1, ::-1].reshape(-1, c)
    m_b = m[:, ::-1, ::-1]
    return (jnp.asarray(w_f, BF16), jnp.asarray(np.ascontiguousarray(w_b), BF16),
            jnp.asarray(m), jnp.asarray(np.ascontiguousarray(m_b)))


def _scan_gates(z, lb, w_ref):
    e = jnp.exp(-jnp.abs(z))
    r = 1.0 / (1.0 + e)
    a = e * r
    sig = jnp.where(z >= 0, r, a)
    sig_neg = jnp.where(z >= 0, a, r)
    one_m_lb = 1.0 - lb
    log_f = jnp.log(lb + one_m_lb * sig)
    k = one_m_lb * sig_neg
    hi, mid, lo = _split3(log_f)
    w = w_ref[...]
    ex = jnp.exp(_dot(w, hi) + _dot(w, mid) + _dot(w, lo))
    return k, ex


def _scan_intra(q, k, ex, m_ref):
    c = CHUNK
    a_mat = m_ref[N_LEVELS] * _dot_nt(q.astype(BF16), k.astype(BF16))
    for j in range(N_LEVELS):
        xj = ex[j * c:(j + 1) * c]
        a_mat = a_mat + m_ref[j] * _dot_nt((q * xj).astype(BF16), (k * xj).astype(BF16))
    return a_mat


def _scan_kernel(fblk, bblk, first, seq, qf_ref, zf_ref, vf_ref, qb_ref, zb_ref, vb_ref, lb_ref,
                 wf_ref, wb_ref, mf_ref, mb_ref, s0f_ref, s0b_ref, of_ref, ob_ref, sf_ref, sb_ref):
    s = pl.program_id(0)

    @pl.when(first[s] == 1)
    def _():
        sf_ref[...] = s0f_ref[...]
        sb_ref[...] = s0b_ref[...]

    c = CHUNK
    dirs = ((qf_ref, zf_ref, vf_ref, 0, wf_ref, mf_ref, sf_ref, of_ref, c - 1,
             tuple(range(SCAN_CHUNKS))),
            (qb_ref, zb_ref, vb_ref, 1, wb_ref, mb_ref, sb_ref, ob_ref, 0,
             tuple(reversed(range(SCAN_CHUNKS)))))
    units = [(d, slice(sub * c, (sub + 1) * c)) for d in dirs for sub in d[-1]]
    gates = [_scan_gates(d[1][rows, :], lb_ref[d[3]:d[3] + 1, :], d[4]) for d, rows in units]
    intra = []
    for (d, rows), (k, ex) in zip(units, gates):
        for h in range(REC_HEADS):
            cols = slice(h * REC_DK, (h + 1) * REC_DK)
            intra.append(_scan_intra(d[0][rows, cols], k[:, cols], ex[:, cols], d[5]))
    idx = 0
    for (d, rows), (k, ex) in zip(units, gates):
        q_ref, _, v_ref, _, _, _, st_ref, o_ref, last, _ = d
        for h in range(REC_HEADS):
            cols = slice(h * REC_DK, (h + 1) * REC_DK)
            q, v = q_ref[rows, cols], v_ref[rows, cols]
            ex_b = ex[N_LEVELS * c:(N_LEVELS + 1) * c, cols]
            ex_l = ex[(N_LEVELS + 1) * c:(N_LEVELS + 2) * c, cols]
            st = st_ref[0, h]
            o_ref[rows, cols] = (_dot(intra[idx].astype(BF16), v.astype(BF16))
                                 + _dot_nt((q * ex_b).astype(BF16), st.astype(BF16)))
            st_ref[0, h] = (st * ex_b[last:last + 1, :]
                            + _dot(v.T.astype(BF16), (k[:, cols] * ex_l).astype(BF16)))
            idx += 1


def _scan(rec, lb, s0f_t, s0b_t, seq_lens):
    n_tok = rec.shape[0]
    fblk, bblk, first, seq = [], [], [], []
    base = 0
    step_rows = CHUNK * SCAN_CHUNKS
    for si, ln in enumerate(seq_lens):
        assert ln % step_rows == 0
        n = ln // step_rows
        for ci in range(n):
            fblk.append(base + ci)
            bblk.append(base + n - 1 - ci)
            first.append(1 if ci == 0 else 0)
            seq.append(si)
        base += n
    steps = len(fblk)
    as_i32 = lambda a: jnp.asarray(np.asarray(a, np.int32))
    w_f, w_b, m_f, m_b = _scan_constants()
    n_seq = len(seq_lens)

    def col(block_of, c):
        return pl.BlockSpec((step_rows, REC_WIDTH), lambda s, fb, bb, fi, sq: (block_of(fb, bb)[s], c))

    fwd = lambda fb, bb: fb
    bwd = lambda fb, bb: bb
    const2 = lambda s, fb, bb, fi, sq: (0, 0)
    const3 = lambda s, fb, bb, fi, sq: (0, 0, 0)
    state_spec = pl.BlockSpec((1, REC_HEADS, REC_DK, REC_DK), lambda s, fb, bb, fi, sq: (sq[s], 0, 0, 0))
    grid_spec = pltpu.PrefetchScalarGridSpec(
        num_scalar_prefetch=4,
        grid=(steps,),
        in_specs=[col(fwd, 0), col(fwd, 1), col(fwd, 3), col(bwd, 0), col(bwd, 2), col(bwd, 3),
                  pl.BlockSpec((2, REC_WIDTH), const2),
                  pl.BlockSpec(w_f.shape, const2), pl.BlockSpec(w_b.shape, const2),
                  pl.BlockSpec(m_f.shape, const3), pl.BlockSpec(m_b.shape, const3),
                  state_spec, state_spec],
        out_specs=(pl.BlockSpec((step_rows, REC_WIDTH), lambda s, fb, bb, fi, sq: (fb[s], 0)),
                   pl.BlockSpec((step_rows, REC_WIDTH), lambda s, fb, bb, fi, sq: (bb[s], 0)),
                   state_spec, state_spec),
    )
    st_shape = jax.ShapeDtypeStruct((n_seq, REC_HEADS, REC_DK, REC_DK), F32)
    return pl.pallas_call(
        _scan_kernel,
        out_shape=(jax.ShapeDtypeStruct((n_tok, REC_WIDTH), F32),
                   jax.ShapeDtypeStruct((n_tok, REC_WIDTH), F32), st_shape, st_shape),
        grid_spec=grid_spec,
        compiler_params=pltpu.CompilerParams(dimension_semantics=("arbitrary",),
                                             vmem_limit_bytes=VMEM_LIMIT),
        name="hgrn2_scan",
    )(as_i32(fblk), as_i32(bblk), as_i32(first), as_i32(seq),
      rec, rec, rec, rec, rec, rec, lb, w_f, w_b, m_f, m_b, s0f_t, s0b_t)


def _post_kernel(n_prompt_tiles, alpha, of_ref, ob_ref, g_ref, conv_ref, xp_ref, xs_ref, mod_ref,
                 wout_ref, rg_ref, l1g_ref, l1b_ref, rwh_ref, rwl_ref, rb_ref, tri_ref,
                 x1_ref, h2_ref, rank_t_ref, slot_t_ref, w4_ref, cnt_ref, start_ref, run_ref):
    i = pl.program_id(0)

    @pl.when(i == 0)
    def _():
        run_ref[...] = jnp.zeros_like(run_ref)

    x = jnp.where(i < n_prompt_tiles, xp_ref[...], xs_ref[...])
    mod = mod_ref[0]
    g1 = mod[:, 2 * D_MODEL:3 * D_MODEL]
    sh2 = mod[:, 3 * D_MODEL:4 * D_MODEL]
    sc2 = mod[:, 4 * D_MODEL:5 * D_MODEL]

    g = g_ref[...]
    silu_g = g * _sigmoid(g)
    mix = _dot(conv_ref[...], wout_ref[0:CONV_WIDTH, :])
    for h in range(REC_HEADS):
        cols = slice(h * REC_DK, (h + 1) * REC_DK)
        o = of_ref[:, cols] + ob_ref[:, cols]
        o = o * lax.rsqrt(jnp.mean(o * o, axis=-1, keepdims=True) + RMS_EPS)
        rec_out = (o * rg_ref[:, cols] * silu_g[:, cols]).astype(BF16)
        mix = mix + _dot(rec_out, wout_ref[CONV_WIDTH + h * REC_DK:CONV_WIDTH + (h + 1) * REC_DK, :])

    x1 = _layer_norm(alpha * x + g1 * mix) * l1g_ref[...] + l1b_ref[...]
    x1_ref[...] = x1
    h2 = _layer_norm(x1) * (1.0 + sc2) + sh2
    _store_row_tiles(h2_ref, h2)

    h_hi = h2.astype(BF16)
    h_lo = (h2 - h_hi.astype(F32)).astype(BF16)
    logits = (_dot(h_hi, rwh_ref[...]) + _dot(h_lo, rwh_ref[...]) + _dot(h_hi, rwl_ref[...])
              + rb_ref[...])
    lane = lax.broadcasted_iota(jnp.int32, logits.shape, 1)
    lane_f = lane.astype(F32)
    neg_inf = jnp.float32(-jnp.inf)
    work = jnp.where(lane < N_EXPERTS, logits, neg_inf)
    vals, sels = [], []
    for _ in range(TOP_K):
        m = jnp.max(work, axis=-1, keepdims=True)
        idx = jnp.min(jnp.where(work == m, lane_f, float(LANES)), axis=-1, keepdims=True)
        sel = lane_f == idx
        vals.append(m)
        sels.append(sel)
        work = jnp.where(sel, neg_inf, work)
    exps = [jnp.exp(v - vals[0]) for v in vals]
    denom = exps[0] + exps[1] + exps[2] + exps[3]

    sel_any = jnp.zeros(logits.shape, F32)
    for sel in sels:
        sel_any = jnp.where(sel, 1.0, sel_any)
    run_old = run_ref[...]
    prior = _dot(tri_ref[...], sel_any.astype(BF16)) + run_old
    run_new = run_old + jnp.sum(sel_any, axis=0, keepdims=True)
    run_ref[...] = run_new
    cnt_ref[...] = run_new
    start_ref[0] = run_old

    slot = jnp.zeros(logits.shape, F32)
    w4 = jnp.zeros(logits.shape, F32)
    for j in range(TOP_K):
        slot = jnp.where(sels[j], float(j), slot)
        w4 = jnp.where(lane == j, exps[j] / denom, w4)
    routed = sel_any > 0.0
    rank_t_ref[...] = jnp.where(routed, prior, -1.0).T[:N_EXPERTS]
    slot_t_ref[...] = jnp.where(routed, slot, -1.0).T[:N_EXPERTS]
    w4_ref[...] = w4


def _post(o_f, o_b, rec, conv, xp, xs, mod3, w_out_bf, rec_g, ln1_g, ln1_b, rw_hi, rw_lo, rb,
          alpha, sample_seq):
    n_p, n_s = xp.shape[0], xs.shape[0]
    n_tok = n_p + n_s
    npt = n_p // TOK_TILE
    tiles_per_sample_seq = sample_seq // TOK_TILE
    tri = jnp.asarray(np.tril(np.ones((TOK_TILE, TOK_TILE), np.float32), -1), BF16)

    def mod_row(i):
        return jnp.where(i < npt, 0, 1 + (i - npt) // tiles_per_sample_seq)

    const = lambda i: (0, 0)
    tok = lambda w: pl.BlockSpec((TOK_TILE, w), lambda i: (i, 0))
    return pl.pallas_call(
        functools.partial(_post_kernel, npt, alpha),
        out_shape=(jax.ShapeDtypeStruct((n_tok, D_MODEL), F32),
                   jax.ShapeDtypeStruct((n_tok * SUBLANES, LANES), F32),
                   jax.ShapeDtypeStruct((N_EXPERTS, n_tok), F32),
                   jax.ShapeDtypeStruct((N_EXPERTS, n_tok), F32),
                   jax.ShapeDtypeStruct((n_tok, LANES), F32),
                   jax.ShapeDtypeStruct((1, LANES), F32),
                   jax.ShapeDtypeStruct((n_tok // TOK_TILE, 1, LANES), F32)),
        grid=(n_tok // TOK_TILE,),
        in_specs=[tok(REC_WIDTH), tok(REC_WIDTH),
                  pl.BlockSpec((TOK_TILE, REC_WIDTH), lambda i: (i, 4)),
                  tok(CONV_WIDTH),
                  pl.BlockSpec((TOK_TILE, D_MODEL), lambda i: (jnp.minimum(i, npt - 1), 0)),
                  pl.BlockSpec((TOK_TILE, D_MODEL), lambda i: (jnp.maximum(i - npt, 0), 0)),
                  pl.BlockSpec((1, 1, 6 * D_MODEL), lambda i: (mod_row(i), 0, 0)),
                  pl.BlockSpec(w_out_bf.shape, const),
                  pl.BlockSpec((1, REC_WIDTH), const),
                  pl.BlockSpec((1, D_MODEL), const), pl.BlockSpec((1, D_MODEL), const),
                  pl.BlockSpec((D_MODEL, LANES), const), pl.BlockSpec((D_MODEL, LANES), const),
                  pl.BlockSpec((1, LANES), const),
                  pl.BlockSpec((TOK_TILE, TOK_TILE), const)],
        out_specs=(tok(D_MODEL), pl.BlockSpec((TOK_TILE * SUBLANES, LANES), lambda i: (i, 0)),
                   pl.BlockSpec((N_EXPERTS, TOK_TILE), lambda i: (0, i)),
                   pl.BlockSpec((N_EXPERTS, TOK_TILE), lambda i: (0, i)),
                   tok(LANES), pl.BlockSpec((1, LANES), const),
                   pl.BlockSpec((1, 1, LANES), lambda i: (i, 0, 0))),
        scratch_shapes=[pltpu.VMEM((1, LANES), F32)],
        compiler_params=pltpu.CompilerParams(dimension_semantics=("arbitrary",),
                                             vmem_limit_bytes=VMEM_LIMIT),
        name="post_mixer_router",
    )(o_f, o_b, rec, conv, xp, xs, mod3, w_out_bf, rec_g, ln1_g, ln1_b, rw_hi, rw_lo, rb, tri)


PLAN_ROWS = 32


def _plan_kernel(n_tok, tile_e, rank0, win0, n_win, n_act, cum, rank_t_ref, slot_t_ref, src_ref,
                 dst_ref, acc_ref):
    i = pl.program_id(0)

    @pl.when(i < n_act[0])
    def _():
        e = tile_e[i]
        base = rank0[i]
        row = lax.broadcasted_iota(jnp.int32, (EXP_TILE, 1), 0)
        rows_iota = lax.broadcasted_iota(jnp.int32, (PLAN_ROWS, 1), 0)
        lane = lax.broadcasted_iota(jnp.int32, (1, TOK_TILE), 1)
        acc_ref[...] = jnp.zeros_like(acc_ref)

        def window(w, carry):
            b = win0[i] + w
            tok0 = pl.multiple_of(b * TOK_TILE, TOK_TILE)
            rk = rank_t_ref[pl.ds(e, 1), pl.ds(tok0, TOK_TILE)]
            sl = slot_t_ref[pl.ds(e, 1), pl.ds(tok0, TOK_TILE)]
            code = ((tok0 + lane) * TOP_K).astype(F32) + sl + 1.0
            lo = jnp.maximum(cum[b * N_EXPERTS + e] - base, 0)
            hi = jnp.minimum(cum[(b + 1) * N_EXPERTS + e] - base, EXP_TILE)

            def rows(k, c):
                r0 = pl.multiple_of(k * PLAN_ROWS, PLAN_ROWS)
                want = (base + r0 + rows_iota).astype(F32)
                hit = jnp.where(rk == want, code, 0.0)
                part = hit[:, 0:LANES]
                for c0 in range(LANES, TOK_TILE, LANES):
                    part = part + hit[:, c0:c0 + LANES]
                acc_ref[pl.ds(r0, PLAN_ROWS), :] += part
                return c

            lax.fori_loop(lo // PLAN_ROWS, (hi + PLAN_ROWS - 1) // PLAN_ROWS, rows, 0)
            return carry

        lax.fori_loop(0, n_win[i], window, 0)
        code = jnp.sum(acc_ref[...], axis=-1, keepdims=True).astype(jnp.int32) - 1
        tok = code >> 2
        src = jnp.where(code >= 0, tok, lax.rem(i * EXP_TILE + row, n_tok))
        spare = TOP_K * n_tok + lax.rem(i, N_YBUF) * EXP_TILE + row
        dst = jnp.where(code >= 0, (code & (TOP_K - 1)) * n_tok + tok, spare)
        src_ref[0] = src * SUBLANES
        dst_ref[0] = dst * SUBLANES

    @pl.when(i >= n_act[0])
    def _():
        src_ref[...] = jnp.zeros_like(src_ref)
        dst_ref[...] = jnp.zeros_like(dst_ref)


def _plan(rank_t, slot_t, tile_e, rank0, win0, n_win, n_act, cum):
    n_tok = rank_t.shape[1]
    n_tiles = tile_e.shape[0]
    whole = lambda i, *_: (0, 0)
    per_tile = pl.BlockSpec((1, EXP_TILE, 1), lambda i, *_: (i, 0, 0))
    grid_spec = pltpu.PrefetchScalarGridSpec(
        num_scalar_prefetch=6,
        grid=(n_tiles,),
        in_specs=[pl.BlockSpec(rank_t.shape, whole), pl.BlockSpec(slot_t.shape, whole)],
        out_specs=(per_tile, per_tile),
        scratch_shapes=[pltpu.VMEM((EXP_TILE, LANES), F32)],
    )
    shape = jax.ShapeDtypeStruct((n_tiles, EXP_TILE, 1), jnp.int32)
    return pl.pallas_call(
        functools.partial(_plan_kernel, n_tok),
        out_shape=(shape, shape),
        grid_spec=grid_spec,
        compiler_params=pltpu.CompilerParams(dimension_semantics=("arbitrary",),
                                             vmem_limit_bytes=VMEM_LIMIT),
        name="moe_plan",
    )(tile_e, rank0, win0, n_win, n_act, cum, rank_t, slot_t)


N_YBUF = 3
DMA_UNROLL = 8


def _expert_kernel(tile_e, n_act, src0_ref, srcn_ref, dst_ref, h_ref, wgu_ref, bgu_ref, wd_ref,
                   bd_ref, yk_ref, xbuf, ybuf, gsem, ssem, wgu_bf, wd_bf):
    i = pl.program_id(0)
    n_steps = pl.num_programs(0)
    n_active = n_act[0]
    tile_rows = EXP_TILE * SUBLANES
    n_slot_rows = yk_ref.shape[0] - N_YBUF * tile_rows

    def row_tile(ref, start):
        return ref.at[pl.ds(pl.multiple_of(start, SUBLANES), SUBLANES)]

    def for_each_row(fn):
        def body(g, carry):
            for k in range(DMA_UNROLL):
                fn(g * DMA_UNROLL + k)
            return carry
        lax.fori_loop(0, EXP_TILE // DMA_UNROLL, body, 0)

    def on_slot(slot, n, fn):
        if isinstance(slot, int):
            fn(slot)
            return
        for s in range(n):
            pl.when(slot == s)(functools.partial(fn, s))

    def gather_rows(src_ref, slot):
        def run(s):
            def one(r):
                pltpu.make_async_copy(row_tile(h_ref, src_ref[0, 0, r]),
                                      row_tile(xbuf.at[s], r * SUBLANES), gsem.at[s]).start()
            for_each_row(one)
        on_slot(slot, 2, run)

    def scatter_rows(slot):
        def run(s):
            def one(r):
                pltpu.make_async_copy(row_tile(ybuf.at[s], r * SUBLANES),
                                      row_tile(yk_ref, dst_ref[0, 0, r]), ssem.at[s]).start()
            for_each_row(one)
        on_slot(slot, N_YBUF, run)

    def scatter_wait(slot):
        pltpu.make_async_copy(ybuf.at[slot], yk_ref.at[pl.ds(0, tile_rows)], ssem.at[slot]).wait()

    @pl.when(i == 0)
    def _():
        ybuf[0] = jnp.zeros((tile_rows, LANES), F32)
        for spare in range(N_YBUF):
            cp = pltpu.make_async_copy(
                ybuf.at[0], yk_ref.at[pl.ds(n_slot_rows + spare * tile_rows, tile_rows)],
                ssem.at[0])
            cp.start()
            cp.wait()
        gather_rows(src0_ref, 0)

    @pl.when(i + 1 < n_active)
    def _():
        gather_rows(srcn_ref, (i + 1) % 2)

    prev = tile_e[jnp.maximum(i - 1, 0)]
    new_expert = jnp.logical_or(i == 0, tile_e[i] != prev)

    @pl.when(jnp.logical_and(new_expert, i < n_active))
    def _():
        rows = 128

        def cast(r, carry):
            sl = pl.ds(pl.multiple_of(r * rows, rows), rows)
            wgu_bf[sl, :] = wgu_ref[0, sl, :].astype(BF16)
            wd_bf[sl, :] = wd_ref[0, sl, :].astype(BF16)
            return carry

        lax.fori_loop(0, D_MODEL // rows, cast, 0)

    @pl.when(i < n_active)
    def _():
        xslot = i % 2
        yslot = i % N_YBUF
        pltpu.make_async_copy(h_ref.at[pl.ds(0, tile_rows)], xbuf.at[xslot], gsem.at[xslot]).wait()

        @pl.when(i >= N_YBUF)
        def _():
            scatter_wait(yslot)

        x = _load_row_tiles(xbuf.at[xslot], EXP_TILE).astype(BF16)
        gu = _dot(x, wgu_bf[...]) + bgu_ref[0]
        gate = jnp.minimum(gu[:, :D_FF], SWIGLU_LIMIT)
        up = jnp.clip(gu[:, D_FF:], -SWIGLU_LIMIT, SWIGLU_LIMIT)
        act = (up + 1.0) * gate * _sigmoid(SWIGLU_ALPHA * gate)
        _store_row_tiles(ybuf.at[yslot], _dot(act.astype(BF16), wd_bf[...]) + bd_ref[0])
        scatter_rows(yslot)

    @pl.when(i == n_steps - 1)
    def _():
        for back in range(1, N_YBUF + 1):
            scatter_wait((n_active - back) % N_YBUF)


def _experts(h2, src3, dst3, tile_e, n_act, w_gate_up, b_gate_up, w_down, b_down):
    n_tok = h2.shape[0] // SUBLANES
    n_tiles = src3.shape[0]
    idx_block = (1, 1, EXP_TILE)
    grid_spec = pltpu.PrefetchScalarGridSpec(
        num_scalar_prefetch=2,
        grid=(n_tiles,),
        in_specs=[pl.BlockSpec(idx_block, lambda i, te, na: (0, 0, 0), memory_space=pltpu.SMEM),
                  pl.BlockSpec(idx_block, lambda i, te, na: (jnp.minimum(i + 1, n_tiles - 1), 0, 0),
                               memory_space=pltpu.SMEM),
                  pl.BlockSpec(idx_block, lambda i, te, na: (i, 0, 0), memory_space=pltpu.SMEM),
                  pl.BlockSpec(memory_space=pl.ANY),
                  pl.BlockSpec((1, D_MODEL, 2 * D_FF), lambda i, te, na: (te[i], 0, 0)),
                  pl.BlockSpec((1, 1, 2 * D_FF), lambda i, te, na: (te[i], 0, 0)),
                  pl.BlockSpec((1, D_FF, D_MODEL), lambda i, te, na: (te[i], 0, 0)),
                  pl.BlockSpec((1, 1, D_MODEL), lambda i, te, na: (te[i], 0, 0))],
        out_specs=pl.BlockSpec(memory_space=pl.ANY),
        scratch_shapes=[pltpu.VMEM((2, EXP_TILE * SUBLANES, LANES), F32),
                        pltpu.VMEM((N_YBUF, EXP_TILE * SUBLANES, LANES), F32),
                        pltpu.SemaphoreType.DMA((2,)), pltpu.SemaphoreType.DMA((N_YBUF,)),
                        pltpu.VMEM((D_MODEL, 2 * D_FF), BF16), pltpu.VMEM((D_FF, D_MODEL), BF16)],
    )
    return pl.pallas_call(
        _expert_kernel,
        out_shape=jax.ShapeDtypeStruct(((TOP_K * n_tok + N_YBUF * EXP_TILE) * SUBLANES, LANES), F32),
        grid_spec=grid_spec,
        compiler_params=pltpu.CompilerParams(dimension_semantics=("arbitrary",),
                                             vmem_limit_bytes=VMEM_LIMIT),
        name="moe_experts",
    )(tile_e, n_act, src3, src3, dst3, h2, w_gate_up, b_gate_up.reshape(N_EXPERTS, 1, 2 * D_FF),
      w_down, b_down.reshape(N_EXPERTS, 1, D_MODEL))


def _combine_kernel(n_prompt_tiles, alpha, y0_ref, y1_ref, y2_ref, y3_ref, x1_ref, w4_ref, mod_ref,
                    l2g_ref, l2b_ref, yp_ref, ysm_ref):
    i = pl.program_id(0)
    w4 = w4_ref[...]
    ff = _load_row_tiles(y0_ref, TOK_TILE) * w4[:, 0:1]
    for j, y_ref in enumerate((y1_ref, y2_ref, y3_ref), start=1):
        ff = ff + _load_row_tiles(y_ref, TOK_TILE) * w4[:, j:j + 1]
    g2 = mod_ref[0][:, 5 * D_MODEL:6 * D_MODEL]
    out = _layer_norm(alpha * x1_ref[...] + g2 * ff) * l2g_ref[...] + l2b_ref[...]

    @pl.when(i < n_prompt_tiles)
    def _():
        yp_ref[...] = out

    @pl.when(i >= n_prompt_tiles)
    def _():
        ysm_ref[...] = out


def _combine(yk, x1, w4, mod3, ln2_g, ln2_b, n_p, alpha, sample_seq):
    n_tok = x1.shape[0]
    n_s = n_tok - n_p
    npt = n_p // TOK_TILE
    n_tok_tiles = n_tok // TOK_TILE
    tiles_per_sample_seq = sample_seq // TOK_TILE

    def mod_row(i):
        return jnp.where(i < npt, 0, 1 + (i - npt) // tiles_per_sample_seq)

    def slot(j):
        return pl.BlockSpec((TOK_TILE * SUBLANES, LANES), lambda i: (j * n_tok_tiles + i, 0))

    const = lambda i: (0, 0)
    return pl.pallas_call(
        functools.partial(_combine_kernel, npt, alpha),
        out_shape=(jax.ShapeDtypeStruct((n_p, D_MODEL), F32),
                   jax.ShapeDtypeStruct((n_s, D_MODEL), F32)),
        grid=(n_tok_tiles,),
        in_specs=[slot(0), slot(1), slot(2), slot(3),
                  pl.BlockSpec((TOK_TILE, D_MODEL), lambda i: (i, 0)),
                  pl.BlockSpec((TOK_TILE, LANES), lambda i: (i, 0)),
                  pl.BlockSpec((1, 1, 6 * D_MODEL), lambda i: (mod_row(i), 0, 0)),
                  pl.BlockSpec((1, D_MODEL), const), pl.BlockSpec((1, D_MODEL), const)],
        out_specs=(pl.BlockSpec((TOK_TILE, D_MODEL), lambda i: (jnp.minimum(i, npt - 1), 0)),
                   pl.BlockSpec((TOK_TILE, D_MODEL), lambda i: (jnp.maximum(i - npt, 0), 0))),
        compiler_params=pltpu.CompilerParams(dimension_semantics=("arbitrary",),
                                             vmem_limit_bytes=VMEM_LIMIT),
        name="moe_combine",
    )(yk, yk, yk, yk, x1, w4, mod3, ln2_g, ln2_b)


def kernel(x_prompt, x_sample, c, state_fwd, state_bwd, c_ctx, w_ada, b_ada, w_in, conv_w, conv_b,
           conv_ln_g, conv_ln_b, lb_logits, rec_norm_g, w_out, ln1_g, ln1_b, ln2_g, ln2_b,
           router_w, router_b, w_gate_up, b_gate_up, w_down, b_down):
    depth = w_ada.shape[0]
    assert depth == 1
    alpha = (2.0 * depth) ** 0.25
    nb_p, seq_p, _ = x_prompt.shape
    nb_s, seq_s, _ = x_sample.shape
    n_p, n_s = nb_p * seq_p, nb_s * seq_s
    n_tok = n_p + n_s
    row2 = lambda a: a.reshape(1, -1)

    lb = jnp.cumsum(jax.nn.softmax(lb_logits.astype(F32), axis=0), axis=0)[0]

    cond = jnp.concatenate([c_ctx[None, :], c, jnp.zeros((16 - 1 - nb_s, D_MODEL), F32)], axis=0)
    mod = _ada(cond, w_ada[0], b_ada[0])
    mod3 = mod.reshape(16, 1, 6 * D_MODEL)

    xp = x_prompt.reshape(n_p, D_MODEL)
    xs = x_sample.reshape(n_s, D_MODEL)
    conv_w_pad = jnp.concatenate([conv_w[0], jnp.zeros((1, CONV_WIDTH), F32)], axis=0)
    conv_out, rec = _inproj(xp, xs, mod3, w_in[0].astype(BF16), conv_w_pad, row2(conv_b[0]),
                            row2(conv_ln_g[0]), row2(conv_ln_b[0]), seq_p, seq_s)

    zeros_state = jnp.zeros((nb_p, REC_HEADS, REC_DK, REC_DK), F32)
    s0f = jnp.concatenate([zeros_state, jnp.swapaxes(state_fwd[:, 0], -1, -2)], axis=0)
    s0b = jnp.concatenate([zeros_state, jnp.swapaxes(state_bwd[:, 0], -1, -2)], axis=0)
    o_f, o_b, sf_t, sb_t = _scan(rec, lb, s0f, s0b, [seq_p] * nb_p + [seq_s] * nb_s)
    new_f = jnp.swapaxes(sf_t[:nb_p], -1, -2)[:, None]
    new_b = jnp.swapaxes(sb_t[:nb_p], -1, -2)[:, None]

    rw = jnp.pad(router_w[0], ((0, 0), (0, LANES - N_EXPERTS)))
    rw_hi = rw.astype(BF16)
    rw_lo = (rw - rw_hi.astype(F32)).astype(BF16)
    rb = jnp.pad(router_b[0], (0, LANES - N_EXPERTS)).reshape(1, LANES)
    x1, h2, rank_t, slot_t, w4, counts, tile_start = _post(
        o_f, o_b, rec, conv_out, xp, xs, mod3, w_out[0].astype(BF16), row2(rec_norm_g[0]),
        row2(ln1_g[0]), row2(ln1_b[0]), rw_hi, rw_lo, rb, alpha, seq_s)

    cnt = counts[0, :N_EXPERTS].astype(jnp.int32)
    gpad = ((cnt + EXP_TILE - 1) // EXP_TILE) * EXP_TILE
    gend = jnp.cumsum(gpad)
    gstart = gend - gpad
    n_rows = n_tok * TOP_K + N_EXPERTS * EXP_TILE
    n_tiles = n_rows // EXP_TILE
    n_act = gend[-1] // EXP_TILE
    tile_ids = jnp.minimum(jnp.arange(n_tiles, dtype=jnp.int32), n_act - 1)
    tile_e = jnp.sum((gend[None, :] <= tile_ids[:, None] * EXP_TILE).astype(jnp.int32), axis=1)
    rank0 = tile_ids * EXP_TILE - gstart[tile_e]
    starts = tile_start[:, 0, :N_EXPERTS].astype(jnp.int32)
    ends = jnp.concatenate([starts[1:], cnt[None, :]], axis=0)
    starts_t, ends_t = starts.T[tile_e], ends.T[tile_e]
    win0 = jnp.sum((ends_t <= rank0[:, None]).astype(jnp.int32), axis=1)
    win1 = jnp.sum((starts_t < rank0[:, None] + EXP_TILE).astype(jnp.int32), axis=1)
    n_win = jnp.maximum(win1 - win0, 0)
    n_act1 = n_act.reshape(1).astype(jnp.int32)
    cum = jnp.concatenate([starts, cnt[None, :]], axis=0).reshape(-1)
    src, dst = _plan(rank_t, slot_t, tile_e, rank0, win0, n_win, n_act1, cum)

    as_smem_rows = lambda a: a.reshape(n_tiles, 1, EXP_TILE)
    yk = _experts(h2, as_smem_rows(src), as_smem_rows(dst), tile_e, n_act1, w_gate_up[0],
                  b_gate_up[0], w_down[0], b_down[0])
    y_p, y_s = _combine(yk, x1, w4, mod3, row2(ln2_g[0]), row2(ln2_b[0]), n_p, alpha, seq_s)
    return (y_p.reshape(nb_p, seq_p, D_MODEL), y_s.reshape(nb_s, seq_s, D_MODEL), new_f, new_b)
```

```python
import functools

import numpy as np
import jax
import jax.numpy as jnp
from jax import lax
from jax.experimental import pallas as pl
from jax.experimental.pallas import tpu as pltpu

F32 = jnp.float32
BF16 = jnp.bfloat16

D_MODEL = 1024
CONV_WIDTH = 512
CONV_TAPS = 31
REC_HEADS = 4
REC_DK = 128
REC_WIDTH = REC_HEADS * REC_DK
REC_COLS = 5 * REC_WIDTH
CHUNK = 64
N_LEVELS = 6
SCAN_CHUNKS = 4
N_EXPERTS = 32
TOP_K = 4
D_FF = 1024
SWIGLU_LIMIT = 7.0
SWIGLU_ALPHA = 1.702
LN_EPS = 1e-5
RMS_EPS = 1e-6
GRID_W = 64

LANES = 128
TOK_TILE = 256
EXP_TILE = 512
VMEM_LIMIT = 56 * 1024 * 1024


def _sigmoid(x):
    return 1.0 / (1.0 + jnp.exp(-x))


def _layer_norm(x):
    mu = jnp.mean(x, axis=-1, keepdims=True)
    xc = x - mu
    var = jnp.mean(xc * xc, axis=-1, keepdims=True)
    return xc * lax.rsqrt(var + LN_EPS)


def _dot(a, b):
    return jnp.dot(a, b, preferred_element_type=F32)


def _dot_nt(a, b):
    return lax.dot_general(a, b, (((1,), (1,)), ((), ())), preferred_element_type=F32)


SUBLANES = 8
assert D_MODEL == SUBLANES * LANES


def _store_row_tiles(ref, value):
    for s in range(SUBLANES):
        ref[pl.ds(s, value.shape[0], stride=SUBLANES), :] = value[:, s * LANES:(s + 1) * LANES]


def _load_row_tiles(ref, n_rows):
    return jnp.concatenate(
        [ref[pl.ds(s, n_rows, stride=SUBLANES), :] for s in range(SUBLANES)], axis=-1)


def _split3(x):
    hi = x.astype(BF16)
    r1 = x - hi.astype(F32)
    mid = r1.astype(BF16)
    lo = (r1 - mid.astype(F32)).astype(BF16)
    return hi, mid, lo


def _ada_kernel(c_ref, w_ref, b_ref, o_ref):
    c = c_ref[...]
    s = (c * _sigmoid(c)).astype(BF16)
    o_ref[...] = _dot(s, w_ref[...].astype(BF16)) + b_ref[...]


def _ada(cond, w_ada, b_ada):
    rows = cond.shape[0]
    ncol = w_ada.shape[1]
    tn = 1024
    return pl.pallas_call(
        _ada_kernel,
        out_shape=jax.ShapeDtypeStruct((rows, ncol), F32),
        grid=(ncol // tn,),
        in_specs=[pl.BlockSpec((rows, D_MODEL), lambda j: (0, 0)),
                  pl.BlockSpec((D_MODEL, tn), lambda j: (0, j)),
                  pl.BlockSpec((1, tn), lambda j: (0, j))],
        out_specs=pl.BlockSpec((rows, tn), lambda j: (0, j)),
        compiler_params=pltpu.CompilerParams(dimension_semantics=("parallel",),
                                             vmem_limit_bytes=VMEM_LIMIT),
        name="ada_mod",
    )(cond, w_ada, b_ada.reshape(1, ncol))


def _inproj_kernel(n_prompt_tiles, prompt_seg, xp_ref, xs_ref, mod_ref, win_ref, cw_ref, cb_ref,
                   cg_ref, cbeta_ref, conv_ref, rec_ref):
    i = pl.program_id(0)
    is_prompt = i < n_prompt_tiles
    x = jnp.where(is_prompt, xp_ref[...], xs_ref[...])
    mod = mod_ref[0]
    sh1 = mod[:, 0:D_MODEL]
    sc1 = mod[:, D_MODEL:2 * D_MODEL]
    h = (_layer_norm(x) * (1.0 + sc1) + sh1).astype(BF16)

    glu = _dot(h, win_ref[:, 0:2 * CONV_WIDTH])
    u = glu[:, :CONV_WIDTH] * _sigmoid(glu[:, CONV_WIDTH:])

    seg_len = jnp.where(is_prompt, prompt_seg, GRID_W)
    pos = lax.broadcasted_iota(jnp.int32, u.shape, 0) & (seg_len - 1)
    half = CONV_TAPS // 2
    acc = u * cw_ref[half:half + 1, :] + cb_ref[...]
    taps = [j for j in range(CONV_TAPS) if j != half]
    n_groups = REC_COLS // REC_WIDTH
    per_group = len(taps) // n_groups
    anchor = None
    for g in range(n_groups):
        cols = slice(g * REC_WIDTH, (g + 1) * REC_WIDTH)
        rec = _dot(h, win_ref[:, 2 * CONV_WIDTH + g * REC_WIDTH:
                              2 * CONV_WIDTH + (g + 1) * REC_WIDTH])
        rec_ref[:, cols] = rec
        for j in taps[g * per_group:(g + 1) * per_group]:
            d = j - half
            shifted = pltpu.roll(u, (-d) % TOK_TILE, axis=0)
            valid = (pos >= -d) if d < 0 else (pos < seg_len - d)
            w_tap = cw_ref[j:j + 1, :]
            if anchor is not None:
                w_tap = w_tap + anchor
                anchor = None
            acc = acc + jnp.where(valid, shifted, 0.0) * w_tap
        bits = pltpu.bitcast(rec[0:SUBLANES, :], jnp.uint32)
        anchor = pltpu.bitcast((bits >> 16) >> 16, F32)[0:1, :]
    y = _layer_norm(acc) * cg_ref[...] + cbeta_ref[...]
    conv_ref[...] = (y * _sigmoid(y)).astype(BF16)


def _inproj(xp, xs, mod3, w_in_bf, conv_w, conv_b, conv_g, conv_beta, prompt_seq, sample_seq):
    n_p, n_s = xp.shape[0], xs.shape[0]
    n_tok = n_p + n_s
    npt = n_p // TOK_TILE
    tiles_per_sample_seq = sample_seq // TOK_TILE
    assert prompt_seq == TOK_TILE and sample_seq % TOK_TILE == 0

    def mod_row(i):
        return jnp.where(i < npt, 0, 1 + (i - npt) // tiles_per_sample_seq)

    const = lambda i: (0, 0)
    return pl.pallas_call(
        functools.partial(_inproj_kernel, npt, prompt_seq),
        out_shape=(jax.ShapeDtypeStruct((n_tok, CONV_WIDTH), BF16),
                   jax.ShapeDtypeStruct((n_tok, REC_COLS), F32)),
        grid=(n_tok // TOK_TILE,),
        in_specs=[pl.BlockSpec((TOK_TILE, D_MODEL), lambda i: (jnp.minimum(i, npt - 1), 0)),
                  pl.BlockSpec((TOK_TILE, D_MODEL), lambda i: (jnp.maximum(i - npt, 0), 0)),
                  pl.BlockSpec((1, 1, 6 * D_MODEL), lambda i: (mod_row(i), 0, 0)),
                  pl.BlockSpec(w_in_bf.shape, const),
                  pl.BlockSpec(conv_w.shape, const),
                  pl.BlockSpec((1, CONV_WIDTH), const),
                  pl.BlockSpec((1, CONV_WIDTH), const),
                  pl.BlockSpec((1, CONV_WIDTH), const)],
        out_specs=(pl.BlockSpec((TOK_TILE, CONV_WIDTH), lambda i: (i, 0)),
                   pl.BlockSpec((TOK_TILE, REC_COLS), lambda i: (i, 0))),
        compiler_params=pltpu.CompilerParams(dimension_semantics=("parallel",),
                                             vmem_limit_bytes=VMEM_LIMIT),
        name="inproj_conv",
    )(xp, xs, mod3, w_in_bf, conv_w, conv_b, conv_g, conv_beta)


def _scan_constants():
    c = CHUNK
    w = np.zeros((N_LEVELS + 2, c, c), np.float32)
    m = np.zeros((N_LEVELS + 1, c, c), np.float32)
    t = np.arange(c)
    for j in range(N_LEVELS):
        half = 1 << j
        for ti in range(c):
            ref = (ti & ~(2 * half - 1)) + half - 1
            if ti & half:
                w[j, ti, ref + 1:ti + 1] = 1.0
            else:
                w[j, ti, ti + 1:ref + 1] = 1.0
        upper = (t[:, None] & half) != 0
        lower = (t[None, :] & half) == 0
        same = (t[:, None] >> (j + 1)) == (t[None, :] >> (j + 1))
        m[j] = (upper & lower & same).astype(np.float32)
    w[N_LEVELS] = (t[None, :] <= t[:, None]).astype(np.float32)
    w[N_LEVELS + 1] = (t[None, :] > t[:, None]).astype(np.float32)
    m[N_LEVELS] = np.eye(c, dtype=np.float32)
    w_f = w.reshape(-1, c)
    w_b = w[:, ::-1, ::-1].reshape(-1, c)
    m_b = m[:, ::-1, ::-1]
    return (jnp.asarray(w_f, BF16), jnp.asarray(np.ascontiguousarray(w_b), BF16),
            jnp.asarray(m), jnp.asarray(np.ascontiguousarray(m_b)))


def _scan_gates(z, lb, w_ref):
    e = jnp.exp(-jnp.abs(z))
    r = 1.0 / (1.0 + e)
    a = e * r
    sig = jnp.where(z >= 0, r, a)
    sig_neg = jnp.where(z >= 0, a, r)
    one_m_lb = 1.0 - lb
    log_f = jnp.log(lb + one_m_lb * sig)
    k = one_m_lb * sig_neg
    hi, mid, lo = _split3(log_f)
    w = w_ref[...]
    ex = jnp.exp(_dot(w, hi) + _dot(w, mid) + _dot(w, lo))
    return k, ex


def _scan_intra(q, k, ex, m_ref):
    c = CHUNK
    a_mat = m_ref[N_LEVELS] * _dot_nt(q.astype(BF16), k.astype(BF16))
    for j in range(N_LEVELS):
        xj = ex[j * c:(j + 1) * c]
        a_mat = a_mat + m_ref[j] * _dot_nt((q * xj).astype(BF16), (k * xj).astype(BF16))
    return a_mat


def _scan_kernel(fblk, bblk, first, seq, qf_ref, zf_ref, vf_ref, qb_ref, zb_ref, vb_ref, lb_ref,
                 wf_ref, wb_ref, mf_ref, mb_ref, s0f_ref, s0b_ref, of_ref, ob_ref, sf_ref, sb_ref):
    s = pl.program_id(0)

    @pl.when(first[s] == 1)
    def _():
        sf_ref[...] = s0f_ref[...]
        sb_ref[...] = s0b_ref[...]

    c = CHUNK
    dirs = ((qf_ref, zf_ref, vf_ref, 0, wf_ref, mf_ref, sf_ref, of_ref, c - 1,
             tuple(range(SCAN_CHUNKS))),
            (qb_ref, zb_ref, vb_ref, 1, wb_ref, mb_ref, sb_ref, ob_ref, 0,
             tuple(reversed(range(SCAN_CHUNKS)))))
    units = [(d, slice(sub * c, (sub + 1) * c)) for d in dirs for sub in d[-1]]
    gates = [_scan_gates(d[1][rows, :], lb_ref[d[3]:d[3] + 1, :], d[4]) for d, rows in units]
    intra = []
    for (d, rows), (k, ex) in zip(units, gates):
        for h in range(REC_HEADS):
            cols = slice(h * REC_DK, (h + 1) * REC_DK)
            intra.append(_scan_intra(d[0][rows, cols], k[:, cols], ex[:, cols], d[5]))
    idx = 0
    for (d, rows), (k, ex) in zip(units, gates):
        q_ref, _, v_ref, _, _, _, st_ref, o_ref, last, _ = d
        for h in range(REC_HEADS):
            cols = slice(h * REC_DK, (h + 1) * REC_DK)
            q, v = q_ref[rows, cols], v_ref[rows, cols]
            ex_b = ex[N_LEVELS * c:(N_LEVELS + 1) * c, cols]
            ex_l = ex[(N_LEVELS + 1) * c:(N_LEVELS + 2) * c, cols]
            st = st_ref[0, h]
            o_ref[rows, cols] = (_dot(intra[idx].astype(BF16), v.astype(BF16))
                                 + _dot_nt((q * ex_b).astype(BF16), st.astype(BF16)))
            st_ref[0, h] = (st * ex_b[last:last + 1, :]
                            + _dot(v.T.astype(BF16), (k[:, cols] * ex_l).astype(BF16)))
            idx += 1


def _scan(rec, lb, s0f_t, s0b_t, seq_lens):
    n_tok = rec.shape[0]
    fblk, bblk, first, seq = [], [], [], []
    base = 0
    step_rows = CHUNK * SCAN_CHUNKS
    for si, ln in enumerate(seq_lens):
        assert ln % step_rows == 0
        n = ln // step_rows
        for ci in range(n):
            fblk.append(base + ci)
            bblk.append(base + n - 1 - ci)
            first.append(1 if ci == 0 else 0)
            seq.append(si)
        base += n
    steps = len(fblk)
    as_i32 = lambda a: jnp.asarray(np.asarray(a, np.int32))
    w_f, w_b, m_f, m_b = _scan_constants()
    n_seq = len(seq_lens)

    def col(block_of, c):
        return pl.BlockSpec((step_rows, REC_WIDTH), lambda s, fb, bb, fi, sq: (block_of(fb, bb)[s], c))

    fwd = lambda fb, bb: fb
    bwd = lambda fb, bb: bb
    const2 = lambda s, fb, bb, fi, sq: (0, 0)
    const3 = lambda s, fb, bb, fi, sq: (0, 0, 0)
    state_spec = pl.BlockSpec((1, REC_HEADS, REC_DK, REC_DK), lambda s, fb, bb, fi, sq: (sq[s], 0, 0, 0))
    grid_spec = pltpu.PrefetchScalarGridSpec(
        num_scalar_prefetch=4,
        grid=(steps,),
        in_specs=[col(fwd, 0), col(fwd, 1), col(fwd, 3), col(bwd, 0), col(bwd, 2), col(bwd, 3),
                  pl.BlockSpec((2, REC_WIDTH), const2),
                  pl.BlockSpec(w_f.shape, const2), pl.BlockSpec(w_b.shape, const2),
                  pl.BlockSpec(m_f.shape, const3), pl.BlockSpec(m_b.shape, const3),
                  state_spec, state_spec],
        out_specs=(pl.BlockSpec((step_rows, REC_WIDTH), lambda s, fb, bb, fi, sq: (fb[s], 0)),
                   pl.BlockSpec((step_rows, REC_WIDTH), lambda s, fb, bb, fi, sq: (bb[s], 0)),
                   state_spec, state_spec),
    )
    st_shape = jax.ShapeDtypeStruct((n_seq, REC_HEADS, REC_DK, REC_DK), F32)
    return pl.pallas_call(
        _scan_kernel,
        out_shape=(jax.ShapeDtypeStruct((n_tok, REC_WIDTH), F32),
                   jax.ShapeDtypeStruct((n_tok, REC_WIDTH), F32), st_shape, st_shape),
        grid_spec=grid_spec,
        compiler_params=pltpu.CompilerParams(dimension_semantics=("arbitrary",),
                                             vmem_limit_bytes=VMEM_LIMIT),
        name="hgrn2_scan",
    )(as_i32(fblk), as_i32(bblk), as_i32(first), as_i32(seq),
      rec, rec, rec, rec, rec, rec, lb, w_f, w_b, m_f, m_b, s0f_t, s0b_t)


def _post_kernel(n_prompt_tiles, alpha, of_ref, ob_ref, g_ref, conv_ref, xp_ref, xs_ref, mod_ref,
                 wout_ref, rg_ref, l1g_ref, l1b_ref, rwh_ref, rwl_ref, rb_ref, tri_ref,
                 x1_ref, h2_ref, rank_t_ref, slot_t_ref, w4_ref, cnt_ref, start_ref, run_ref):
    i = pl.program_id(0)

    @pl.when(i == 0)
    def _():
        run_ref[...] = jnp.zeros_like(run_ref)

    x = jnp.where(i < n_prompt_tiles, xp_ref[...], xs_ref[...])
    mod = mod_ref[0]
    g1 = mod[:, 2 * D_MODEL:3 * D_MODEL]
    sh2 = mod[:, 3 * D_MODEL:4 * D_MODEL]
    sc2 = mod[:, 4 * D_MODEL:5 * D_MODEL]

    g = g_ref[...]
    silu_g = g * _sigmoid(g)
    mix = _dot(conv_ref[...], wout_ref[0:CONV_WIDTH, :])
    for h in range(REC_HEADS):
        cols = slice(h * REC_DK, (h + 1) * REC_DK)
        o = of_ref[:, cols] + ob_ref[:, cols]
        o = o * lax.rsqrt(jnp.mean(o * o, axis=-1, keepdims=True) + RMS_EPS)
        rec_out = (o * rg_ref[:, cols] * silu_g[:, cols]).astype(BF16)
        mix = mix + _dot(rec_out, wout_ref[CONV_WIDTH + h * REC_DK:CONV_WIDTH + (h + 1) * REC_DK, :])

    x1 = _layer_norm(alpha * x + g1 * mix) * l1g_ref[...] + l1b_ref[...]
    x1_ref[...] = x1
    h2 = _layer_norm(x1) * (1.0 + sc2) + sh2
    _store_row_tiles(h2_ref, h2)

    h_hi = h2.astype(BF16)
    h_lo = (h2 - h_hi.astype(F32)).astype(BF16)
    logits = (_dot(h_hi, rwh_ref[...]) + _dot(h_lo, rwh_ref[...]) + _dot(h_hi, rwl_ref[...])
              + rb_ref[...])
    lane = lax.broadcasted_iota(jnp.int32, logits.shape, 1)
    lane_f = lane.astype(F32)
    neg_inf = jnp.float32(-jnp.inf)
    work = jnp.where(lane < N_EXPERTS, logits, neg_inf)
    vals, sels = [], []
    for _ in range(TOP_K):
        m = jnp.max(work, axis=-1, keepdims=True)
        idx = jnp.min(jnp.where(work == m, lane_f, float(LANES)), axis=-1, keepdims=True)
        sel = lane_f == idx
        vals.append(m)
        sels.append(sel)
        work = jnp.where(sel, neg_inf, work)
    exps = [jnp.exp(v - vals[0]) for v in vals]
    denom = exps[0] + exps[1] + exps[2] + exps[3]

    sel_any = jnp.zeros(logits.shape, F32)
    for sel in sels:
        sel_any = jnp.where(sel, 1.0, sel_any)
    run_old = run_ref[...]
    prior = _dot(tri_ref[...], sel_any.astype(BF16)) + run_old
    run_new = run_old + jnp.sum(sel_any, axis=0, keepdims=True)
    run_ref[...] = run_new
    cnt_ref[...] = run_new
    start_ref[0] = run_old

    slot = jnp.zeros(logits.shape, F32)
    w4 = jnp.zeros(logits.shape, F32)
    for j in range(TOP_K):
        slot = jnp.where(sels[j], float(j), slot)
        w4 = jnp.where(lane == j, exps[j] / denom, w4)
    routed = sel_any > 0.0
    rank_t_ref[...] = jnp.where(routed, prior, -1.0).T[:N_EXPERTS]
    slot_t_ref[...] = jnp.where(routed, slot, -1.0).T[:N_EXPERTS]
    w4_ref[...] = w4


def _post(o_f, o_b, rec, conv, xp, xs, mod3, w_out_bf, rec_g, ln1_g, ln1_b, rw_hi, rw_lo, rb,
          alpha, sample_seq):
    n_p, n_s = xp.shape[0], xs.shape[0]
    n_tok = n_p + n_s
    npt = n_p // TOK_TILE
    tiles_per_sample_seq = sample_seq // TOK_TILE
    tri = jnp.asarray(np.tril(np.ones((TOK_TILE, TOK_TILE), np.float32), -1), BF16)

    def mod_row(i):
        return jnp.where(i < npt, 0, 1 + (i - npt) // tiles_per_sample_seq)

    const = lambda i: (0, 0)
    tok = lambda w: pl.BlockSpec((TOK_TILE, w), lambda i: (i, 0))
    return pl.pallas_call(
        functools.partial(_post_kernel, npt, alpha),
        out_shape=(jax.ShapeDtypeStruct((n_tok, D_MODEL), F32),
                   jax.ShapeDtypeStruct((n_tok * SUBLANES, LANES), F32),
                   jax.ShapeDtypeStruct((N_EXPERTS, n_tok), F32),
                   jax.ShapeDtypeStruct((N_EXPERTS, n_tok), F32),
                   jax.ShapeDtypeStruct((n_tok, LANES), F32),
                   jax.ShapeDtypeStruct((1, LANES), F32),
                   jax.ShapeDtypeStruct((n_tok // TOK_TILE, 1, LANES), F32)),
        grid=(n_tok // TOK_TILE,),
        in_specs=[tok(REC_WIDTH), tok(REC_WIDTH),
                  pl.BlockSpec((TOK_TILE, REC_WIDTH), lambda i: (i, 4)),
                  tok(CONV_WIDTH),
                  pl.BlockSpec((TOK_TILE, D_MODEL), lambda i: (jnp.minimum(i, npt - 1), 0)),
                  pl.BlockSpec((TOK_TILE, D_MODEL), lambda i: (jnp.maximum(i - npt, 0), 0)),
                  pl.BlockSpec((1, 1, 6 * D_MODEL), lambda i: (mod_row(i), 0, 0)),
                  pl.BlockSpec(w_out_bf.shape, const),
                  pl.BlockSpec((1, REC_WIDTH), const),
                  pl.BlockSpec((1, D_MODEL), const), pl.BlockSpec((1, D_MODEL), const),
                  pl.BlockSpec((D_MODEL, LANES), const), pl.BlockSpec((D_MODEL, LANES), const),
                  pl.BlockSpec((1, LANES), const),
                  pl.BlockSpec((TOK_TILE, TOK_TILE), const)],
        out_specs=(tok(D_MODEL), pl.BlockSpec((TOK_TILE * SUBLANES, LANES), lambda i: (i, 0)),
                   pl.BlockSpec((N_EXPERTS, TOK_TILE), lambda i: (0, i)),
                   pl.BlockSpec((N_EXPERTS, TOK_TILE), lambda i: (0, i)),
                   tok(LANES), pl.BlockSpec((1, LANES), const),
                   pl.BlockSpec((1, 1, LANES), lambda i: (i, 0, 0))),
        scratch_shapes=[pltpu.VMEM((1, LANES), F32)],
        compiler_params=pltpu.CompilerParams(dimension_semantics=("arbitrary",),
                                             vmem_limit_bytes=VMEM_LIMIT),
        name="post_mixer_router",
    )(o_f, o_b, rec, conv, xp, xs, mod3, w_out_bf, rec_g, ln1_g, ln1_b, rw_hi, rw_lo, rb, tri)


PLAN_ROWS = 32


def _plan_kernel(n_tok, tile_e, rank0, win0, n_win, n_act, cum, rank_t_ref, slot_t_ref, src_ref,
                 dst_ref, acc_ref):
    i = pl.program_id(0)

    @pl.when(i < n_act[0])
    def _():
        e = tile_e[i]
        base = rank0[i]
        row = lax.broadcasted_iota(jnp.int32, (EXP_TILE, 1), 0)
        rows_iota = lax.broadcasted_iota(jnp.int32, (PLAN_ROWS, 1), 0)
        lane = lax.broadcasted_iota(jnp.int32, (1, TOK_TILE), 1)
        acc_ref[...] = jnp.zeros_like(acc_ref)

        def window(w, carry):
            b = win0[i] + w
            tok0 = pl.multiple_of(b * TOK_TILE, TOK_TILE)
            rk = rank_t_ref[pl.ds(e, 1), pl.ds(tok0, TOK_TILE)]
            sl = slot_t_ref[pl.ds(e, 1), pl.ds(tok0, TOK_TILE)]
            code = ((tok0 + lane) * TOP_K).astype(F32) + sl + 1.0
            lo = jnp.maximum(cum[b * N_EXPERTS + e] - base, 0)
            hi = jnp.minimum(cum[(b + 1) * N_EXPERTS + e] - base, EXP_TILE)

            def rows(k, c):
                r0 = pl.multiple_of(k * PLAN_ROWS, PLAN_ROWS)
                want = (base + r0 + rows_iota).astype(F32)
                hit = jnp.where(rk == want, code, 0.0)
                part = hit[:, 0:LANES]
                for c0 in range(LANES, TOK_TILE, LANES):
                    part = part + hit[:, c0:c0 + LANES]
                acc_ref[pl.ds(r0, PLAN_ROWS), :] += part
                return c

            lax.fori_loop(lo // PLAN_ROWS, (hi + PLAN_ROWS - 1) // PLAN_ROWS, rows, 0)
            return carry

        lax.fori_loop(0, n_win[i], window, 0)
        code = jnp.sum(acc_ref[...], axis=-1, keepdims=True).astype(jnp.int32) - 1
        tok = code >> 2
        src = jnp.where(code >= 0, tok, lax.rem(i * EXP_TILE + row, n_tok))
        spare = TOP_K * n_tok + lax.rem(i, N_YBUF) * EXP_TILE + row
        dst = jnp.where(code >= 0, (code & (TOP_K - 1)) * n_tok + tok, spare)
        src_ref[0] = src * SUBLANES
        dst_ref[0] = dst * SUBLANES

    @pl.when(i >= n_act[0])
    def _():
        src_ref[...] = jnp.zeros_like(src_ref)
        dst_ref[...] = jnp.zeros_like(dst_ref)


def _plan(rank_t, slot_t, tile_e, rank0, win0, n_win, n_act, cum):
    n_tok = rank_t.shape[1]
    n_tiles = tile_e.shape[0]
    whole = lambda i, *_: (0, 0)
    per_tile = pl.BlockSpec((1, EXP_TILE, 1), lambda i, *_: (i, 0, 0))
    grid_spec = pltpu.PrefetchScalarGridSpec(
        num_scalar_prefetch=6,
        grid=(n_tiles,),
        in_specs=[pl.BlockSpec(rank_t.shape, whole), pl.BlockSpec(slot_t.shape, whole)],
        out_specs=(per_tile, per_tile),
        scratch_shapes=[pltpu.VMEM((EXP_TILE, LANES), F32)],
    )
    shape = jax.ShapeDtypeStruct((n_tiles, EXP_TILE, 1), jnp.int32)
    return pl.pallas_call(
        functools.partial(_plan_kernel, n_tok),
        out_shape=(shape, shape),
        grid_spec=grid_spec,
        compiler_params=pltpu.CompilerParams(dimension_semantics=("arbitrary",),
                                             vmem_limit_bytes=VMEM_LIMIT),
        name="moe_plan",
    )(tile_e, rank0, win0, n_win, n_act, cum, rank_t, slot_t)


N_YBUF = 3
DMA_UNROLL = 8


def _expert_kernel(tile_e, n_act, src0_ref, srcn_ref, dst_ref, dstp_ref, h_ref, wgu_ref, bgu_ref,
                   wd_ref, bd_ref, yk_ref, xbuf, ybuf, gsem, ssem, wgu_bf, wd_bf):
    i = pl.program_id(0)
    n_steps = pl.num_programs(0)
    n_active = n_act[0]
    tile_rows = EXP_TILE * SUBLANES
    n_slot_rows = yk_ref.shape[0] - N_YBUF * tile_rows

    def row_tile(ref, start):
        return ref.at[pl.ds(pl.multiple_of(start, SUBLANES), SUBLANES)]

    def for_each_row(fn):
        def body(g, carry):
            for k in range(DMA_UNROLL):
                fn(g * DMA_UNROLL + k)
            return carry
        lax.fori_loop(0, EXP_TILE // DMA_UNROLL, body, 0)

    def on_slot(slot, n, fn):
        if isinstance(slot, int):
            fn(slot)
            return
        for s in range(n):
            pl.when(slot == s)(functools.partial(fn, s))

    def gather_rows(src_ref, slot):
        def run(s):
            def one(r):
                pltpu.make_async_copy(row_tile(h_ref, src_ref[0, 0, r]),
                                      row_tile(xbuf.at[s], r * SUBLANES), gsem.at[s]).start()
            for_each_row(one)
        on_slot(slot, 2, run)

    def scatter_rows(slot):
        def run(s):
            def one(r):
                pltpu.make_async_copy(row_tile(ybuf.at[s], r * SUBLANES),
                                      row_tile(yk_ref, dst_ref[0, 0, r]), ssem.at[s]).start()
            for_each_row(one)
        on_slot(slot, N_YBUF, run)

    def scatter_wait(slot):
        pltpu.make_async_copy(ybuf.at[slot], yk_ref.at[pl.ds(0, tile_rows)], ssem.at[slot]).wait()

    @pl.when(i == 0)
    def _():
        for s in range(N_YBUF):
            ybuf[s] = jnp.zeros((tile_rows, LANES), F32)
        for s in range(N_YBUF - 1):
            pltpu.make_async_copy(
                ybuf.at[s], yk_ref.at[pl.ds(n_slot_rows + s * tile_rows, tile_rows)],
                ssem.at[s]).start()
        gather_rows(src0_ref, 0)

    prev = tile_e[jnp.maximum(i - 1, 0)]
    new_expert = jnp.logical_or(i == 0, tile_e[i] != prev)

    @pl.when(jnp.logical_and(new_expert, i < n_active))
    def _():
        rows = 128

        def cast(r, carry):
            sl = pl.ds(pl.multiple_of(r * rows, rows), rows)
            wgu_bf[sl, :] = wgu_ref[0, sl, :].astype(BF16)
            wd_bf[sl, :] = wd_ref[0, sl, :].astype(BF16)
            return carry

        lax.fori_loop(0, D_MODEL // rows, cast, 0)

    @pl.when(i < n_active)
    def _():
        xslot = i % 2
        yslot = i % N_YBUF
        pltpu.make_async_copy(h_ref.at[pl.ds(0, tile_rows)], xbuf.at[xslot], gsem.at[xslot]).wait()

        scatter_wait(yslot)
        x = _load_row_tiles(xbuf.at[xslot], EXP_TILE).astype(BF16)
        next_x = xbuf.at[(i + 1) % 2]
        next_sem = gsem.at[(i + 1) % 2]
        prev_y = ybuf.at[(i + N_YBUF - 1) % N_YBUF]
        prev_sem = ssem.at[(i + N_YBUF - 1) % N_YBUF]
        gu = _dot(x, wgu_bf[...]) + bgu_ref[0]
        for r in range(EXP_TILE):
            pltpu.make_async_copy(row_tile(h_ref, srcn_ref[0, 0, r]),
                                  row_tile(next_x, r * SUBLANES), next_sem).start()
        gate = jnp.minimum(gu[:, :D_FF], SWIGLU_LIMIT)
        up = jnp.clip(gu[:, D_FF:], -SWIGLU_LIMIT, SWIGLU_LIMIT)
        act = (up + 1.0) * gate * _sigmoid(SWIGLU_ALPHA * gate)
        for r in range(EXP_TILE):
            pltpu.make_async_copy(row_tile(prev_y, r * SUBLANES),
                                  row_tile(yk_ref, dstp_ref[0, 0, r]), prev_sem).start()
        _store_row_tiles(ybuf.at[yslot], _dot(act.astype(BF16), wd_bf[...]) + bd_ref[0])

        @pl.when(i == n_active - 1)
        def _():
            scatter_rows(yslot)

    @pl.when(i == n_steps - 1)
    def _():
        for back in range(1, N_YBUF + 1):
            scatter_wait((n_active - back) % N_YBUF)
        last_x = n_active % 2
        pltpu.make_async_copy(h_ref.at[pl.ds(0, tile_rows)], xbuf.at[last_x], gsem.at[last_x]).wait()


def _experts(h2, src3, dst3, tile_e, n_act, w_gate_up, b_gate_up, w_down, b_down):
    n_tok = h2.shape[0] // SUBLANES
    n_tiles = src3.shape[0]
    idx_block = (1, 1, EXP_TILE)
    spare_rows = TOP_K * n_tok + (N_YBUF - 1) * EXP_TILE + jnp.arange(EXP_TILE, dtype=jnp.int32)
    dst_prev3 = jnp.concatenate([(spare_rows * SUBLANES).reshape(idx_block), dst3[:-1]], axis=0)
    grid_spec = pltpu.PrefetchScalarGridSpec(
        num_scalar_prefetch=2,
        grid=(n_tiles,),
        in_specs=[pl.BlockSpec(idx_block, lambda i, te, na: (0, 0, 0), memory_space=pltpu.SMEM),
                  pl.BlockSpec(idx_block, lambda i, te, na: (jnp.minimum(i + 1, n_tiles - 1), 0, 0),
                               memory_space=pltpu.SMEM),
                  pl.BlockSpec(idx_block, lambda i, te, na: (i, 0, 0), memory_space=pltpu.SMEM),
                  pl.BlockSpec(idx_block, lambda i, te, na: (i, 0, 0), memory_space=pltpu.SMEM),
                  pl.BlockSpec(memory_space=pl.ANY),
                  pl.BlockSpec((1, D_MODEL, 2 * D_FF), lambda i, te, na: (te[i], 0, 0)),
                  pl.BlockSpec((1, 1, 2 * D_FF), lambda i, te, na: (te[i], 0, 0)),
                  pl.BlockSpec((1, D_FF, D_MODEL), lambda i, te, na: (te[i], 0, 0)),
                  pl.BlockSpec((1, 1, D_MODEL), lambda i, te, na: (te[i], 0, 0))],
        out_specs=pl.BlockSpec(memory_space=pl.ANY),
        scratch_shapes=[pltpu.VMEM((2, EXP_TILE * SUBLANES, LANES), F32),
                        pltpu.VMEM((N_YBUF, EXP_TILE * SUBLANES, LANES), F32),
                        pltpu.SemaphoreType.DMA((2,)), pltpu.SemaphoreType.DMA((N_YBUF,)),
                        pltpu.VMEM((D_MODEL, 2 * D_FF), BF16), pltpu.VMEM((D_FF, D_MODEL), BF16)],
    )
    return pl.pallas_call(
        _expert_kernel,
        out_shape=jax.ShapeDtypeStruct(((TOP_K * n_tok + N_YBUF * EXP_TILE) * SUBLANES, LANES), F32),
        grid_spec=grid_spec,
        compiler_params=pltpu.CompilerParams(dimension_semantics=("arbitrary",),
                                             vmem_limit_bytes=VMEM_LIMIT),
        name="moe_experts",
    )(tile_e, n_act, src3, src3, dst3, dst_prev3, h2, w_gate_up,
      b_gate_up.reshape(N_EXPERTS, 1, 2 * D_FF), w_down, b_down.reshape(N_EXPERTS, 1, D_MODEL))


def _combine_kernel(n_prompt_tiles, alpha, y0_ref, y1_ref, y2_ref, y3_ref, x1_ref, w4_ref, mod_ref,
                    l2g_ref, l2b_ref, yp_ref, ysm_ref):
    i = pl.program_id(0)
    w4 = w4_ref[...]
    ff = _load_row_tiles(y0_ref, TOK_TILE) * w4[:, 0:1]
    for j, y_ref in enumerate((y1_ref, y2_ref, y3_ref), start=1):
        ff = ff + _load_row_tiles(y_ref, TOK_TILE) * w4[:, j:j + 1]
    g2 = mod_ref[0][:, 5 * D_MODEL:6 * D_MODEL]
    out = _layer_norm(alpha * x1_ref[...] + g2 * ff) * l2g_ref[...] + l2b_ref[...]

    @pl.when(i < n_prompt_tiles)
    def _():
        yp_ref[...] = out

    @pl.when(i >= n_prompt_tiles)
    def _():
        ysm_ref[...] = out


def _combine(yk, x1, w4, mod3, ln2_g, ln2_b, n_p, alpha, sample_seq):
    n_tok = x1.shape[0]
    n_s = n_tok - n_p
    npt = n_p // TOK_TILE
    n_tok_tiles = n_tok // TOK_TILE
    tiles_per_sample_seq = sample_seq // TOK_TILE

    def mod_row(i):
        return jnp.where(i < npt, 0, 1 + (i - npt) // tiles_per_sample_seq)

    def slot(j):
        return pl.BlockSpec((TOK_TILE * SUBLANES, LANES), lambda i: (j * n_tok_tiles + i, 0))

    const = lambda i: (0, 0)
    return pl.pallas_call(
        functools.partial(_combine_kernel, npt, alpha),
        out_shape=(jax.ShapeDtypeStruct((n_p, D_MODEL), F32),
                   jax.ShapeDtypeStruct((n_s, D_MODEL), F32)),
        grid=(n_tok_tiles,),
        in_specs=[slot(0), slot(1), slot(2), slot(3),
                  pl.BlockSpec((TOK_TILE, D_MODEL), lambda i: (i, 0)),
                  pl.BlockSpec((TOK_TILE, LANES), lambda i: (i, 0)),
                  pl.BlockSpec((1, 1, 6 * D_MODEL), lambda i: (mod_row(i), 0, 0)),
                  pl.BlockSpec((1, D_MODEL), const), pl.BlockSpec((1, D_MODEL), const)],
        out_specs=(pl.BlockSpec((TOK_TILE, D_MODEL), lambda i: (jnp.minimum(i, npt - 1), 0)),
                   pl.BlockSpec((TOK_TILE, D_MODEL), lambda i: (jnp.maximum(i - npt, 0), 0))),
        compiler_params=pltpu.CompilerParams(dimension_semantics=("arbitrary",),
                                             vmem_limit_bytes=VMEM_LIMIT),
        name="moe_combine",
    )(yk, yk, yk, yk, x1, w4, mod3, ln2_g, ln2_b)


def kernel(x_prompt, x_sample, c, state_fwd, state_bwd, c_ctx, w_ada, b_ada, w_in, conv_w, conv_b,
           conv_ln_g, conv_ln_b, lb_logits, rec_norm_g, w_out, ln1_g, ln1_b, ln2_g, ln2_b,
           router_w, router_b, w_gate_up, b_gate_up, w_down, b_down):
    depth = w_ada.shape[0]
    assert depth == 1
    alpha = (2.0 * depth) ** 0.25
    nb_p, seq_p, _ = x_prompt.shape
    nb_s, seq_s, _ = x_sample.shape
    n_p, n_s = nb_p * seq_p, nb_s * seq_s
    n_tok = n_p + n_s
    row2 = lambda a: a.reshape(1, -1)

    lb = jnp.cumsum(jax.nn.softmax(lb_logits.astype(F32), axis=0), axis=0)[0]

    cond = jnp.concatenate([c_ctx[None, :], c, jnp.zeros((16 - 1 - nb_s, D_MODEL), F32)], axis=0)
    mod = _ada(cond, w_ada[0], b_ada[0])
    mod3 = mod.reshape(16, 1, 6 * D_MODEL)

    xp = x_prompt.reshape(n_p, D_MODEL)
    xs = x_sample.reshape(n_s, D_MODEL)
    conv_w_pad = jnp.concatenate([conv_w[0], jnp.zeros((1, CONV_WIDTH), F32)], axis=0)
    conv_out, rec = _inproj(xp, xs, mod3, w_in[0].astype(BF16), conv_w_pad, row2(conv_b[0]),
                            row2(conv_ln_g[0]), row2(conv_ln_b[0]), seq_p, seq_s)

    zeros_state = jnp.zeros((nb_p, REC_HEADS, REC_DK, REC_DK), F32)
    s0f = jnp.concatenate([zeros_state, jnp.swapaxes(state_fwd[:, 0], -1, -2)], axis=0)
    s0b = jnp.concatenate([zeros_state, jnp.swapaxes(state_bwd[:, 0], -1, -2)], axis=0)
    o_f, o_b, sf_t, sb_t = _scan(rec, lb, s0f, s0b, [seq_p] * nb_p + [seq_s] * nb_s)
    new_f = jnp.swapaxes(sf_t[:nb_p], -1, -2)[:, None]
    new_b = jnp.swapaxes(sb_t[:nb_p], -1, -2)[:, None]

    rw = jnp.pad(router_w[0], ((0, 0), (0, LANES - N_EXPERTS)))
    rw_hi = rw.astype(BF16)
    rw_lo = (rw - rw_hi.astype(F32)).astype(BF16)
    rb = jnp.pad(router_b[0], (0, LANES - N_EXPERTS)).reshape(1, LANES)
    x1, h2, rank_t, slot_t, w4, counts, tile_start = _post(
        o_f, o_b, rec, conv_out, xp, xs, mod3, w_out[0].astype(BF16), row2(rec_norm_g[0]),
        row2(ln1_g[0]), row2(ln1_b[0]), rw_hi, rw_lo, rb, alpha, seq_s)

    cnt = counts[0, :N_EXPERTS].astype(jnp.int32)
    gpad = ((cnt + EXP_TILE - 1) // EXP_TILE) * EXP_TILE
    gend = jnp.cumsum(gpad)
    gstart = gend - gpad
    n_rows = n_tok * TOP_K + N_EXPERTS * EXP_TILE
    n_tiles = n_rows // EXP_TILE
    n_act = gend[-1] // EXP_TILE
    tile_ids = jnp.minimum(jnp.arange(n_tiles, dtype=jnp.int32), n_act - 1)
    tile_e = jnp.sum((gend[None, :] <= tile_ids[:, None] * EXP_TILE).astype(jnp.int32), axis=1)
    rank0 = tile_ids * EXP_TILE - gstart[tile_e]
    starts = tile_start[:, 0, :N_EXPERTS].astype(jnp.int32)
    ends = jnp.concatenate([starts[1:], cnt[None, :]], axis=0)
    starts_t, ends_t = starts.T[tile_e], ends.T[tile_e]
    win0 = jnp.sum((ends_t <= rank0[:, None]).astype(jnp.int32), axis=1)
    win1 = jnp.sum((starts_t < rank0[:, None] + EXP_TILE).astype(jnp.int32), axis=1)
    n_win = jnp.maximum(win1 - win0, 0)
    n_act1 = n_act.reshape(1).astype(jnp.int32)
    cum = jnp.concatenate([starts, cnt[None, :]], axis=0).reshape(-1)
    src, dst = _plan(rank_t, slot_t, tile_e, rank0, win0, n_win, n_act1, cum)

    as_smem_rows = lambda a: a.reshape(n_tiles, 1, EXP_TILE)
    yk = _experts(h2, as_smem_rows(src), as_smem_rows(dst), tile_e, n_act1, w_gate_up[0],
                  b_gate_up[0], w_down[0], b_down[0])
    y_p, y_s = _combine(yk, x1, w4, mod3, row2(ln2_g[0]), row2(ln2_b[0]), n_p, alpha, seq_s)
    return (y_p.reshape(nb_p, seq_p, D_MODEL), y_s.reshape(nb_s, seq_s, D_MODEL), new_f, new_b)
```

```python
import functools

import numpy as np
import jax
import jax.numpy as jnp
from jax import lax
from jax.experimental import pallas as pl
from jax.experimental.pallas import tpu as pltpu

F32 = jnp.float32
BF16 = jnp.bfloat16

D_MODEL = 1024
CONV_WIDTH = 512
CONV_TAPS = 31
REC_HEADS = 4
REC_DK = 128
REC_WIDTH = REC_HEADS * REC_DK
REC_COLS = 5 * REC_WIDTH
CHUNK = 64
N_LEVELS = 6
SCAN_CHUNKS = 4
N_EXPERTS = 32
TOP_K = 4
D_FF = 1024
SWIGLU_LIMIT = 7.0
SWIGLU_ALPHA = 1.702
LN_EPS = 1e-5
RMS_EPS = 1e-6
GRID_W = 64

LANES = 128
TOK_TILE = 256
EXP_TILE = 512
VMEM_LIMIT = 56 * 1024 * 1024


def _sigmoid(x):
    return 1.0 / (1.0 + jnp.exp(-x))


def _layer_norm(x):
    mu = jnp.mean(x, axis=-1, keepdims=True)
    xc = x - mu
    var = jnp.mean(xc * xc, axis=-1, keepdims=True)
    return xc * lax.rsqrt(var + LN_EPS)


def _dot(a, b):
    return jnp.dot(a, b, preferred_element_type=F32)


def _dot_nt(a, b):
    return lax.dot_general(a, b, (((1,), (1,)), ((), ())), preferred_element_type=F32)


SUBLANES = 8
assert D_MODEL == SUBLANES * LANES


def _store_row_tiles(ref, value):
    for s in range(SUBLANES):
        ref[pl.ds(s, value.shape[0], stride=SUBLANES), :] = value[:, s * LANES:(s + 1) * LANES]


def _load_row_tiles(ref, n_rows):
    return jnp.concatenate(
        [ref[pl.ds(s, n_rows, stride=SUBLANES), :] for s in range(SUBLANES)], axis=-1)


def _split3(x):
    hi = x.astype(BF16)
    r1 = x - hi.astype(F32)
    mid = r1.astype(BF16)
    lo = (r1 - mid.astype(F32)).astype(BF16)
    return hi, mid, lo


def _ada_kernel(c_ref, w_ref, b_ref, o_ref):
    c = c_ref[...]
    s = (c * _sigmoid(c)).astype(BF16)
    o_ref[...] = _dot(s, w_ref[...].astype(BF16)) + b_ref[...]


def _ada(cond, w_ada, b_ada):
    rows = cond.shape[0]
    ncol = w_ada.shape[1]
    tn = 1024
    return pl.pallas_call(
        _ada_kernel,
        out_shape=jax.ShapeDtypeStruct((rows, ncol), F32),
        grid=(ncol // tn,),
        in_specs=[pl.BlockSpec((rows, D_MODEL), lambda j: (0, 0)),
                  pl.BlockSpec((D_MODEL, tn), lambda j: (0, j)),
                  pl.BlockSpec((1, tn), lambda j: (0, j))],
        out_specs=pl.BlockSpec((rows, tn), lambda j: (0, j)),
        compiler_params=pltpu.CompilerParams(dimension_semantics=("parallel",),
                                             vmem_limit_bytes=VMEM_LIMIT),
        name="ada_mod",
    )(cond, w_ada, b_ada.reshape(1, ncol))


def _inproj_kernel(n_prompt_tiles, prompt_seg, xp_ref, xs_ref, mod_ref, win_ref, cw_ref, cb_ref,
                   cg_ref, cbeta_ref, conv_ref, rec_ref):
    i = pl.program_id(0)
    is_prompt = i < n_prompt_tiles
    x = jnp.where(is_prompt, xp_ref[...], xs_ref[...])
    mod = mod_ref[0]
    sh1 = mod[:, 0:D_MODEL]
    sc1 = mod[:, D_MODEL:2 * D_MODEL]
    h = (_layer_norm(x) * (1.0 + sc1) + sh1).astype(BF16)

    glu = _dot(h, win_ref[:, 0:2 * CONV_WIDTH])
    u = glu[:, :CONV_WIDTH] * _sigmoid(glu[:, CONV_WIDTH:])

    seg_len = jnp.where(is_prompt, prompt_seg, GRID_W)
    pos = lax.broadcasted_iota(jnp.int32, u.shape, 0) & (seg_len - 1)
    half = CONV_TAPS // 2
    acc = u * cw_ref[half:half + 1, :] + cb_ref[...]
    taps = [j for j in range(CONV_TAPS) if j != half]
    n_groups = REC_COLS // REC_WIDTH
    per_group = len(taps) // n_groups
    anchor = None
    for g in range(n_groups):
        cols = slice(g * REC_WIDTH, (g + 1) * REC_WIDTH)
        rec = _dot(h, win_ref[:, 2 * CONV_WIDTH + g * REC_WIDTH:
                              2 * CONV_WIDTH + (g + 1) * REC_WIDTH])
        rec_ref[:, cols] = rec
        for j in taps[g * per_group:(g + 1) * per_group]:
            d = j - half
            shifted = pltpu.roll(u, (-d) % TOK_TILE, axis=0)
            valid = (pos >= -d) if d < 0 else (pos < seg_len - d)
            w_tap = cw_ref[j:j + 1, :]
            if anchor is not None:
                w_tap = w_tap + anchor
                anchor = None
            acc = acc + jnp.where(valid, shifted, 0.0) * w_tap
        if g + 1 < n_groups:
            bits = pltpu.bitcast(rec[0:SUBLANES, :], jnp.uint32)
            anchor = pltpu.bitcast((bits >> 16) >> 16, F32)[0:1, :]
    y = _layer_norm(acc) * cg_ref[...] + cbeta_ref[...]
    conv_ref[...] = (y * _sigmoid(y)).astype(BF16)


def _inproj(xp, xs, mod3, w_in_bf, conv_w, conv_b, conv_g, conv_beta, prompt_seq, sample_seq):
    n_p, n_s = xp.shape[0], xs.shape[0]
    n_tok = n_p + n_s
    npt = n_p // TOK_TILE
    tiles_per_sample_seq = sample_seq // TOK_TILE
    assert prompt_seq == TOK_TILE and sample_seq % TOK_TILE == 0

    def mod_row(i):
        return jnp.where(i < npt, 0, 1 + (i - npt) // tiles_per_sample_seq)

    const = lambda i: (0, 0)
    return pl.pallas_call(
        functools.partial(_inproj_kernel, npt, prompt_seq),
        out_shape=(jax.ShapeDtypeStruct((n_tok, CONV_WIDTH), BF16),
                   jax.ShapeDtypeStruct((n_tok, REC_COLS), F32)),
        grid=(n_tok // TOK_TILE,),
        in_specs=[pl.BlockSpec((TOK_TILE, D_MODEL), lambda i: (jnp.minimum(i, npt - 1), 0)),
                  pl.BlockSpec((TOK_TILE, D_MODEL), lambda i: (jnp.maximum(i - npt, 0), 0)),
                  pl.BlockSpec((1, 1, 6 * D_MODEL), lambda i: (mod_row(i), 0, 0)),
                  pl.BlockSpec(w_in_bf.shape, const),
                  pl.BlockSpec(conv_w.shape, const),
                  pl.BlockSpec((1, CONV_WIDTH), const),
                  pl.BlockSpec((1, CONV_WIDTH), const),
                  pl.BlockSpec((1, CONV_WIDTH), const)],
        out_specs=(pl.BlockSpec((TOK_TILE, CONV_WIDTH), lambda i: (i, 0)),
                   pl.BlockSpec((TOK_TILE, REC_COLS), lambda i: (i, 0))),
        compiler_params=pltpu.CompilerParams(dimension_semantics=("parallel",),
                                             vmem_limit_bytes=VMEM_LIMIT),
        name="inproj_conv",
    )(xp, xs, mod3, w_in_bf, conv_w, conv_b, conv_g, conv_beta)


def _scan_constants():
    c = CHUNK
    w = np.zeros((N_LEVELS + 2, c, c), np.float32)
    m = np.zeros((N_LEVELS + 1, c, c), np.float32)
    t = np.arange(c)
    for j in range(N_LEVELS):
        half = 1 << j
        for ti in range(c):
            ref = (ti & ~(2 * half - 1)) + half - 1
            if ti & half:
                w[j, ti, ref + 1:ti + 1] = 1.0
            else:
                w[j, ti, ti + 1:ref + 1] = 1.0
        upper = (t[:, None] & half) != 0
        lower = (t[None, :] & half) == 0
        same = (t[:, None] >> (j + 1)) == (t[None, :] >> (j + 1))
        m[j] = (upper & lower & same).astype(np.float32)
    w[N_LEVELS] = (t[None, :] <= t[:, None]).astype(np.float32)
    w[N_LEVELS + 1] = (t[None, :] > t[:, None]).astype(np.float32)
    m[N_LEVELS] = np.eye(c, dtype=np.float32)
    w_f = w.reshape(-1, c)
    w_b = w[:, ::-1, ::-1].reshape(-1, c)
    m_b = m[:, ::-1, ::-1]
    return (jnp.asarray(w_f, BF16), jnp.asarray(np.ascontiguousarray(w_b), BF16),
            jnp.asarray(m), jnp.asarray(np.ascontiguousarray(m_b)))


def _scan_gates(z, lb, w_ref):
    e = jnp.exp(-jnp.abs(z))
    r = 1.0 / (1.0 + e)
    a = e * r
    sig = jnp.where(z >= 0, r, a)
    sig_neg = jnp.where(z >= 0, a, r)
    one_m_lb = 1.0 - lb
    log_f = jnp.log(lb + one_m_lb * sig)
    k = one_m_lb * sig_neg
    hi, mid, lo = _split3(log_f)
    w = w_ref[...]
    ex = jnp.exp(_dot(w, hi) + _dot(w, mid) + _dot(w, lo))
    return k, ex


def _scan_intra(q, k, ex, m_ref):
    c = CHUNK
    a_mat = m_ref[N_LEVELS] * _dot_nt(q.astype(BF16), k.astype(BF16))
    for j in range(N_LEVELS):
        xj = ex[j * c:(j + 1) * c]
        a_mat = a_mat + m_ref[j] * _dot_nt((q * xj).astype(BF16), (k * xj).astype(BF16))
    return a_mat


def _scan_kernel(fblk, bblk, first, seq, qf_ref, zf_ref, vf_ref, qb_ref, zb_ref, vb_ref, lb_ref,
                 wf_ref, wb_ref, mf_ref, mb_ref, s0f_ref, s0b_ref, of_ref, ob_ref, sf_ref, sb_ref):
    s = pl.program_id(0)

    @pl.when(first[s] == 1)
    def _():
        sf_ref[...] = s0f_ref[...]
        sb_ref[...] = s0b_ref[...]

    c = CHUNK
    dirs = ((qf_ref, zf_ref, vf_ref, 0, wf_ref, mf_ref, sf_ref, of_ref, c - 1,
             tuple(range(SCAN_CHUNKS))),
            (qb_ref, zb_ref, vb_ref, 1, wb_ref, mb_ref, sb_ref, ob_ref, 0,
             tuple(reversed(range(SCAN_CHUNKS)))))
    units = [(d, slice(sub * c, (sub + 1) * c)) for d in dirs for sub in d[-1]]
    gates = [_scan_gates(d[1][rows, :], lb_ref[d[3]:d[3] + 1, :], d[4]) for d, rows in units]
    intra = []
    for (d, rows), (k, ex) in zip(units, gates):
        for h in range(REC_HEADS):
            cols = slice(h * REC_DK, (h + 1) * REC_DK)
            intra.append(_scan_intra(d[0][rows, cols], k[:, cols], ex[:, cols], d[5]))
    idx = 0
    for (d, rows), (k, ex) in zip(units, gates):
        q_ref, _, v_ref, _, _, _, st_ref, o_ref, last, _ = d
        for h in range(REC_HEADS):
            cols = slice(h * REC_DK, (h + 1) * REC_DK)
            q, v = q_ref[rows, cols], v_ref[rows, cols]
            ex_b = ex[N_LEVELS * c:(N_LEVELS + 1) * c, cols]
            ex_l = ex[(N_LEVELS + 1) * c:(N_LEVELS + 2) * c, cols]
            st = st_ref[0, h]
            o_ref[rows, cols] = (_dot(intra[idx].astype(BF16), v.astype(BF16))
                                 + _dot_nt((q * ex_b).astype(BF16), st.astype(BF16)))
            st_ref[0, h] = (st * ex_b[last:last + 1, :]
                            + _dot(v.T.astype(BF16), (k[:, cols] * ex_l).astype(BF16)))
            idx += 1


def _scan(rec, lb, s0f_t, s0b_t, seq_lens):
    n_tok = rec.shape[0]
    fblk, bblk, first, seq = [], [], [], []
    base = 0
    step_rows = CHUNK * SCAN_CHUNKS
    for si, ln in enumerate(seq_lens):
        assert ln % step_rows == 0
        n = ln // step_rows
        for ci in range(n):
            fblk.append(base + ci)
            bblk.append(base + n - 1 - ci)
            first.append(1 if ci == 0 else 0)
            seq.append(si)
        base += n
    steps = len(fblk)
    as_i32 = lambda a: jnp.asarray(np.asarray(a, np.int32))
    w_f, w_b, m_f, m_b = _scan_constants()
    n_seq = len(seq_lens)

    def col(block_of, c):
        return pl.BlockSpec((step_rows, REC_WIDTH), lambda s, fb, bb, fi, sq: (block_of(fb, bb)[s], c))

    fwd = lambda fb, bb: fb
    bwd = lambda fb, bb: bb
    const2 = lambda s, fb, bb, fi, sq: (0, 0)
    const3 = lambda s, fb, bb, fi, sq: (0, 0, 0)
    state_spec = pl.BlockSpec((1, REC_HEADS, REC_DK, REC_DK), lambda s, fb, bb, fi, sq: (sq[s], 0, 0, 0))
    grid_spec = pltpu.PrefetchScalarGridSpec(
        num_scalar_prefetch=4,
        grid=(steps,),
        in_specs=[col(fwd, 0), col(fwd, 1), col(fwd, 3), col(bwd, 0), col(bwd, 2), col(bwd, 3),
                  pl.BlockSpec((2, REC_WIDTH), const2),
                  pl.BlockSpec(w_f.shape, const2), pl.BlockSpec(w_b.shape, const2),
                  pl.BlockSpec(m_f.shape, const3), pl.BlockSpec(m_b.shape, const3),
                  state_spec, state_spec],
        out_specs=(pl.BlockSpec((step_rows, REC_WIDTH), lambda s, fb, bb, fi, sq: (fb[s], 0)),
                   pl.BlockSpec((step_rows, REC_WIDTH), lambda s, fb, bb, fi, sq: (bb[s], 0)),
                   state_spec, state_spec),
    )
    st_shape = jax.ShapeDtypeStruct((n_seq, REC_HEADS, REC_DK, REC_DK), F32)
    return pl.pallas_call(
        _scan_kernel,
        out_shape=(jax.ShapeDtypeStruct((n_tok, REC_WIDTH), F32),
                   jax.ShapeDtypeStruct((n_tok, REC_WIDTH), F32), st_shape, st_shape),
        grid_spec=grid_spec,
        compiler_params=pltpu.CompilerParams(dimension_semantics=("arbitrary",),
                                             vmem_limit_bytes=VMEM_LIMIT),
        name="hgrn2_scan",
    )(as_i32(fblk), as_i32(bblk), as_i32(first), as_i32(seq),
      rec, rec, rec, rec, rec, rec, lb, w_f, w_b, m_f, m_b, s0f_t, s0b_t)


def _post_kernel(n_prompt_tiles, alpha, of_ref, ob_ref, g_ref, conv_ref, xp_ref, xs_ref, mod_ref,
                 wout_ref, rg_ref, l1g_ref, l1b_ref, rwh_ref, rwl_ref, rb_ref, tri_ref,
                 x1_ref, h2_ref, rank_t_ref, slot_t_ref, w4_ref, cnt_ref, start_ref, run_ref):
    i = pl.program_id(0)

    @pl.when(i == 0)
    def _():
        run_ref[...] = jnp.zeros_like(run_ref)

    x = jnp.where(i < n_prompt_tiles, xp_ref[...], xs_ref[...])
    mod = mod_ref[0]
    g1 = mod[:, 2 * D_MODEL:3 * D_MODEL]
    sh2 = mod[:, 3 * D_MODEL:4 * D_MODEL]
    sc2 = mod[:, 4 * D_MODEL:5 * D_MODEL]

    g = g_ref[...]
    silu_g = g * _sigmoid(g)
    mix = _dot(conv_ref[...], wout_ref[0:CONV_WIDTH, :])
    for h in range(REC_HEADS):
        cols = slice(h * REC_DK, (h + 1) * REC_DK)
        o = of_ref[:, cols] + ob_ref[:, cols]
        o = o * lax.rsqrt(jnp.mean(o * o, axis=-1, keepdims=True) + RMS_EPS)
        rec_out = (o * rg_ref[:, cols] * silu_g[:, cols]).astype(BF16)
        mix = mix + _dot(rec_out, wout_ref[CONV_WIDTH + h * REC_DK:CONV_WIDTH + (h + 1) * REC_DK, :])

    x1 = _layer_norm(alpha * x + g1 * mix) * l1g_ref[...] + l1b_ref[...]
    x1_ref[...] = x1
    h2 = _layer_norm(x1) * (1.0 + sc2) + sh2
    _store_row_tiles(h2_ref, h2)

    h_hi = h2.astype(BF16)
    h_lo = (h2 - h_hi.astype(F32)).astype(BF16)
    logits = (_dot(h_hi, rwh_ref[...]) + _dot(h_lo, rwh_ref[...]) + _dot(h_hi, rwl_ref[...])
              + rb_ref[...])
    lane = lax.broadcasted_iota(jnp.int32, logits.shape, 1)
    lane_f = lane.astype(F32)
    neg_inf = jnp.float32(-jnp.inf)
    work = jnp.where(lane < N_EXPERTS, logits, neg_inf)
    vals, sels = [], []
    for _ in range(TOP_K):
        m = jnp.max(work, axis=-1, keepdims=True)
        idx = jnp.min(jnp.where(work == m, lane_f, float(LANES)), axis=-1, keepdims=True)
        sel = lane_f == idx
        vals.append(m)
        sels.append(sel)
        work = jnp.where(sel, neg_inf, work)
    exps = [jnp.exp(v - vals[0]) for v in vals]
    denom = exps[0] + exps[1] + exps[2] + exps[3]

    sel_any = jnp.zeros(logits.shape, F32)
    for sel in sels:
        sel_any = jnp.where(sel, 1.0, sel_any)
    run_old = run_ref[...]
    prior = _dot(tri_ref[...], sel_any.astype(BF16)) + run_old
    run_new = run_old + jnp.sum(sel_any, axis=0, keepdims=True)
    run_ref[...] = run_new
    cnt_ref[...] = run_new
    start_ref[0] = run_old

    slot = jnp.zeros(logits.shape, F32)
    w4 = jnp.zeros(logits.shape, F32)
    for j in range(TOP_K):
        slot = jnp.where(sels[j], float(j), slot)
        w4 = jnp.where(lane == j, exps[j] / denom, w4)
    routed = sel_any > 0.0
    rank_t_ref[...] = jnp.where(routed, prior, -1.0).T[:N_EXPERTS]
    slot_t_ref[...] = jnp.where(routed, slot, -1.0).T[:N_EXPERTS]
    w4_ref[...] = w4


def _post(o_f, o_b, rec, conv, xp, xs, mod3, w_out_bf, rec_g, ln1_g, ln1_b, rw_hi, rw_lo, rb,
          alpha, sample_seq):
    n_p, n_s = xp.shape[0], xs.shape[0]
    n_tok = n_p + n_s
    npt = n_p // TOK_TILE
    tiles_per_sample_seq = sample_seq // TOK_TILE
    tri = jnp.asarray(np.tril(np.ones((TOK_TILE, TOK_TILE), np.float32), -1), BF16)

    def mod_row(i):
        return jnp.where(i < npt, 0, 1 + (i - npt) // tiles_per_sample_seq)

    const = lambda i: (0, 0)
    tok = lambda w: pl.BlockSpec((TOK_TILE, w), lambda i: (i, 0))
    return pl.pallas_call(
        functools.partial(_post_kernel, npt, alpha),
        out_shape=(jax.ShapeDtypeStruct((n_tok, D_MODEL), F32),
                   jax.ShapeDtypeStruct((n_tok * SUBLANES, LANES), F32),
                   jax.ShapeDtypeStruct((N_EXPERTS, n_tok), F32),
                   jax.ShapeDtypeStruct((N_EXPERTS, n_tok), F32),
                   jax.ShapeDtypeStruct((n_tok, LANES), F32),
                   jax.ShapeDtypeStruct((1, LANES), F32),
                   jax.ShapeDtypeStruct((n_tok // TOK_TILE, 1, LANES), F32)),
        grid=(n_tok // TOK_TILE,),
        in_specs=[tok(REC_WIDTH), tok(REC_WIDTH),
                  pl.BlockSpec((TOK_TILE, REC_WIDTH), lambda i: (i, 4)),
                  tok(CONV_WIDTH),
                  pl.BlockSpec((TOK_TILE, D_MODEL), lambda i: (jnp.minimum(i, npt - 1), 0)),
                  pl.BlockSpec((TOK_TILE, D_MODEL), lambda i: (jnp.maximum(i - npt, 0), 0)),
                  pl.BlockSpec((1, 1, 6 * D_MODEL), lambda i: (mod_row(i), 0, 0)),
                  pl.BlockSpec(w_out_bf.shape, const),
                  pl.BlockSpec((1, REC_WIDTH), const),
                  pl.BlockSpec((1, D_MODEL), const), pl.BlockSpec((1, D_MODEL), const),
                  pl.BlockSpec((D_MODEL, LANES), const), pl.BlockSpec((D_MODEL, LANES), const),
                  pl.BlockSpec((1, LANES), const),
                  pl.BlockSpec((TOK_TILE, TOK_TILE), const)],
        out_specs=(tok(D_MODEL), pl.BlockSpec((TOK_TILE * SUBLANES, LANES), lambda i: (i, 0)),
                   pl.BlockSpec((N_EXPERTS, TOK_TILE), lambda i: (0, i)),
                   pl.BlockSpec((N_EXPERTS, TOK_TILE), lambda i: (0, i)),
                   tok(LANES), pl.BlockSpec((1, LANES), const),
                   pl.BlockSpec((1, 1, LANES), lambda i: (i, 0, 0))),
        scratch_shapes=[pltpu.VMEM((1, LANES), F32)],
        compiler_params=pltpu.CompilerParams(dimension_semantics=("arbitrary",),
                                             vmem_limit_bytes=VMEM_LIMIT),
        name="post_mixer_router",
    )(o_f, o_b, rec, conv, xp, xs, mod3, w_out_bf, rec_g, ln1_g, ln1_b, rw_hi, rw_lo, rb, tri)


PLAN_ROWS = 32


def _plan_kernel(n_tok, tile_e, gstart, n_act, cum, starts_ref, ends_ref, rank_t_ref, slot_t_ref,
                 src_ref, dst_ref, acc_ref):
    i = pl.program_id(0)

    @pl.when(i < n_act[0])
    def _():
        e = tile_e[i]
        base = i * EXP_TILE - gstart[e]
        rows_iota = lax.broadcasted_iota(jnp.int32, (PLAN_ROWS, 1), 0)
        lane = lax.broadcasted_iota(jnp.int32, (1, TOK_TILE), 1)
        acc_ref[...] = jnp.zeros_like(acc_ref)

        base_f = base.astype(F32)
        win0 = jnp.sum((ends_ref[pl.ds(e, 1), :] <= base_f).astype(jnp.int32))
        win1 = jnp.sum((starts_ref[pl.ds(e, 1), :] < base_f + EXP_TILE).astype(jnp.int32))

        def window(b, carry):
            tok0 = pl.multiple_of(b * TOK_TILE, TOK_TILE)
            rk = rank_t_ref[pl.ds(e, 1), pl.ds(tok0, TOK_TILE)]
            sl = slot_t_ref[pl.ds(e, 1), pl.ds(tok0, TOK_TILE)]
            code = ((tok0 + lane) * TOP_K).astype(F32) + sl + 1.0
            lo = jnp.maximum(cum[b * N_EXPERTS + e] - base, 0)
            hi = jnp.minimum(cum[(b + 1) * N_EXPERTS + e] - base, EXP_TILE)

            def rows(k, c):
                r0 = pl.multiple_of(k * PLAN_ROWS, PLAN_ROWS)
                want = (base + r0 + rows_iota).astype(F32)
                hit = jnp.where(rk == want, code, 0.0)
                part = hit[:, 0:LANES]
                for c0 in range(LANES, TOK_TILE, LANES):
                    part = part + hit[:, c0:c0 + LANES]
                acc_ref[pl.ds(r0, PLAN_ROWS), :] += part
                return c

            shift = PLAN_ROWS.bit_length() - 1
            lax.fori_loop(lo >> shift, (hi + PLAN_ROWS - 1) >> shift, rows, 0)
            return carry

        lax.fori_loop(win0, win1, window, 0)
        code = jnp.sum(acc_ref[...].T, axis=0, keepdims=True).astype(jnp.int32) - 1
        pos = lax.broadcasted_iota(jnp.int32, (1, EXP_TILE), 1)
        tok = code >> 2
        src = jnp.where(code >= 0, tok, lax.rem(i * EXP_TILE + pos, n_tok))
        spare = TOP_K * n_tok + lax.rem(i, N_YBUF) * EXP_TILE + pos
        dst = jnp.where(code >= 0, (code & (TOP_K - 1)) * n_tok + tok, spare)
        src_ref[0] = src * SUBLANES
        dst_ref[0] = dst * SUBLANES

    @pl.when(i >= n_act[0])
    def _():
        src_ref[...] = jnp.zeros_like(src_ref)
        dst_ref[...] = jnp.zeros_like(dst_ref)


def _plan(rank_t, slot_t, tile_e, gstart, n_act, cum2d):
    n_tok = rank_t.shape[1]
    n_tiles = tile_e.shape[0]
    n_tok_tiles = cum2d.shape[0] - 1
    assert n_tok_tiles <= LANES
    huge = jnp.float32(2.0 ** 30)
    as_rows = lambda a: jnp.pad(a.T.astype(F32), ((0, 0), (0, LANES - n_tok_tiles)),
                                constant_values=huge)
    whole = lambda i, *_: (0, 0)
    per_tile = pl.BlockSpec((1, 1, EXP_TILE), lambda i, *_: (i, 0, 0))
    grid_spec = pltpu.PrefetchScalarGridSpec(
        num_scalar_prefetch=4,
        grid=(n_tiles,),
        in_specs=[pl.BlockSpec((N_EXPERTS, LANES), whole), pl.BlockSpec((N_EXPERTS, LANES), whole),
                  pl.BlockSpec(rank_t.shape, whole), pl.BlockSpec(slot_t.shape, whole)],
        out_specs=(per_tile, per_tile),
        scratch_shapes=[pltpu.VMEM((EXP_TILE, LANES), F32)],
    )
    shape = jax.ShapeDtypeStruct((n_tiles, 1, EXP_TILE), jnp.int32)
    return pl.pallas_call(
        functools.partial(_plan_kernel, n_tok),
        out_shape=(shape, shape),
        grid_spec=grid_spec,
        compiler_params=pltpu.CompilerParams(dimension_semantics=("arbitrary",),
                                             vmem_limit_bytes=VMEM_LIMIT),
        name="moe_plan",
    )(tile_e, gstart, n_act, cum2d.reshape(-1), as_rows(cum2d[:-1]), as_rows(cum2d[1:]),
      rank_t, slot_t)


N_YBUF = 3
DMA_UNROLL = 8


def _expert_kernel(tile_e, n_act, src0_ref, srcn_ref, dst_ref, h_ref, wgu_ref, bgu_ref, wd_ref,
                   bd_ref, yk_ref, xbuf, ybuf, gsem, ssem, wgu_bf, wd_bf):
    i = pl.program_id(0)
    n_steps = pl.num_programs(0)
    n_active = n_act[0]
    tile_rows = EXP_TILE * SUBLANES
    n_slot_rows = yk_ref.shape[0] - N_YBUF * tile_rows

    def row_tile(ref, start):
        return ref.at[pl.ds(pl.multiple_of(start, SUBLANES), SUBLANES)]

    def for_each_row(fn):
        def body(g, carry):
            for k in range(DMA_UNROLL):
                fn(g * DMA_UNROLL + k)
            return carry
        lax.fori_loop(0, EXP_TILE // DMA_UNROLL, body, 0)

    def on_slot(slot, n, fn):
        if isinstance(slot, int):
            fn(slot)
            return
        for s in range(n):
            pl.when(slot == s)(functools.partial(fn, s))

    def gather_rows(src_ref, slot):
        def run(s):
            def one(r):
                pltpu.make_async_copy(row_tile(h_ref, src_ref[0, 0, r]),
                                      row_tile(xbuf.at[s], r * SUBLANES), gsem.at[s]).start()
            for_each_row(one)
        on_slot(slot, 2, run)

    def scatter_rows(slot):
        def run(s):
            def one(r):
                pltpu.make_async_copy(row_tile(ybuf.at[s], r * SUBLANES),
                                      row_tile(yk_ref, dst_ref[0, 0, r]), ssem.at[s]).start()
            for_each_row(one)
        on_slot(slot, N_YBUF, run)

    def scatter_wait(slot):
        pltpu.make_async_copy(ybuf.at[slot], yk_ref.at[pl.ds(0, tile_rows)], ssem.at[slot]).wait()

    @pl.when(i == 0)
    def _():
        ybuf[0] = jnp.zeros((tile_rows, LANES), F32)
        for spare in range(N_YBUF):
            cp = pltpu.make_async_copy(
                ybuf.at[0], yk_ref.at[pl.ds(n_slot_rows + spare * tile_rows, tile_rows)],
                ssem.at[0])
            cp.start()
            cp.wait()
        gather_rows(src0_ref, 0)

    @pl.when(i + 1 < n_active)
    def _():
        gather_rows(srcn_ref, (i + 1) % 2)

    prev = tile_e[jnp.maximum(i - 1, 0)]
    new_expert = jnp.logical_or(i == 0, tile_e[i] != prev)

    @pl.when(jnp.logical_and(new_expert, i < n_active))
    def _():
        rows = 128

        def cast(r, carry):
            sl = pl.ds(pl.multiple_of(r * rows, rows), rows)
            wgu_bf[sl, :] = wgu_ref[0, sl, :].astype(BF16)
            wd_bf[sl, :] = wd_ref[0, sl, :].astype(BF16)
            return carry

        lax.fori_loop(0, D_MODEL // rows, cast, 0)

    @pl.when(i < n_active)
    def _():
        xslot = i % 2
        yslot = i % N_YBUF
        pltpu.make_async_copy(h_ref.at[pl.ds(0, tile_rows)], xbuf.at[xslot], gsem.at[xslot]).wait()

        @pl.when(i >= N_YBUF)
        def _():
            scatter_wait(yslot)

        x = _load_row_tiles(xbuf.at[xslot], EXP_TILE).astype(BF16)
        gu = _dot(x, wgu_bf[...]) + bgu_ref[0]
        gate = jnp.minimum(gu[:, :D_FF], SWIGLU_LIMIT)
        up = jnp.clip(gu[:, D_FF:], -SWIGLU_LIMIT, SWIGLU_LIMIT)
        act = (up + 1.0) * gate * _sigmoid(SWIGLU_ALPHA * gate)
        _store_row_tiles(ybuf.at[yslot], _dot(act.astype(BF16), wd_bf[...]) + bd_ref[0])
        scatter_rows(yslot)

    @pl.when(i == n_steps - 1)
    def _():
        for back in range(1, N_YBUF + 1):
            scatter_wait((n_active - back) % N_YBUF)


def _experts(h2, src3, dst3, tile_e, n_act, w_gate_up, b_gate_up, w_down, b_down):
    n_tok = h2.shape[0] // SUBLANES
    n_tiles = src3.shape[0]
    idx_block = (1, 1, EXP_TILE)
    grid_spec = pltpu.PrefetchScalarGridSpec(
        num_scalar_prefetch=2,
        grid=(n_tiles,),
        in_specs=[pl.BlockSpec(idx_block, lambda i, te, na: (0, 0, 0), memory_space=pltpu.SMEM),
                  pl.BlockSpec(idx_block, lambda i, te, na: (jnp.minimum(i + 1, n_tiles - 1), 0, 0),
                               memory_space=pltpu.SMEM),
                  pl.BlockSpec(idx_block, lambda i, te, na: (i, 0, 0), memory_space=pltpu.SMEM),
                  pl.BlockSpec(memory_space=pl.ANY),
                  pl.BlockSpec((1, D_MODEL, 2 * D_FF), lambda i, te, na: (te[i], 0, 0)),
                  pl.BlockSpec((1, 1, 2 * D_FF), lambda i, te, na: (te[i], 0, 0)),
                  pl.BlockSpec((1, D_FF, D_MODEL), lambda i, te, na: (te[i], 0, 0)),
                  pl.BlockSpec((1, 1, D_MODEL), lambda i, te, na: (te[i], 0, 0))],
        out_specs=pl.BlockSpec(memory_space=pl.ANY),
        scratch_shapes=[pltpu.VMEM((2, EXP_TILE * SUBLANES, LANES), F32),
                        pltpu.VMEM((N_YBUF, EXP_TILE * SUBLANES, LANES), F32),
                        pltpu.SemaphoreType.DMA((2,)), pltpu.SemaphoreType.DMA((N_YBUF,)),
                        pltpu.VMEM((D_MODEL, 2 * D_FF), BF16), pltpu.VMEM((D_FF, D_MODEL), BF16)],
    )
    return pl.pallas_call(
        _expert_kernel,
        out_shape=jax.ShapeDtypeStruct(((TOP_K * n_tok + N_YBUF * EXP_TILE) * SUBLANES, LANES), F32),
        grid_spec=grid_spec,
        compiler_params=pltpu.CompilerParams(dimension_semantics=("arbitrary",),
                                             vmem_limit_bytes=VMEM_LIMIT),
        name="moe_experts",
    )(tile_e, n_act, src3, src3, dst3, h2, w_gate_up, b_gate_up.reshape(N_EXPERTS, 1, 2 * D_FF),
      w_down, b_down.reshape(N_EXPERTS, 1, D_MODEL))


def _combine_kernel(n_prompt_tiles, alpha, y0_ref, y1_ref, y2_ref, y3_ref, x1_ref, w4_ref, mod_ref,
                    l2g_ref, l2b_ref, yp_ref, ysm_ref):
    i = pl.program_id(0)
    w4 = w4_ref[...]
    ff = _load_row_tiles(y0_ref, TOK_TILE) * w4[:, 0:1]
    for j, y_ref in enumerate((y1_ref, y2_ref, y3_ref), start=1):
        ff = ff + _load_row_tiles(y_ref, TOK_TILE) * w4[:, j:j + 1]
    g2 = mod_ref[0][:, 5 * D_MODEL:6 * D_MODEL]
    out = _layer_norm(alpha * x1_ref[...] + g2 * ff) * l2g_ref[...] + l2b_ref[...]

    @pl.when(i < n_prompt_tiles)
    def _():
        yp_ref[...] = out

    @pl.when(i >= n_prompt_tiles)
    def _():
        ysm_ref[...] = out


def _combine(yk, x1, w4, mod3, ln2_g, ln2_b, n_p, alpha, sample_seq):
    n_tok = x1.shape[0]
    n_s = n_tok - n_p
    npt = n_p // TOK_TILE
    n_tok_tiles = n_tok // TOK_TILE
    tiles_per_sample_seq = sample_seq // TOK_TILE

    def mod_row(i):
        return jnp.where(i < npt, 0, 1 + (i - npt) // tiles_per_sample_seq)

    def slot(j):
        return pl.BlockSpec((TOK_TILE * SUBLANES, LANES), lambda i: (j * n_tok_tiles + i, 0))

    const = lambda i: (0, 0)
    return pl.pallas_call(
        functools.partial(_combine_kernel, npt, alpha),
        out_shape=(jax.ShapeDtypeStruct((n_p, D_MODEL), F32),
                   jax.ShapeDtypeStruct((n_s, D_MODEL), F32)),
        grid=(n_tok_tiles,),
        in_specs=[slot(0), slot(1), slot(2), slot(3),
                  pl.BlockSpec((TOK_TILE, D_MODEL), lambda i: (i, 0)),
                  pl.BlockSpec((TOK_TILE, LANES), lambda i: (i, 0)),
                  pl.BlockSpec((1, 1, 6 * D_MODEL), lambda i: (mod_row(i), 0, 0)),
                  pl.BlockSpec((1, D_MODEL), const), pl.BlockSpec((1, D_MODEL), const)],
        out_specs=(pl.BlockSpec((TOK_TILE, D_MODEL), lambda i: (jnp.minimum(i, npt - 1), 0)),
                   pl.BlockSpec((TOK_TILE, D_MODEL), lambda i: (jnp.maximum(i - npt, 0), 0))),
        compiler_params=pltpu.CompilerParams(dimension_semantics=("arbitrary",),
                                             vmem_limit_bytes=VMEM_LIMIT),
        name="moe_combine",
    )(yk, yk, yk, yk, x1, w4, mod3, ln2_g, ln2_b)


def kernel(x_prompt, x_sample, c, state_fwd, state_bwd, c_ctx, w_ada, b_ada, w_in, conv_w, conv_b,
           conv_ln_g, conv_ln_b, lb_logits, rec_norm_g, w_out, ln1_g, ln1_b, ln2_g, ln2_b,
           router_w, router_b, w_gate_up, b_gate_up, w_down, b_down):
    depth = w_ada.shape[0]
    assert depth == 1
    alpha = (2.0 * depth) ** 0.25
    nb_p, seq_p, _ = x_prompt.shape
    nb_s, seq_s, _ = x_sample.shape
    n_p, n_s = nb_p * seq_p, nb_s * seq_s
    n_tok = n_p + n_s
    row2 = lambda a: a.reshape(1, -1)

    lb = jnp.cumsum(jax.nn.softmax(lb_logits.astype(F32), axis=0), axis=0)[0]

    cond = jnp.concatenate([c_ctx[None, :], c, jnp.zeros((16 - 1 - nb_s, D_MODEL), F32)], axis=0)
    mod = _ada(cond, w_ada[0], b_ada[0])
    mod3 = mod.reshape(16, 1, 6 * D_MODEL)

    xp = x_prompt.reshape(n_p, D_MODEL)
    xs = x_sample.reshape(n_s, D_MODEL)
    conv_w_pad = jnp.concatenate([conv_w[0], jnp.zeros((1, CONV_WIDTH), F32)], axis=0)
    conv_out, rec = _inproj(xp, xs, mod3, w_in[0].astype(BF16), conv_w_pad, row2(conv_b[0]),
                            row2(conv_ln_g[0]), row2(conv_ln_b[0]), seq_p, seq_s)

    zeros_state = jnp.zeros((nb_p, REC_HEADS, REC_DK, REC_DK), F32)
    s0f = jnp.concatenate([zeros_state, jnp.swapaxes(state_fwd[:, 0], -1, -2)], axis=0)
    s0b = jnp.concatenate([zeros_state, jnp.swapaxes(state_bwd[:, 0], -1, -2)], axis=0)
    o_f, o_b, sf_t, sb_t = _scan(rec, lb, s0f, s0b, [seq_p] * nb_p + [seq_s] * nb_s)
    new_f = jnp.swapaxes(sf_t[:nb_p], -1, -2)[:, None]
    new_b = jnp.swapaxes(sb_t[:nb_p], -1, -2)[:, None]

    rw = jnp.pad(router_w[0], ((0, 0), (0, LANES - N_EXPERTS)))
    rw_hi = rw.astype(BF16)
    rw_lo = (rw - rw_hi.astype(F32)).astype(BF16)
    rb = jnp.pad(router_b[0], (0, LANES - N_EXPERTS)).reshape(1, LANES)
    x1, h2, rank_t, slot_t, w4, counts, tile_start = _post(
        o_f, o_b, rec, conv_out, xp, xs, mod3, w_out[0].astype(BF16), row2(rec_norm_g[0]),
        row2(ln1_g[0]), row2(ln1_b[0]), rw_hi, rw_lo, rb, alpha, seq_s)

    cnt = counts[0, :N_EXPERTS].astype(jnp.int32)
    gpad = ((cnt + EXP_TILE - 1) // EXP_TILE) * EXP_TILE
    gend = jnp.cumsum(gpad)
    gstart = gend - gpad
    n_rows = n_tok * TOP_K + N_EXPERTS * EXP_TILE
    n_tiles = n_rows // EXP_TILE
    n_act = gend[-1] // EXP_TILE
    tile_ids = jnp.minimum(jnp.arange(n_tiles, dtype=jnp.int32), n_act - 1)
    tile_e = jnp.sum((gend[None, :] <= tile_ids[:, None] * EXP_TILE).astype(jnp.int32), axis=1)
    n_act1 = n_act.reshape(1).astype(jnp.int32)
    starts = tile_start[:, 0, :N_EXPERTS].astype(jnp.int32)
    cum2d = jnp.concatenate([starts, cnt[None, :]], axis=0)
    src, dst = _plan(rank_t, slot_t, tile_e, gstart.astype(jnp.int32), n_act1, cum2d)
    yk = _experts(h2, src, dst, tile_e, n_act1, w_gate_up[0], b_gate_up[0], w_down[0], b_down[0])
    y_p, y_s = _combine(yk, x1, w4, mod3, row2(ln2_g[0]), row2(ln2_b[0]), n_p, alpha, seq_s)
    return (y_p.reshape(nb_p, seq_p, D_MODEL), y_s.reshape(nb_s, seq_s, D_MODEL), new_f, new_b)
```

```python
import functools

import numpy as np
import jax
import jax.numpy as jnp
from jax import lax
from jax.experimental import pallas as pl
from jax.experimental.pallas import tpu as pltpu

F32 = jnp.float32
BF16 = jnp.bfloat16

D_MODEL = 1024
CONV_WIDTH = 512
CONV_TAPS = 31
REC_HEADS = 4
REC_DK = 128
REC_WIDTH = REC_HEADS * REC_DK
REC_COLS = 5 * REC_WIDTH
CHUNK = 64
N_LEVELS = 6
SCAN_CHUNKS = 4
N_EXPERTS = 32
TOP_K = 4
D_FF = 1024
SWIGLU_LIMIT = 7.0
SWIGLU_ALPHA = 1.702
LN_EPS = 1e-5
RMS_EPS = 1e-6
GRID_W = 64

LANES = 128
TOK_TILE = 256
EXP_TILE = 512
VMEM_LIMIT = 56 * 1024 * 1024


def _sigmoid(x):
    return 1.0 / (1.0 + jnp.exp(-x))


def _layer_norm(x):
    mu = jnp.mean(x, axis=-1, keepdims=True)
    xc = x - mu
    var = jnp.mean(xc * xc, axis=-1, keepdims=True)
    return xc * lax.rsqrt(var + LN_EPS)


def _dot(a, b):
    return jnp.dot(a, b, preferred_element_type=F32)


def _dot_nt(a, b):
    return lax.dot_general(a, b, (((1,), (1,)), ((), ())), preferred_element_type=F32)


SUBLANES = 8
assert D_MODEL == SUBLANES * LANES


def _store_row_tiles(ref, value):
    for s in range(SUBLANES):
        ref[pl.ds(s, value.shape[0], stride=SUBLANES), :] = value[:, s * LANES:(s + 1) * LANES]


def _load_row_tiles(ref, n_rows):
    return jnp.concatenate(
        [ref[pl.ds(s, n_rows, stride=SUBLANES), :] for s in range(SUBLANES)], axis=-1)


def _split3(x):
    hi = x.astype(BF16)
    r1 = x - hi.astype(F32)
    mid = r1.astype(BF16)
    lo = (r1 - mid.astype(F32)).astype(BF16)
    return hi, mid, lo


def _ada_kernel(c_ref, w_ref, b_ref, o_ref):
    c = c_ref[...]
    s = (c * _sigmoid(c)).astype(BF16)
    o_ref[...] = _dot(s, w_ref[...].astype(BF16)) + b_ref[...]


def _ada(cond, w_ada, b_ada):
    rows = cond.shape[0]
    ncol = w_ada.shape[1]
    tn = 1024
    return pl.pallas_call(
        _ada_kernel,
        out_shape=jax.ShapeDtypeStruct((rows, ncol), F32),
        grid=(ncol // tn,),
        in_specs=[pl.BlockSpec((rows, D_MODEL), lambda j: (0, 0)),
                  pl.BlockSpec((D_MODEL, tn), lambda j: (0, j)),
                  pl.BlockSpec((1, tn), lambda j: (0, j))],
        out_specs=pl.BlockSpec((rows, tn), lambda j: (0, j)),
        compiler_params=pltpu.CompilerParams(dimension_semantics=("parallel",),
                                             vmem_limit_bytes=VMEM_LIMIT),
        name="ada_mod",
    )(cond, w_ada, b_ada.reshape(1, ncol))


def _inproj_kernel(n_prompt_tiles, prompt_seg, xp_ref, xs_ref, mod_ref, win_ref, cw_ref, cb_ref,
                   cg_ref, cbeta_ref, conv_ref, rec_ref):
    i = pl.program_id(0)
    is_prompt = i < n_prompt_tiles
    x = jnp.where(is_prompt, xp_ref[...], xs_ref[...])
    mod = mod_ref[0]
    sh1 = mod[:, 0:D_MODEL]
    sc1 = mod[:, D_MODEL:2 * D_MODEL]
    h = (_layer_norm(x) * (1.0 + sc1) + sh1).astype(BF16)

    glu = _dot(h, win_ref[:, 0:2 * CONV_WIDTH])
    u = glu[:, :CONV_WIDTH] * _sigmoid(glu[:, CONV_WIDTH:])

    seg_len = jnp.where(is_prompt, prompt_seg, GRID_W)
    half = CONV_TAPS // 2
    pos = lax.broadcasted_iota(jnp.int32, (TOK_TILE, 1), 0) & (seg_len - 1)

    def masked_shift(d):
        sh = pltpu.roll(u, (-d) % TOK_TILE, axis=0)
        edge_rows = SUBLANES * ((abs(d) + SUBLANES - 1) // SUBLANES)
        pieces = []
        for s0 in range(0, TOK_TILE, GRID_W):
            if d < 0:
                edge = slice(s0, s0 + edge_rows)
                pieces.append(jnp.where(pos[edge] >= -d, sh[edge], 0.0))
                pieces.append(sh[s0 + edge_rows:s0 + GRID_W])
            else:
                edge = slice(s0 + GRID_W - edge_rows, s0 + GRID_W)
                pieces.append(sh[s0:s0 + GRID_W - edge_rows])
                pieces.append(jnp.where(pos[edge] < seg_len - d, sh[edge], 0.0))
        return jnp.concatenate(pieces, axis=0)

    acc = u * cw_ref[half:half + 1, :] + cb_ref[...]
    taps = [j for j in range(CONV_TAPS) if j != half]
    n_groups = REC_COLS // REC_WIDTH
    per_group = len(taps) // n_groups
    anchor = None
    for g in range(n_groups):
        cols = slice(g * REC_WIDTH, (g + 1) * REC_WIDTH)
        rec = _dot(h, win_ref[:, 2 * CONV_WIDTH + g * REC_WIDTH:
                              2 * CONV_WIDTH + (g + 1) * REC_WIDTH])
        rec_ref[:, cols] = rec
        for j in taps[g * per_group:(g + 1) * per_group]:
            w_tap = cw_ref[j:j + 1, :]
            if anchor is not None:
                w_tap = w_tap + anchor
                anchor = None
            acc = acc + masked_shift(j - half) * w_tap
        if g + 1 < n_groups:
            bits = pltpu.bitcast(rec[0:SUBLANES, :], jnp.uint32)
            anchor = pltpu.bitcast((bits >> 16) >> 16, F32)[0:1, :]
    y = _layer_norm(acc) * cg_ref[...] + cbeta_ref[...]
    conv_ref[...] = (y * _sigmoid(y)).astype(BF16)


def _inproj(xp, xs, mod3, w_in_bf, conv_w, conv_b, conv_g, conv_beta, prompt_seq, sample_seq):
    n_p, n_s = xp.shape[0], xs.shape[0]
    n_tok = n_p + n_s
    npt = n_p // TOK_TILE
    tiles_per_sample_seq = sample_seq // TOK_TILE
    assert prompt_seq == TOK_TILE and sample_seq % TOK_TILE == 0

    def mod_row(i):
        return jnp.where(i < npt, 0, 1 + (i - npt) // tiles_per_sample_seq)

    const = lambda i: (0, 0)
    return pl.pallas_call(
        functools.partial(_inproj_kernel, npt, prompt_seq),
        out_shape=(jax.ShapeDtypeStruct((n_tok, CONV_WIDTH), BF16),
                   jax.ShapeDtypeStruct((n_tok, REC_COLS), F32)),
        grid=(n_tok // TOK_TILE,),
        in_specs=[pl.BlockSpec((TOK_TILE, D_MODEL), lambda i: (jnp.minimum(i, npt - 1), 0)),
                  pl.BlockSpec((TOK_TILE, D_MODEL), lambda i: (jnp.maximum(i - npt, 0), 0)),
                  pl.BlockSpec((1, 1, 6 * D_MODEL), lambda i: (mod_row(i), 0, 0)),
                  pl.BlockSpec(w_in_bf.shape, const),
                  pl.BlockSpec(conv_w.shape, const),
                  pl.BlockSpec((1, CONV_WIDTH), const),
                  pl.BlockSpec((1, CONV_WIDTH), const),
                  pl.BlockSpec((1, CONV_WIDTH), const)],
        out_specs=(pl.BlockSpec((TOK_TILE, CONV_WIDTH), lambda i: (i, 0)),
                   pl.BlockSpec((TOK_TILE, REC_COLS), lambda i: (i, 0))),
        compiler_params=pltpu.CompilerParams(dimension_semantics=("parallel",),
                                             vmem_limit_bytes=VMEM_LIMIT),
        name="inproj_conv",
    )(xp, xs, mod3, w_in_bf, conv_w, conv_b, conv_g, conv_beta)


def _scan_constants():
    c = CHUNK
    w = np.zeros((N_LEVELS + 2, c, c), np.float32)
    m = np.zeros((N_LEVELS + 1, c, c), np.float32)
    t = np.arange(c)
    for j in range(N_LEVELS):
        half = 1 << j
        for ti in range(c):
            ref = (ti & ~(2 * half - 1)) + half - 1
            if ti & half:
                w[j, ti, ref + 1:ti + 1] = 1.0
            else:
                w[j, ti, ti + 1:ref + 1] = 1.0
        upper = (t[:, None] & half) != 0
        lower = (t[None, :] & half) == 0
        same = (t[:, None] >> (j + 1)) == (t[None, :] >> (j + 1))
        m[j] = (upper & lower & same).astype(np.float32)
    w[N_LEVELS] = (t[None, :] <= t[:, None]).astype(np.float32)
    w[N_LEVELS + 1] = (t[None, :] > t[:, None]).astype(np.float32)
    m[N_LEVELS] = np.eye(c, dtype=np.float32)
    w_f = np.tile(w.reshape(-1, c), (1, 3))
    w_b = np.tile(w[:, ::-1, ::-1].reshape(-1, c), (1, 3))
    m_b = m[:, ::-1, ::-1]
    return (jnp.asarray(w_f, BF16), jnp.asarray(np.ascontiguousarray(w_b), BF16),
            jnp.asarray(m), jnp.asarray(np.ascontiguousarray(m_b)))


def _scan_gates(z, lb, w_ref):
    e = jnp.exp(-jnp.abs(z))
    r = 1.0 / (1.0 + e)
    a = e * r
    sig = jnp.where(z >= 0, r, a)
    sig_neg = jnp.where(z >= 0, a, r)
    one_m_lb = 1.0 - lb
    log_f = jnp.log2(lb + one_m_lb * sig)
    k = one_m_lb * sig_neg
    terms = jnp.concatenate(_split3(log_f), axis=0)
    ex = jnp.exp2(_dot(w_ref[...], terms))
    return k, ex


def _scan_intra(q, k, ex, m_ref):
    c = CHUNK
    a_mat = m_ref[N_LEVELS] * _dot_nt(q.astype(BF16), k.astype(BF16))
    for j in range(N_LEVELS):
        xj = ex[j * c:(j + 1) * c]
        a_mat = a_mat + m_ref[j] * _dot_nt((q * xj).astype(BF16), (k * xj).astype(BF16))
    return a_mat


def _scan_kernel(fblk, bblk, first, seq, qf_ref, zf_ref, vf_ref, qb_ref, zb_ref, vb_ref, lb_ref,
                 wf_ref, wb_ref, mf_ref, mb_ref, s0f_ref, s0b_ref, of_ref, ob_ref, sf_ref, sb_ref):
    s = pl.program_id(0)

    @pl.when(first[s] == 1)
    def _():
        sf_ref[...] = s0f_ref[...]
        sb_ref[...] = s0b_ref[...]

    c = CHUNK
    dirs = ((qf_ref, zf_ref, vf_ref, 0, wf_ref, mf_ref, sf_ref, of_ref, c - 1,
             tuple(range(SCAN_CHUNKS))),
            (qb_ref, zb_ref, vb_ref, 1, wb_ref, mb_ref, sb_ref, ob_ref, 0,
             tuple(reversed(range(SCAN_CHUNKS)))))
    units = [(d, slice(sub * c, (sub + 1) * c)) for d in dirs for sub in d[-1]]
    gates = [_scan_gates(d[1][rows, :], lb_ref[d[3]:d[3] + 1, :], d[4]) for d, rows in units]
    intra = []
    for (d, rows), (k, ex) in zip(units, gates):
        for h in range(REC_HEADS):
            cols = slice(h * REC_DK, (h + 1) * REC_DK)
            intra.append(_scan_intra(d[0][rows, cols], k[:, cols], ex[:, cols], d[5]))
    idx = 0
    for (d, rows), (k, ex) in zip(units, gates):
        q_ref, _, v_ref, _, _, _, st_ref, o_ref, last, _ = d
        for h in range(REC_HEADS):
            cols = slice(h * REC_DK, (h + 1) * REC_DK)
            q, v = q_ref[rows, cols], v_ref[rows, cols]
            ex_b = ex[N_LEVELS * c:(N_LEVELS + 1) * c, cols]
            ex_l = ex[(N_LEVELS + 1) * c:(N_LEVELS + 2) * c, cols]
            st = st_ref[0, h]
            o_ref[rows, cols] = (_dot(intra[idx].astype(BF16), v.astype(BF16))
                                 + _dot_nt((q * ex_b).astype(BF16), st.astype(BF16)))
            st_ref[0, h] = (st * ex_b[last:last + 1, :]
                            + _dot(v.T.astype(BF16), (k[:, cols] * ex_l).astype(BF16)))
            idx += 1


def _scan(rec, lb, s0f_t, s0b_t, seq_lens):
    n_tok = rec.shape[0]
    fblk, bblk, first, seq = [], [], [], []
    base = 0
    step_rows = CHUNK * SCAN_CHUNKS
    for si, ln in enumerate(seq_lens):
        assert ln % step_rows == 0
        n = ln // step_rows
        for ci in range(n):
            fblk.append(base + ci)
            bblk.append(base + n - 1 - ci)
            first.append(1 if ci == 0 else 0)
            seq.append(si)
        base += n
    steps = len(fblk)
    as_i32 = lambda a: jnp.asarray(np.asarray(a, np.int32))
    w_f, w_b, m_f, m_b = _scan_constants()
    n_seq = len(seq_lens)

    def col(block_of, c):
        return pl.BlockSpec((step_rows, REC_WIDTH), lambda s, fb, bb, fi, sq: (block_of(fb, bb)[s], c))

    fwd = lambda fb, bb: fb
    bwd = lambda fb, bb: bb
    const2 = lambda s, fb, bb, fi, sq: (0, 0)
    const3 = lambda s, fb, bb, fi, sq: (0, 0, 0)
    state_spec = pl.BlockSpec((1, REC_HEADS, REC_DK, REC_DK), lambda s, fb, bb, fi, sq: (sq[s], 0, 0, 0))
    grid_spec = pltpu.PrefetchScalarGridSpec(
        num_scalar_prefetch=4,
        grid=(steps,),
        in_specs=[col(fwd, 0), col(fwd, 1), col(fwd, 3), col(bwd, 0), col(bwd, 2), col(bwd, 3),
                  pl.BlockSpec((2, REC_WIDTH), const2),
                  pl.BlockSpec(w_f.shape, const2), pl.BlockSpec(w_b.shape, const2),
                  pl.BlockSpec(m_f.shape, const3), pl.BlockSpec(m_b.shape, const3),
                  state_spec, state_spec],
        out_specs=(pl.BlockSpec((step_rows, REC_WIDTH), lambda s, fb, bb, fi, sq: (fb[s], 0)),
                   pl.BlockSpec((step_rows, REC_WIDTH), lambda s, fb, bb, fi, sq: (bb[s], 0)),
                   state_spec, state_spec),
    )
    st_shape = jax.ShapeDtypeStruct((n_seq, REC_HEADS, REC_DK, REC_DK), F32)
    return pl.pallas_call(
        _scan_kernel,
        out_shape=(jax.ShapeDtypeStruct((n_tok, REC_WIDTH), F32),
                   jax.ShapeDtypeStruct((n_tok, REC_WIDTH), F32), st_shape, st_shape),
        grid_spec=grid_spec,
        compiler_params=pltpu.CompilerParams(dimension_semantics=("arbitrary",),
                                             vmem_limit_bytes=VMEM_LIMIT),
        name="hgrn2_scan",
    )(as_i32(fblk), as_i32(bblk), as_i32(first), as_i32(seq),
      rec, rec, rec, rec, rec, rec, lb, w_f, w_b, m_f, m_b, s0f_t, s0b_t)


def _post_kernel(n_prompt_tiles, alpha, of_ref, ob_ref, g_ref, conv_ref, xp_ref, xs_ref, mod_ref,
                 wout_ref, rg_ref, l1g_ref, l1b_ref, rwh_ref, rwl_ref, rb_ref, tri_ref,
                 x1_ref, h2_ref, rank_t_ref, slot_t_ref, w4_ref, cnt_ref, start_ref, run_ref):
    i = pl.program_id(0)

    @pl.when(i == 0)
    def _():
        run_ref[...] = jnp.zeros_like(run_ref)

    x = jnp.where(i < n_prompt_tiles, xp_ref[...], xs_ref[...])
    mod = mod_ref[0]
    g1 = mod[:, 2 * D_MODEL:3 * D_MODEL]
    sh2 = mod[:, 3 * D_MODEL:4 * D_MODEL]
    sc2 = mod[:, 4 * D_MODEL:5 * D_MODEL]

    g = g_ref[...]
    silu_g = g * _sigmoid(g)
    mix = _dot(conv_ref[...], wout_ref[0:CONV_WIDTH, :])
    for h in range(REC_HEADS):
        cols = slice(h * REC_DK, (h + 1) * REC_DK)
        o = of_ref[:, cols] + ob_ref[:, cols]
        o = o * lax.rsqrt(jnp.mean(o * o, axis=-1, keepdims=True) + RMS_EPS)
        rec_out = (o * rg_ref[:, cols] * silu_g[:, cols]).astype(BF16)
        mix = mix + _dot(rec_out, wout_ref[CONV_WIDTH + h * REC_DK:CONV_WIDTH + (h + 1) * REC_DK, :])

    x1 = _layer_norm(alpha * x + g1 * mix) * l1g_ref[...] + l1b_ref[...]
    x1_ref[...] = x1
    h2 = _layer_norm(x1) * (1.0 + sc2) + sh2
    _store_row_tiles(h2_ref, h2)

    h_hi = h2.astype(BF16)
    h_lo = (h2 - h_hi.astype(F32)).astype(BF16)
    logits = (_dot(h_hi, rwh_ref[...]) + _dot(h_lo, rwh_ref[...]) + _dot(h_hi, rwl_ref[...])
              + rb_ref[...])
    lane = lax.broadcasted_iota(jnp.int32, logits.shape, 1)
    lane_f = lane.astype(F32)
    neg_inf = jnp.float32(-jnp.inf)
    work = jnp.where(lane < N_EXPERTS, logits, neg_inf)
    vals, sels = [], []
    for _ in range(TOP_K):
        m = jnp.max(work, axis=-1, keepdims=True)
        idx = jnp.min(jnp.where(work == m, lane_f, float(LANES)), axis=-1, keepdims=True)
        sel = lane_f == idx
        vals.append(m)
        sels.append(sel)
        work = jnp.where(sel, neg_inf, work)
    exps = [jnp.exp(v - vals[0]) for v in vals]
    denom = exps[0] + exps[1] + exps[2] + exps[3]

    sel_any = jnp.zeros(logits.shape, F32)
    for sel in sels:
        sel_any = jnp.where(sel, 1.0, sel_any)
    run_old = run_ref[...]
    prior = _dot(tri_ref[...], sel_any.astype(BF16)) + run_old
    run_new = run_old + jnp.sum(sel_any, axis=0, keepdims=True)
    run_ref[...] = run_new
    cnt_ref[...] = run_new
    start_ref[0] = run_old

    slot = jnp.zeros(logits.shape, F32)
    w4 = jnp.zeros(logits.shape, F32)
    for j in range(TOP_K):
        slot = jnp.where(sels[j], float(j), slot)
        w4 = jnp.where(lane == j, exps[j] / denom, w4)
    routed = sel_any > 0.0
    rank_t_ref[...] = jnp.where(routed, prior, -1.0).T[:N_EXPERTS]
    slot_t_ref[...] = jnp.where(routed, slot, -1.0).T[:N_EXPERTS]
    w4_ref[...] = w4


def _post(o_f, o_b, rec, conv, xp, xs, mod3, w_out_bf, rec_g, ln1_g, ln1_b, rw_hi, rw_lo, rb,
          alpha, sample_seq):
    n_p, n_s = xp.shape[0], xs.shape[0]
    n_tok = n_p + n_s
    npt = n_p // TOK_TILE
    tiles_per_sample_seq = sample_seq // TOK_TILE
    tri = jnp.asarray(np.tril(np.ones((TOK_TILE, TOK_TILE), np.float32), -1), BF16)

    def mod_row(i):
        return jnp.where(i < npt, 0, 1 + (i - npt) // tiles_per_sample_seq)

    const = lambda i: (0, 0)
    tok = lambda w: pl.BlockSpec((TOK_TILE, w), lambda i: (i, 0))
    return pl.pallas_call(
        functools.partial(_post_kernel, npt, alpha),
        out_shape=(jax.ShapeDtypeStruct((n_tok, D_MODEL), F32),
                   jax.ShapeDtypeStruct((n_tok * SUBLANES, LANES), F32),
                   jax.ShapeDtypeStruct((N_EXPERTS, n_tok), F32),
                   jax.ShapeDtypeStruct((N_EXPERTS, n_tok), F32),
                   jax.ShapeDtypeStruct((n_tok, LANES), F32),
                   jax.ShapeDtypeStruct((1, LANES), F32),
                   jax.ShapeDtypeStruct((n_tok // TOK_TILE, 1, LANES), F32)),
        grid=(n_tok // TOK_TILE,),
        in_specs=[tok(REC_WIDTH), tok(REC_WIDTH),
                  pl.BlockSpec((TOK_TILE, REC_WIDTH), lambda i: (i, 4)),
                  tok(CONV_WIDTH),
                  pl.BlockSpec((TOK_TILE, D_MODEL), lambda i: (jnp.minimum(i, npt - 1), 0)),
                  pl.BlockSpec((TOK_TILE, D_MODEL), lambda i: (jnp.maximum(i - npt, 0), 0)),
                  pl.BlockSpec((1, 1, 6 * D_MODEL), lambda i: (mod_row(i), 0, 0)),
                  pl.BlockSpec(w_out_bf.shape, const),
                  pl.BlockSpec((1, REC_WIDTH), const),
                  pl.BlockSpec((1, D_MODEL), const), pl.BlockSpec((1, D_MODEL), const),
                  pl.BlockSpec((D_MODEL, LANES), const), pl.BlockSpec((D_MODEL, LANES), const),
                  pl.BlockSpec((1, LANES), const),
                  pl.BlockSpec((TOK_TILE, TOK_TILE), const)],
        out_specs=(tok(D_MODEL), pl.BlockSpec((TOK_TILE * SUBLANES, LANES), lambda i: (i, 0)),
                   pl.BlockSpec((N_EXPERTS, TOK_TILE), lambda i: (0, i)),
                   pl.BlockSpec((N_EXPERTS, TOK_TILE), lambda i: (0, i)),
                   tok(LANES), pl.BlockSpec((1, LANES), const),
                   pl.BlockSpec((1, 1, LANES), lambda i: (i, 0, 0))),
        scratch_shapes=[pltpu.VMEM((1, LANES), F32)],
        compiler_params=pltpu.CompilerParams(dimension_semantics=("arbitrary",),
                                             vmem_limit_bytes=VMEM_LIMIT),
        name="post_mixer_router",
    )(o_f, o_b, rec, conv, xp, xs, mod3, w_out_bf, rec_g, ln1_g, ln1_b, rw_hi, rw_lo, rb, tri)


PLAN_ROWS = 32


def _plan_kernel(n_tok, tile_e, gstart, n_act, cum, starts_ref, ends_ref, rank_t_ref, slot_t_ref,
                 src_ref, dst_ref, acc_ref):
    i = pl.program_id(0)

    @pl.when(i < n_act[0])
    def _():
        e = tile_e[i]
        base = i * EXP_TILE - gstart[e]
        rows_iota = lax.broadcasted_iota(jnp.int32, (PLAN_ROWS, 1), 0)
        lane = lax.broadcasted_iota(jnp.int32, (1, TOK_TILE), 1)
        acc_ref[...] = jnp.zeros_like(acc_ref)

        base_f = base.astype(F32)
        win0 = jnp.sum((ends_ref[pl.ds(e, 1), :] <= base_f).astype(jnp.int32))
        win1 = jnp.sum((starts_ref[pl.ds(e, 1), :] < base_f + EXP_TILE).astype(jnp.int32))

        def window(b, carry):
            tok0 = pl.multiple_of(b * TOK_TILE, TOK_TILE)
            rk = rank_t_ref[pl.ds(e, 1), pl.ds(tok0, TOK_TILE)]
            sl = slot_t_ref[pl.ds(e, 1), pl.ds(tok0, TOK_TILE)]
            code = ((tok0 + lane) * TOP_K).astype(F32) + sl + 1.0
            lo = jnp.maximum(cum[b * N_EXPERTS + e] - base, 0)
            hi = jnp.minimum(cum[(b + 1) * N_EXPERTS + e] - base, EXP_TILE)

            def rows(k, c):
                r0 = pl.multiple_of(k * PLAN_ROWS, PLAN_ROWS)
                want = (base + r0 + rows_iota).astype(F32)
                hit = jnp.where(rk == want, code, 0.0)
                part = hit[:, 0:LANES]
                for c0 in range(LANES, TOK_TILE, LANES):
                    part = part + hit[:, c0:c0 + LANES]
                acc_ref[pl.ds(r0, PLAN_ROWS), :] += part
                return c

            shift = PLAN_ROWS.bit_length() - 1
            lax.fori_loop(lo >> shift, (hi + PLAN_ROWS - 1) >> shift, rows, 0)
            return carry

        lax.fori_loop(win0, win1, window, 0)
        code = jnp.sum(acc_ref[...].T, axis=0, keepdims=True).astype(jnp.int32) - 1
        pos = lax.broadcasted_iota(jnp.int32, (1, EXP_TILE), 1)
        tok = code >> 2
        src = jnp.where(code >= 0, tok, lax.rem(i * EXP_TILE + pos, n_tok))
        spare = TOP_K * n_tok + lax.rem(i, N_YBUF) * EXP_TILE + pos
        dst = jnp.where(code >= 0, (code & (TOP_K - 1)) * n_tok + tok, spare)
        src_ref[0] = src * SUBLANES
        dst_ref[0] = dst * SUBLANES

    @pl.when(i >= n_act[0])
    def _():
        src_ref[...] = jnp.zeros_like(src_ref)
        dst_ref[...] = jnp.zeros_like(dst_ref)


def _plan(rank_t, slot_t, tile_e, gstart, n_act, cum2d):
    n_tok = rank_t.shape[1]
    n_tiles = tile_e.shape[0]
    n_tok_tiles = cum2d.shape[0] - 1
    assert n_tok_tiles <= LANES
    huge = jnp.float32(2.0 ** 30)
    as_rows = lambda a: jnp.pad(a.T.astype(F32), ((0, 0), (0, LANES - n_tok_tiles)),
                                constant_values=huge)
    whole = lambda i, *_: (0, 0)
    per_tile = pl.BlockSpec((1, 1, EXP_TILE), lambda i, *_: (i, 0, 0))
    grid_spec = pltpu.PrefetchScalarGridSpec(
        num_scalar_prefetch=4,
        grid=(n_tiles,),
        in_specs=[pl.BlockSpec((N_EXPERTS, LANES), whole), pl.BlockSpec((N_EXPERTS, LANES), whole),
                  pl.BlockSpec(rank_t.shape, whole), pl.BlockSpec(slot_t.shape, whole)],
        out_specs=(per_tile, per_tile),
        scratch_shapes=[pltpu.VMEM((EXP_TILE, LANES), F32)],
    )
    shape = jax.ShapeDtypeStruct((n_tiles, 1, EXP_TILE), jnp.int32)
    return pl.pallas_call(
        functools.partial(_plan_kernel, n_tok),
        out_shape=(shape, shape),
        grid_spec=grid_spec,
        compiler_params=pltpu.CompilerParams(dimension_semantics=("arbitrary",),
                                             vmem_limit_bytes=VMEM_LIMIT),
        name="moe_plan",
    )(tile_e, gstart, n_act, cum2d.reshape(-1), as_rows(cum2d[:-1]), as_rows(cum2d[1:]),
      rank_t, slot_t)


N_YBUF = 3
DMA_UNROLL = 8


def _expert_kernel(tile_e, n_act, src0_ref, srcn_ref, dst_ref, h_ref, wgu_ref, bgu_ref, wd_ref,
                   bd_ref, yk_ref, xbuf, ybuf, gsem, ssem, wgu_bf, wd_bf):
    i = pl.program_id(0)
    n_steps = pl.num_programs(0)
    n_active = n_act[0]
    tile_rows = EXP_TILE * SUBLANES
    n_slot_rows = yk_ref.shape[0] - N_YBUF * tile_rows

    def row_tile(ref, start):
        return ref.at[pl.ds(pl.multiple_of(start, SUBLANES), SUBLANES)]

    def for_each_row(fn):
        def body(g, carry):
            for k in range(DMA_UNROLL):
                fn(g * DMA_UNROLL + k)
            return carry
        lax.fori_loop(0, EXP_TILE // DMA_UNROLL, body, 0)

    def on_slot(slot, n, fn):
        if isinstance(slot, int):
            fn(slot)
            return
        for s in range(n):
            pl.when(slot == s)(functools.partial(fn, s))

    def gather_rows(src_ref, slot):
        def run(s):
            def one(r):
                pltpu.make_async_copy(row_tile(h_ref, src_ref[0, 0, r]),
                                      row_tile(xbuf.at[s], r * SUBLANES), gsem.at[s]).start()
            for_each_row(one)
        on_slot(slot, 2, run)

    def scatter_rows(slot):
        def run(s):
            def one(r):
                pltpu.make_async_copy(row_tile(ybuf.at[s], r * SUBLANES),
                                      row_tile(yk_ref, dst_ref[0, 0, r]), ssem.at[s]).start()
            for_each_row(one)
        on_slot(slot, N_YBUF, run)

    def scatter_wait(slot):
        pltpu.make_async_copy(ybuf.at[slot], yk_ref.at[pl.ds(0, tile_rows)], ssem.at[slot]).wait()

    @pl.when(i == 0)
    def _():
        ybuf[0] = jnp.zeros((tile_rows, LANES), F32)
        for spare in range(N_YBUF):
            cp = pltpu.make_async_copy(
                ybuf.at[0], yk_ref.at[pl.ds(n_slot_rows + spare * tile_rows, tile_rows)],
                ssem.at[0])
            cp.start()
            cp.wait()
        gather_rows(src0_ref, 0)

    @pl.when(i + 1 < n_active)
    def _():
        gather_rows(srcn_ref, (i + 1) % 2)

    prev = tile_e[jnp.maximum(i - 1, 0)]
    new_expert = jnp.logical_or(i == 0, tile_e[i] != prev)

    @pl.when(jnp.logical_and(new_expert, i < n_active))
    def _():
        rows = 128

        def cast(r, carry):
            sl = pl.ds(pl.multiple_of(r * rows, rows), rows)
            wgu_bf[sl, :] = wgu_ref[0, sl, :].astype(BF16)
            wd_bf[sl, :] = wd_ref[0, sl, :].astype(BF16)
            return carry

        lax.fori_loop(0, D_MODEL // rows, cast, 0)

    @pl.when(i < n_active)
    def _():
        xslot = i % 2
        yslot = i % N_YBUF
        pltpu.make_async_copy(h_ref.at[pl.ds(0, tile_rows)], xbuf.at[xslot], gsem.at[xslot]).wait()

        @pl.when(i >= N_YBUF)
        def _():
            scatter_wait(yslot)

        x = _load_row_tiles(xbuf.at[xslot], EXP_TILE).astype(BF16)
        gu = _dot(x, wgu_bf[...]) + bgu_ref[0]
        gate = jnp.minimum(gu[:, :D_FF], SWIGLU_LIMIT)
        up = jnp.clip(gu[:, D_FF:], -SWIGLU_LIMIT, SWIGLU_LIMIT)
        act = (up + 1.0) * gate * _sigmoid(SWIGLU_ALPHA * gate)
        _store_row_tiles(ybuf.at[yslot], _dot(act.astype(BF16), wd_bf[...]) + bd_ref[0])
        scatter_rows(yslot)

    @pl.when(i == n_steps - 1)
    def _():
        for back in range(1, N_YBUF + 1):
            scatter_wait((n_active - back) % N_YBUF)


def _experts(h2, src3, dst3, tile_e, n_act, w_gate_up, b_gate_up, w_down, b_down):
    n_tok = h2.shape[0] // SUBLANES
    n_tiles = src3.shape[0]
    idx_block = (1, 1, EXP_TILE)
    grid_spec = pltpu.PrefetchScalarGridSpec(
        num_scalar_prefetch=2,
        grid=(n_tiles,),
        in_specs=[pl.BlockSpec(idx_block, lambda i, te, na: (0, 0, 0), memory_space=pltpu.SMEM),
                  pl.BlockSpec(idx_block, lambda i, te, na: (jnp.minimum(i + 1, n_tiles - 1), 0, 0),
                               memory_space=pltpu.SMEM),
                  pl.BlockSpec(idx_block, lambda i, te, na: (i, 0, 0), memory_space=pltpu.SMEM),
                  pl.BlockSpec(memory_space=pl.ANY),
                  pl.BlockSpec((1, D_MODEL, 2 * D_FF), lambda i, te, na: (te[i], 0, 0)),
                  pl.BlockSpec((1, 1, 2 * D_FF), lambda i, te, na: (te[i], 0, 0)),
                  pl.BlockSpec((1, D_FF, D_MODEL), lambda i, te, na: (te[i], 0, 0)),
                  pl.BlockSpec((1, 1, D_MODEL), lambda i, te, na: (te[i], 0, 0))],
        out_specs=pl.BlockSpec(memory_space=pl.ANY),
        scratch_shapes=[pltpu.VMEM((2, EXP_TILE * SUBLANES, LANES), F32),
                        pltpu.VMEM((N_YBUF, EXP_TILE * SUBLANES, LANES), F32),
                        pltpu.SemaphoreType.DMA((2,)), pltpu.SemaphoreType.DMA((N_YBUF,)),
                        pltpu.VMEM((D_MODEL, 2 * D_FF), BF16), pltpu.VMEM((D_FF, D_MODEL), BF16)],
    )
    return pl.pallas_call(
        _expert_kernel,
        out_shape=jax.ShapeDtypeStruct(((TOP_K * n_tok + N_YBUF * EXP_TILE) * SUBLANES, LANES), F32),
        grid_spec=grid_spec,
        compiler_params=pltpu.CompilerParams(dimension_semantics=("arbitrary",),
                                             vmem_limit_bytes=VMEM_LIMIT),
        name="moe_experts",
    )(tile_e, n_act, src3, src3, dst3, h2, w_gate_up, b_gate_up.reshape(N_EXPERTS, 1, 2 * D_FF),
      w_down, b_down.reshape(N_EXPERTS, 1, D_MODEL))


def _combine_kernel(n_prompt_tiles, alpha, y0_ref, y1_ref, y2_ref, y3_ref, x1_ref, w4_ref, mod_ref,
                    l2g_ref, l2b_ref, yp_ref, ysm_ref):
    i = pl.program_id(0)
    w4 = w4_ref[...]
    ff = _load_row_tiles(y0_ref, TOK_TILE) * w4[:, 0:1]
    for j, y_ref in enumerate((y1_ref, y2_ref, y3_ref), start=1):
        ff = ff + _load_row_tiles(y_ref, TOK_TILE) * w4[:, j:j + 1]
    g2 = mod_ref[0][:, 5 * D_MODEL:6 * D_MODEL]
    out = _layer_norm(alpha * x1_ref[...] + g2 * ff) * l2g_ref[...] + l2b_ref[...]

    @pl.when(i < n_prompt_tiles)
    def _():
        yp_ref[...] = out

    @pl.when(i >= n_prompt_tiles)
    def _():
        ysm_ref[...] = out


def _combine(yk, x1, w4, mod3, ln2_g, ln2_b, n_p, alpha, sample_seq):
    n_tok = x1.shape[0]
    n_s = n_tok - n_p
    npt = n_p // TOK_TILE
    n_tok_tiles = n_tok // TOK_TILE
    tiles_per_sample_seq = sample_seq // TOK_TILE

    def mod_row(i):
        return jnp.where(i < npt, 0, 1 + (i - npt) // tiles_per_sample_seq)

    def slot(j):
        return pl.BlockSpec((TOK_TILE * SUBLANES, LANES), lambda i: (j * n_tok_tiles + i, 0))

    const = lambda i: (0, 0)
    return pl.pallas_call(
        functools.partial(_combine_kernel, npt, alpha),
        out_shape=(jax.ShapeDtypeStruct((n_p, D_MODEL), F32),
                   jax.ShapeDtypeStruct((n_s, D_MODEL), F32)),
        grid=(n_tok_tiles,),
        in_specs=[slot(0), slot(1), slot(2), slot(3),
                  pl.BlockSpec((TOK_TILE, D_MODEL), lambda i: (i, 0)),
                  pl.BlockSpec((TOK_TILE, LANES), lambda i: (i, 0)),
                  pl.BlockSpec((1, 1, 6 * D_MODEL), lambda i: (mod_row(i), 0, 0)),
                  pl.BlockSpec((1, D_MODEL), const), pl.BlockSpec((1, D_MODEL), const)],
        out_specs=(pl.BlockSpec((TOK_TILE, D_MODEL), lambda i: (jnp.minimum(i, npt - 1), 0)),
                   pl.BlockSpec((TOK_TILE, D_MODEL), lambda i: (jnp.maximum(i - npt, 0), 0))),
        compiler_params=pltpu.CompilerParams(dimension_semantics=("arbitrary",),
                                             vmem_limit_bytes=VMEM_LIMIT),
        name="moe_combine",
    )(yk, yk, yk, yk, x1, w4, mod3, ln2_g, ln2_b)


def kernel(x_prompt, x_sample, c, state_fwd, state_bwd, c_ctx, w_ada, b_ada, w_in, conv_w, conv_b,
           conv_ln_g, conv_ln_b, lb_logits, rec_norm_g, w_out, ln1_g, ln1_b, ln2_g, ln2_b,
           router_w, router_b, w_gate_up, b_gate_up, w_down, b_down):
    depth = w_ada.shape[0]
    assert depth == 1
    alpha = (2.0 * depth) ** 0.25
    nb_p, seq_p, _ = x_prompt.shape
    nb_s, seq_s, _ = x_sample.shape
    n_p, n_s = nb_p * seq_p, nb_s * seq_s
    n_tok = n_p + n_s
    row2 = lambda a: a.reshape(1, -1)

    lb = jnp.cumsum(jax.nn.softmax(lb_logits.astype(F32), axis=0), axis=0)[0]

    cond = jnp.concatenate([c_ctx[None, :], c, jnp.zeros((16 - 1 - nb_s, D_MODEL), F32)], axis=0)
    mod = _ada(cond, w_ada[0], b_ada[0])
    mod3 = mod.reshape(16, 1, 6 * D_MODEL)

    xp = x_prompt.reshape(n_p, D_MODEL)
    xs = x_sample.reshape(n_s, D_MODEL)
    conv_w_pad = jnp.concatenate([conv_w[0], jnp.zeros((1, CONV_WIDTH), F32)], axis=0)
    conv_out, rec = _inproj(xp, xs, mod3, w_in[0].astype(BF16), conv_w_pad, row2(conv_b[0]),
                            row2(conv_ln_g[0]), row2(conv_ln_b[0]), seq_p, seq_s)

    zeros_state = jnp.zeros((nb_p, REC_HEADS, REC_DK, REC_DK), F32)
    s0f = jnp.concatenate([zeros_state, jnp.swapaxes(state_fwd[:, 0], -1, -2)], axis=0)
    s0b = jnp.concatenate([zeros_state, jnp.swapaxes(state_bwd[:, 0], -1, -2)], axis=0)
    o_f, o_b, sf_t, sb_t = _scan(rec, lb, s0f, s0b, [seq_p] * nb_p + [seq_s] * nb_s)
    new_f = jnp.swapaxes(sf_t[:nb_p], -1, -2)[:, None]
    new_b = jnp.swapaxes(sb_t[:nb_p], -1, -2)[:, None]

    rw = jnp.pad(router_w[0], ((0, 0), (0, LANES - N_EXPERTS)))
    rw_hi = rw.astype(BF16)
    rw_lo = (rw - rw_hi.astype(F32)).astype(BF16)
    rb = jnp.pad(router_b[0], (0, LANES - N_EXPERTS)).reshape(1, LANES)
    x1, h2, rank_t, slot_t, w4, counts, tile_start = _post(
        o_f, o_b, rec, conv_out, xp, xs, mod3, w_out[0].astype(BF16), row2(rec_norm_g[0]),
        row2(ln1_g[0]), row2(ln1_b[0]), rw_hi, rw_lo, rb, alpha, seq_s)

    cnt = counts[0, :N_EXPERTS].astype(jnp.int32)
    gpad = ((cnt + EXP_TILE - 1) // EXP_TILE) * EXP_TILE
    gend = jnp.cumsum(gpad)
    gstart = gend - gpad
    n_rows = n_tok * TOP_K + N_EXPERTS * EXP_TILE
    n_tiles = n_rows // EXP_TILE
    n_act = gend[-1] // EXP_TILE
    tile_ids = jnp.minimum(jnp.arange(n_tiles, dtype=jnp.int32), n_act - 1)
    tile_e = jnp.sum((gend[None, :] <= tile_ids[:, None] * EXP_TILE).astype(jnp.int32), axis=1)
    n_act1 = n_act.reshape(1).astype(jnp.int32)
    starts = tile_start[:, 0, :N_EXPERTS].astype(jnp.int32)
    cum2d = jnp.concatenate([starts, cnt[None, :]], axis=0)
    src, dst = _plan(rank_t, slot_t, tile_e, gstart.astype(jnp.int32), n_act1, cum2d)
    yk = _experts(h2, src, dst, tile_e, n_act1, w_gate_up[0], b_gate_up[0], w_down[0], b_down[0])
    y_p, y_s = _combine(yk, x1, w4, mod3, row2(ln2_g[0]), row2(ln2_b[0]), n_p, alpha, seq_s)
    return (y_p.reshape(nb_p, seq_p, D_MODEL), y_s.reshape(nb_s, seq_s, D_MODEL), new_f, new_b)
```

```python
import functools

import numpy as np
import jax
import jax.numpy as jnp
from jax import lax
from jax.experimental import pallas as pl
from jax.experimental.pallas import tpu as pltpu

F32 = jnp.float32
BF16 = jnp.bfloat16

D_MODEL = 1024
CONV_WIDTH = 512
CONV_TAPS = 31
REC_HEADS = 4
REC_DK = 128
REC_WIDTH = REC_HEADS * REC_DK
REC_COLS = 5 * REC_WIDTH
CHUNK = 64
N_LEVELS = 6
SCAN_CHUNKS = 4
N_EXPERTS = 32
TOP_K = 4
D_FF = 1024
SWIGLU_LIMIT = 7.0
SWIGLU_ALPHA = 1.702
LN_EPS = 1e-5
RMS_EPS = 1e-6
GRID_W = 64

LANES = 128
TOK_TILE = 256
EXP_TILE = 512
VMEM_LIMIT = 56 * 1024 * 1024


def _sigmoid(x):
    return 1.0 / (1.0 + jnp.exp(-x))


def _layer_norm(x):
    mu = jnp.mean(x, axis=-1, keepdims=True)
    xc = x - mu
    var = jnp.mean(xc * xc, axis=-1, keepdims=True)
    return xc * lax.rsqrt(var + LN_EPS)


def _dot(a, b):
    return jnp.dot(a, b, preferred_element_type=F32)


def _dot_nt(a, b):
    return lax.dot_general(a, b, (((1,), (1,)), ((), ())), preferred_element_type=F32)


SUBLANES = 8
assert D_MODEL == SUBLANES * LANES


def _store_row_tiles(ref, value):
    for s in range(SUBLANES):
        ref[pl.ds(s, value.shape[0], stride=SUBLANES), :] = value[:, s * LANES:(s + 1) * LANES]


def _load_row_tiles(ref, n_rows):
    return jnp.concatenate(
        [ref[pl.ds(s, n_rows, stride=SUBLANES), :] for s in range(SUBLANES)], axis=-1)


def _split3(x):
    hi = x.astype(BF16)
    r1 = x - hi.astype(F32)
    mid = r1.astype(BF16)
    lo = (r1 - mid.astype(F32)).astype(BF16)
    return hi, mid, lo


def _ada_kernel(c_ref, w_ref, b_ref, o_ref):
    c = c_ref[...]
    s = (c * _sigmoid(c)).astype(BF16)
    o_ref[...] = _dot(s, w_ref[...].astype(BF16)) + b_ref[...]


def _ada(cond, w_ada, b_ada):
    rows = cond.shape[0]
    ncol = w_ada.shape[1]
    tn = 1024
    return pl.pallas_call(
        _ada_kernel,
        out_shape=jax.ShapeDtypeStruct((rows, ncol), F32),
        grid=(ncol // tn,),
        in_specs=[pl.BlockSpec((rows, D_MODEL), lambda j: (0, 0)),
                  pl.BlockSpec((D_MODEL, tn), lambda j: (0, j)),
                  pl.BlockSpec((1, tn), lambda j: (0, j))],
        out_specs=pl.BlockSpec((rows, tn), lambda j: (0, j)),
        compiler_params=pltpu.CompilerParams(dimension_semantics=("parallel",),
                                             vmem_limit_bytes=VMEM_LIMIT),
        name="ada_mod",
    )(cond, w_ada, b_ada.reshape(1, ncol))


def _inproj_kernel(n_prompt_tiles, prompt_seg, xp_ref, xs_ref, mod_ref, win_ref, cw_ref, cb_ref,
                   cg_ref, cbeta_ref, conv_ref, rec_ref):
    i = pl.program_id(0)
    is_prompt = i < n_prompt_tiles
    x = jnp.where(is_prompt, xp_ref[...], xs_ref[...])
    mod = mod_ref[0]
    sh1 = mod[:, 0:D_MODEL]
    sc1 = mod[:, D_MODEL:2 * D_MODEL]
    h = (_layer_norm(x) * (1.0 + sc1) + sh1).astype(BF16)

    glu = _dot(h, win_ref[:, 0:2 * CONV_WIDTH])
    u = glu[:, :CONV_WIDTH] * _sigmoid(glu[:, CONV_WIDTH:])

    seg_len = jnp.where(is_prompt, prompt_seg, GRID_W)
    half = CONV_TAPS // 2
    pos = lax.broadcasted_iota(jnp.int32, (TOK_TILE, 1), 0) & (seg_len - 1)

    def masked_shift(d):
        sh = pltpu.roll(u, (-d) % TOK_TILE, axis=0)
        edge_rows = SUBLANES * ((abs(d) + SUBLANES - 1) // SUBLANES)
        pieces = []
        for s0 in range(0, TOK_TILE, GRID_W):
            if d < 0:
                edge = slice(s0, s0 + edge_rows)
                pieces.append(jnp.where(pos[edge] >= -d, sh[edge], 0.0))
                pieces.append(sh[s0 + edge_rows:s0 + GRID_W])
            else:
                edge = slice(s0 + GRID_W - edge_rows, s0 + GRID_W)
                pieces.append(sh[s0:s0 + GRID_W - edge_rows])
                pieces.append(jnp.where(pos[edge] < seg_len - d, sh[edge], 0.0))
        return jnp.concatenate(pieces, axis=0)

    acc = u * cw_ref[half:half + 1, :] + cb_ref[...]
    taps = [j for j in range(CONV_TAPS) if j != half]
    n_groups = REC_COLS // REC_WIDTH
    per_group = len(taps) // n_groups
    anchor = None
    for g in range(n_groups):
        cols = slice(g * REC_WIDTH, (g + 1) * REC_WIDTH)
        rec = _dot(h, win_ref[:, 2 * CONV_WIDTH + g * REC_WIDTH:
                              2 * CONV_WIDTH + (g + 1) * REC_WIDTH])
        rec_ref[:, cols] = rec
        for j in taps[g * per_group:(g + 1) * per_group]:
            w_tap = cw_ref[j:j + 1, :]
            if anchor is not None:
                w_tap = w_tap + anchor
                anchor = None
            acc = acc + masked_shift(j - half) * w_tap
        if g + 1 < n_groups:
            bits = pltpu.bitcast(rec[0:SUBLANES, :], jnp.uint32)
            anchor = pltpu.bitcast((bits >> 16) >> 16, F32)[0:1, :]
    y = _layer_norm(acc) * cg_ref[...] + cbeta_ref[...]
    conv_ref[...] = (y * _sigmoid(y)).astype(BF16)


def _inproj(xp, xs, mod3, w_in_bf, conv_w, conv_b, conv_g, conv_beta, prompt_seq, sample_seq):
    n_p, n_s = xp.shape[0], xs.shape[0]
    n_tok = n_p + n_s
    npt = n_p // TOK_TILE
    tiles_per_sample_seq = sample_seq // TOK_TILE
    assert prompt_seq == TOK_TILE and sample_seq % TOK_TILE == 0

    def mod_row(i):
        return jnp.where(i < npt, 0, 1 + (i - npt) // tiles_per_sample_seq)

    const = lambda i: (0, 0)
    return pl.pallas_call(
        functools.partial(_inproj_kernel, npt, prompt_seq),
        out_shape=(jax.ShapeDtypeStruct((n_tok, CONV_WIDTH), BF16),
                   jax.ShapeDtypeStruct((n_tok, REC_COLS), F32)),
        grid=(n_tok // TOK_TILE,),
        in_specs=[pl.BlockSpec((TOK_TILE, D_MODEL), lambda i: (jnp.minimum(i, npt - 1), 0)),
                  pl.BlockSpec((TOK_TILE, D_MODEL), lambda i: (jnp.maximum(i - npt, 0), 0)),
                  pl.BlockSpec((1, 1, 6 * D_MODEL), lambda i: (mod_row(i), 0, 0)),
                  pl.BlockSpec(w_in_bf.shape, const),
                  pl.BlockSpec(conv_w.shape, const),
                  pl.BlockSpec((1, CONV_WIDTH), const),
                  pl.BlockSpec((1, CONV_WIDTH), const),
                  pl.BlockSpec((1, CONV_WIDTH), const)],
        out_specs=(pl.BlockSpec((TOK_TILE, CONV_WIDTH), lambda i: (i, 0)),
                   pl.BlockSpec((TOK_TILE, REC_COLS), lambda i: (i, 0))),
        compiler_params=pltpu.CompilerParams(dimension_semantics=("parallel",),
                                             vmem_limit_bytes=VMEM_LIMIT),
        name="inproj_conv",
    )(xp, xs, mod3, w_in_bf, conv_w, conv_b, conv_g, conv_beta)


def _scan_constants():
    c = CHUNK
    w = np.zeros((N_LEVELS + 2, c, c), np.float32)
    m = np.zeros((N_LEVELS + 1, c, c), np.float32)
    t = np.arange(c)
    for j in range(N_LEVELS):
        half = 1 << j
        for ti in range(c):
            ref = (ti & ~(2 * half - 1)) + half - 1
            if ti & half:
                w[j, ti, ref + 1:ti + 1] = 1.0
            else:
                w[j, ti, ti + 1:ref + 1] = 1.0
        upper = (t[:, None] & half) != 0
        lower = (t[None, :] & half) == 0
        same = (t[:, None] >> (j + 1)) == (t[None, :] >> (j + 1))
        m[j] = (upper & lower & same).astype(np.float32)
    w[N_LEVELS] = (t[None, :] <= t[:, None]).astype(np.float32)
    w[N_LEVELS + 1] = (t[None, :] > t[:, None]).astype(np.float32)
    m[N_LEVELS] = np.eye(c, dtype=np.float32)
    w_f = np.tile(w.reshape(-1, c), (1, 3))
    w_b = np.tile(w[:, ::-1, ::-1].reshape(-1, c), (1, 3))
    m_b = m[:, ::-1, ::-1]
    return (jnp.asarray(w_f, BF16), jnp.asarray(np.ascontiguousarray(w_b), BF16),
            jnp.asarray(m), jnp.asarray(np.ascontiguousarray(m_b)))


def _scan_gates(z, lb, w_ref):
    e = jnp.exp(-jnp.abs(z))
    r = 1.0 / (1.0 + e)
    a = e * r
    sig = jnp.where(z >= 0, r, a)
    sig_neg = jnp.where(z >= 0, a, r)
    one_m_lb = 1.0 - lb
    log_f = jnp.log2(lb + one_m_lb * sig)
    k = one_m_lb * sig_neg
    terms = jnp.concatenate(_split3(log_f), axis=0)
    ex = jnp.exp2(_dot(w_ref[...], terms))
    return k, ex


def _scan_intra(q, k, ex, m_ref):
    c = CHUNK
    a_mat = m_ref[N_LEVELS] * _dot_nt(q.astype(BF16), k.astype(BF16))
    for j in range(N_LEVELS):
        xj = ex[j * c:(j + 1) * c]
        a_mat = a_mat + m_ref[j] * _dot_nt((q * xj).astype(BF16), (k * xj).astype(BF16))
    return a_mat


def _scan_kernel(fblk, bblk, first, seq, qf_ref, zf_ref, vf_ref, qb_ref, zb_ref, vb_ref, lb_ref,
                 wf_ref, wb_ref, mf_ref, mb_ref, s0f_ref, s0b_ref, of_ref, ob_ref, sf_ref, sb_ref):
    s = pl.program_id(0)

    @pl.when(first[s] == 1)
    def _():
        sf_ref[...] = s0f_ref[...]
        sb_ref[...] = s0b_ref[...]

    c = CHUNK
    dirs = ((qf_ref, zf_ref, vf_ref, 0, wf_ref, mf_ref, sf_ref, of_ref, c - 1,
             tuple(range(SCAN_CHUNKS))),
            (qb_ref, zb_ref, vb_ref, 1, wb_ref, mb_ref, sb_ref, ob_ref, 0,
             tuple(reversed(range(SCAN_CHUNKS)))))
    units = [(d, slice(sub * c, (sub + 1) * c)) for d in dirs for sub in d[-1]]
    gates = [_scan_gates(d[1][rows, :], lb_ref[d[3]:d[3] + 1, :], d[4]) for d, rows in units]
    intra = []
    for (d, rows), (k, ex) in zip(units, gates):
        for h in range(REC_HEADS):
            cols = slice(h * REC_DK, (h + 1) * REC_DK)
            intra.append(_scan_intra(d[0][rows, cols], k[:, cols], ex[:, cols], d[5]))
    idx = 0
    for (d, rows), (k, ex) in zip(units, gates):
        q_ref, _, v_ref, _, _, _, st_ref, o_ref, last, _ = d
        for h in range(REC_HEADS):
            cols = slice(h * REC_DK, (h + 1) * REC_DK)
            q, v = q_ref[rows, cols], v_ref[rows, cols]
            ex_b = ex[N_LEVELS * c:(N_LEVELS + 1) * c, cols]
            ex_l = ex[(N_LEVELS + 1) * c:(N_LEVELS + 2) * c, cols]
            st = st_ref[0, h]
            o_ref[rows, cols] = (_dot(intra[idx].astype(BF16), v.astype(BF16))
                                 + _dot_nt((q * ex_b).astype(BF16), st.astype(BF16)))
            st_ref[0, h] = (st * ex_b[last:last + 1, :]
                            + _dot(v.T.astype(BF16), (k[:, cols] * ex_l).astype(BF16)))
            idx += 1


def _scan(rec, lb, s0f_t, s0b_t, seq_lens):
    n_tok = rec.shape[0]
    fblk, bblk, first, seq = [], [], [], []
    base = 0
    step_rows = CHUNK * SCAN_CHUNKS
    for si, ln in enumerate(seq_lens):
        assert ln % step_rows == 0
        n = ln // step_rows
        for ci in range(n):
            fblk.append(base + ci)
            bblk.append(base + n - 1 - ci)
            first.append(1 if ci == 0 else 0)
            seq.append(si)
        base += n
    steps = len(fblk)
    as_i32 = lambda a: jnp.asarray(np.asarray(a, np.int32))
    w_f, w_b, m_f, m_b = _scan_constants()
    n_seq = len(seq_lens)

    def col(block_of, c):
        return pl.BlockSpec((step_rows, REC_WIDTH), lambda s, fb, bb, fi, sq: (block_of(fb, bb)[s], c))

    fwd = lambda fb, bb: fb
    bwd = lambda fb, bb: bb
    const2 = lambda s, fb, bb, fi, sq: (0, 0)
    const3 = lambda s, fb, bb, fi, sq: (0, 0, 0)
    state_spec = pl.BlockSpec((1, REC_HEADS, REC_DK, REC_DK), lambda s, fb, bb, fi, sq: (sq[s], 0, 0, 0))
    grid_spec = pltpu.PrefetchScalarGridSpec(
        num_scalar_prefetch=4,
        grid=(steps,),
        in_specs=[col(fwd, 0), col(fwd, 1), col(fwd, 3), col(bwd, 0), col(bwd, 2), col(bwd, 3),
                  pl.BlockSpec((2, REC_WIDTH), const2),
                  pl.BlockSpec(w_f.shape, const2), pl.BlockSpec(w_b.shape, const2),
                  pl.BlockSpec(m_f.shape, const3), pl.BlockSpec(m_b.shape, const3),
                  state_spec, state_spec],
        out_specs=(pl.BlockSpec((step_rows, REC_WIDTH), lambda s, fb, bb, fi, sq: (fb[s], 0)),
                   pl.BlockSpec((step_rows, REC_WIDTH), lambda s, fb, bb, fi, sq: (bb[s], 0)),
                   state_spec, state_spec),
    )
    st_shape = jax.ShapeDtypeStruct((n_seq, REC_HEADS, REC_DK, REC_DK), F32)
    return pl.pallas_call(
        _scan_kernel,
        out_shape=(jax.ShapeDtypeStruct((n_tok, REC_WIDTH), F32),
                   jax.ShapeDtypeStruct((n_tok, REC_WIDTH), F32), st_shape, st_shape),
        grid_spec=grid_spec,
        compiler_params=pltpu.CompilerParams(dimension_semantics=("arbitrary",),
                                             vmem_limit_bytes=VMEM_LIMIT),
        name="hgrn2_scan",
    )(as_i32(fblk), as_i32(bblk), as_i32(first), as_i32(seq),
      rec, rec, rec, rec, rec, rec, lb, w_f, w_b, m_f, m_b, s0f_t, s0b_t)


def _post_kernel(n_prompt_tiles, alpha, of_ref, ob_ref, g_ref, conv_ref, xp_ref, xs_ref, mod_ref,
                 wout_ref, rg_ref, l1g_ref, l1b_ref, rwh_ref, rwl_ref, rb_ref, tri_ref,
                 x1_ref, h2_ref, rank_t_ref, slot_t_ref, w4_ref, cnt_ref, start_ref, run_ref):
    i = pl.program_id(0)

    @pl.when(i == 0)
    def _():
        run_ref[...] = jnp.zeros_like(run_ref)

    x = jnp.where(i < n_prompt_tiles, xp_ref[...], xs_ref[...])
    mod = mod_ref[0]
    g1 = mod[:, 2 * D_MODEL:3 * D_MODEL]
    sh2 = mod[:, 3 * D_MODEL:4 * D_MODEL]
    sc2 = mod[:, 4 * D_MODEL:5 * D_MODEL]

    g = g_ref[...]
    silu_g = g * _sigmoid(g)
    mix = _dot(conv_ref[...], wout_ref[0:CONV_WIDTH, :])
    for h in range(REC_HEADS):
        cols = slice(h * REC_DK, (h + 1) * REC_DK)
        o = of_ref[:, cols] + ob_ref[:, cols]
        o = o * lax.rsqrt(jnp.mean(o * o, axis=-1, keepdims=True) + RMS_EPS)
        rec_out = (o * rg_ref[:, cols] * silu_g[:, cols]).astype(BF16)
        mix = mix + _dot(rec_out, wout_ref[CONV_WIDTH + h * REC_DK:CONV_WIDTH + (h + 1) * REC_DK, :])

    x1 = _layer_norm(alpha * x + g1 * mix) * l1g_ref[...] + l1b_ref[...]
    x1_ref[...] = x1
    h2 = _layer_norm(x1) * (1.0 + sc2) + sh2
    _store_row_tiles(h2_ref, h2)

    h_hi = h2.astype(BF16)
    h_lo = (h2 - h_hi.astype(F32)).astype(BF16)
    logits = (_dot(h_hi, rwh_ref[...]) + _dot(h_lo, rwh_ref[...]) + _dot(h_hi, rwl_ref[...])
              + rb_ref[...])
    lane = lax.broadcasted_iota(jnp.int32, logits.shape, 1)
    lane_f = lane.astype(F32)
    neg_inf = jnp.float32(-jnp.inf)
    work = jnp.where(lane < N_EXPERTS, logits, neg_inf)
    vals, sels = [], []
    for _ in range(TOP_K):
        m = jnp.max(work, axis=-1, keepdims=True)
        idx = jnp.min(jnp.where(work == m, lane_f, float(LANES)), axis=-1, keepdims=True)
        sel = lane_f == idx
        vals.append(m)
        sels.append(sel)
        work = jnp.where(sel, neg_inf, work)
    exps = [jnp.exp(v - vals[0]) for v in vals]
    denom = exps[0] + exps[1] + exps[2] + exps[3]

    sel_any = jnp.zeros(logits.shape, F32)
    for sel in sels:
        sel_any = jnp.where(sel, 1.0, sel_any)
    run_old = run_ref[...]
    prior = _dot(tri_ref[...], sel_any.astype(BF16)) + run_old
    run_new = run_old + jnp.sum(sel_any, axis=0, keepdims=True)
    run_ref[...] = run_new
    cnt_ref[...] = run_new
    start_ref[0] = run_old

    slot = jnp.zeros(logits.shape, F32)
    w4 = jnp.zeros(logits.shape, F32)
    for j in range(TOP_K):
        slot = jnp.where(sels[j], float(j), slot)
        w4 = jnp.where(lane == j, exps[j] / denom, w4)
    routed = sel_any > 0.0
    rank_t_ref[...] = jnp.where(routed, prior, -1.0).T[:N_EXPERTS]
    slot_t_ref[...] = jnp.where(routed, slot, -1.0).T[:N_EXPERTS]
    w4_ref[...] = w4


def _post(o_f, o_b, rec, conv, xp, xs, mod3, w_out_bf, rec_g, ln1_g, ln1_b, rw_hi, rw_lo, rb,
          alpha, sample_seq):
    n_p, n_s = xp.shape[0], xs.shape[0]
    n_tok = n_p + n_s
    npt = n_p // TOK_TILE
    tiles_per_sample_seq = sample_seq // TOK_TILE
    tri = jnp.asarray(np.tril(np.ones((TOK_TILE, TOK_TILE), np.float32), -1), BF16)

    def mod_row(i):
        return jnp.where(i < npt, 0, 1 + (i - npt) // tiles_per_sample_seq)

    const = lambda i: (0, 0)
    tok = lambda w: pl.BlockSpec((TOK_TILE, w), lambda i: (i, 0))
    return pl.pallas_call(
        functools.partial(_post_kernel, npt, alpha),
        out_shape=(jax.ShapeDtypeStruct((n_tok, D_MODEL), F32),
                   jax.ShapeDtypeStruct((n_tok * SUBLANES, LANES), F32),
                   jax.ShapeDtypeStruct((N_EXPERTS, n_tok), F32),
                   jax.ShapeDtypeStruct((N_EXPERTS, n_tok), F32),
                   jax.ShapeDtypeStruct((n_tok, LANES), F32),
                   jax.ShapeDtypeStruct((1, LANES), F32),
                   jax.ShapeDtypeStruct((n_tok // TOK_TILE, 1, LANES), F32)),
        grid=(n_tok // TOK_TILE,),
        in_specs=[tok(REC_WIDTH), tok(REC_WIDTH),
                  pl.BlockSpec((TOK_TILE, REC_WIDTH), lambda i: (i, 4)),
                  tok(CONV_WIDTH),
                  pl.BlockSpec((TOK_TILE, D_MODEL), lambda i: (jnp.minimum(i, npt - 1), 0)),
                  pl.BlockSpec((TOK_TILE, D_MODEL), lambda i: (jnp.maximum(i - npt, 0), 0)),
                  pl.BlockSpec((1, 1, 6 * D_MODEL), lambda i: (mod_row(i), 0, 0)),
                  pl.BlockSpec(w_out_bf.shape, const),
                  pl.BlockSpec((1, REC_WIDTH), const),
                  pl.BlockSpec((1, D_MODEL), const), pl.BlockSpec((1, D_MODEL), const),
                  pl.BlockSpec((D_MODEL, LANES), const), pl.BlockSpec((D_MODEL, LANES), const),
                  pl.BlockSpec((1, LANES), const),
                  pl.BlockSpec((TOK_TILE, TOK_TILE), const)],
        out_specs=(tok(D_MODEL), pl.BlockSpec((TOK_TILE * SUBLANES, LANES), lambda i: (i, 0)),
                   pl.BlockSpec((N_EXPERTS, TOK_TILE), lambda i: (0, i)),
                   pl.BlockSpec((N_EXPERTS, TOK_TILE), lambda i: (0, i)),
                   tok(LANES), pl.BlockSpec((1, LANES), const),
                   pl.BlockSpec((1, 1, LANES), lambda i: (i, 0, 0))),
        scratch_shapes=[pltpu.VMEM((1, LANES), F32)],
        compiler_params=pltpu.CompilerParams(dimension_semantics=("arbitrary",),
                                             vmem_limit_bytes=VMEM_LIMIT),
        name="post_mixer_router",
    )(o_f, o_b, rec, conv, xp, xs, mod3, w_out_bf, rec_g, ln1_g, ln1_b, rw_hi, rw_lo, rb, tri)


PLAN_ROWS = 32


def _plan_kernel(n_tok, tile_e, gstart, n_act, cum, starts_ref, ends_ref, rank_t_ref, slot_t_ref,
                 src_ref, dst_ref, acc_ref):
    i = pl.program_id(0)

    @pl.when(i < n_act[0])
    def _():
        e = tile_e[i]
        base = i * EXP_TILE - gstart[e]
        rows_iota = lax.broadcasted_iota(jnp.int32, (PLAN_ROWS, 1), 0)
        lane = lax.broadcasted_iota(jnp.int32, (1, TOK_TILE), 1)
        acc_ref[...] = jnp.zeros_like(acc_ref)

        base_f = base.astype(F32)
        win0 = jnp.sum((ends_ref[pl.ds(e, 1), :] <= base_f).astype(jnp.int32))
        win1 = jnp.sum((starts_ref[pl.ds(e, 1), :] < base_f + EXP_TILE).astype(jnp.int32))

        def window(b, carry):
            tok0 = pl.multiple_of(b * TOK_TILE, TOK_TILE)
            rk = rank_t_ref[pl.ds(e, 1), pl.ds(tok0, TOK_TILE)]
            sl = slot_t_ref[pl.ds(e, 1), pl.ds(tok0, TOK_TILE)]
            code = ((tok0 + lane) * TOP_K).astype(F32) + sl + 1.0
            lo = jnp.maximum(cum[b * N_EXPERTS + e] - base, 0)
            hi = jnp.minimum(cum[(b + 1) * N_EXPERTS + e] - base, EXP_TILE)

            def rows(k, c):
                r0 = pl.multiple_of(k * PLAN_ROWS, PLAN_ROWS)
                want = (base + r0 + rows_iota).astype(F32)
                hit = jnp.where(rk == want, code, 0.0)
                part = hit[:, 0:LANES]
                for c0 in range(LANES, TOK_TILE, LANES):
                    part = part + hit[:, c0:c0 + LANES]
                acc_ref[pl.ds(r0, PLAN_ROWS), :] += part
                return c

            shift = PLAN_ROWS.bit_length() - 1
            lax.fori_loop(lo >> shift, (hi + PLAN_ROWS - 1) >> shift, rows, 0)
            return carry

        lax.fori_loop(win0, win1, window, 0)
        code = jnp.sum(acc_ref[...].T, axis=0, keepdims=True).astype(jnp.int32) - 1
        pos = lax.broadcasted_iota(jnp.int32, (1, EXP_TILE), 1)
        tok = code >> 2
        src = jnp.where(code >= 0, tok, lax.rem(i * EXP_TILE + pos, n_tok))
        spare = TOP_K * n_tok + lax.rem(i, N_YBUF) * EXP_TILE + pos
        dst = jnp.where(code >= 0, (code & (TOP_K - 1)) * n_tok + tok, spare)
        src_ref[0] = src * SUBLANES
        dst_ref[0] = dst * SUBLANES

    @pl.when(i >= n_act[0])
    def _():
        src_ref[...] = jnp.zeros_like(src_ref)
        dst_ref[...] = jnp.zeros_like(dst_ref)


def _plan(rank_t, slot_t, tile_e, gstart, n_act, cum2d):
    n_tok = rank_t.shape[1]
    n_tiles = tile_e.shape[0]
    n_tok_tiles = cum2d.shape[0] - 1
    assert n_tok_tiles <= LANES
    huge = jnp.float32(2.0 ** 30)
    as_rows = lambda a: jnp.pad(a.T.astype(F32), ((0, 0), (0, LANES - n_tok_tiles)),
                                constant_values=huge)
    whole = lambda i, *_: (0, 0)
    per_tile = pl.BlockSpec((1, 1, EXP_TILE), lambda i, *_: (i, 0, 0))
    grid_spec = pltpu.PrefetchScalarGridSpec(
        num_scalar_prefetch=4,
        grid=(n_tiles,),
        in_specs=[pl.BlockSpec((N_EXPERTS, LANES), whole), pl.BlockSpec((N_EXPERTS, LANES), whole),
                  pl.BlockSpec(rank_t.shape, whole), pl.BlockSpec(slot_t.shape, whole)],
        out_specs=(per_tile, per_tile),
        scratch_shapes=[pltpu.VMEM((EXP_TILE, LANES), F32)],
    )
    shape = jax.ShapeDtypeStruct((n_tiles, 1, EXP_TILE), jnp.int32)
    return pl.pallas_call(
        functools.partial(_plan_kernel, n_tok),
        out_shape=(shape, shape),
        grid_spec=grid_spec,
        compiler_params=pltpu.CompilerParams(dimension_semantics=("arbitrary",),
                                             vmem_limit_bytes=VMEM_LIMIT),
        name="moe_plan",
    )(tile_e, gstart, n_act, cum2d.reshape(-1), as_rows(cum2d[:-1]), as_rows(cum2d[1:]),
      rank_t, slot_t)


N_YBUF = 3
DMA_UNROLL = 8


def _expert_kernel(tile_e, n_act, src0_ref, srcn_ref, dst_ref, h_ref, wgu_ref, bgu_ref, wd_ref,
                   bd_ref, yk_ref, xbuf, ybuf, gsem, ssem, wgu_bf, wd_bf):
    i = pl.program_id(0)
    n_steps = pl.num_programs(0)
    n_active = n_act[0]
    tile_rows = EXP_TILE * SUBLANES
    n_slot_rows = yk_ref.shape[0] - N_YBUF * tile_rows

    def row_tile(ref, start):
        return ref.at[pl.ds(pl.multiple_of(start, SUBLANES), SUBLANES)]

    def for_each_row(fn):
        def body(g, carry):
            for k in range(DMA_UNROLL):
                fn(g * DMA_UNROLL + k, k % 2)
            return carry
        lax.fori_loop(0, EXP_TILE // DMA_UNROLL, body, 0)

    def on_slot(slot, n, fn):
        if isinstance(slot, int):
            fn(slot)
            return
        for s in range(n):
            pl.when(slot == s)(functools.partial(fn, s))

    def gather_rows(src_ref, slot):
        def run(s):
            def one(r, queue):
                pltpu.make_async_copy(row_tile(h_ref, src_ref[0, 0, r]),
                                      row_tile(xbuf.at[s], r * SUBLANES),
                                      gsem.at[s]).start(priority=queue)
            for_each_row(one)
        on_slot(slot, 2, run)

    def scatter_rows(slot):
        def run(s):
            def one(r, queue):
                pltpu.make_async_copy(row_tile(ybuf.at[s], r * SUBLANES),
                                      row_tile(yk_ref, dst_ref[0, 0, r]),
                                      ssem.at[s]).start(priority=queue)
            for_each_row(one)
        on_slot(slot, N_YBUF, run)

    def scatter_wait(slot):
        pltpu.make_async_copy(ybuf.at[slot], yk_ref.at[pl.ds(0, tile_rows)], ssem.at[slot]).wait()

    @pl.when(i == 0)
    def _():
        ybuf[0] = jnp.zeros((tile_rows, LANES), F32)
        for spare in range(N_YBUF):
            cp = pltpu.make_async_copy(
                ybuf.at[0], yk_ref.at[pl.ds(n_slot_rows + spare * tile_rows, tile_rows)],
                ssem.at[0])
            cp.start()
            cp.wait()
        gather_rows(src0_ref, 0)

    @pl.when(i + 1 < n_active)
    def _():
        gather_rows(srcn_ref, (i + 1) % 2)

    prev = tile_e[jnp.maximum(i - 1, 0)]
    new_expert = jnp.logical_or(i == 0, tile_e[i] != prev)

    @pl.when(jnp.logical_and(new_expert, i < n_active))
    def _():
        rows = 128

        def cast(r, carry):
            sl = pl.ds(pl.multiple_of(r * rows, rows), rows)
            wgu_bf[sl, :] = wgu_ref[0, sl, :].astype(BF16)
            wd_bf[sl, :] = wd_ref[0, sl, :].astype(BF16)
            return carry

        lax.fori_loop(0, D_MODEL // rows, cast, 0)

    @pl.when(i < n_active)
    def _():
        xslot = i % 2
        yslot = i % N_YBUF
        pltpu.make_async_copy(h_ref.at[pl.ds(0, tile_rows)], xbuf.at[xslot], gsem.at[xslot]).wait()

        @pl.when(i >= N_YBUF)
        def _():
            scatter_wait(yslot)

        x = _load_row_tiles(xbuf.at[xslot], EXP_TILE).astype(BF16)
        gu = _dot(x, wgu_bf[...]) + bgu_ref[0]
        gate = jnp.minimum(gu[:, :D_FF], SWIGLU_LIMIT)
        up = jnp.clip(gu[:, D_FF:], -SWIGLU_LIMIT, SWIGLU_LIMIT)
        act = (up + 1.0) * gate * _sigmoid(SWIGLU_ALPHA * gate)
        _store_row_tiles(ybuf.at[yslot], _dot(act.astype(BF16), wd_bf[...]) + bd_ref[0])
        scatter_rows(yslot)

    @pl.when(i == n_steps - 1)
    def _():
        for back in range(1, N_YBUF + 1):
            scatter_wait((n_active - back) % N_YBUF)


def _experts(h2, src3, dst3, tile_e, n_act, w_gate_up, b_gate_up, w_down, b_down):
    n_tok = h2.shape[0] // SUBLANES
    n_tiles = src3.shape[0]
    idx_block = (1, 1, EXP_TILE)
    grid_spec = pltpu.PrefetchScalarGridSpec(
        num_scalar_prefetch=2,
        grid=(n_tiles,),
        in_specs=[pl.BlockSpec(idx_block, lambda i, te, na: (0, 0, 0), memory_space=pltpu.SMEM),
                  pl.BlockSpec(idx_block, lambda i, te, na: (jnp.minimum(i + 1, n_tiles - 1), 0, 0),
                               memory_space=pltpu.SMEM),
                  pl.BlockSpec(idx_block, lambda i, te, na: (i, 0, 0), memory_space=pltpu.SMEM),
                  pl.BlockSpec(memory_space=pl.ANY),
                  pl.BlockSpec((1, D_MODEL, 2 * D_FF), lambda i, te, na: (te[i], 0, 0)),
                  pl.BlockSpec((1, 1, 2 * D_FF), lambda i, te, na: (te[i], 0, 0)),
                  pl.BlockSpec((1, D_FF, D_MODEL), lambda i, te, na: (te[i], 0, 0)),
                  pl.BlockSpec((1, 1, D_MODEL), lambda i, te, na: (te[i], 0, 0))],
        out_specs=pl.BlockSpec(memory_space=pl.ANY),
        scratch_shapes=[pltpu.VMEM((2, EXP_TILE * SUBLANES, LANES), F32),
                        pltpu.VMEM((N_YBUF, EXP_TILE * SUBLANES, LANES), F32),
                        pltpu.SemaphoreType.DMA((2,)), pltpu.SemaphoreType.DMA((N_YBUF,)),
                        pltpu.VMEM((D_MODEL, 2 * D_FF), BF16), pltpu.VMEM((D_FF, D_MODEL), BF16)],
    )
    return pl.pallas_call(
        _expert_kernel,
        out_shape=jax.ShapeDtypeStruct(((TOP_K * n_tok + N_YBUF * EXP_TILE) * SUBLANES, LANES), F32),
        grid_spec=grid_spec,
        compiler_params=pltpu.CompilerParams(dimension_semantics=("arbitrary",),
                                             vmem_limit_bytes=VMEM_LIMIT),
        name="moe_experts",
    )(tile_e, n_act, src3, src3, dst3, h2, w_gate_up, b_gate_up.reshape(N_EXPERTS, 1, 2 * D_FF),
      w_down, b_down.reshape(N_EXPERTS, 1, D_MODEL))


def _combine_kernel(n_prompt_tiles, alpha, y0_ref, y1_ref, y2_ref, y3_ref, x1_ref, w4_ref, mod_ref,
                    l2g_ref, l2b_ref, yp_ref, ysm_ref):
    i = pl.program_id(0)
    w4 = w4_ref[...]
    ff = _load_row_tiles(y0_ref, TOK_TILE) * w4[:, 0:1]
    for j, y_ref in enumerate((y1_ref, y2_ref, y3_ref), start=1):
        ff = ff + _load_row_tiles(y_ref, TOK_TILE) * w4[:, j:j + 1]
    g2 = mod_ref[0][:, 5 * D_MODEL:6 * D_MODEL]
    out = _layer_norm(alpha * x1_ref[...] + g2 * ff) * l2g_ref[...] + l2b_ref[...]

    @pl.when(i < n_prompt_tiles)
    def _():
        yp_ref[...] = out

    @pl.when(i >= n_prompt_tiles)
    def _():
        ysm_ref[...] = out


def _combine(yk, x1, w4, mod3, ln2_g, ln2_b, n_p, alpha, sample_seq):
    n_tok = x1.shape[0]
    n_s = n_tok - n_p
    npt = n_p // TOK_TILE
    n_tok_tiles = n_tok // TOK_TILE
    tiles_per_sample_seq = sample_seq // TOK_TILE

    def mod_row(i):
        return jnp.where(i < npt, 0, 1 + (i - npt) // tiles_per_sample_seq)

    def slot(j):
        return pl.BlockSpec((TOK_TILE * SUBLANES, LANES), lambda i: (j * n_tok_tiles + i, 0))

    const = lambda i: (0, 0)
    return pl.pallas_call(
        functools.partial(_combine_kernel, npt, alpha),
        out_shape=(jax.ShapeDtypeStruct((n_p, D_MODEL), F32),
                   jax.ShapeDtypeStruct((n_s, D_MODEL), F32)),
        grid=(n_tok_tiles,),
        in_specs=[slot(0), slot(1), slot(2), slot(3),
                  pl.BlockSpec((TOK_TILE, D_MODEL), lambda i: (i, 0)),
                  pl.BlockSpec((TOK_TILE, LANES), lambda i: (i, 0)),
                  pl.BlockSpec((1, 1, 6 * D_MODEL), lambda i: (mod_row(i), 0, 0)),
                  pl.BlockSpec((1, D_MODEL), const), pl.BlockSpec((1, D_MODEL), const)],
        out_specs=(pl.BlockSpec((TOK_TILE, D_MODEL), lambda i: (jnp.minimum(i, npt - 1), 0)),
                   pl.BlockSpec((TOK_TILE, D_MODEL), lambda i: (jnp.maximum(i - npt, 0), 0))),
        compiler_params=pltpu.CompilerParams(dimension_semantics=("arbitrary",),
                                             vmem_limit_bytes=VMEM_LIMIT),
        name="moe_combine",
    )(yk, yk, yk, yk, x1, w4, mod3, ln2_g, ln2_b)


def kernel(x_prompt, x_sample, c, state_fwd, state_bwd, c_ctx, w_ada, b_ada, w_in, conv_w, conv_b,
           conv_ln_g, conv_ln_b, lb_logits, rec_norm_g, w_out, ln1_g, ln1_b, ln2_g, ln2_b,
           router_w, router_b, w_gate_up, b_gate_up, w_down, b_down):
    depth = w_ada.shape[0]
    assert depth == 1
    alpha = (2.0 * depth) ** 0.25
    nb_p, seq_p, _ = x_prompt.shape
    nb_s, seq_s, _ = x_sample.shape
    n_p, n_s = nb_p * seq_p, nb_s * seq_s
    n_tok = n_p + n_s
    row2 = lambda a: a.reshape(1, -1)

    lb = jnp.cumsum(jax.nn.softmax(lb_logits.astype(F32), axis=0), axis=0)[0]

    cond = jnp.concatenate([c_ctx[None, :], c, jnp.zeros((16 - 1 - nb_s, D_MODEL), F32)], axis=0)
    mod = _ada(cond, w_ada[0], b_ada[0])
    mod3 = mod.reshape(16, 1, 6 * D_MODEL)

    xp = x_prompt.reshape(n_p, D_MODEL)
    xs = x_sample.reshape(n_s, D_MODEL)
    conv_w_pad = jnp.concatenate([conv_w[0], jnp.zeros((1, CONV_WIDTH), F32)], axis=0)
    conv_out, rec = _inproj(xp, xs, mod3, w_in[0].astype(BF16), conv_w_pad, row2(conv_b[0]),
                            row2(conv_ln_g[0]), row2(conv_ln_b[0]), seq_p, seq_s)

    zeros_state = jnp.zeros((nb_p, REC_HEADS, REC_DK, REC_DK), F32)
    s0f = jnp.concatenate([zeros_state, jnp.swapaxes(state_fwd[:, 0], -1, -2)], axis=0)
    s0b = jnp.concatenate([zeros_state, jnp.swapaxes(state_bwd[:, 0], -1, -2)], axis=0)
    o_f, o_b, sf_t, sb_t = _scan(rec, lb, s0f, s0b, [seq_p] * nb_p + [seq_s] * nb_s)
    new_f = jnp.swapaxes(sf_t[:nb_p], -1, -2)[:, None]
    new_b = jnp.swapaxes(sb_t[:nb_p], -1, -2)[:, None]

    rw = jnp.pad(router_w[0], ((0, 0), (0, LANES - N_EXPERTS)))
    rw_hi = rw.astype(BF16)
    rw_lo = (rw - rw_hi.astype(F32)).astype(BF16)
    rb = jnp.pad(router_b[0], (0, LANES - N_EXPERTS)).reshape(1, LANES)
    x1, h2, rank_t, slot_t, w4, counts, tile_start = _post(
        o_f, o_b, rec, conv_out, xp, xs, mod3, w_out[0].astype(BF16), row2(rec_norm_g[0]),
        row2(ln1_g[0]), row2(ln1_b[0]), rw_hi, rw_lo, rb, alpha, seq_s)

    cnt = counts[0, :N_EXPERTS].astype(jnp.int32)
    gpad = ((cnt + EXP_TILE - 1) // EXP_TILE) * EXP_TILE
    gend = jnp.cumsum(gpad)
    gstart = gend - gpad
    n_rows = n_tok * TOP_K + N_EXPERTS * EXP_TILE
    n_tiles = n_rows // EXP_TILE
    n_act = gend[-1] // EXP_TILE
    tile_ids = jnp.minimum(jnp.arange(n_tiles, dtype=jnp.int32), n_act - 1)
    tile_e = jnp.sum((gend[None, :] <= tile_ids[:, None] * EXP_TILE).astype(jnp.int32), axis=1)
    n_act1 = n_act.reshape(1).astype(jnp.int32)
    starts = tile_start[:, 0, :N_EXPERTS].astype(jnp.int32)
    cum2d = jnp.concatenate([starts, cnt[None, :]], axis=0)
    src, dst = _plan(rank_t, slot_t, tile_e, gstart.astype(jnp.int32), n_act1, cum2d)
    yk = _experts(h2, src, dst, tile_e, n_act1, w_gate_up[0], b_gate_up[0], w_down[0], b_down[0])
    y_p, y_s = _combine(yk, x1, w4, mod3, row2(ln2_g[0]), row2(ln2_b[0]), n_p, alpha, seq_s)
    return (y_p.reshape(nb_p, seq_p, D_MODEL), y_s.reshape(nb_s, seq_s, D_MODEL), new_f, new_b)
```

```python
import functools

import numpy as np
import jax
import jax.numpy as jnp
from jax import lax
from jax.experimental import pallas as pl
from jax.experimental.pallas import tpu as pltpu

F32 = jnp.float32
BF16 = jnp.bfloat16

D_MODEL = 1024
CONV_WIDTH = 512
CONV_TAPS = 31
REC_HEADS = 4
REC_DK = 128
REC_WIDTH = REC_HEADS * REC_DK
REC_COLS = 5 * REC_WIDTH
CHUNK = 64
N_LEVELS = 6
SCAN_CHUNKS = 4
N_EXPERTS = 32
TOP_K = 4
D_FF = 1024
SWIGLU_LIMIT = 7.0
SWIGLU_ALPHA = 1.702
LN_EPS = 1e-5
RMS_EPS = 1e-6
GRID_W = 64

LANES = 128
TOK_TILE = 256
EXP_TILE = 512
VMEM_LIMIT = 56 * 1024 * 1024


def _sigmoid(x):
    return 1.0 / (1.0 + jnp.exp(-x))


def _layer_norm(x):
    mu = jnp.mean(x, axis=-1, keepdims=True)
    xc = x - mu
    var = jnp.mean(xc * xc, axis=-1, keepdims=True)
    return xc * lax.rsqrt(var + LN_EPS)


def _dot(a, b):
    return jnp.dot(a, b, preferred_element_type=F32)


def _dot_nt(a, b):
    return lax.dot_general(a, b, (((1,), (1,)), ((), ())), preferred_element_type=F32)


SUBLANES = 8
assert D_MODEL == SUBLANES * LANES


def _store_row_tiles(ref, value):
    for s in range(SUBLANES):
        ref[pl.ds(s, value.shape[0], stride=SUBLANES), :] = value[:, s * LANES:(s + 1) * LANES]


def _load_row_tiles(ref, n_rows):
    return jnp.concatenate(
        [ref[pl.ds(s, n_rows, stride=SUBLANES), :] for s in range(SUBLANES)], axis=-1)


def _split3(x):
    hi = x.astype(BF16)
    r1 = x - hi.astype(F32)
    mid = r1.astype(BF16)
    lo = (r1 - mid.astype(F32)).astype(BF16)
    return hi, mid, lo


def _ada_kernel(c_ref, w_ref, b_ref, o_ref):
    c = c_ref[...]
    s = (c * _sigmoid(c)).astype(BF16)
    o_ref[...] = _dot(s, w_ref[...].astype(BF16)) + b_ref[...]


def _ada(cond, w_ada, b_ada):
    rows = cond.shape[0]
    ncol = w_ada.shape[1]
    tn = 1024
    return pl.pallas_call(
        _ada_kernel,
        out_shape=jax.ShapeDtypeStruct((rows, ncol), F32),
        grid=(ncol // tn,),
        in_specs=[pl.BlockSpec((rows, D_MODEL), lambda j: (0, 0)),
                  pl.BlockSpec((D_MODEL, tn), lambda j: (0, j)),
                  pl.BlockSpec((1, tn), lambda j: (0, j))],
        out_specs=pl.BlockSpec((rows, tn), lambda j: (0, j)),
        compiler_params=pltpu.CompilerParams(dimension_semantics=("parallel",),
                                             vmem_limit_bytes=VMEM_LIMIT),
        name="ada_mod",
    )(cond, w_ada, b_ada.reshape(1, ncol))


def _inproj_kernel(n_prompt_tiles, prompt_seg, xp_ref, xs_ref, mod_ref, win_ref, cw_ref, cb_ref,
                   cg_ref, cbeta_ref, conv_ref, rec_ref):
    i = pl.program_id(0)
    is_prompt = i < n_prompt_tiles
    x = jnp.where(is_prompt, xp_ref[...], xs_ref[...])
    mod = mod_ref[0]
    sh1 = mod[:, 0:D_MODEL]
    sc1 = mod[:, D_MODEL:2 * D_MODEL]
    h = (_layer_norm(x) * (1.0 + sc1) + sh1).astype(BF16)

    glu = _dot(h, win_ref[:, 0:2 * CONV_WIDTH])
    u = glu[:, :CONV_WIDTH] * _sigmoid(glu[:, CONV_WIDTH:])

    seg_len = jnp.where(is_prompt, prompt_seg, GRID_W)
    half = CONV_TAPS // 2
    pos = lax.broadcasted_iota(jnp.int32, (TOK_TILE, 1), 0) & (seg_len - 1)

    def masked_shift(d):
        sh = pltpu.roll(u, (-d) % TOK_TILE, axis=0)
        edge_rows = SUBLANES * ((abs(d) + SUBLANES - 1) // SUBLANES)
        pieces = []
        for s0 in range(0, TOK_TILE, GRID_W):
            if d < 0:
                edge = slice(s0, s0 + edge_rows)
                pieces.append(jnp.where(pos[edge] >= -d, sh[edge], 0.0))
                pieces.append(sh[s0 + edge_rows:s0 + GRID_W])
            else:
                edge = slice(s0 + GRID_W - edge_rows, s0 + GRID_W)
                pieces.append(sh[s0:s0 + GRID_W - edge_rows])
                pieces.append(jnp.where(pos[edge] < seg_len - d, sh[edge], 0.0))
        return jnp.concatenate(pieces, axis=0)

    acc = u * cw_ref[half:half + 1, :] + cb_ref[...]
    taps = [j for j in range(CONV_TAPS) if j != half]
    n_groups = REC_COLS // REC_WIDTH
    per_group = len(taps) // n_groups
    anchor = None
    for g in range(n_groups):
        cols = slice(g * REC_WIDTH, (g + 1) * REC_WIDTH)
        rec = _dot(h, win_ref[:, 2 * CONV_WIDTH + g * REC_WIDTH:
                              2 * CONV_WIDTH + (g + 1) * REC_WIDTH])
        rec_ref[:, cols] = rec
        for j in taps[g * per_group:(g + 1) * per_group]:
            w_tap = cw_ref[j:j + 1, :]
            if anchor is not None:
                w_tap = w_tap + anchor
                anchor = None
            acc = acc + masked_shift(j - half) * w_tap
        if g + 1 < n_groups:
            bits = pltpu.bitcast(rec[0:SUBLANES, :], jnp.uint32)
            anchor = pltpu.bitcast((bits >> 16) >> 16, F32)[0:1, :]
    y = _layer_norm(acc) * cg_ref[...] + cbeta_ref[...]
    conv_ref[...] = (y * _sigmoid(y)).astype(BF16)


def _inproj(xp, xs, mod3, w_in_bf, conv_w, conv_b, conv_g, conv_beta, prompt_seq, sample_seq):
    n_p, n_s = xp.shape[0], xs.shape[0]
    n_tok = n_p + n_s
    npt = n_p // TOK_TILE
    tiles_per_sample_seq = sample_seq // TOK_TILE
    assert prompt_seq == TOK_TILE and sample_seq % TOK_TILE == 0

    def mod_row(i):
        return jnp.where(i < npt, 0, 1 + (i - npt) // tiles_per_sample_seq)

    const = lambda i: (0, 0)
    return pl.pallas_call(
        functools.partial(_inproj_kernel, npt, prompt_seq),
        out_shape=(jax.ShapeDtypeStruct((n_tok, CONV_WIDTH), BF16),
                   jax.ShapeDtypeStruct((n_tok, REC_COLS), F32)),
        grid=(n_tok // TOK_TILE,),
        in_specs=[pl.BlockSpec((TOK_TILE, D_MODEL), lambda i: (jnp.minimum(i, npt - 1), 0)),
                  pl.BlockSpec((TOK_TILE, D_MODEL), lambda i: (jnp.maximum(i - npt, 0), 0)),
                  pl.BlockSpec((1, 1, 6 * D_MODEL), lambda i: (mod_row(i), 0, 0)),
                  pl.BlockSpec(w_in_bf.shape, const),
                  pl.BlockSpec(conv_w.shape, const),
                  pl.BlockSpec((1, CONV_WIDTH), const),
                  pl.BlockSpec((1, CONV_WIDTH), const),
                  pl.BlockSpec((1, CONV_WIDTH), const)],
        out_specs=(pl.BlockSpec((TOK_TILE, CONV_WIDTH), lambda i: (i, 0)),
                   pl.BlockSpec((TOK_TILE, REC_COLS), lambda i: (i, 0))),
        compiler_params=pltpu.CompilerParams(dimension_semantics=("parallel",),
                                             vmem_limit_bytes=VMEM_LIMIT),
        name="inproj_conv",
    )(xp, xs, mod3, w_in_bf, conv_w, conv_b, conv_g, conv_beta)


def _scan_constants():
    c = CHUNK
    w = np.zeros((N_LEVELS + 2, c, c), np.float32)
    m = np.zeros((N_LEVELS + 1, c, c), np.float32)
    t = np.arange(c)
    for j in range(N_LEVELS):
        half = 1 << j
        for ti in range(c):
            ref = (ti & ~(2 * half - 1)) + half - 1
            if ti & half:
                w[j, ti, ref + 1:ti + 1] = 1.0
            else:
                w[j, ti, ti + 1:ref + 1] = 1.0
        upper = (t[:, None] & half) != 0
        lower = (t[None, :] & half) == 0
        same = (t[:, None] >> (j + 1)) == (t[None, :] >> (j + 1))
        m[j] = (upper & lower & same).astype(np.float32)
    w[N_LEVELS] = (t[None, :] <= t[:, None]).astype(np.float32)
    w[N_LEVELS + 1] = (t[None, :] > t[:, None]).astype(np.float32)
    m[N_LEVELS] = np.eye(c, dtype=np.float32)
    w_f = np.tile(w.reshape(-1, c), (1, 3))
    w_b = np.tile(w[:, ::-1, ::-1].reshape(-1, c), (1, 3))
    m_b = m[:, ::-1, ::-1]
    return (jnp.asarray(w_f, BF16), jnp.asarray(np.ascontiguousarray(w_b), BF16),
            jnp.asarray(m), jnp.asarray(np.ascontiguousarray(m_b)))


def _scan_gates(z, lb, w_ref):
    e = jnp.exp(-jnp.abs(z))
    r = 1.0 / (1.0 + e)
    a = e * r
    sig = jnp.where(z >= 0, r, a)
    sig_neg = jnp.where(z >= 0, a, r)
    one_m_lb = 1.0 - lb
    log_f = jnp.log2(lb + one_m_lb * sig)
    k = one_m_lb * sig_neg
    terms = jnp.concatenate(_split3(log_f), axis=0)
    ex = jnp.exp2(_dot(w_ref[...], terms))
    return k, ex


def _scan_intra(q, k, ex, m_ref):
    c = CHUNK
    qb, kb = q.astype(BF16), k.astype(BF16)
    a_mat = m_ref[N_LEVELS] * _dot_nt(qb, kb)
    for j in range(N_LEVELS):
        xj = ex[j * c:(j + 1) * c].astype(BF16)
        a_mat = a_mat + m_ref[j] * _dot_nt(qb * xj, kb * xj)
    return a_mat


def _scan_kernel(fblk, bblk, first, seq, qf_ref, zf_ref, vf_ref, qb_ref, zb_ref, vb_ref, lb_ref,
                 wf_ref, wb_ref, mf_ref, mb_ref, s0f_ref, s0b_ref, of_ref, ob_ref, sf_ref, sb_ref):
    s = pl.program_id(0)

    @pl.when(first[s] == 1)
    def _():
        sf_ref[...] = s0f_ref[...]
        sb_ref[...] = s0b_ref[...]

    c = CHUNK
    dirs = ((qf_ref, zf_ref, vf_ref, 0, wf_ref, mf_ref, sf_ref, of_ref, c - 1,
             tuple(range(SCAN_CHUNKS))),
            (qb_ref, zb_ref, vb_ref, 1, wb_ref, mb_ref, sb_ref, ob_ref, 0,
             tuple(reversed(range(SCAN_CHUNKS)))))
    units = [(d, slice(sub * c, (sub + 1) * c)) for d in dirs for sub in d[-1]]
    gates = [_scan_gates(d[1][rows, :], lb_ref[d[3]:d[3] + 1, :], d[4]) for d, rows in units]
    intra = []
    for (d, rows), (k, ex) in zip(units, gates):
        for h in range(REC_HEADS):
            cols = slice(h * REC_DK, (h + 1) * REC_DK)
            intra.append(_scan_intra(d[0][rows, cols], k[:, cols], ex[:, cols], d[5]))
    idx = 0
    for (d, rows), (k, ex) in zip(units, gates):
        q_ref, _, v_ref, _, _, _, st_ref, o_ref, last, _ = d
        for h in range(REC_HEADS):
            cols = slice(h * REC_DK, (h + 1) * REC_DK)
            q, v = q_ref[rows, cols], v_ref[rows, cols]
            ex_b = ex[N_LEVELS * c:(N_LEVELS + 1) * c, cols]
            ex_l = ex[(N_LEVELS + 1) * c:(N_LEVELS + 2) * c, cols]
            st = st_ref[0, h]
            o_ref[rows, cols] = (_dot(intra[idx].astype(BF16), v.astype(BF16))
                                 + _dot_nt((q * ex_b).astype(BF16), st.astype(BF16)))
            st_ref[0, h] = (st * ex_b[last:last + 1, :]
                            + _dot(v.T.astype(BF16), (k[:, cols] * ex_l).astype(BF16)))
            idx += 1


def _scan(rec, lb, s0f_t, s0b_t, seq_lens):
    n_tok = rec.shape[0]
    fblk, bblk, first, seq = [], [], [], []
    base = 0
    step_rows = CHUNK * SCAN_CHUNKS
    for si, ln in enumerate(seq_lens):
        assert ln % step_rows == 0
        n = ln // step_rows
        for ci in range(n):
            fblk.append(base + ci)
            bblk.append(base + n - 1 - ci)
            first.append(1 if ci == 0 else 0)
            seq.append(si)
        base += n
    steps = len(fblk)
    as_i32 = lambda a: jnp.asarray(np.asarray(a, np.int32))
    w_f, w_b, m_f, m_b = _scan_constants()
    n_seq = len(seq_lens)

    def col(block_of, c):
        return pl.BlockSpec((step_rows, REC_WIDTH), lambda s, fb, bb, fi, sq: (block_of(fb, bb)[s], c))

    fwd = lambda fb, bb: fb
    bwd = lambda fb, bb: bb
    const2 = lambda s, fb, bb, fi, sq: (0, 0)
    const3 = lambda s, fb, bb, fi, sq: (0, 0, 0)
    state_spec = pl.BlockSpec((1, REC_HEADS, REC_DK, REC_DK), lambda s, fb, bb, fi, sq: (sq[s], 0, 0, 0))
    grid_spec = pltpu.PrefetchScalarGridSpec(
        num_scalar_prefetch=4,
        grid=(steps,),
        in_specs=[col(fwd, 0), col(fwd, 1), col(fwd, 3), col(bwd, 0), col(bwd, 2), col(bwd, 3),
                  pl.BlockSpec((2, REC_WIDTH), const2),
                  pl.BlockSpec(w_f.shape, const2), pl.BlockSpec(w_b.shape, const2),
                  pl.BlockSpec(m_f.shape, const3), pl.BlockSpec(m_b.shape, const3),
                  state_spec, state_spec],
        out_specs=(pl.BlockSpec((step_rows, REC_WIDTH), lambda s, fb, bb, fi, sq: (fb[s], 0)),
                   pl.BlockSpec((step_rows, REC_WIDTH), lambda s, fb, bb, fi, sq: (bb[s], 0)),
                   state_spec, state_spec),
    )
    st_shape = jax.ShapeDtypeStruct((n_seq, REC_HEADS, REC_DK, REC_DK), F32)
    return pl.pallas_call(
        _scan_kernel,
        out_shape=(jax.ShapeDtypeStruct((n_tok, REC_WIDTH), F32),
                   jax.ShapeDtypeStruct((n_tok, REC_WIDTH), F32), st_shape, st_shape),
        grid_spec=grid_spec,
        compiler_params=pltpu.CompilerParams(dimension_semantics=("arbitrary",),
                                             vmem_limit_bytes=VMEM_LIMIT),
        name="hgrn2_scan",
    )(as_i32(fblk), as_i32(bblk), as_i32(first), as_i32(seq),
      rec, rec, rec, rec, rec, rec, lb, w_f, w_b, m_f, m_b, s0f_t, s0b_t)


def _post_kernel(n_prompt_tiles, alpha, of_ref, ob_ref, g_ref, conv_ref, xp_ref, xs_ref, mod_ref,
                 wout_ref, rg_ref, l1g_ref, l1b_ref, rwh_ref, rwl_ref, rb_ref, tri_ref,
                 x1_ref, h2_ref, rank_t_ref, slot_t_ref, w4_ref, cnt_ref, start_ref, run_ref):
    i = pl.program_id(0)

    @pl.when(i == 0)
    def _():
        run_ref[...] = jnp.zeros_like(run_ref)

    x = jnp.where(i < n_prompt_tiles, xp_ref[...], xs_ref[...])
    mod = mod_ref[0]
    g1 = mod[:, 2 * D_MODEL:3 * D_MODEL]
    sh2 = mod[:, 3 * D_MODEL:4 * D_MODEL]
    sc2 = mod[:, 4 * D_MODEL:5 * D_MODEL]

    g = g_ref[...]
    silu_g = g * _sigmoid(g)
    mix = _dot(conv_ref[...], wout_ref[0:CONV_WIDTH, :])
    for h in range(REC_HEADS):
        cols = slice(h * REC_DK, (h + 1) * REC_DK)
        o = of_ref[:, cols] + ob_ref[:, cols]
        o = o * lax.rsqrt(jnp.mean(o * o, axis=-1, keepdims=True) + RMS_EPS)
        rec_out = (o * rg_ref[:, cols] * silu_g[:, cols]).astype(BF16)
        mix = mix + _dot(rec_out, wout_ref[CONV_WIDTH + h * REC_DK:CONV_WIDTH + (h + 1) * REC_DK, :])

    x1 = _layer_norm(alpha * x + g1 * mix) * l1g_ref[...] + l1b_ref[...]
    x1_ref[...] = x1
    h2 = _layer_norm(x1) * (1.0 + sc2) + sh2
    _store_row_tiles(h2_ref, h2)

    h_hi = h2.astype(BF16)
    h_lo = (h2 - h_hi.astype(F32)).astype(BF16)
    logits = (_dot(h_hi, rwh_ref[...]) + _dot(h_lo, rwh_ref[...]) + _dot(h_hi, rwl_ref[...])
              + rb_ref[...])
    lane = lax.broadcasted_iota(jnp.int32, logits.shape, 1)
    lane_f = lane.astype(F32)
    neg_inf = jnp.float32(-jnp.inf)
    work = jnp.where(lane < N_EXPERTS, logits, neg_inf)
    vals, sels = [], []
    for _ in range(TOP_K):
        m = jnp.max(work, axis=-1, keepdims=True)
        idx = jnp.min(jnp.where(work == m, lane_f, float(LANES)), axis=-1, keepdims=True)
        sel = lane_f == idx
        vals.append(m)
        sels.append(sel)
        work = jnp.where(sel, neg_inf, work)
    exps = [jnp.exp(v - vals[0]) for v in vals]
    denom = exps[0] + exps[1] + exps[2] + exps[3]

    sel_any = jnp.zeros(logits.shape, F32)
    for sel in sels:
        sel_any = jnp.where(sel, 1.0, sel_any)
    run_old = run_ref[...]
    prior = _dot(tri_ref[...], sel_any.astype(BF16)) + run_old
    run_new = run_old + jnp.sum(sel_any, axis=0, keepdims=True)
    run_ref[...] = run_new
    cnt_ref[...] = run_new
    start_ref[0] = run_old

    slot = jnp.zeros(logits.shape, F32)
    w4 = jnp.zeros(logits.shape, F32)
    for j in range(TOP_K):
        slot = jnp.where(sels[j], float(j), slot)
        w4 = jnp.where(lane == j, exps[j] / denom, w4)
    routed = sel_any > 0.0
    rank_t_ref[...] = jnp.where(routed, prior, -1.0).T[:N_EXPERTS]
    slot_t_ref[...] = jnp.where(routed, slot, -1.0).T[:N_EXPERTS]
    w4_ref[...] = w4


def _post(o_f, o_b, rec, conv, xp, xs, mod3, w_out_bf, rec_g, ln1_g, ln1_b, rw_hi, rw_lo, rb,
          alpha, sample_seq):
    n_p, n_s = xp.shape[0], xs.shape[0]
    n_tok = n_p + n_s
    npt = n_p // TOK_TILE
    tiles_per_sample_seq = sample_seq // TOK_TILE
    tri = jnp.asarray(np.tril(np.ones((TOK_TILE, TOK_TILE), np.float32), -1), BF16)

    def mod_row(i):
        return jnp.where(i < npt, 0, 1 + (i - npt) // tiles_per_sample_seq)

    const = lambda i: (0, 0)
    tok = lambda w: pl.BlockSpec((TOK_TILE, w), lambda i: (i, 0))
    return pl.pallas_call(
        functools.partial(_post_kernel, npt, alpha),
        out_shape=(jax.ShapeDtypeStruct((n_tok, D_MODEL), F32),
                   jax.ShapeDtypeStruct((n_tok * SUBLANES, LANES), F32),
                   jax.ShapeDtypeStruct((N_EXPERTS, n_tok), F32),
                   jax.ShapeDtypeStruct((N_EXPERTS, n_tok), F32),
                   jax.ShapeDtypeStruct((n_tok, LANES), F32),
                   jax.ShapeDtypeStruct((1, LANES), F32),
                   jax.ShapeDtypeStruct((n_tok // TOK_TILE, 1, LANES), F32)),
        grid=(n_tok // TOK_TILE,),
        in_specs=[tok(REC_WIDTH), tok(REC_WIDTH),
                  pl.BlockSpec((TOK_TILE, REC_WIDTH), lambda i: (i, 4)),
                  tok(CONV_WIDTH),
                  pl.BlockSpec((TOK_TILE, D_MODEL), lambda i: (jnp.minimum(i, npt - 1), 0)),
                  pl.BlockSpec((TOK_TILE, D_MODEL), lambda i: (jnp.maximum(i - npt, 0), 0)),
                  pl.BlockSpec((1, 1, 6 * D_MODEL), lambda i: (mod_row(i), 0, 0)),
                  pl.BlockSpec(w_out_bf.shape, const),
                  pl.BlockSpec((1, REC_WIDTH), const),
                  pl.BlockSpec((1, D_MODEL), const), pl.BlockSpec((1, D_MODEL), const),
                  pl.BlockSpec((D_MODEL, LANES), const), pl.BlockSpec((D_MODEL, LANES), const),
                  pl.BlockSpec((1, LANES), const),
                  pl.BlockSpec((TOK_TILE, TOK_TILE), const)],
        out_specs=(tok(D_MODEL), pl.BlockSpec((TOK_TILE * SUBLANES, LANES), lambda i: (i, 0)),
                   pl.BlockSpec((N_EXPERTS, TOK_TILE), lambda i: (0, i)),
                   pl.BlockSpec((N_EXPERTS, TOK_TILE), lambda i: (0, i)),
                   tok(LANES), pl.BlockSpec((1, LANES), const),
                   pl.BlockSpec((1, 1, LANES), lambda i: (i, 0, 0))),
        scratch_shapes=[pltpu.VMEM((1, LANES), F32)],
        compiler_params=pltpu.CompilerParams(dimension_semantics=("arbitrary",),
                                             vmem_limit_bytes=VMEM_LIMIT),
        name="post_mixer_router",
    )(o_f, o_b, rec, conv, xp, xs, mod3, w_out_bf, rec_g, ln1_g, ln1_b, rw_hi, rw_lo, rb, tri)


PLAN_ROWS = 32


def _plan_kernel(n_tok, tile_e, gstart, n_act, cum, starts_ref, ends_ref, rank_t_ref, slot_t_ref,
                 src_ref, dst_ref, acc_ref):
    i = pl.program_id(0)

    @pl.when(i < n_act[0])
    def _():
        e = tile_e[i]
        base = i * EXP_TILE - gstart[e]
        rows_iota = lax.broadcasted_iota(jnp.int32, (PLAN_ROWS, 1), 0)
        lane = lax.broadcasted_iota(jnp.int32, (1, TOK_TILE), 1)
        acc_ref[...] = jnp.zeros_like(acc_ref)

        base_f = base.astype(F32)
        win0 = jnp.sum((ends_ref[pl.ds(e, 1), :] <= base_f).astype(jnp.int32))
        win1 = jnp.sum((starts_ref[pl.ds(e, 1), :] < base_f + EXP_TILE).astype(jnp.int32))

        def window(b, carry):
            tok0 = pl.multiple_of(b * TOK_TILE, TOK_TILE)
            rk = rank_t_ref[pl.ds(e, 1), pl.ds(tok0, TOK_TILE)]
            sl = slot_t_ref[pl.ds(e, 1), pl.ds(tok0, TOK_TILE)]
            code = ((tok0 + lane) * TOP_K).astype(F32) + sl + 1.0
            lo = jnp.maximum(cum[b * N_EXPERTS + e] - base, 0)
            hi = jnp.minimum(cum[(b + 1) * N_EXPERTS + e] - base, EXP_TILE)

            def rows(k, c):
                r0 = pl.multiple_of(k * PLAN_ROWS, PLAN_ROWS)
                want = (base + r0 + rows_iota).astype(F32)
                hit = jnp.where(rk == want, code, 0.0)
                part = hit[:, 0:LANES]
                for c0 in range(LANES, TOK_TILE, LANES):
                    part = part + hit[:, c0:c0 + LANES]
                acc_ref[pl.ds(r0, PLAN_ROWS), :] += part
                return c

            shift = PLAN_ROWS.bit_length() - 1
            lax.fori_loop(lo >> shift, (hi + PLAN_ROWS - 1) >> shift, rows, 0)
            return carry

        lax.fori_loop(win0, win1, window, 0)
        code = jnp.sum(acc_ref[...].T, axis=0, keepdims=True).astype(jnp.int32) - 1
        pos = lax.broadcasted_iota(jnp.int32, (1, EXP_TILE), 1)
        tok = code >> 2
        src = jnp.where(code >= 0, tok, lax.rem(i * EXP_TILE + pos, n_tok))
        spare = TOP_K * n_tok + lax.rem(i, N_YBUF) * EXP_TILE + pos
        dst = jnp.where(code >= 0, (code & (TOP_K - 1)) * n_tok + tok, spare)
        src_ref[0] = src * SUBLANES
        dst_ref[0] = dst * SUBLANES

    @pl.when(i >= n_act[0])
    def _():
        src_ref[...] = jnp.zeros_like(src_ref)
        dst_ref[...] = jnp.zeros_like(dst_ref)


def _plan(rank_t, slot_t, tile_e, gstart, n_act, cum2d):
    n_tok = rank_t.shape[1]
    n_tiles = tile_e.shape[0]
    n_tok_tiles = cum2d.shape[0] - 1
    assert n_tok_tiles <= LANES
    huge = jnp.float32(2.0 ** 30)
    as_rows = lambda a: jnp.pad(a.T.astype(F32), ((0, 0), (0, LANES - n_tok_tiles)),
                                constant_values=huge)
    whole = lambda i, *_: (0, 0)
    per_tile = pl.BlockSpec((1, 1, EXP_TILE), lambda i, *_: (i, 0, 0))
    grid_spec = pltpu.PrefetchScalarGridSpec(
        num_scalar_prefetch=4,
        grid=(n_tiles,),
        in_specs=[pl.BlockSpec((N_EXPERTS, LANES), whole), pl.BlockSpec((N_EXPERTS, LANES), whole),
                  pl.BlockSpec(rank_t.shape, whole), pl.BlockSpec(slot_t.shape, whole)],
        out_specs=(per_tile, per_tile),
        scratch_shapes=[pltpu.VMEM((EXP_TILE, LANES), F32)],
    )
    shape = jax.ShapeDtypeStruct((n_tiles, 1, EXP_TILE), jnp.int32)
    return pl.pallas_call(
        functools.partial(_plan_kernel, n_tok),
        out_shape=(shape, shape),
        grid_spec=grid_spec,
        compiler_params=pltpu.CompilerParams(dimension_semantics=("arbitrary",),
                                             vmem_limit_bytes=VMEM_LIMIT),
        name="moe_plan",
    )(tile_e, gstart, n_act, cum2d.reshape(-1), as_rows(cum2d[:-1]), as_rows(cum2d[1:]),
      rank_t, slot_t)


N_YBUF = 3
DMA_UNROLL = 8


def _expert_kernel(tile_e, n_act, src0_ref, srcn_ref, dst_ref, h_ref, wgu_ref, bgu_ref, wd_ref,
                   bd_ref, yk_ref, xbuf, ybuf, gsem, ssem, wgu_bf, wd_bf):
    i = pl.program_id(0)
    n_steps = pl.num_programs(0)
    n_active = n_act[0]
    tile_rows = EXP_TILE * SUBLANES
    n_slot_rows = yk_ref.shape[0] - N_YBUF * tile_rows

    def row_tile(ref, start):
        return ref.at[pl.ds(pl.multiple_of(start, SUBLANES), SUBLANES)]

    def for_each_row(fn):
        def body(g, carry):
            for k in range(DMA_UNROLL):
                fn(g * DMA_UNROLL + k)
            return carry
        lax.fori_loop(0, EXP_TILE // DMA_UNROLL, body, 0)

    def on_slot(slot, n, fn):
        if isinstance(slot, int):
            fn(slot)
            return
        for s in range(n):
            pl.when(slot == s)(functools.partial(fn, s))

    def gather_rows(src_ref, slot):
        def run(s):
            def one(r):
                pltpu.make_async_copy(row_tile(h_ref, src_ref[0, 0, r]),
                                      row_tile(xbuf.at[s], r * SUBLANES), gsem.at[s]).start()
            for_each_row(one)
        on_slot(slot, 2, run)

    def scatter_rows(slot):
        def run(s):
            def one(r):
                pltpu.make_async_copy(row_tile(ybuf.at[s], r * SUBLANES),
                                      row_tile(yk_ref, dst_ref[0, 0, r]), ssem.at[s]).start()
            for_each_row(one)
        on_slot(slot, N_YBUF, run)

    def scatter_wait(slot):
        pltpu.make_async_copy(ybuf.at[slot], yk_ref.at[pl.ds(0, tile_rows)], ssem.at[slot]).wait()

    @pl.when(i == 0)
    def _():
        ybuf[0] = jnp.zeros((tile_rows, LANES), F32)
        for spare in range(N_YBUF):
            cp = pltpu.make_async_copy(
                ybuf.at[0], yk_ref.at[pl.ds(n_slot_rows + spare * tile_rows, tile_rows)],
                ssem.at[0])
            cp.start()
            cp.wait()
        gather_rows(src0_ref, 0)

    @pl.when(i + 1 < n_active)
    def _():
        gather_rows(srcn_ref, (i + 1) % 2)

    prev = tile_e[jnp.maximum(i - 1, 0)]
    new_expert = jnp.logical_or(i == 0, tile_e[i] != prev)

    @pl.when(jnp.logical_and(new_expert, i < n_active))
    def _():
        rows = 128

        def cast(r, carry):
            sl = pl.ds(pl.multiple_of(r * rows, rows), rows)
            wgu_bf[sl, :] = wgu_ref[0, sl, :].astype(BF16)
            wd_bf[sl, :] = wd_ref[0, sl, :].astype(BF16)
            return carry

        lax.fori_loop(0, D_MODEL // rows, cast, 0)

    @pl.when(i < n_active)
    def _():
        xslot = i % 2
        yslot = i % N_YBUF
        pltpu.make_async_copy(h_ref.at[pl.ds(0, tile_rows)], xbuf.at[xslot], gsem.at[xslot]).wait()

        @pl.when(i >= N_YBUF)
        def _():
            scatter_wait(yslot)

        x = _load_row_tiles(xbuf.at[xslot], EXP_TILE).astype(BF16)
        gu = _dot(x, wgu_bf[...]) + bgu_ref[0]
        gate = jnp.minimum(gu[:, :D_FF], SWIGLU_LIMIT)
        up = jnp.clip(gu[:, D_FF:], -SWIGLU_LIMIT, SWIGLU_LIMIT)
        act = (up + 1.0) * gate * _sigmoid(SWIGLU_ALPHA * gate)
        _store_row_tiles(ybuf.at[yslot], _dot(act.astype(BF16), wd_bf[...]) + bd_ref[0])
        scatter_rows(yslot)

    @pl.when(i == n_steps - 1)
    def _():
        for back in range(1, N_YBUF + 1):
            scatter_wait((n_active - back) % N_YBUF)


def _experts(h2, src3, dst3, tile_e, n_act, w_gate_up, b_gate_up, w_down, b_down):
    n_tok = h2.shape[0] // SUBLANES
    n_tiles = src3.shape[0]
    idx_block = (1, 1, EXP_TILE)
    grid_spec = pltpu.PrefetchScalarGridSpec(
        num_scalar_prefetch=2,
        grid=(n_tiles,),
        in_specs=[pl.BlockSpec(idx_block, lambda i, te, na: (0, 0, 0), memory_space=pltpu.SMEM),
                  pl.BlockSpec(idx_block, lambda i, te, na: (jnp.minimum(i + 1, n_tiles - 1), 0, 0),
                               memory_space=pltpu.SMEM),
                  pl.BlockSpec(idx_block, lambda i, te, na: (i, 0, 0), memory_space=pltpu.SMEM),
                  pl.BlockSpec(memory_space=pl.ANY),
                  pl.BlockSpec((1, D_MODEL, 2 * D_FF), lambda i, te, na: (te[i], 0, 0)),
                  pl.BlockSpec((1, 1, 2 * D_FF), lambda i, te, na: (te[i], 0, 0)),
                  pl.BlockSpec((1, D_FF, D_MODEL), lambda i, te, na: (te[i], 0, 0)),
                  pl.BlockSpec((1, 1, D_MODEL), lambda i, te, na: (te[i], 0, 0))],
        out_specs=pl.BlockSpec(memory_space=pl.ANY),
        scratch_shapes=[pltpu.VMEM((2, EXP_TILE * SUBLANES, LANES), F32),
                        pltpu.VMEM((N_YBUF, EXP_TILE * SUBLANES, LANES), F32),
                        pltpu.SemaphoreType.DMA((2,)), pltpu.SemaphoreType.DMA((N_YBUF,)),
                        pltpu.VMEM((D_MODEL, 2 * D_FF), BF16), pltpu.VMEM((D_FF, D_MODEL), BF16)],
    )
    return pl.pallas_call(
        _expert_kernel,
        out_shape=jax.ShapeDtypeStruct(((TOP_K * n_tok + N_YBUF * EXP_TILE) * SUBLANES, LANES), F32),
        grid_spec=grid_spec,
        compiler_params=pltpu.CompilerParams(dimension_semantics=("arbitrary",),
                                             vmem_limit_bytes=VMEM_LIMIT),
        name="moe_experts",
    )(tile_e, n_act, src3, src3, dst3, h2, w_gate_up, b_gate_up.reshape(N_EXPERTS, 1, 2 * D_FF),
      w_down, b_down.reshape(N_EXPERTS, 1, D_MODEL))


COMBINE_TILE = 512


def _combine_kernel(n_prompt_tiles, alpha, y0_ref, y1_ref, y2_ref, y3_ref, x1_ref, w4_ref, mod_ref,
                    l2g_ref, l2b_ref, yp_ref, ysm_ref):
    i = pl.program_id(0)
    w4 = w4_ref[...]
    ff = _load_row_tiles(y0_ref, COMBINE_TILE) * w4[:, 0:1]
    for j, y_ref in enumerate((y1_ref, y2_ref, y3_ref), start=1):
        ff = ff + _load_row_tiles(y_ref, COMBINE_TILE) * w4[:, j:j + 1]
    g2 = mod_ref[0][:, 5 * D_MODEL:6 * D_MODEL]
    out = _layer_norm(alpha * x1_ref[...] + g2 * ff) * l2g_ref[...] + l2b_ref[...]

    @pl.when(i < n_prompt_tiles)
    def _():
        yp_ref[...] = out

    @pl.when(i >= n_prompt_tiles)
    def _():
        ysm_ref[...] = out


def _combine(yk, x1, w4, mod3, ln2_g, ln2_b, n_p, alpha, sample_seq):
    n_tok = x1.shape[0]
    n_s = n_tok - n_p
    assert n_p % COMBINE_TILE == 0 and sample_seq % COMBINE_TILE == 0
    npt = n_p // COMBINE_TILE
    n_tok_tiles = n_tok // COMBINE_TILE
    tiles_per_sample_seq = sample_seq // COMBINE_TILE

    def mod_row(i):
        return jnp.where(i < npt, 0, 1 + (i - npt) // tiles_per_sample_seq)

    def slot(j):
        return pl.BlockSpec((COMBINE_TILE * SUBLANES, LANES), lambda i: (j * n_tok_tiles + i, 0))

    const = lambda i: (0, 0)
    return pl.pallas_call(
        functools.partial(_combine_kernel, npt, alpha),
        out_shape=(jax.ShapeDtypeStruct((n_p, D_MODEL), F32),
                   jax.ShapeDtypeStruct((n_s, D_MODEL), F32)),
        grid=(n_tok_tiles,),
        in_specs=[slot(0), slot(1), slot(2), slot(3),
                  pl.BlockSpec((COMBINE_TILE, D_MODEL), lambda i: (i, 0)),
                  pl.BlockSpec((COMBINE_TILE, LANES), lambda i: (i, 0)),
                  pl.BlockSpec((1, 1, 6 * D_MODEL), lambda i: (mod_row(i), 0, 0)),
                  pl.BlockSpec((1, D_MODEL), const), pl.BlockSpec((1, D_MODEL), const)],
        out_specs=(pl.BlockSpec((COMBINE_TILE, D_MODEL), lambda i: (jnp.minimum(i, npt - 1), 0)),
                   pl.BlockSpec((COMBINE_TILE, D_MODEL), lambda i: (jnp.maximum(i - npt, 0), 0))),
        compiler_params=pltpu.CompilerParams(dimension_semantics=("arbitrary",),
                                             vmem_limit_bytes=VMEM_LIMIT),
        name="moe_combine",
    )(yk, yk, yk, yk, x1, w4, mod3, ln2_g, ln2_b)


def kernel(x_prompt, x_sample, c, state_fwd, state_bwd, c_ctx, w_ada, b_ada, w_in, conv_w, conv_b,
           conv_ln_g, conv_ln_b, lb_logits, rec_norm_g, w_out, ln1_g, ln1_b, ln2_g, ln2_b,
           router_w, router_b, w_gate_up, b_gate_up, w_down, b_down):
    depth = w_ada.shape[0]
    assert depth == 1
    alpha = (2.0 * depth) ** 0.25
    nb_p, seq_p, _ = x_prompt.shape
    nb_s, seq_s, _ = x_sample.shape
    n_p, n_s = nb_p * seq_p, nb_s * seq_s
    n_tok = n_p + n_s
    row2 = lambda a: a.reshape(1, -1)

    lb = jnp.cumsum(jax.nn.softmax(lb_logits.astype(F32), axis=0), axis=0)[0]

    cond = jnp.concatenate([c_ctx[None, :], c, jnp.zeros((16 - 1 - nb_s, D_MODEL), F32)], axis=0)
    mod = _ada(cond, w_ada[0], b_ada[0])
    mod3 = mod.reshape(16, 1, 6 * D_MODEL)

    xp = x_prompt.reshape(n_p, D_MODEL)
    xs = x_sample.reshape(n_s, D_MODEL)
    conv_w_pad = jnp.concatenate([conv_w[0], jnp.zeros((1, CONV_WIDTH), F32)], axis=0)
    conv_out, rec = _inproj(xp, xs, mod3, w_in[0].astype(BF16), conv_w_pad, row2(conv_b[0]),
                            row2(conv_ln_g[0]), row2(conv_ln_b[0]), seq_p, seq_s)

    zeros_state = jnp.zeros((nb_p, REC_HEADS, REC_DK, REC_DK), F32)
    s0f = jnp.concatenate([zeros_state, jnp.swapaxes(state_fwd[:, 0], -1, -2)], axis=0)
    s0b = jnp.concatenate([zeros_state, jnp.swapaxes(state_bwd[:, 0], -1, -2)], axis=0)
    o_f, o_b, sf_t, sb_t = _scan(rec, lb, s0f, s0b, [seq_p] * nb_p + [seq_s] * nb_s)
    new_f = jnp.swapaxes(sf_t[:nb_p], -1, -2)[:, None]
    new_b = jnp.swapaxes(sb_t[:nb_p], -1, -2)[:, None]

    rw = jnp.pad(router_w[0], ((0, 0), (0, LANES - N_EXPERTS)))
    rw_hi = rw.astype(BF16)
    rw_lo = (rw - rw_hi.astype(F32)).astype(BF16)
    rb = jnp.pad(router_b[0], (0, LANES - N_EXPERTS)).reshape(1, LANES)
    x1, h2, rank_t, slot_t, w4, counts, tile_start = _post(
        o_f, o_b, rec, conv_out, xp, xs, mod3, w_out[0].astype(BF16), row2(rec_norm_g[0]),
        row2(ln1_g[0]), row2(ln1_b[0]), rw_hi, rw_lo, rb, alpha, seq_s)

    cnt = counts[0, :N_EXPERTS].astype(jnp.int32)
    gpad = ((cnt + EXP_TILE - 1) // EXP_TILE) * EXP_TILE
    gend = jnp.cumsum(gpad)
    gstart = gend - gpad
    n_rows = n_tok * TOP_K + N_EXPERTS * EXP_TILE
    n_tiles = n_rows // EXP_TILE
    n_act = gend[-1] // EXP_TILE
    tile_ids = jnp.minimum(jnp.arange(n_tiles, dtype=jnp.int32), n_act - 1)
    tile_e = jnp.sum((gend[None, :] <= tile_ids[:, None] * EXP_TILE).astype(jnp.int32), axis=1)
    n_act1 = n_act.reshape(1).astype(jnp.int32)
    starts = tile_start[:, 0, :N_EXPERTS].astype(jnp.int32)
    cum2d = jnp.concatenate([starts, cnt[None, :]], axis=0)
    src, dst = _plan(rank_t, slot_t, tile_e, gstart.astype(jnp.int32), n_act1, cum2d)
    yk = _experts(h2, src, dst, tile_e, n_act1, w_gate_up[0], b_gate_up[0], w_down[0], b_down[0])
    y_p, y_s = _combine(yk, x1, w4, mod3, row2(ln2_g[0]), row2(ln2_b[0]), n_p, alpha, seq_s)
    return (y_p.reshape(nb_p, seq_p, D_MODEL), y_s.reshape(nb_s, seq_s, D_MODEL), new_f, new_b)
```

```python
import functools

import numpy as np
import jax
import jax.numpy as jnp
from jax import lax
from jax.experimental import pallas as pl
from jax.experimental.pallas import tpu as pltpu

F32 = jnp.float32
BF16 = jnp.bfloat16

D_MODEL = 1024
CONV_WIDTH = 512
CONV_TAPS = 31
REC_HEADS = 4
REC_DK = 128
REC_WIDTH = REC_HEADS * REC_DK
REC_COLS = 5 * REC_WIDTH
CHUNK = 64
N_LEVELS = 6
SCAN_CHUNKS = 4
N_EXPERTS = 32
TOP_K = 4
D_FF = 1024
SWIGLU_LIMIT = 7.0
SWIGLU_ALPHA = 1.702
LN_EPS = 1e-5
RMS_EPS = 1e-6
GRID_W = 64

LANES = 128
TOK_TILE = 512
EXP_TILE = 512
VMEM_LIMIT = 56 * 1024 * 1024


def _sigmoid(x):
    return 1.0 / (1.0 + jnp.exp(-x))


def _layer_norm(x):
    mu = jnp.mean(x, axis=-1, keepdims=True)
    xc = x - mu
    var = jnp.mean(xc * xc, axis=-1, keepdims=True)
    return xc * lax.rsqrt(var + LN_EPS)


def _dot(a, b):
    return jnp.dot(a, b, preferred_element_type=F32)


def _dot_nt(a, b):
    return lax.dot_general(a, b, (((1,), (1,)), ((), ())), preferred_element_type=F32)


SUBLANES = 8
assert D_MODEL == SUBLANES * LANES


def _store_row_tiles(ref, value):
    for s in range(SUBLANES):
        ref[pl.ds(s, value.shape[0], stride=SUBLANES), :] = value[:, s * LANES:(s + 1) * LANES]


def _load_row_tiles(ref, n_rows):
    return jnp.concatenate(
        [ref[pl.ds(s, n_rows, stride=SUBLANES), :] for s in range(SUBLANES)], axis=-1)


def _split3(x):
    hi = x.astype(BF16)
    r1 = x - hi.astype(F32)
    mid = r1.astype(BF16)
    lo = (r1 - mid.astype(F32)).astype(BF16)
    return hi, mid, lo


def _ada_kernel(c_ref, w_ref, b_ref, o_ref):
    c = c_ref[...]
    s = (c * _sigmoid(c)).astype(BF16)
    o_ref[...] = _dot(s, w_ref[...].astype(BF16)) + b_ref[...]


def _ada(cond, w_ada, b_ada):
    rows = cond.shape[0]
    ncol = w_ada.shape[1]
    tn = 1024
    return pl.pallas_call(
        _ada_kernel,
        out_shape=jax.ShapeDtypeStruct((rows, ncol), F32),
        grid=(ncol // tn,),
        in_specs=[pl.BlockSpec((rows, D_MODEL), lambda j: (0, 0)),
                  pl.BlockSpec((D_MODEL, tn), lambda j: (0, j)),
                  pl.BlockSpec((1, tn), lambda j: (0, j))],
        out_specs=pl.BlockSpec((rows, tn), lambda j: (0, j)),
        compiler_params=pltpu.CompilerParams(dimension_semantics=("parallel",),
                                             vmem_limit_bytes=VMEM_LIMIT),
        name="ada_mod",
    )(cond, w_ada, b_ada.reshape(1, ncol))


def _inproj_kernel(n_prompt_tiles, prompt_seg, xp_ref, xs_ref, mod_ref, win_ref, cw_ref, cb_ref,
                   cg_ref, cbeta_ref, conv_ref, rec_ref):
    i = pl.program_id(0)
    is_prompt = i < n_prompt_tiles
    x = jnp.where(is_prompt, xp_ref[...], xs_ref[...])
    mod = mod_ref[0]
    sh1 = mod[:, 0:D_MODEL]
    sc1 = mod[:, D_MODEL:2 * D_MODEL]
    h = (_layer_norm(x) * (1.0 + sc1) + sh1).astype(BF16)

    glu = _dot(h, win_ref[:, 0:2 * CONV_WIDTH])
    u = glu[:, :CONV_WIDTH] * _sigmoid(glu[:, CONV_WIDTH:])

    seg_len = jnp.where(is_prompt, prompt_seg, GRID_W)
    half = CONV_TAPS // 2
    pos = lax.broadcasted_iota(jnp.int32, (TOK_TILE, 1), 0) & (seg_len - 1)

    def masked_shift(d):
        sh = pltpu.roll(u, (-d) % TOK_TILE, axis=0)
        edge_rows = SUBLANES * ((abs(d) + SUBLANES - 1) // SUBLANES)
        pieces = []
        for s0 in range(0, TOK_TILE, GRID_W):
            if d < 0:
                edge = slice(s0, s0 + edge_rows)
                pieces.append(jnp.where(pos[edge] >= -d, sh[edge], 0.0))
                pieces.append(sh[s0 + edge_rows:s0 + GRID_W])
            else:
                edge = slice(s0 + GRID_W - edge_rows, s0 + GRID_W)
                pieces.append(sh[s0:s0 + GRID_W - edge_rows])
                pieces.append(jnp.where(pos[edge] < seg_len - d, sh[edge], 0.0))
        return jnp.concatenate(pieces, axis=0)

    acc = u * cw_ref[half:half + 1, :] + cb_ref[...]
    taps = [j for j in range(CONV_TAPS) if j != half]
    n_groups = REC_COLS // REC_WIDTH
    per_group = len(taps) // n_groups
    anchor = None
    for g in range(n_groups):
        cols = slice(g * REC_WIDTH, (g + 1) * REC_WIDTH)
        rec = _dot(h, win_ref[:, 2 * CONV_WIDTH + g * REC_WIDTH:
                              2 * CONV_WIDTH + (g + 1) * REC_WIDTH])
        rec_ref[:, cols] = rec
        for j in taps[g * per_group:(g + 1) * per_group]:
            w_tap = cw_ref[j:j + 1, :]
            if anchor is not None:
                w_tap = w_tap + anchor
                anchor = None
            acc = acc + masked_shift(j - half) * w_tap
        if g + 1 < n_groups:
            bits = pltpu.bitcast(rec[0:SUBLANES, :], jnp.uint32)
            anchor = pltpu.bitcast((bits >> 16) >> 16, F32)[0:1, :]
    y = _layer_norm(acc) * cg_ref[...] + cbeta_ref[...]
    conv_ref[...] = (y * _sigmoid(y)).astype(BF16)


def _inproj(xp, xs, mod3, w_in_bf, conv_w, conv_b, conv_g, conv_beta, prompt_seq, sample_seq):
    n_p, n_s = xp.shape[0], xs.shape[0]
    n_tok = n_p + n_s
    npt = n_p // TOK_TILE
    tiles_per_sample_seq = sample_seq // TOK_TILE
    assert TOK_TILE % prompt_seq == 0 and n_p % TOK_TILE == 0 and sample_seq % TOK_TILE == 0

    def mod_row(i):
        return jnp.where(i < npt, 0, 1 + (i - npt) // tiles_per_sample_seq)

    const = lambda i: (0, 0)
    return pl.pallas_call(
        functools.partial(_inproj_kernel, npt, prompt_seq),
        out_shape=(jax.ShapeDtypeStruct((n_tok, CONV_WIDTH), BF16),
                   jax.ShapeDtypeStruct((n_tok, REC_COLS), F32)),
        grid=(n_tok // TOK_TILE,),
        in_specs=[pl.BlockSpec((TOK_TILE, D_MODEL), lambda i: (jnp.minimum(i, npt - 1), 0)),
                  pl.BlockSpec((TOK_TILE, D_MODEL), lambda i: (jnp.maximum(i - npt, 0), 0)),
                  pl.BlockSpec((1, 1, 6 * D_MODEL), lambda i: (mod_row(i), 0, 0)),
                  pl.BlockSpec(w_in_bf.shape, const),
                  pl.BlockSpec(conv_w.shape, const),
                  pl.BlockSpec((1, CONV_WIDTH), const),
                  pl.BlockSpec((1, CONV_WIDTH), const),
                  pl.BlockSpec((1, CONV_WIDTH), const)],
        out_specs=(pl.BlockSpec((TOK_TILE, CONV_WIDTH), lambda i: (i, 0)),
                   pl.BlockSpec((TOK_TILE, REC_COLS), lambda i: (i, 0))),
        compiler_params=pltpu.CompilerParams(dimension_semantics=("parallel",),
                                             vmem_limit_bytes=VMEM_LIMIT),
        name="inproj_conv",
    )(xp, xs, mod3, w_in_bf, conv_w, conv_b, conv_g, conv_beta)


def _scan_constants():
    c = CHUNK
    w = np.zeros((N_LEVELS + 2, c, c), np.float32)
    m = np.zeros((N_LEVELS + 1, c, c), np.float32)
    t = np.arange(c)
    for j in range(N_LEVELS):
        half = 1 << j
        for ti in range(c):
            ref = (ti & ~(2 * half - 1)) + half - 1
            if ti & half:
                w[j, ti, ref + 1:ti + 1] = 1.0
            else:
                w[j, ti, ti + 1:ref + 1] = 1.0
        upper = (t[:, None] & half) != 0
        lower = (t[None, :] & half) == 0
        same = (t[:, None] >> (j + 1)) == (t[None, :] >> (j + 1))
        m[j] = (upper & lower & same).astype(np.float32)
    w[N_LEVELS] = (t[None, :] <= t[:, None]).astype(np.float32)
    w[N_LEVELS + 1] = (t[None, :] > t[:, None]).astype(np.float32)
    m[N_LEVELS] = np.eye(c, dtype=np.float32)
    w_f = np.tile(w.reshape(-1, c), (1, 3))
    w_b = np.tile(w[:, ::-1, ::-1].reshape(-1, c), (1, 3))
    m_b = m[:, ::-1, ::-1]
    return (jnp.asarray(w_f, BF16), jnp.asarray(np.ascontiguousarray(w_b), BF16),
            jnp.asarray(m), jnp.asarray(np.ascontiguousarray(m_b)))


def _scan_gates(z, lb, w_ref):
    e = jnp.exp(-jnp.abs(z))
    r = 1.0 / (1.0 + e)
    a = e * r
    sig = jnp.where(z >= 0, r, a)
    sig_neg = jnp.where(z >= 0, a, r)
    one_m_lb = 1.0 - lb
    log_f = jnp.log2(lb + one_m_lb * sig)
    k = one_m_lb * sig_neg
    terms = jnp.concatenate(_split3(log_f), axis=0)
    ex = jnp.exp2(_dot(w_ref[...], terms))
    return k, ex


def _scan_intra(q, k, ex, m_ref):
    c = CHUNK
    qb, kb = q.astype(BF16), k.astype(BF16)
    a_mat = m_ref[N_LEVELS] * _dot_nt(qb, kb)
    for j in range(N_LEVELS):
        xj = ex[j * c:(j + 1) * c].astype(BF16)
        a_mat = a_mat + m_ref[j] * _dot_nt(qb * xj, kb * xj)
    return a_mat


def _scan_kernel(fblk, bblk, first, seq, qf_ref, zf_ref, vf_ref, qb_ref, zb_ref, vb_ref, lb_ref,
                 wf_ref, wb_ref, mf_ref, mb_ref, s0f_ref, s0b_ref, of_ref, ob_ref, sf_ref, sb_ref):
    s = pl.program_id(0)

    @pl.when(first[s] == 1)
    def _():
        sf_ref[...] = s0f_ref[...]
        sb_ref[...] = s0b_ref[...]

    c = CHUNK
    dirs = ((qf_ref, zf_ref, vf_ref, 0, wf_ref, mf_ref, sf_ref, of_ref, c - 1,
             tuple(range(SCAN_CHUNKS))),
            (qb_ref, zb_ref, vb_ref, 1, wb_ref, mb_ref, sb_ref, ob_ref, 0,
             tuple(reversed(range(SCAN_CHUNKS)))))
    units = [(d, slice(sub * c, (sub + 1) * c)) for d in dirs for sub in d[-1]]
    gates = [_scan_gates(d[1][rows, :], lb_ref[d[3]:d[3] + 1, :], d[4]) for d, rows in units]
    intra = []
    for (d, rows), (k, ex) in zip(units, gates):
        for h in range(REC_HEADS):
            cols = slice(h * REC_DK, (h + 1) * REC_DK)
            intra.append(_scan_intra(d[0][rows, cols], k[:, cols], ex[:, cols], d[5]))
    idx = 0
    for (d, rows), (k, ex) in zip(units, gates):
        q_ref, _, v_ref, _, _, _, st_ref, o_ref, last, _ = d
        for h in range(REC_HEADS):
            cols = slice(h * REC_DK, (h + 1) * REC_DK)
            q, v = q_ref[rows, cols], v_ref[rows, cols]
            ex_b = ex[N_LEVELS * c:(N_LEVELS + 1) * c, cols]
            ex_l = ex[(N_LEVELS + 1) * c:(N_LEVELS + 2) * c, cols]
            st = st_ref[0, h]
            o_ref[rows, cols] = (_dot(intra[idx].astype(BF16), v.astype(BF16))
                                 + _dot_nt((q * ex_b).astype(BF16), st.astype(BF16)))
            st_ref[0, h] = (st * ex_b[last:last + 1, :]
                            + _dot(v.T.astype(BF16), (k[:, cols] * ex_l).astype(BF16)))
            idx += 1


def _scan(rec, lb, s0f_t, s0b_t, seq_lens):
    n_tok = rec.shape[0]
    fblk, bblk, first, seq = [], [], [], []
    base = 0
    step_rows = CHUNK * SCAN_CHUNKS
    for si, ln in enumerate(seq_lens):
        assert ln % step_rows == 0
        n = ln // step_rows
        for ci in range(n):
            fblk.append(base + ci)
            bblk.append(base + n - 1 - ci)
            first.append(1 if ci == 0 else 0)
            seq.append(si)
        base += n
    steps = len(fblk)
    as_i32 = lambda a: jnp.asarray(np.asarray(a, np.int32))
    w_f, w_b, m_f, m_b = _scan_constants()
    n_seq = len(seq_lens)

    def col(block_of, c):
        return pl.BlockSpec((step_rows, REC_WIDTH), lambda s, fb, bb, fi, sq: (block_of(fb, bb)[s], c))

    fwd = lambda fb, bb: fb
    bwd = lambda fb, bb: bb
    const2 = lambda s, fb, bb, fi, sq: (0, 0)
    const3 = lambda s, fb, bb, fi, sq: (0, 0, 0)
    state_spec = pl.BlockSpec((1, REC_HEADS, REC_DK, REC_DK), lambda s, fb, bb, fi, sq: (sq[s], 0, 0, 0))
    grid_spec = pltpu.PrefetchScalarGridSpec(
        num_scalar_prefetch=4,
        grid=(steps,),
        in_specs=[col(fwd, 0), col(fwd, 1), col(fwd, 3), col(bwd, 0), col(bwd, 2), col(bwd, 3),
                  pl.BlockSpec((2, REC_WIDTH), const2),
                  pl.BlockSpec(w_f.shape, const2), pl.BlockSpec(w_b.shape, const2),
                  pl.BlockSpec(m_f.shape, const3), pl.BlockSpec(m_b.shape, const3),
                  state_spec, state_spec],
        out_specs=(pl.BlockSpec((step_rows, REC_WIDTH), lambda s, fb, bb, fi, sq: (fb[s], 0)),
                   pl.BlockSpec((step_rows, REC_WIDTH), lambda s, fb, bb, fi, sq: (bb[s], 0)),
                   state_spec, state_spec),
    )
    st_shape = jax.ShapeDtypeStruct((n_seq, REC_HEADS, REC_DK, REC_DK), F32)
    return pl.pallas_call(
        _scan_kernel,
        out_shape=(jax.ShapeDtypeStruct((n_tok, REC_WIDTH), F32),
                   jax.ShapeDtypeStruct((n_tok, REC_WIDTH), F32), st_shape, st_shape),
        grid_spec=grid_spec,
        compiler_params=pltpu.CompilerParams(dimension_semantics=("arbitrary",),
                                             vmem_limit_bytes=VMEM_LIMIT),
        name="hgrn2_scan",
    )(as_i32(fblk), as_i32(bblk), as_i32(first), as_i32(seq),
      rec, rec, rec, rec, rec, rec, lb, w_f, w_b, m_f, m_b, s0f_t, s0b_t)


def _post_kernel(n_prompt_tiles, alpha, of_ref, ob_ref, g_ref, conv_ref, xp_ref, xs_ref, mod_ref,
                 wout_ref, rg_ref, l1g_ref, l1b_ref, rwh_ref, rwl_ref, rb_ref, tri_ref,
                 x1_ref, h2_ref, rank_t_ref, slot_t_ref, w4_ref, cnt_ref, start_ref, run_ref):
    i = pl.program_id(0)

    @pl.when(i == 0)
    def _():
        run_ref[...] = jnp.zeros_like(run_ref)

    x = jnp.where(i < n_prompt_tiles, xp_ref[...], xs_ref[...])
    mod = mod_ref[0]
    g1 = mod[:, 2 * D_MODEL:3 * D_MODEL]
    sh2 = mod[:, 3 * D_MODEL:4 * D_MODEL]
    sc2 = mod[:, 4 * D_MODEL:5 * D_MODEL]

    g = g_ref[...]
    silu_g = g * _sigmoid(g)
    mix = _dot(conv_ref[...], wout_ref[0:CONV_WIDTH, :])
    for h in range(REC_HEADS):
        cols = slice(h * REC_DK, (h + 1) * REC_DK)
        o = of_ref[:, cols] + ob_ref[:, cols]
        o = o * lax.rsqrt(jnp.mean(o * o, axis=-1, keepdims=True) + RMS_EPS)
        rec_out = (o * rg_ref[:, cols] * silu_g[:, cols]).astype(BF16)
        mix = mix + _dot(rec_out, wout_ref[CONV_WIDTH + h * REC_DK:CONV_WIDTH + (h + 1) * REC_DK, :])

    x1 = _layer_norm(alpha * x + g1 * mix) * l1g_ref[...] + l1b_ref[...]
    x1_ref[...] = x1
    h2 = _layer_norm(x1) * (1.0 + sc2) + sh2
    _store_row_tiles(h2_ref, h2)

    h_hi = h2.astype(BF16)
    h_lo = (h2 - h_hi.astype(F32)).astype(BF16)
    logits = (_dot(h_hi, rwh_ref[...]) + _dot(h_lo, rwh_ref[...]) + _dot(h_hi, rwl_ref[...])
              + rb_ref[...])
    lane = lax.broadcasted_iota(jnp.int32, logits.shape, 1)
    lane_f = lane.astype(F32)
    neg_inf = jnp.float32(-jnp.inf)
    work = jnp.where(lane < N_EXPERTS, logits, neg_inf)
    vals, sels = [], []
    for _ in range(TOP_K):
        m = jnp.max(work, axis=-1, keepdims=True)
        idx = jnp.min(jnp.where(work == m, lane_f, float(LANES)), axis=-1, keepdims=True)
        sel = lane_f == idx
        vals.append(m)
        sels.append(sel)
        work = jnp.where(sel, neg_inf, work)
    exps = [jnp.exp(v - vals[0]) for v in vals]
    denom = exps[0] + exps[1] + exps[2] + exps[3]

    sel_any = jnp.zeros(logits.shape, F32)
    for sel in sels:
        sel_any = jnp.where(sel, 1.0, sel_any)
    run_old = run_ref[...]
    prior = _dot(tri_ref[...], sel_any.astype(BF16)) + run_old
    run_new = run_old + jnp.sum(sel_any, axis=0, keepdims=True)
    run_ref[...] = run_new
    cnt_ref[...] = run_new
    start_ref[0] = run_old

    slot = jnp.zeros(logits.shape, F32)
    w4 = jnp.zeros(logits.shape, F32)
    for j in range(TOP_K):
        slot = jnp.where(sels[j], float(j), slot)
        w4 = jnp.where(lane == j, exps[j] / denom, w4)
    routed = sel_any > 0.0
    rank_t_ref[...] = jnp.where(routed, prior, -1.0).T[:N_EXPERTS]
    slot_t_ref[...] = jnp.where(routed, slot, -1.0).T[:N_EXPERTS]
    w4_ref[...] = w4


def _post(o_f, o_b, rec, conv, xp, xs, mod3, w_out_bf, rec_g, ln1_g, ln1_b, rw_hi, rw_lo, rb,
          alpha, sample_seq):
    n_p, n_s = xp.shape[0], xs.shape[0]
    n_tok = n_p + n_s
    npt = n_p // TOK_TILE
    tiles_per_sample_seq = sample_seq // TOK_TILE
    tri = jnp.asarray(np.tril(np.ones((TOK_TILE, TOK_TILE), np.float32), -1), BF16)

    def mod_row(i):
        return jnp.where(i < npt, 0, 1 + (i - npt) // tiles_per_sample_seq)

    const = lambda i: (0, 0)
    tok = lambda w: pl.BlockSpec((TOK_TILE, w), lambda i: (i, 0))
    return pl.pallas_call(
        functools.partial(_post_kernel, npt, alpha),
        out_shape=(jax.ShapeDtypeStruct((n_tok, D_MODEL), F32),
                   jax.ShapeDtypeStruct((n_tok * SUBLANES, LANES), F32),
                   jax.ShapeDtypeStruct((N_EXPERTS, n_tok), F32),
                   jax.ShapeDtypeStruct((N_EXPERTS, n_tok), F32),
                   jax.ShapeDtypeStruct((n_tok, LANES), F32),
                   jax.ShapeDtypeStruct((1, LANES), F32),
                   jax.ShapeDtypeStruct((n_tok // TOK_TILE, 1, LANES), F32)),
        grid=(n_tok // TOK_TILE,),
        in_specs=[tok(REC_WIDTH), tok(REC_WIDTH),
                  pl.BlockSpec((TOK_TILE, REC_WIDTH), lambda i: (i, 4)),
                  tok(CONV_WIDTH),
                  pl.BlockSpec((TOK_TILE, D_MODEL), lambda i: (jnp.minimum(i, npt - 1), 0)),
                  pl.BlockSpec((TOK_TILE, D_MODEL), lambda i: (jnp.maximum(i - npt, 0), 0)),
                  pl.BlockSpec((1, 1, 6 * D_MODEL), lambda i: (mod_row(i), 0, 0)),
                  pl.BlockSpec(w_out_bf.shape, const),
                  pl.BlockSpec((1, REC_WIDTH), const),
                  pl.BlockSpec((1, D_MODEL), const), pl.BlockSpec((1, D_MODEL), const),
                  pl.BlockSpec((D_MODEL, LANES), const), pl.BlockSpec((D_MODEL, LANES), const),
                  pl.BlockSpec((1, LANES), const),
                  pl.BlockSpec((TOK_TILE, TOK_TILE), const)],
        out_specs=(tok(D_MODEL), pl.BlockSpec((TOK_TILE * SUBLANES, LANES), lambda i: (i, 0)),
                   pl.BlockSpec((N_EXPERTS, TOK_TILE), lambda i: (0, i)),
                   pl.BlockSpec((N_EXPERTS, TOK_TILE), lambda i: (0, i)),
                   tok(LANES), pl.BlockSpec((1, LANES), const),
                   pl.BlockSpec((1, 1, LANES), lambda i: (i, 0, 0))),
        scratch_shapes=[pltpu.VMEM((1, LANES), F32)],
        compiler_params=pltpu.CompilerParams(dimension_semantics=("arbitrary",),
                                             vmem_limit_bytes=VMEM_LIMIT),
        name="post_mixer_router",
    )(o_f, o_b, rec, conv, xp, xs, mod3, w_out_bf, rec_g, ln1_g, ln1_b, rw_hi, rw_lo, rb, tri)


PLAN_ROWS = 32


def _plan_kernel(n_tok, tile_e, gstart, n_act, cum, starts_ref, ends_ref, rank_t_ref, slot_t_ref,
                 src_ref, dst_ref, acc_ref):
    i = pl.program_id(0)

    @pl.when(i < n_act[0])
    def _():
        e = tile_e[i]
        base = i * EXP_TILE - gstart[e]
        rows_iota = lax.broadcasted_iota(jnp.int32, (PLAN_ROWS, 1), 0)
        lane = lax.broadcasted_iota(jnp.int32, (1, TOK_TILE), 1)
        acc_ref[...] = jnp.zeros_like(acc_ref)

        base_f = base.astype(F32)
        win0 = jnp.sum((ends_ref[pl.ds(e, 1), :] <= base_f).astype(jnp.int32))
        win1 = jnp.sum((starts_ref[pl.ds(e, 1), :] < base_f + EXP_TILE).astype(jnp.int32))

        def window(b, carry):
            tok0 = pl.multiple_of(b * TOK_TILE, TOK_TILE)
            rk = rank_t_ref[pl.ds(e, 1), pl.ds(tok0, TOK_TILE)]
            sl = slot_t_ref[pl.ds(e, 1), pl.ds(tok0, TOK_TILE)]
            code = ((tok0 + lane) * TOP_K).astype(F32) + sl + 1.0
            lo = jnp.maximum(cum[b * N_EXPERTS + e] - base, 0)
            hi = jnp.minimum(cum[(b + 1) * N_EXPERTS + e] - base, EXP_TILE)

            def rows(k, c):
                r0 = pl.multiple_of(k * PLAN_ROWS, PLAN_ROWS)
                want = (base + r0 + rows_iota).astype(F32)
                hit = jnp.where(rk == want, code, 0.0)
                part = hit[:, 0:LANES]
                for c0 in range(LANES, TOK_TILE, LANES):
                    part = part + hit[:, c0:c0 + LANES]
                acc_ref[pl.ds(r0, PLAN_ROWS), :] += part
                return c

            shift = PLAN_ROWS.bit_length() - 1
            lax.fori_loop(lo >> shift, (hi + PLAN_ROWS - 1) >> shift, rows, 0)
            return carry

        lax.fori_loop(win0, win1, window, 0)
        code = jnp.sum(acc_ref[...].T, axis=0, keepdims=True).astype(jnp.int32) - 1
        pos = lax.broadcasted_iota(jnp.int32, (1, EXP_TILE), 1)
        tok = code >> 2
        src = jnp.where(code >= 0, tok, lax.rem(i * EXP_TILE + pos, n_tok))
        spare = TOP_K * n_tok + lax.rem(i, N_YBUF) * EXP_TILE + pos
        dst = jnp.where(code >= 0, (code & (TOP_K - 1)) * n_tok + tok, spare)
        src_ref[0] = src * SUBLANES
        dst_ref[0] = dst * SUBLANES

    @pl.when(i >= n_act[0])
    def _():
        src_ref[...] = jnp.zeros_like(src_ref)
        dst_ref[...] = jnp.zeros_like(dst_ref)


def _plan(rank_t, slot_t, tile_e, gstart, n_act, cum2d):
    n_tok = rank_t.shape[1]
    n_tiles = tile_e.shape[0]
    n_tok_tiles = cum2d.shape[0] - 1
    assert n_tok_tiles <= LANES
    huge = jnp.float32(2.0 ** 30)
    as_rows = lambda a: jnp.pad(a.T.astype(F32), ((0, 0), (0, LANES - n_tok_tiles)),
                                constant_values=huge)
    whole = lambda i, *_: (0, 0)
    per_tile = pl.BlockSpec((1, 1, EXP_TILE), lambda i, *_: (i, 0, 0))
    grid_spec = pltpu.PrefetchScalarGridSpec(
        num_scalar_prefetch=4,
        grid=(n_tiles,),
        in_specs=[pl.BlockSpec((N_EXPERTS, LANES), whole), pl.BlockSpec((N_EXPERTS, LANES), whole),
                  pl.BlockSpec(rank_t.shape, whole), pl.BlockSpec(slot_t.shape, whole)],
        out_specs=(per_tile, per_tile),
        scratch_shapes=[pltpu.VMEM((EXP_TILE, LANES), F32)],
    )
    shape = jax.ShapeDtypeStruct((n_tiles, 1, EXP_TILE), jnp.int32)
    return pl.pallas_call(
        functools.partial(_plan_kernel, n_tok),
        out_shape=(shape, shape),
        grid_spec=grid_spec,
        compiler_params=pltpu.CompilerParams(dimension_semantics=("arbitrary",),
                                             vmem_limit_bytes=VMEM_LIMIT),
        name="moe_plan",
    )(tile_e, gstart, n_act, cum2d.reshape(-1), as_rows(cum2d[:-1]), as_rows(cum2d[1:]),
      rank_t, slot_t)


N_YBUF = 3
DMA_UNROLL = 8


def _expert_kernel(tile_e, n_act, src0_ref, srcn_ref, dst_ref, h_ref, wgu_ref, bgu_ref, wd_ref,
                   bd_ref, yk_ref, xbuf, ybuf, gsem, ssem, wgu_bf, wd_bf):
    i = pl.program_id(0)
    n_steps = pl.num_programs(0)
    n_active = n_act[0]
    tile_rows = EXP_TILE * SUBLANES
    n_slot_rows = yk_ref.shape[0] - N_YBUF * tile_rows

    def row_tile(ref, start):
        return ref.at[pl.ds(pl.multiple_of(start, SUBLANES), SUBLANES)]

    def for_each_row(fn):
        def body(g, carry):
            for k in range(DMA_UNROLL):
                fn(g * DMA_UNROLL + k)
            return carry
        lax.fori_loop(0, EXP_TILE // DMA_UNROLL, body, 0)

    def on_slot(slot, n, fn):
        if isinstance(slot, int):
            fn(slot)
            return
        for s in range(n):
            pl.when(slot == s)(functools.partial(fn, s))

    def gather_rows(src_ref, slot):
        def run(s):
            def one(r):
                pltpu.make_async_copy(row_tile(h_ref, src_ref[0, 0, r]),
                                      row_tile(xbuf.at[s], r * SUBLANES), gsem.at[s]).start()
            for_each_row(one)
        on_slot(slot, 2, run)

    def scatter_rows(slot):
        def run(s):
            def one(r):
                pltpu.make_async_copy(row_tile(ybuf.at[s], r * SUBLANES),
                                      row_tile(yk_ref, dst_ref[0, 0, r]), ssem.at[s]).start()
            for_each_row(one)
        on_slot(slot, N_YBUF, run)

    def scatter_wait(slot):
        pltpu.make_async_copy(ybuf.at[slot], yk_ref.at[pl.ds(0, tile_rows)], ssem.at[slot]).wait()

    @pl.when(i == 0)
    def _():
        ybuf[0] = jnp.zeros((tile_rows, LANES), F32)
        for spare in range(N_YBUF):
            cp = pltpu.make_async_copy(
                ybuf.at[0], yk_ref.at[pl.ds(n_slot_rows + spare * tile_rows, tile_rows)],
                ssem.at[0])
            cp.start()
            cp.wait()
        gather_rows(src0_ref, 0)

    @pl.when(i + 1 < n_active)
    def _():
        gather_rows(srcn_ref, (i + 1) % 2)

    prev = tile_e[jnp.maximum(i - 1, 0)]
    new_expert = jnp.logical_or(i == 0, tile_e[i] != prev)

    @pl.when(jnp.logical_and(new_expert, i < n_active))
    def _():
        rows = 128

        def cast(r, carry):
            sl = pl.ds(pl.multiple_of(r * rows, rows), rows)
            wgu_bf[sl, :] = wgu_ref[0, sl, :].astype(BF16)
            wd_bf[sl, :] = wd_ref[0, sl, :].astype(BF16)
            return carry

        lax.fori_loop(0, D_MODEL // rows, cast, 0)

    @pl.when(i < n_active)
    def _():
        xslot = i % 2
        yslot = i % N_YBUF
        pltpu.make_async_copy(h_ref.at[pl.ds(0, tile_rows)], xbuf.at[xslot], gsem.at[xslot]).wait()

        @pl.when(i >= N_YBUF)
        def _():
            scatter_wait(yslot)

        x = _load_row_tiles(xbuf.at[xslot], EXP_TILE).astype(BF16)
        gu = _dot(x, wgu_bf[...]) + bgu_ref[0]
        gate = jnp.minimum(gu[:, :D_FF], SWIGLU_LIMIT)
        up = jnp.clip(gu[:, D_FF:], -SWIGLU_LIMIT, SWIGLU_LIMIT)
        act = (up + 1.0) * gate * _sigmoid(SWIGLU_ALPHA * gate)
        _store_row_tiles(ybuf.at[yslot], _dot(act.astype(BF16), wd_bf[...]) + bd_ref[0])
        scatter_rows(yslot)

    @pl.when(i == n_steps - 1)
    def _():
        for back in range(1, N_YBUF + 1):
            scatter_wait((n_active - back) % N_YBUF)


def _experts(h2, src3, dst3, tile_e, n_act, w_gate_up, b_gate_up, w_down, b_down):
    n_tok = h2.shape[0] // SUBLANES
    n_tiles = src3.shape[0]
    idx_block = (1, 1, EXP_TILE)
    grid_spec = pltpu.PrefetchScalarGridSpec(
        num_scalar_prefetch=2,
        grid=(n_tiles,),
        in_specs=[pl.BlockSpec(idx_block, lambda i, te, na: (0, 0, 0), memory_space=pltpu.SMEM),
                  pl.BlockSpec(idx_block, lambda i, te, na: (jnp.minimum(i + 1, n_tiles - 1), 0, 0),
                               memory_space=pltpu.SMEM),
                  pl.BlockSpec(idx_block, lambda i, te, na: (i, 0, 0), memory_space=pltpu.SMEM),
                  pl.BlockSpec(memory_space=pl.ANY),
                  pl.BlockSpec((1, D_MODEL, 2 * D_FF), lambda i, te, na: (te[i], 0, 0)),
                  pl.BlockSpec((1, 1, 2 * D_FF), lambda i, te, na: (te[i], 0, 0)),
                  pl.BlockSpec((1, D_FF, D_MODEL), lambda i, te, na: (te[i], 0, 0)),
                  pl.BlockSpec((1, 1, D_MODEL), lambda i, te, na: (te[i], 0, 0))],
        out_specs=pl.BlockSpec(memory_space=pl.ANY),
        scratch_shapes=[pltpu.VMEM((2, EXP_TILE * SUBLANES, LANES), F32),
                        pltpu.VMEM((N_YBUF, EXP_TILE * SUBLANES, LANES), F32),
                        pltpu.SemaphoreType.DMA((2,)), pltpu.SemaphoreType.DMA((N_YBUF,)),
                        pltpu.VMEM((D_MODEL, 2 * D_FF), BF16), pltpu.VMEM((D_FF, D_MODEL), BF16)],
    )
    return pl.pallas_call(
        _expert_kernel,
        out_shape=jax.ShapeDtypeStruct(((TOP_K * n_tok + N_YBUF * EXP_TILE) * SUBLANES, LANES), F32),
        grid_spec=grid_spec,
        compiler_params=pltpu.CompilerParams(dimension_semantics=("arbitrary",),
                                             vmem_limit_bytes=VMEM_LIMIT),
        name="moe_experts",
    )(tile_e, n_act, src3, src3, dst3, h2, w_gate_up, b_gate_up.reshape(N_EXPERTS, 1, 2 * D_FF),
      w_down, b_down.reshape(N_EXPERTS, 1, D_MODEL))


COMBINE_TILE = 512


def _combine_kernel(n_prompt_tiles, alpha, y0_ref, y1_ref, y2_ref, y3_ref, x1_ref, w4_ref, mod_ref,
                    l2g_ref, l2b_ref, yp_ref, ysm_ref):
    i = pl.program_id(0)
    w4 = w4_ref[...]
    ff = _load_row_tiles(y0_ref, COMBINE_TILE) * w4[:, 0:1]
    for j, y_ref in enumerate((y1_ref, y2_ref, y3_ref), start=1):
        ff = ff + _load_row_tiles(y_ref, COMBINE_TILE) * w4[:, j:j + 1]
    g2 = mod_ref[0][:, 5 * D_MODEL:6 * D_MODEL]
    out = _layer_norm(alpha * x1_ref[...] + g2 * ff) * l2g_ref[...] + l2b_ref[...]

    @pl.when(i < n_prompt_tiles)
    def _():
        yp_ref[...] = out

    @pl.when(i >= n_prompt_tiles)
    def _():
        ysm_ref[...] = out


def _combine(yk, x1, w4, mod3, ln2_g, ln2_b, n_p, alpha, sample_seq):
    n_tok = x1.shape[0]
    n_s = n_tok - n_p
    assert n_p % COMBINE_TILE == 0 and sample_seq % COMBINE_TILE == 0
    npt = n_p // COMBINE_TILE
    n_tok_tiles = n_tok // COMBINE_TILE
    tiles_per_sample_seq = sample_seq // COMBINE_TILE

    def mod_row(i):
        return jnp.where(i < npt, 0, 1 + (i - npt) // tiles_per_sample_seq)

    def slot(j):
        return pl.BlockSpec((COMBINE_TILE * SUBLANES, LANES), lambda i: (j * n_tok_tiles + i, 0))

    const = lambda i: (0, 0)
    return pl.pallas_call(
        functools.partial(_combine_kernel, npt, alpha),
        out_shape=(jax.ShapeDtypeStruct((n_p, D_MODEL), F32),
                   jax.ShapeDtypeStruct((n_s, D_MODEL), F32)),
        grid=(n_tok_tiles,),
        in_specs=[slot(0), slot(1), slot(2), slot(3),
                  pl.BlockSpec((COMBINE_TILE, D_MODEL), lambda i: (i, 0)),
                  pl.BlockSpec((COMBINE_TILE, LANES), lambda i: (i, 0)),
                  pl.BlockSpec((1, 1, 6 * D_MODEL), lambda i: (mod_row(i), 0, 0)),
                  pl.BlockSpec((1, D_MODEL), const), pl.BlockSpec((1, D_MODEL), const)],
        out_specs=(pl.BlockSpec((COMBINE_TILE, D_MODEL), lambda i: (jnp.minimum(i, npt - 1), 0)),
                   pl.BlockSpec((COMBINE_TILE, D_MODEL), lambda i: (jnp.maximum(i - npt, 0), 0))),
        compiler_params=pltpu.CompilerParams(dimension_semantics=("arbitrary",),
                                             vmem_limit_bytes=VMEM_LIMIT),
        name="moe_combine",
    )(yk, yk, yk, yk, x1, w4, mod3, ln2_g, ln2_b)


def kernel(x_prompt, x_sample, c, state_fwd, state_bwd, c_ctx, w_ada, b_ada, w_in, conv_w, conv_b,
           conv_ln_g, conv_ln_b, lb_logits, rec_norm_g, w_out, ln1_g, ln1_b, ln2_g, ln2_b,
           router_w, router_b, w_gate_up, b_gate_up, w_down, b_down):
    depth = w_ada.shape[0]
    assert depth == 1
    alpha = (2.0 * depth) ** 0.25
    nb_p, seq_p, _ = x_prompt.shape
    nb_s, seq_s, _ = x_sample.shape
    n_p, n_s = nb_p * seq_p, nb_s * seq_s
    n_tok = n_p + n_s
    row2 = lambda a: a.reshape(1, -1)

    lb = jnp.cumsum(jax.nn.softmax(lb_logits.astype(F32), axis=0), axis=0)[0]

    cond = jnp.concatenate([c_ctx[None, :], c, jnp.zeros((16 - 1 - nb_s, D_MODEL), F32)], axis=0)
    mod = _ada(cond, w_ada[0], b_ada[0])
    mod3 = mod.reshape(16, 1, 6 * D_MODEL)

    xp = x_prompt.reshape(n_p, D_MODEL)
    xs = x_sample.reshape(n_s, D_MODEL)
    conv_w_pad = jnp.concatenate([conv_w[0], jnp.zeros((1, CONV_WIDTH), F32)], axis=0)
    conv_out, rec = _inproj(xp, xs, mod3, w_in[0].astype(BF16), conv_w_pad, row2(conv_b[0]),
                            row2(conv_ln_g[0]), row2(conv_ln_b[0]), seq_p, seq_s)

    zeros_state = jnp.zeros((nb_p, REC_HEADS, REC_DK, REC_DK), F32)
    s0f = jnp.concatenate([zeros_state, jnp.swapaxes(state_fwd[:, 0], -1, -2)], axis=0)
    s0b = jnp.concatenate([zeros_state, jnp.swapaxes(state_bwd[:, 0], -1, -2)], axis=0)
    o_f, o_b, sf_t, sb_t = _scan(rec, lb, s0f, s0b, [seq_p] * nb_p + [seq_s] * nb_s)
    new_f = jnp.swapaxes(sf_t[:nb_p], -1, -2)[:, None]
    new_b = jnp.swapaxes(sb_t[:nb_p], -1, -2)[:, None]

    rw = jnp.pad(router_w[0], ((0, 0), (0, LANES - N_EXPERTS)))
    rw_hi = rw.astype(BF16)
    rw_lo = (rw - rw_hi.astype(F32)).astype(BF16)
    rb = jnp.pad(router_b[0], (0, LANES - N_EXPERTS)).reshape(1, LANES)
    x1, h2, rank_t, slot_t, w4, counts, tile_start = _post(
        o_f, o_b, rec, conv_out, xp, xs, mod3, w_out[0].astype(BF16), row2(rec_norm_g[0]),
        row2(ln1_g[0]), row2(ln1_b[0]), rw_hi, rw_lo, rb, alpha, seq_s)

    cnt = counts[0, :N_EXPERTS].astype(jnp.int32)
    gpad = ((cnt + EXP_TILE - 1) // EXP_TILE) * EXP_TILE
    gend = jnp.cumsum(gpad)
    gstart = gend - gpad
    n_rows = n_tok * TOP_K + N_EXPERTS * EXP_TILE
    n_tiles = n_rows // EXP_TILE
    n_act = gend[-1] // EXP_TILE
    tile_ids = jnp.minimum(jnp.arange(n_tiles, dtype=jnp.int32), n_act - 1)
    tile_e = jnp.sum((gend[None, :] <= tile_ids[:, None] * EXP_TILE).astype(jnp.int32), axis=1)
    n_act1 = n_act.reshape(1).astype(jnp.int32)
    starts = tile_start[:, 0, :N_EXPERTS].astype(jnp.int32)
    cum2d = jnp.concatenate([starts, cnt[None, :]], axis=0)
    src, dst = _plan(rank_t, slot_t, tile_e, gstart.astype(jnp.int32), n_act1, cum2d)
    yk = _experts(h2, src, dst, tile_e, n_act1, w_gate_up[0], b_gate_up[0], w_down[0], b_down[0])
    y_p, y_s = _combine(yk, x1, w4, mod3, row2(ln2_g[0]), row2(ln2_b[0]), n_p, alpha, seq_s)
    return (y_p.reshape(nb_p, seq_p, D_MODEL), y_s.reshape(nb_s, seq_s, D_MODEL), new_f, new_b)
```

```python
import functools

import numpy as np
import jax
import jax.numpy as jnp
from jax import lax
from jax.experimental import pallas as pl
from jax.experimental.pallas import tpu as pltpu

F32 = jnp.float32
BF16 = jnp.bfloat16

D_MODEL = 1024
CONV_WIDTH = 512
CONV_TAPS = 31
REC_HEADS = 4
REC_DK = 128
REC_WIDTH = REC_HEADS * REC_DK
REC_COLS = 5 * REC_WIDTH
CHUNK = 64
N_LEVELS = 6
SCAN_CHUNKS = 4
N_EXPERTS = 32
TOP_K = 4
D_FF = 1024
SWIGLU_LIMIT = 7.0
SWIGLU_ALPHA = 1.702
LN_EPS = 1e-5
RMS_EPS = 1e-6
GRID_W = 64

LANES = 128
TOK_TILE = 512
EXP_TILE = 512
VMEM_LIMIT = 56 * 1024 * 1024


def _sigmoid(x):
    return 1.0 / (1.0 + jnp.exp(-x))


def _layer_norm(x):
    mu = jnp.mean(x, axis=-1, keepdims=True)
    xc = x - mu
    var = jnp.mean(xc * xc, axis=-1, keepdims=True)
    return xc * lax.rsqrt(var + LN_EPS)


def _dot(a, b):
    return jnp.dot(a, b, preferred_element_type=F32)


def _dot_nt(a, b):
    return lax.dot_general(a, b, (((1,), (1,)), ((), ())), preferred_element_type=F32)


SUBLANES = 8
assert D_MODEL == SUBLANES * LANES


def _store_row_tiles(ref, value):
    for s in range(SUBLANES):
        ref[pl.ds(s, value.shape[0], stride=SUBLANES), :] = value[:, s * LANES:(s + 1) * LANES]


def _load_row_tiles(ref, n_rows):
    return jnp.concatenate(
        [ref[pl.ds(s, n_rows, stride=SUBLANES), :] for s in range(SUBLANES)], axis=-1)


def _split3(x):
    hi = x.astype(BF16)
    r1 = x - hi.astype(F32)
    mid = r1.astype(BF16)
    lo = (r1 - mid.astype(F32)).astype(BF16)
    return hi, mid, lo


def _ada_kernel(c_ref, w_ref, b_ref, o_ref):
    c = c_ref[...]
    s = (c * _sigmoid(c)).astype(BF16)
    o_ref[...] = _dot(s, w_ref[...].astype(BF16)) + b_ref[...]


def _ada(cond, w_ada, b_ada):
    rows = cond.shape[0]
    ncol = w_ada.shape[1]
    tn = 1024
    return pl.pallas_call(
        _ada_kernel,
        out_shape=jax.ShapeDtypeStruct((rows, ncol), F32),
        grid=(ncol // tn,),
        in_specs=[pl.BlockSpec((rows, D_MODEL), lambda j: (0, 0)),
                  pl.BlockSpec((D_MODEL, tn), lambda j: (0, j)),
                  pl.BlockSpec((1, tn), lambda j: (0, j))],
        out_specs=pl.BlockSpec((rows, tn), lambda j: (0, j)),
        compiler_params=pltpu.CompilerParams(dimension_semantics=("parallel",),
                                             vmem_limit_bytes=VMEM_LIMIT),
        name="ada_mod",
    )(cond, w_ada, b_ada.reshape(1, ncol))


def _inproj_kernel(n_prompt_tiles, prompt_seg, xp_ref, xs_ref, mod_ref, win_ref, cw_ref, cb_ref,
                   cg_ref, cbeta_ref, conv_ref, rec_ref):
    i = pl.program_id(0)
    is_prompt = i < n_prompt_tiles
    x = jnp.where(is_prompt, xp_ref[...], xs_ref[...])
    mod = mod_ref[0]
    sh1 = mod[:, 0:D_MODEL]
    sc1 = mod[:, D_MODEL:2 * D_MODEL]
    h = (_layer_norm(x) * (1.0 + sc1) + sh1).astype(BF16)

    glu = _dot(h, win_ref[:, 0:2 * CONV_WIDTH])
    u = glu[:, :CONV_WIDTH] * _sigmoid(glu[:, CONV_WIDTH:])

    seg_len = jnp.where(is_prompt, prompt_seg, GRID_W)
    half = CONV_TAPS // 2
    pos = lax.broadcasted_iota(jnp.int32, (TOK_TILE, 1), 0) & (seg_len - 1)

    def masked_shift(d):
        sh = pltpu.roll(u, (-d) % TOK_TILE, axis=0)
        edge_rows = SUBLANES * ((abs(d) + SUBLANES - 1) // SUBLANES)
        pieces = []
        for s0 in range(0, TOK_TILE, GRID_W):
            if d < 0:
                edge = slice(s0, s0 + edge_rows)
                pieces.append(jnp.where(pos[edge] >= -d, sh[edge], 0.0))
                pieces.append(sh[s0 + edge_rows:s0 + GRID_W])
            else:
                edge = slice(s0 + GRID_W - edge_rows, s0 + GRID_W)
                pieces.append(sh[s0:s0 + GRID_W - edge_rows])
                pieces.append(jnp.where(pos[edge] < seg_len - d, sh[edge], 0.0))
        return jnp.concatenate(pieces, axis=0)

    acc = u * cw_ref[half:half + 1, :] + cb_ref[...]
    taps = [j for j in range(CONV_TAPS) if j != half]
    n_groups = REC_COLS // REC_WIDTH
    per_group = len(taps) // n_groups
    anchor = None
    for g in range(n_groups):
        cols = slice(g * REC_WIDTH, (g + 1) * REC_WIDTH)
        rec = _dot(h, win_ref[:, 2 * CONV_WIDTH + g * REC_WIDTH:
                              2 * CONV_WIDTH + (g + 1) * REC_WIDTH])
        rec_ref[:, cols] = rec
        for j in taps[g * per_group:(g + 1) * per_group]:
            w_tap = cw_ref[j:j + 1, :]
            if anchor is not None:
                w_tap = w_tap + anchor
                anchor = None
            acc = acc + masked_shift(j - half) * w_tap
        if g + 1 < n_groups:
            bits = pltpu.bitcast(rec[0:SUBLANES, :], jnp.uint32)
            anchor = pltpu.bitcast((bits >> 16) >> 16, F32)[0:1, :]
    y = _layer_norm(acc) * cg_ref[...] + cbeta_ref[...]
    conv_ref[...] = (y * _sigmoid(y)).astype(BF16)


def _inproj(xp, xs, mod3, w_in_bf, conv_w, conv_b, conv_g, conv_beta, prompt_seq, sample_seq):
    n_p, n_s = xp.shape[0], xs.shape[0]
    n_tok = n_p + n_s
    npt = n_p // TOK_TILE
    tiles_per_sample_seq = sample_seq // TOK_TILE
    assert TOK_TILE % prompt_seq == 0 and n_p % TOK_TILE == 0 and sample_seq % TOK_TILE == 0

    def mod_row(i):
        return jnp.where(i < npt, 0, 1 + (i - npt) // tiles_per_sample_seq)

    const = lambda i: (0, 0)
    return pl.pallas_call(
        functools.partial(_inproj_kernel, npt, prompt_seq),
        out_shape=(jax.ShapeDtypeStruct((n_tok, CONV_WIDTH), BF16),
                   jax.ShapeDtypeStruct((n_tok, REC_COLS), F32)),
        grid=(n_tok // TOK_TILE,),
        in_specs=[pl.BlockSpec((TOK_TILE, D_MODEL), lambda i: (jnp.minimum(i, npt - 1), 0)),
                  pl.BlockSpec((TOK_TILE, D_MODEL), lambda i: (jnp.maximum(i - npt, 0), 0)),
                  pl.BlockSpec((1, 1, 6 * D_MODEL), lambda i: (mod_row(i), 0, 0)),
                  pl.BlockSpec(w_in_bf.shape, const),
                  pl.BlockSpec(conv_w.shape, const),
                  pl.BlockSpec((1, CONV_WIDTH), const),
                  pl.BlockSpec((1, CONV_WIDTH), const),
                  pl.BlockSpec((1, CONV_WIDTH), const)],
        out_specs=(pl.BlockSpec((TOK_TILE, CONV_WIDTH), lambda i: (i, 0)),
                   pl.BlockSpec((TOK_TILE, REC_COLS), lambda i: (i, 0))),
        compiler_params=pltpu.CompilerParams(dimension_semantics=("parallel",),
                                             vmem_limit_bytes=VMEM_LIMIT),
        name="inproj_conv",
    )(xp, xs, mod3, w_in_bf, conv_w, conv_b, conv_g, conv_beta)


def _scan_constants():
    c = CHUNK
    w = np.zeros((N_LEVELS + 2, c, c), np.float32)
    m = np.zeros((N_LEVELS + 1, c, c), np.float32)
    t = np.arange(c)
    for j in range(N_LEVELS):
        half = 1 << j
        for ti in range(c):
            ref = (ti & ~(2 * half - 1)) + half - 1
            if ti & half:
                w[j, ti, ref + 1:ti + 1] = 1.0
            else:
                w[j, ti, ti + 1:ref + 1] = 1.0
        upper = (t[:, None] & half) != 0
        lower = (t[None, :] & half) == 0
        same = (t[:, None] >> (j + 1)) == (t[None, :] >> (j + 1))
        m[j] = (upper & lower & same).astype(np.float32)
    w[N_LEVELS] = (t[None, :] <= t[:, None]).astype(np.float32)
    w[N_LEVELS + 1] = (t[None, :] > t[:, None]).astype(np.float32)
    m[N_LEVELS] = np.eye(c, dtype=np.float32)
    w_f = np.tile(w.reshape(-1, c), (1, 3))
    w_b = np.tile(w[:, ::-1, ::-1].reshape(-1, c), (1, 3))
    m_b = m[:, ::-1, ::-1]
    return (jnp.asarray(w_f, BF16), jnp.asarray(np.ascontiguousarray(w_b), BF16),
            jnp.asarray(m), jnp.asarray(np.ascontiguousarray(m_b)))


def _scan_gates(z, lb, w_ref):
    e = jnp.exp(-jnp.abs(z))
    r = 1.0 / (1.0 + e)
    a = e * r
    sig = jnp.where(z >= 0, r, a)
    sig_neg = jnp.where(z >= 0, a, r)
    one_m_lb = 1.0 - lb
    log_f = jnp.log2(lb + one_m_lb * sig)
    k = one_m_lb * sig_neg
    terms = jnp.concatenate(_split3(log_f), axis=0)
    ex = jnp.exp2(_dot(w_ref[...], terms))
    return k, ex


def _scan_intra(q, k, ex, m_ref):
    c = CHUNK
    qb, kb = q.astype(BF16), k.astype(BF16)
    a_mat = m_ref[N_LEVELS] * _dot_nt(qb, kb)
    for j in range(N_LEVELS):
        xj = ex[j * c:(j + 1) * c].astype(BF16)
        a_mat = a_mat + m_ref[j] * _dot_nt(qb * xj, kb * xj)
    return a_mat


def _scan_kernel(fblk, bblk, first, seq, qf_ref, zf_ref, vf_ref, qb_ref, zb_ref, vb_ref, lb_ref,
                 wf_ref, wb_ref, mf_ref, mb_ref, s0f_ref, s0b_ref, of_ref, ob_ref, sf_ref, sb_ref):
    s = pl.program_id(0)

    @pl.when(first[s] == 1)
    def _():
        sf_ref[...] = s0f_ref[...]
        sb_ref[...] = s0b_ref[...]

    c = CHUNK
    dirs = ((qf_ref, zf_ref, vf_ref, 0, wf_ref, mf_ref, sf_ref, of_ref, c - 1,
             tuple(range(SCAN_CHUNKS))),
            (qb_ref, zb_ref, vb_ref, 1, wb_ref, mb_ref, sb_ref, ob_ref, 0,
             tuple(reversed(range(SCAN_CHUNKS)))))
    units = [(d, slice(sub * c, (sub + 1) * c)) for d in dirs for sub in d[-1]]
    gates = [_scan_gates(d[1][rows, :], lb_ref[d[3]:d[3] + 1, :], d[4]) for d, rows in units]
    intra = []
    for (d, rows), (k, ex) in zip(units, gates):
        for h in range(REC_HEADS):
            cols = slice(h * REC_DK, (h + 1) * REC_DK)
            intra.append(_scan_intra(d[0][rows, cols], k[:, cols], ex[:, cols], d[5]))
    idx = 0
    for (d, rows), (k, ex) in zip(units, gates):
        q_ref, _, v_ref, _, _, _, st_ref, o_ref, last, _ = d
        for h in range(REC_HEADS):
            cols = slice(h * REC_DK, (h + 1) * REC_DK)
            q, v = q_ref[rows, cols], v_ref[rows, cols]
            ex_b = ex[N_LEVELS * c:(N_LEVELS + 1) * c, cols]
            ex_l = ex[(N_LEVELS + 1) * c:(N_LEVELS + 2) * c, cols]
            st = st_ref[0, h]
            o_ref[rows, cols] = (_dot(intra[idx].astype(BF16), v.astype(BF16))
                                 + _dot_nt((q * ex_b).astype(BF16), st.astype(BF16)))
            st_ref[0, h] = (st * ex_b[last:last + 1, :]
                            + _dot(v.T.astype(BF16), (k[:, cols] * ex_l).astype(BF16)))
            idx += 1


def _scan(rec, lb, s0f_t, s0b_t, seq_lens):
    n_tok = rec.shape[0]
    fblk, bblk, first, seq = [], [], [], []
    base = 0
    step_rows = CHUNK * SCAN_CHUNKS
    for si, ln in enumerate(seq_lens):
        assert ln % step_rows == 0
        n = ln // step_rows
        for ci in range(n):
            fblk.append(base + ci)
            bblk.append(base + n - 1 - ci)
            first.append(1 if ci == 0 else 0)
            seq.append(si)
        base += n
    steps = len(fblk)
    as_i32 = lambda a: jnp.asarray(np.asarray(a, np.int32))
    w_f, w_b, m_f, m_b = _scan_constants()
    n_seq = len(seq_lens)

    def col(block_of, c):
        return pl.BlockSpec((step_rows, REC_WIDTH), lambda s, fb, bb, fi, sq: (block_of(fb, bb)[s], c))

    fwd = lambda fb, bb: fb
    bwd = lambda fb, bb: bb
    const2 = lambda s, fb, bb, fi, sq: (0, 0)
    const3 = lambda s, fb, bb, fi, sq: (0, 0, 0)
    state_spec = pl.BlockSpec((1, REC_HEADS, REC_DK, REC_DK), lambda s, fb, bb, fi, sq: (sq[s], 0, 0, 0))
    grid_spec = pltpu.PrefetchScalarGridSpec(
        num_scalar_prefetch=4,
        grid=(steps,),
        in_specs=[col(fwd, 0), col(fwd, 1), col(fwd, 3), col(bwd, 0), col(bwd, 2), col(bwd, 3),
                  pl.BlockSpec((2, REC_WIDTH), const2),
                  pl.BlockSpec(w_f.shape, const2), pl.BlockSpec(w_b.shape, const2),
                  pl.BlockSpec(m_f.shape, const3), pl.BlockSpec(m_b.shape, const3),
                  state_spec, state_spec],
        out_specs=(pl.BlockSpec((step_rows, REC_WIDTH), lambda s, fb, bb, fi, sq: (fb[s], 0)),
                   pl.BlockSpec((step_rows, REC_WIDTH), lambda s, fb, bb, fi, sq: (bb[s], 0)),
                   state_spec, state_spec),
    )
    st_shape = jax.ShapeDtypeStruct((n_seq, REC_HEADS, REC_DK, REC_DK), F32)
    return pl.pallas_call(
        _scan_kernel,
        out_shape=(jax.ShapeDtypeStruct((n_tok, REC_WIDTH), F32),
                   jax.ShapeDtypeStruct((n_tok, REC_WIDTH), F32), st_shape, st_shape),
        grid_spec=grid_spec,
        compiler_params=pltpu.CompilerParams(dimension_semantics=("arbitrary",),
                                             vmem_limit_bytes=VMEM_LIMIT),
        name="hgrn2_scan",
    )(as_i32(fblk), as_i32(bblk), as_i32(first), as_i32(seq),
      rec, rec, rec, rec, rec, rec, lb, w_f, w_b, m_f, m_b, s0f_t, s0b_t)


def _post_kernel(n_prompt_tiles, alpha, of_ref, ob_ref, g_ref, conv_ref, xp_ref, xs_ref, mod_ref,
                 wout_ref, rg_ref, l1g_ref, l1b_ref, rwh_ref, rwl_ref, rb_ref, tri_ref,
                 x1_ref, h2_ref, rank_t_ref, slot_t_ref, w4_ref, cnt_ref, start_ref, run_ref):
    i = pl.program_id(0)

    @pl.when(i == 0)
    def _():
        run_ref[...] = jnp.zeros_like(run_ref)

    x = jnp.where(i < n_prompt_tiles, xp_ref[...], xs_ref[...])
    mod = mod_ref[0]
    g1 = mod[:, 2 * D_MODEL:3 * D_MODEL]
    sh2 = mod[:, 3 * D_MODEL:4 * D_MODEL]
    sc2 = mod[:, 4 * D_MODEL:5 * D_MODEL]

    g = g_ref[...]
    silu_g = g * _sigmoid(g)
    mix = _dot(conv_ref[...], wout_ref[0:CONV_WIDTH, :])
    for h in range(REC_HEADS):
        cols = slice(h * REC_DK, (h + 1) * REC_DK)
        o = of_ref[:, cols] + ob_ref[:, cols]
        o = o * lax.rsqrt(jnp.mean(o * o, axis=-1, keepdims=True) + RMS_EPS)
        rec_out = (o * rg_ref[:, cols] * silu_g[:, cols]).astype(BF16)
        mix = mix + _dot(rec_out, wout_ref[CONV_WIDTH + h * REC_DK:CONV_WIDTH + (h + 1) * REC_DK, :])

    x1 = _layer_norm(alpha * x + g1 * mix) * l1g_ref[...] + l1b_ref[...]
    x1_ref[...] = x1
    h2 = _layer_norm(x1) * (1.0 + sc2) + sh2
    _store_row_tiles(h2_ref, h2)

    h_hi = h2.astype(BF16)
    h_lo = (h2 - h_hi.astype(F32)).astype(BF16)
    logits = (_dot(h_hi, rwh_ref[...]) + _dot(h_lo, rwh_ref[...]) + _dot(h_hi, rwl_ref[...])
              + rb_ref[...])
    lane = lax.broadcasted_iota(jnp.int32, logits.shape, 1)
    lane_f = lane.astype(F32)
    neg_inf = jnp.float32(-jnp.inf)
    work = jnp.where(lane < N_EXPERTS, logits, neg_inf)
    vals, sels = [], []
    for _ in range(TOP_K):
        m = jnp.max(work, axis=-1, keepdims=True)
        idx = jnp.min(jnp.where(work == m, lane_f, float(LANES)), axis=-1, keepdims=True)
        sel = lane_f == idx
        vals.append(m)
        sels.append(sel)
        work = jnp.where(sel, neg_inf, work)
    exps = [jnp.exp(v - vals[0]) for v in vals]
    denom = exps[0] + exps[1] + exps[2] + exps[3]

    sel_any = jnp.zeros(logits.shape, F32)
    for sel in sels:
        sel_any = jnp.where(sel, 1.0, sel_any)
    run_old = run_ref[...]
    prior = _dot(tri_ref[...], sel_any.astype(BF16)) + run_old
    run_new = run_old + jnp.sum(sel_any, axis=0, keepdims=True)
    run_ref[...] = run_new
    cnt_ref[...] = run_new
    start_ref[0] = run_old

    slot = jnp.zeros(logits.shape, F32)
    w4 = jnp.zeros(logits.shape, F32)
    for j in range(TOP_K):
        slot = jnp.where(sels[j], float(j), slot)
        w4 = jnp.where(lane == j, exps[j] / denom, w4)
    routed = sel_any > 0.0
    rank_t_ref[...] = jnp.where(routed, prior, -1.0).T[:N_EXPERTS]
    slot_t_ref[...] = jnp.where(routed, slot, -1.0).T[:N_EXPERTS]
    w4_ref[...] = w4


def _post(o_f, o_b, rec, conv, xp, xs, mod3, w_out_bf, rec_g, ln1_g, ln1_b, rw_hi, rw_lo, rb,
          alpha, sample_seq):
    n_p, n_s = xp.shape[0], xs.shape[0]
    n_tok = n_p + n_s
    npt = n_p // TOK_TILE
    tiles_per_sample_seq = sample_seq // TOK_TILE
    tri = jnp.asarray(np.tril(np.ones((TOK_TILE, TOK_TILE), np.float32), -1), BF16)

    def mod_row(i):
        return jnp.where(i < npt, 0, 1 + (i - npt) // tiles_per_sample_seq)

    const = lambda i: (0, 0)
    tok = lambda w: pl.BlockSpec((TOK_TILE, w), lambda i: (i, 0))
    return pl.pallas_call(
        functools.partial(_post_kernel, npt, alpha),
        out_shape=(jax.ShapeDtypeStruct((n_tok, D_MODEL), F32),
                   jax.ShapeDtypeStruct((n_tok * SUBLANES, LANES), F32),
                   jax.ShapeDtypeStruct((N_EXPERTS, n_tok), F32),
                   jax.ShapeDtypeStruct((N_EXPERTS, n_tok), F32),
                   jax.ShapeDtypeStruct((n_tok, LANES), F32),
                   jax.ShapeDtypeStruct((1, LANES), F32),
                   jax.ShapeDtypeStruct((n_tok // TOK_TILE, 1, LANES), F32)),
        grid=(n_tok // TOK_TILE,),
        in_specs=[tok(REC_WIDTH), tok(REC_WIDTH),
                  pl.BlockSpec((TOK_TILE, REC_WIDTH), lambda i: (i, 4)),
                  tok(CONV_WIDTH),
                  pl.BlockSpec((TOK_TILE, D_MODEL), lambda i: (jnp.minimum(i, npt - 1), 0)),
                  pl.BlockSpec((TOK_TILE, D_MODEL), lambda i: (jnp.maximum(i - npt, 0), 0)),
                  pl.BlockSpec((1, 1, 6 * D_MODEL), lambda i: (mod_row(i), 0, 0)),
                  pl.BlockSpec(w_out_bf.shape, const),
                  pl.BlockSpec((1, REC_WIDTH), const),
                  pl.BlockSpec((1, D_MODEL), const), pl.BlockSpec((1, D_MODEL), const),
                  pl.BlockSpec((D_MODEL, LANES), const), pl.BlockSpec((D_MODEL, LANES), const),
                  pl.BlockSpec((1, LANES), const),
                  pl.BlockSpec((TOK_TILE, TOK_TILE), const)],
        out_specs=(tok(D_MODEL), pl.BlockSpec((TOK_TILE * SUBLANES, LANES), lambda i: (i, 0)),
                   pl.BlockSpec((N_EXPERTS, TOK_TILE), lambda i: (0, i)),
                   pl.BlockSpec((N_EXPERTS, TOK_TILE), lambda i: (0, i)),
                   tok(LANES), pl.BlockSpec((1, LANES), const),
                   pl.BlockSpec((1, 1, LANES), lambda i: (i, 0, 0))),
        scratch_shapes=[pltpu.VMEM((1, LANES), F32)],
        compiler_params=pltpu.CompilerParams(dimension_semantics=("arbitrary",),
                                             vmem_limit_bytes=VMEM_LIMIT),
        name="post_mixer_router",
    )(o_f, o_b, rec, conv, xp, xs, mod3, w_out_bf, rec_g, ln1_g, ln1_b, rw_hi, rw_lo, rb, tri)


PLAN_ROWS = 32


def _plan_kernel(n_tok, tile_e, gstart, n_act, cum, starts_ref, ends_ref, rank_t_ref, slot_t_ref,
                 src_ref, dst_ref, acc_ref):
    i = pl.program_id(0)

    @pl.when(i < n_act[0])
    def _():
        e = tile_e[i]
        base = i * EXP_TILE - gstart[e]
        rows_iota = lax.broadcasted_iota(jnp.int32, (PLAN_ROWS, 1), 0)
        lane = lax.broadcasted_iota(jnp.int32, (1, TOK_TILE), 1)
        acc_ref[...] = jnp.zeros_like(acc_ref)

        base_f = base.astype(F32)
        win0 = jnp.sum((ends_ref[pl.ds(e, 1), :] <= base_f).astype(jnp.int32))
        win1 = jnp.sum((starts_ref[pl.ds(e, 1), :] < base_f + EXP_TILE).astype(jnp.int32))

        def window(b, carry):
            tok0 = pl.multiple_of(b * TOK_TILE, TOK_TILE)
            rk = rank_t_ref[pl.ds(e, 1), pl.ds(tok0, TOK_TILE)]
            sl = slot_t_ref[pl.ds(e, 1), pl.ds(tok0, TOK_TILE)]
            code = ((tok0 + lane) * TOP_K).astype(F32) + sl + 1.0
            lo = jnp.maximum(cum[b * N_EXPERTS + e] - base, 0)
            hi = jnp.minimum(cum[(b + 1) * N_EXPERTS + e] - base, EXP_TILE)

            def rows(k, c):
                r0 = pl.multiple_of(k * PLAN_ROWS, PLAN_ROWS)
                want = (base + r0 + rows_iota).astype(F32)
                hit = jnp.where(rk == want, code, 0.0)
                part = hit[:, 0:LANES]
                for c0 in range(LANES, TOK_TILE, LANES):
                    part = part + hit[:, c0:c0 + LANES]
                acc_ref[pl.ds(r0, PLAN_ROWS), :] += part
                return c

            shift = PLAN_ROWS.bit_length() - 1
            lax.fori_loop(lo >> shift, (hi + PLAN_ROWS - 1) >> shift, rows, 0)
            return carry

        lax.fori_loop(win0, win1, window, 0)
        code = jnp.sum(acc_ref[...].T, axis=0, keepdims=True).astype(jnp.int32) - 1
        pos = lax.broadcasted_iota(jnp.int32, (1, EXP_TILE), 1)
        tok = code >> 2
        src = jnp.where(code >= 0, tok, lax.rem(i * EXP_TILE + pos, n_tok))
        spare = TOP_K * n_tok + lax.rem(i, N_YBUF) * EXP_TILE + pos
        dst = jnp.where(code >= 0, (code & (TOP_K - 1)) * n_tok + tok, spare)
        src_ref[0] = src * SUBLANES
        dst_ref[0] = dst * SUBLANES

    @pl.when(i >= n_act[0])
    def _():
        src_ref[...] = jnp.zeros_like(src_ref)
        dst_ref[...] = jnp.zeros_like(dst_ref)


def _plan(rank_t, slot_t, tile_e, gstart, n_act, cum2d):
    n_tok = rank_t.shape[1]
    n_tiles = tile_e.shape[0]
    n_tok_tiles = cum2d.shape[0] - 1
    assert n_tok_tiles <= LANES
    huge = jnp.float32(2.0 ** 30)
    as_rows = lambda a: jnp.pad(a.T.astype(F32), ((0, 0), (0, LANES - n_tok_tiles)),
                                constant_values=huge)
    whole = lambda i, *_: (0, 0)
    per_tile = pl.BlockSpec((1, 1, EXP_TILE), lambda i, *_: (i, 0, 0))
    grid_spec = pltpu.PrefetchScalarGridSpec(
        num_scalar_prefetch=4,
        grid=(n_tiles,),
        in_specs=[pl.BlockSpec((N_EXPERTS, LANES), whole), pl.BlockSpec((N_EXPERTS, LANES), whole),
                  pl.BlockSpec(rank_t.shape, whole), pl.BlockSpec(slot_t.shape, whole)],
        out_specs=(per_tile, per_tile),
        scratch_shapes=[pltpu.VMEM((EXP_TILE, LANES), F32)],
    )
    shape = jax.ShapeDtypeStruct((n_tiles, 1, EXP_TILE), jnp.int32)
    return pl.pallas_call(
        functools.partial(_plan_kernel, n_tok),
        out_shape=(shape, shape),
        grid_spec=grid_spec,
        compiler_params=pltpu.CompilerParams(dimension_semantics=("arbitrary",),
                                             vmem_limit_bytes=VMEM_LIMIT),
        name="moe_plan",
    )(tile_e, gstart, n_act, cum2d.reshape(-1), as_rows(cum2d[:-1]), as_rows(cum2d[1:]),
      rank_t, slot_t)


N_YBUF = 3
DMA_UNROLL = 8


def _expert_kernel(tile_e, n_act, idx_ref, h_ref, wgu_ref, bgu_ref, wd_ref, bd_ref, yk_ref, xbuf,
                   ybuf, gsem, ssem, wgu_bf, wd_bf):
    i = pl.program_id(0)
    n_steps = pl.num_programs(0)
    n_active = n_act[0]
    tile_rows = EXP_TILE * SUBLANES
    n_slot_rows = yk_ref.shape[0] - N_YBUF * tile_rows

    def row_tile(ref, start):
        return ref.at[pl.ds(pl.multiple_of(start, SUBLANES), SUBLANES)]

    def for_each_row(fn):
        def body(g, carry):
            for k in range(DMA_UNROLL):
                fn(g * DMA_UNROLL + k)
            return carry
        lax.fori_loop(0, EXP_TILE // DMA_UNROLL, body, 0)

    def on_slot(slot, n, fn):
        if isinstance(slot, int):
            fn(slot)
            return
        for s in range(n):
            pl.when(slot == s)(functools.partial(fn, s))

    def gather_rows(which, slot):
        def run(s):
            def one(r):
                pltpu.make_async_copy(row_tile(h_ref, idx_ref[0, 0, which * EXP_TILE + r]),
                                      row_tile(xbuf.at[s], r * SUBLANES), gsem.at[s]).start()
            for_each_row(one)
        on_slot(slot, 2, run)

    def scatter_rows(slot):
        def run(s):
            def one(r):
                pltpu.make_async_copy(row_tile(ybuf.at[s], r * SUBLANES),
                                      row_tile(yk_ref, idx_ref[0, 0, 2 * EXP_TILE + r]),
                                      ssem.at[s]).start()
            for_each_row(one)
        on_slot(slot, N_YBUF, run)

    def scatter_wait(slot):
        pltpu.make_async_copy(ybuf.at[slot], yk_ref.at[pl.ds(0, tile_rows)], ssem.at[slot]).wait()

    @pl.when(i == 0)
    def _():
        ybuf[0] = jnp.zeros((tile_rows, LANES), F32)
        for spare in range(N_YBUF):
            cp = pltpu.make_async_copy(
                ybuf.at[0], yk_ref.at[pl.ds(n_slot_rows + spare * tile_rows, tile_rows)],
                ssem.at[0])
            cp.start()
            cp.wait()
        gather_rows(0, 0)

    @pl.when(i + 1 < n_active)
    def _():
        gather_rows(1, (i + 1) % 2)

    prev = tile_e[jnp.maximum(i - 1, 0)]
    new_expert = jnp.logical_or(i == 0, tile_e[i] != prev)

    @pl.when(jnp.logical_and(new_expert, i < n_active))
    def _():
        rows = 128

        def cast(r, carry):
            sl = pl.ds(pl.multiple_of(r * rows, rows), rows)
            wgu_bf[sl, :] = wgu_ref[0, sl, :].astype(BF16)
            wd_bf[sl, :] = wd_ref[0, sl, :].astype(BF16)
            return carry

        lax.fori_loop(0, D_MODEL // rows, cast, 0)

    @pl.when(i < n_active)
    def _():
        xslot = i % 2
        yslot = i % N_YBUF
        pltpu.make_async_copy(h_ref.at[pl.ds(0, tile_rows)], xbuf.at[xslot], gsem.at[xslot]).wait()

        @pl.when(i >= N_YBUF)
        def _():
            scatter_wait(yslot)

        x = _load_row_tiles(xbuf.at[xslot], EXP_TILE).astype(BF16)
        gu = _dot(x, wgu_bf[...]) + bgu_ref[0]
        gate = jnp.minimum(gu[:, :D_FF], SWIGLU_LIMIT)
        up = jnp.clip(gu[:, D_FF:], -SWIGLU_LIMIT, SWIGLU_LIMIT)
        act = (up + 1.0) * gate * _sigmoid(SWIGLU_ALPHA * gate)
        _store_row_tiles(ybuf.at[yslot], _dot(act.astype(BF16), wd_bf[...]) + bd_ref[0])
        scatter_rows(yslot)

    @pl.when(i == n_steps - 1)
    def _():
        for back in range(1, N_YBUF + 1):
            scatter_wait((n_active - back) % N_YBUF)


def _experts(h2, src3, dst3, tile_e, n_act, w_gate_up, b_gate_up, w_down, b_down):
    n_tok = h2.shape[0] // SUBLANES
    n_tiles = src3.shape[0]
    src_next = jnp.concatenate([src3[1:], src3[-1:]], axis=0)
    idx3 = jnp.concatenate([src3, src_next, dst3], axis=-1)
    grid_spec = pltpu.PrefetchScalarGridSpec(
        num_scalar_prefetch=2,
        grid=(n_tiles,),
        in_specs=[pl.BlockSpec((1, 1, 3 * EXP_TILE), lambda i, te, na: (i, 0, 0),
                               memory_space=pltpu.SMEM),
                  pl.BlockSpec(memory_space=pl.ANY),
                  pl.BlockSpec((1, D_MODEL, 2 * D_FF), lambda i, te, na: (te[i], 0, 0)),
                  pl.BlockSpec((1, 1, 2 * D_FF), lambda i, te, na: (te[i], 0, 0)),
                  pl.BlockSpec((1, D_FF, D_MODEL), lambda i, te, na: (te[i], 0, 0)),
                  pl.BlockSpec((1, 1, D_MODEL), lambda i, te, na: (te[i], 0, 0))],
        out_specs=pl.BlockSpec(memory_space=pl.ANY),
        scratch_shapes=[pltpu.VMEM((2, EXP_TILE * SUBLANES, LANES), F32),
                        pltpu.VMEM((N_YBUF, EXP_TILE * SUBLANES, LANES), F32),
                        pltpu.SemaphoreType.DMA((2,)), pltpu.SemaphoreType.DMA((N_YBUF,)),
                        pltpu.VMEM((D_MODEL, 2 * D_FF), BF16), pltpu.VMEM((D_FF, D_MODEL), BF16)],
    )
    return pl.pallas_call(
        _expert_kernel,
        out_shape=jax.ShapeDtypeStruct(((TOP_K * n_tok + N_YBUF * EXP_TILE) * SUBLANES, LANES), F32),
        grid_spec=grid_spec,
        compiler_params=pltpu.CompilerParams(dimension_semantics=("arbitrary",),
                                             vmem_limit_bytes=VMEM_LIMIT),
        name="moe_experts",
    )(tile_e, n_act, idx3, h2, w_gate_up, b_gate_up.reshape(N_EXPERTS, 1, 2 * D_FF),
      w_down, b_down.reshape(N_EXPERTS, 1, D_MODEL))


COMBINE_TILE = 512


def _combine_kernel(n_prompt_tiles, alpha, y0_ref, y1_ref, y2_ref, y3_ref, x1_ref, w4_ref, mod_ref,
                    l2g_ref, l2b_ref, yp_ref, ysm_ref):
    i = pl.program_id(0)
    w4 = w4_ref[...]
    ff = _load_row_tiles(y0_ref, COMBINE_TILE) * w4[:, 0:1]
    for j, y_ref in enumerate((y1_ref, y2_ref, y3_ref), start=1):
        ff = ff + _load_row_tiles(y_ref, COMBINE_TILE) * w4[:, j:j + 1]
    g2 = mod_ref[0][:, 5 * D_MODEL:6 * D_MODEL]
    out = _layer_norm(alpha * x1_ref[...] + g2 * ff) * l2g_ref[...] + l2b_ref[...]

    @pl.when(i < n_prompt_tiles)
    def _():
        yp_ref[...] = out

    @pl.when(i >= n_prompt_tiles)
    def _():
        ysm_ref[...] = out


def _combine(yk, x1, w4, mod3, ln2_g, ln2_b, n_p, alpha, sample_seq):
    n_tok = x1.shape[0]
    n_s = n_tok - n_p
    assert n_p % COMBINE_TILE == 0 and sample_seq % COMBINE_TILE == 0
    npt = n_p // COMBINE_TILE
    n_tok_tiles = n_tok // COMBINE_TILE
    tiles_per_sample_seq = sample_seq // COMBINE_TILE

    def mod_row(i):
        return jnp.where(i < npt, 0, 1 + (i - npt) // tiles_per_sample_seq)

    def slot(j):
        return pl.BlockSpec((COMBINE_TILE * SUBLANES, LANES), lambda i: (j * n_tok_tiles + i, 0))

    const = lambda i: (0, 0)
    return pl.pallas_call(
        functools.partial(_combine_kernel, npt, alpha),
        out_shape=(jax.ShapeDtypeStruct((n_p, D_MODEL), F32),
                   jax.ShapeDtypeStruct((n_s, D_MODEL), F32)),
        grid=(n_tok_tiles,),
        in_specs=[slot(0), slot(1), slot(2), slot(3),
                  pl.BlockSpec((COMBINE_TILE, D_MODEL), lambda i: (i, 0)),
                  pl.BlockSpec((COMBINE_TILE, LANES), lambda i: (i, 0)),
                  pl.BlockSpec((1, 1, 6 * D_MODEL), lambda i: (mod_row(i), 0, 0)),
                  pl.BlockSpec((1, D_MODEL), const), pl.BlockSpec((1, D_MODEL), const)],
        out_specs=(pl.BlockSpec((COMBINE_TILE, D_MODEL), lambda i: (jnp.minimum(i, npt - 1), 0)),
                   pl.BlockSpec((COMBINE_TILE, D_MODEL), lambda i: (jnp.maximum(i - npt, 0), 0))),
        compiler_params=pltpu.CompilerParams(dimension_semantics=("arbitrary",),
                                             vmem_limit_bytes=VMEM_LIMIT),
        name="moe_combine",
    )(yk, yk, yk, yk, x1, w4, mod3, ln2_g, ln2_b)


def kernel(x_prompt, x_sample, c, state_fwd, state_bwd, c_ctx, w_ada, b_ada, w_in, conv_w, conv_b,
           conv_ln_g, conv_ln_b, lb_logits, rec_norm_g, w_out, ln1_g, ln1_b, ln2_g, ln2_b,
           router_w, router_b, w_gate_up, b_gate_up, w_down, b_down):
    depth = w_ada.shape[0]
    assert depth == 1
    alpha = (2.0 * depth) ** 0.25
    nb_p, seq_p, _ = x_prompt.shape
    nb_s, seq_s, _ = x_sample.shape
    n_p, n_s = nb_p * seq_p, nb_s * seq_s
    n_tok = n_p + n_s
    row2 = lambda a: a.reshape(1, -1)

    lb = jnp.cumsum(jax.nn.softmax(lb_logits.astype(F32), axis=0), axis=0)[0]

    cond = jnp.concatenate([c_ctx[None, :], c, jnp.zeros((16 - 1 - nb_s, D_MODEL), F32)], axis=0)
    mod = _ada(cond, w_ada[0], b_ada[0])
    mod3 = mod.reshape(16, 1, 6 * D_MODEL)

    xp = x_prompt.reshape(n_p, D_MODEL)
    xs = x_sample.reshape(n_s, D_MODEL)
    conv_w_pad = jnp.concatenate([conv_w[0], jnp.zeros((1, CONV_WIDTH), F32)], axis=0)
    conv_out, rec = _inproj(xp, xs, mod3, w_in[0].astype(BF16), conv_w_pad, row2(conv_b[0]),
                            row2(conv_ln_g[0]), row2(conv_ln_b[0]), seq_p, seq_s)

    zeros_state = jnp.zeros((nb_p, REC_HEADS, REC_DK, REC_DK), F32)
    s0f = jnp.concatenate([zeros_state, jnp.swapaxes(state_fwd[:, 0], -1, -2)], axis=0)
    s0b = jnp.concatenate([zeros_state, jnp.swapaxes(state_bwd[:, 0], -1, -2)], axis=0)
    o_f, o_b, sf_t, sb_t = _scan(rec, lb, s0f, s0b, [seq_p] * nb_p + [seq_s] * nb_s)
    new_f = jnp.swapaxes(sf_t[:nb_p], -1, -2)[:, None]
    new_b = jnp.swapaxes(sb_t[:nb_p], -1, -2)[:, None]

    rw = jnp.pad(router_w[0], ((0, 0), (0, LANES - N_EXPERTS)))
    rw_hi = rw.astype(BF16)
    rw_lo = (rw - rw_hi.astype(F32)).astype(BF16)
    rb = jnp.pad(router_b[0], (0, LANES - N_EXPERTS)).reshape(1, LANES)
    x1, h2, rank_t, slot_t, w4, counts, tile_start = _post(
        o_f, o_b, rec, conv_out, xp, xs, mod3, w_out[0].astype(BF16), row2(rec_norm_g[0]),
        row2(ln1_g[0]), row2(ln1_b[0]), rw_hi, rw_lo, rb, alpha, seq_s)

    cnt = counts[0, :N_EXPERTS].astype(jnp.int32)
    gpad = ((cnt + EXP_TILE - 1) // EXP_TILE) * EXP_TILE
    gend = jnp.cumsum(gpad)
    gstart = gend - gpad
    n_rows = n_tok * TOP_K + N_EXPERTS * EXP_TILE
    n_tiles = n_rows // EXP_TILE
    n_act = gend[-1] // EXP_TILE
    tile_ids = jnp.minimum(jnp.arange(n_tiles, dtype=jnp.int32), n_act - 1)
    tile_e = jnp.sum((gend[None, :] <= tile_ids[:, None] * EXP_TILE).astype(jnp.int32), axis=1)
    n_act1 = n_act.reshape(1).astype(jnp.int32)
    starts = tile_start[:, 0, :N_EXPERTS].astype(jnp.int32)
    cum2d = jnp.concatenate([starts, cnt[None, :]], axis=0)
    src, dst = _plan(rank_t, slot_t, tile_e, gstart.astype(jnp.int32), n_act1, cum2d)
    yk = _experts(h2, src, dst, tile_e, n_act1, w_gate_up[0], b_gate_up[0], w_down[0], b_down[0])
    y_p, y_s = _combine(yk, x1, w4, mod3, row2(ln2_g[0]), row2(ln2_b[0]), n_p, alpha, seq_s)
    return (y_p.reshape(nb_p, seq_p, D_MODEL), y_s.reshape(nb_s, seq_s, D_MODEL), new_f, new_b)
```

```python
import functools

import numpy as np
import jax
import jax.numpy as jnp
from jax import lax
from jax.experimental import pallas as pl
from jax.experimental.pallas import tpu as pltpu

F32 = jnp.float32
BF16 = jnp.bfloat16

D_MODEL = 1024
CONV_WIDTH = 512
CONV_TAPS = 31
REC_HEADS = 4
REC_DK = 128
REC_WIDTH = REC_HEADS * REC_DK
REC_COLS = 5 * REC_WIDTH
CHUNK = 64
N_LEVELS = 6
SCAN_CHUNKS = 4
N_EXPERTS = 32
TOP_K = 4
D_FF = 1024
SWIGLU_LIMIT = 7.0
SWIGLU_ALPHA = 1.702
LN_EPS = 1e-5
RMS_EPS = 1e-6
GRID_W = 64

LANES = 128
TOK_TILE = 512
EXP_TILE = 512
VMEM_LIMIT = 56 * 1024 * 1024


def _sigmoid(x):
    return 1.0 / (1.0 + jnp.exp(-x))


def _layer_norm(x):
    mu = jnp.mean(x, axis=-1, keepdims=True)
    xc = x - mu
    var = jnp.mean(xc * xc, axis=-1, keepdims=True)
    return xc * lax.rsqrt(var + LN_EPS)


def _dot(a, b):
    return jnp.dot(a, b, preferred_element_type=F32)


def _dot_nt(a, b):
    return lax.dot_general(a, b, (((1,), (1,)), ((), ())), preferred_element_type=F32)


SUBLANES = 8
assert D_MODEL == SUBLANES * LANES


def _store_row_tiles(ref, value):
    for s in range(SUBLANES):
        ref[pl.ds(s, value.shape[0], stride=SUBLANES), :] = value[:, s * LANES:(s + 1) * LANES]


def _load_row_tiles(ref, n_rows):
    return jnp.concatenate(
        [ref[pl.ds(s, n_rows, stride=SUBLANES), :] for s in range(SUBLANES)], axis=-1)


def _split3(x):
    hi = x.astype(BF16)
    r1 = x - hi.astype(F32)
    mid = r1.astype(BF16)
    lo = (r1 - mid.astype(F32)).astype(BF16)
    return hi, mid, lo


def _ada_kernel(c_ref, w_ref, b_ref, o_ref):
    c = c_ref[...]
    s = (c * _sigmoid(c)).astype(BF16)
    o_ref[...] = _dot(s, w_ref[...].astype(BF16)) + b_ref[...]


def _ada(cond, w_ada, b_ada):
    rows = cond.shape[0]
    ncol = w_ada.shape[1]
    tn = 1024
    return pl.pallas_call(
        _ada_kernel,
        out_shape=jax.ShapeDtypeStruct((rows, ncol), F32),
        grid=(ncol // tn,),
        in_specs=[pl.BlockSpec((rows, D_MODEL), lambda j: (0, 0)),
                  pl.BlockSpec((D_MODEL, tn), lambda j: (0, j)),
                  pl.BlockSpec((1, tn), lambda j: (0, j))],
        out_specs=pl.BlockSpec((rows, tn), lambda j: (0, j)),
        compiler_params=pltpu.CompilerParams(dimension_semantics=("parallel",),
                                             vmem_limit_bytes=VMEM_LIMIT),
        name="ada_mod",
    )(cond, w_ada, b_ada.reshape(1, ncol))


def _inproj_kernel(n_prompt_tiles, prompt_seg, xp_ref, xs_ref, mod_ref, win_ref, cw_ref, cb_ref,
                   cg_ref, cbeta_ref, conv_ref, rec_ref):
    i = pl.program_id(0)
    is_prompt = i < n_prompt_tiles
    x = jnp.where(is_prompt, xp_ref[...], xs_ref[...])
    mod = mod_ref[0]
    sh1 = mod[:, 0:D_MODEL]
    sc1 = mod[:, D_MODEL:2 * D_MODEL]
    h = (_layer_norm(x) * (1.0 + sc1) + sh1).astype(BF16)

    glu = _dot(h, win_ref[:, 0:2 * CONV_WIDTH])
    u = glu[:, :CONV_WIDTH] * _sigmoid(glu[:, CONV_WIDTH:])

    seg_len = jnp.where(is_prompt, prompt_seg, GRID_W)
    half = CONV_TAPS // 2
    pos = lax.broadcasted_iota(jnp.int32, (TOK_TILE, 1), 0) & (seg_len - 1)

    def masked_shift(d):
        sh = pltpu.roll(u, (-d) % TOK_TILE, axis=0)
        edge_rows = SUBLANES * ((abs(d) + SUBLANES - 1) // SUBLANES)
        pieces = []
        for s0 in range(0, TOK_TILE, GRID_W):
            if d < 0:
                edge = slice(s0, s0 + edge_rows)
                pieces.append(jnp.where(pos[edge] >= -d, sh[edge], 0.0))
                pieces.append(sh[s0 + edge_rows:s0 + GRID_W])
            else:
                edge = slice(s0 + GRID_W - edge_rows, s0 + GRID_W)
                pieces.append(sh[s0:s0 + GRID_W - edge_rows])
                pieces.append(jnp.where(pos[edge] < seg_len - d, sh[edge], 0.0))
        return jnp.concatenate(pieces, axis=0)

    acc = u * cw_ref[half:half + 1, :] + cb_ref[...]
    taps = [j for j in range(CONV_TAPS) if j != half]
    n_groups = REC_COLS // REC_WIDTH
    per_group = len(taps) // n_groups
    anchor = None
    for g in range(n_groups):
        cols = slice(g * REC_WIDTH, (g + 1) * REC_WIDTH)
        rec = _dot(h, win_ref[:, 2 * CONV_WIDTH + g * REC_WIDTH:
                              2 * CONV_WIDTH + (g + 1) * REC_WIDTH])
        rec_ref[:, cols] = rec
        for j in taps[g * per_group:(g + 1) * per_group]:
            w_tap = cw_ref[j:j + 1, :]
            if anchor is not None:
                w_tap = w_tap + anchor
                anchor = None
            acc = acc + masked_shift(j - half) * w_tap
        if g + 1 < n_groups:
            bits = pltpu.bitcast(rec[0:SUBLANES, :], jnp.uint32)
            anchor = pltpu.bitcast((bits >> 16) >> 16, F32)[0:1, :]
    y = _layer_norm(acc) * cg_ref[...] + cbeta_ref[...]
    conv_ref[...] = (y * _sigmoid(y)).astype(BF16)


def _inproj(xp, xs, mod3, w_in_bf, conv_w, conv_b, conv_g, conv_beta, prompt_seq, sample_seq):
    n_p, n_s = xp.shape[0], xs.shape[0]
    n_tok = n_p + n_s
    npt = n_p // TOK_TILE
    tiles_per_sample_seq = sample_seq // TOK_TILE
    assert TOK_TILE % prompt_seq == 0 and n_p % TOK_TILE == 0 and sample_seq % TOK_TILE == 0

    def mod_row(i):
        return jnp.where(i < npt, 0, 1 + (i - npt) // tiles_per_sample_seq)

    const = lambda i: (0, 0)
    return pl.pallas_call(
        functools.partial(_inproj_kernel, npt, prompt_seq),
        out_shape=(jax.ShapeDtypeStruct((n_tok, CONV_WIDTH), BF16),
                   jax.ShapeDtypeStruct((n_tok, REC_COLS), F32)),
        grid=(n_tok // TOK_TILE,),
        in_specs=[pl.BlockSpec((TOK_TILE, D_MODEL), lambda i: (jnp.minimum(i, npt - 1), 0)),
                  pl.BlockSpec((TOK_TILE, D_MODEL), lambda i: (jnp.maximum(i - npt, 0), 0)),
                  pl.BlockSpec((1, 1, 6 * D_MODEL), lambda i: (mod_row(i), 0, 0)),
                  pl.BlockSpec(w_in_bf.shape, const),
                  pl.BlockSpec(conv_w.shape, const),
                  pl.BlockSpec((1, CONV_WIDTH), const),
                  pl.BlockSpec((1, CONV_WIDTH), const),
                  pl.BlockSpec((1, CONV_WIDTH), const)],
        out_specs=(pl.BlockSpec((TOK_TILE, CONV_WIDTH), lambda i: (i, 0)),
                   pl.BlockSpec((TOK_TILE, REC_COLS), lambda i: (i, 0))),
        compiler_params=pltpu.CompilerParams(dimension_semantics=("parallel",),
                                             vmem_limit_bytes=VMEM_LIMIT),
        name="inproj_conv",
    )(xp, xs, mod3, w_in_bf, conv_w, conv_b, conv_g, conv_beta)


def _scan_constants():
    c = CHUNK
    w = np.zeros((N_LEVELS + 2, c, c), np.float32)
    m = np.zeros((N_LEVELS + 1, c, c), np.float32)
    t = np.arange(c)
    for j in range(N_LEVELS):
        half = 1 << j
        for ti in range(c):
            ref = (ti & ~(2 * half - 1)) + half - 1
            if ti & half:
                w[j, ti, ref + 1:ti + 1] = 1.0
            else:
                w[j, ti, ti + 1:ref + 1] = 1.0
        upper = (t[:, None] & half) != 0
        lower = (t[None, :] & half) == 0
        same = (t[:, None] >> (j + 1)) == (t[None, :] >> (j + 1))
        m[j] = (upper & lower & same).astype(np.float32)
    w[N_LEVELS] = (t[None, :] <= t[:, None]).astype(np.float32)
    w[N_LEVELS + 1] = (t[None, :] > t[:, None]).astype(np.float32)
    m[N_LEVELS] = np.eye(c, dtype=np.float32)
    w_f = np.tile(w.reshape(-1, c), (1, 3))
    w_b = np.tile(w[:, ::-1, ::-1].reshape(-1, c), (1, 3))
    m_b = m[:, ::-1, ::-1]
    return (jnp.asarray(w_f, BF16), jnp.asarray(np.ascontiguousarray(w_b), BF16),
            jnp.asarray(m), jnp.asarray(np.ascontiguousarray(m_b)))


def _scan_gates(z, lb, w_ref):
    e = jnp.exp(-jnp.abs(z))
    r = 1.0 / (1.0 + e)
    a = e * r
    sig = jnp.where(z >= 0, r, a)
    sig_neg = jnp.where(z >= 0, a, r)
    one_m_lb = 1.0 - lb
    log_f = jnp.log2(lb + one_m_lb * sig)
    k = one_m_lb * sig_neg
    terms = jnp.concatenate(_split3(log_f), axis=0)
    ex = jnp.exp2(_dot(w_ref[...], terms))
    return k, ex


def _scan_intra(q, k, ex, m_ref):
    c = CHUNK
    qb, kb = q.astype(BF16), k.astype(BF16)
    a_mat = m_ref[N_LEVELS] * _dot_nt(qb, kb)
    for j in range(N_LEVELS):
        xj = ex[j * c:(j + 1) * c].astype(BF16)
        a_mat = a_mat + m_ref[j] * _dot_nt(qb * xj, kb * xj)
    return a_mat


def _scan_kernel(fblk, bblk, first, seq, qf_ref, zf_ref, vf_ref, qb_ref, zb_ref, vb_ref, lb_ref,
                 wf_ref, wb_ref, mf_ref, mb_ref, s0f_ref, s0b_ref, of_ref, ob_ref, sf_ref, sb_ref):
    s = pl.program_id(0)

    @pl.when(first[s] == 1)
    def _():
        sf_ref[...] = s0f_ref[...]
        sb_ref[...] = s0b_ref[...]

    c = CHUNK
    dirs = ((qf_ref, zf_ref, vf_ref, 0, wf_ref, mf_ref, sf_ref, of_ref, c - 1,
             tuple(range(SCAN_CHUNKS))),
            (qb_ref, zb_ref, vb_ref, 1, wb_ref, mb_ref, sb_ref, ob_ref, 0,
             tuple(reversed(range(SCAN_CHUNKS)))))
    units = [(d, slice(sub * c, (sub + 1) * c)) for d in dirs for sub in d[-1]]
    gates = [_scan_gates(d[1][rows, :], lb_ref[d[3]:d[3] + 1, :], d[4]) for d, rows in units]
    intra = []
    for (d, rows), (k, ex) in zip(units, gates):
        for h in range(REC_HEADS):
            cols = slice(h * REC_DK, (h + 1) * REC_DK)
            intra.append(_scan_intra(d[0][rows, cols], k[:, cols], ex[:, cols], d[5]))
    idx = 0
    for (d, rows), (k, ex) in zip(units, gates):
        q_ref, _, v_ref, _, _, _, st_ref, o_ref, last, _ = d
        for h in range(REC_HEADS):
            cols = slice(h * REC_DK, (h + 1) * REC_DK)
            q, v = q_ref[rows, cols], v_ref[rows, cols]
            ex_b = ex[N_LEVELS * c:(N_LEVELS + 1) * c, cols]
            ex_l = ex[(N_LEVELS + 1) * c:(N_LEVELS + 2) * c, cols]
            st = st_ref[0, h]
            o_ref[rows, cols] = (_dot(intra[idx].astype(BF16), v.astype(BF16))
                                 + _dot_nt((q * ex_b).astype(BF16), st.astype(BF16)))
            st_ref[0, h] = (st * ex_b[last:last + 1, :]
                            + _dot(v.T.astype(BF16), (k[:, cols] * ex_l).astype(BF16)))
            idx += 1


def _scan(rec, lb, s0f_t, s0b_t, seq_lens):
    n_tok = rec.shape[0]
    fblk, bblk, first, seq = [], [], [], []
    base = 0
    step_rows = CHUNK * SCAN_CHUNKS
    for si, ln in enumerate(seq_lens):
        assert ln % step_rows == 0
        n = ln // step_rows
        for ci in range(n):
            fblk.append(base + ci)
            bblk.append(base + n - 1 - ci)
            first.append(1 if ci == 0 else 0)
            seq.append(si)
        base += n
    steps = len(fblk)
    as_i32 = lambda a: jnp.asarray(np.asarray(a, np.int32))
    w_f, w_b, m_f, m_b = _scan_constants()
    n_seq = len(seq_lens)

    def col(block_of, c):
        return pl.BlockSpec((step_rows, REC_WIDTH), lambda s, fb, bb, fi, sq: (block_of(fb, bb)[s], c))

    fwd = lambda fb, bb: fb
    bwd = lambda fb, bb: bb
    const2 = lambda s, fb, bb, fi, sq: (0, 0)
    const3 = lambda s, fb, bb, fi, sq: (0, 0, 0)
    state_spec = pl.BlockSpec((1, REC_HEADS, REC_DK, REC_DK), lambda s, fb, bb, fi, sq: (sq[s], 0, 0, 0))
    grid_spec = pltpu.PrefetchScalarGridSpec(
        num_scalar_prefetch=4,
        grid=(steps,),
        in_specs=[col(fwd, 0), col(fwd, 1), col(fwd, 3), col(bwd, 0), col(bwd, 2), col(bwd, 3),
                  pl.BlockSpec((2, REC_WIDTH), const2),
                  pl.BlockSpec(w_f.shape, const2), pl.BlockSpec(w_b.shape, const2),
                  pl.BlockSpec(m_f.shape, const3), pl.BlockSpec(m_b.shape, const3),
                  state_spec, state_spec],
        out_specs=(pl.BlockSpec((step_rows, REC_WIDTH), lambda s, fb, bb, fi, sq: (fb[s], 0)),
                   pl.BlockSpec((step_rows, REC_WIDTH), lambda s, fb, bb, fi, sq: (bb[s], 0)),
                   state_spec, state_spec),
    )
    st_shape = jax.ShapeDtypeStruct((n_seq, REC_HEADS, REC_DK, REC_DK), F32)
    return pl.pallas_call(
        _scan_kernel,
        out_shape=(jax.ShapeDtypeStruct((n_tok, REC_WIDTH), F32),
                   jax.ShapeDtypeStruct((n_tok, REC_WIDTH), F32), st_shape, st_shape),
        grid_spec=grid_spec,
        compiler_params=pltpu.CompilerParams(dimension_semantics=("arbitrary",),
                                             vmem_limit_bytes=VMEM_LIMIT),
        name="hgrn2_scan",
    )(as_i32(fblk), as_i32(bblk), as_i32(first), as_i32(seq),
      rec, rec, rec, rec, rec, rec, lb, w_f, w_b, m_f, m_b, s0f_t, s0b_t)


def _post_kernel(n_prompt_tiles, alpha, of_ref, ob_ref, g_ref, conv_ref, xp_ref, xs_ref, mod_ref,
                 wout_ref, rg_ref, l1g_ref, l1b_ref, rwh_ref, rwl_ref, rb_ref, tri_ref,
                 x1_ref, h2_ref, rank_t_ref, slot_t_ref, w4_ref, cnt_ref, start_ref, run_ref):
    i = pl.program_id(0)

    @pl.when(i == 0)
    def _():
        run_ref[...] = jnp.zeros_like(run_ref)

    x = jnp.where(i < n_prompt_tiles, xp_ref[...], xs_ref[...])
    mod = mod_ref[0]
    g1 = mod[:, 2 * D_MODEL:3 * D_MODEL]
    sh2 = mod[:, 3 * D_MODEL:4 * D_MODEL]
    sc2 = mod[:, 4 * D_MODEL:5 * D_MODEL]

    g = g_ref[...]
    silu_g = g * _sigmoid(g)
    mix = _dot(conv_ref[...], wout_ref[0:CONV_WIDTH, :])
    for h in range(REC_HEADS):
        cols = slice(h * REC_DK, (h + 1) * REC_DK)
        o = of_ref[:, cols] + ob_ref[:, cols]
        o = o * lax.rsqrt(jnp.mean(o * o, axis=-1, keepdims=True) + RMS_EPS)
        rec_out = (o * rg_ref[:, cols] * silu_g[:, cols]).astype(BF16)
        mix = mix + _dot(rec_out, wout_ref[CONV_WIDTH + h * REC_DK:CONV_WIDTH + (h + 1) * REC_DK, :])

    x1 = _layer_norm(alpha * x + g1 * mix) * l1g_ref[...] + l1b_ref[...]
    x1_ref[...] = x1
    h2 = _layer_norm(x1) * (1.0 + sc2) + sh2
    _store_row_tiles(h2_ref, h2)

    h_hi = h2.astype(BF16)
    h_lo = (h2 - h_hi.astype(F32)).astype(BF16)
    logits = (_dot(h_hi, rwh_ref[...]) + _dot(h_lo, rwh_ref[...]) + _dot(h_hi, rwl_ref[...])
              + rb_ref[...])
    lane = lax.broadcasted_iota(jnp.int32, logits.shape, 1)
    lane_f = lane.astype(F32)
    neg_inf = jnp.float32(-jnp.inf)
    work = jnp.where(lane < N_EXPERTS, logits, neg_inf)
    vals, sels = [], []
    for _ in range(TOP_K):
        m = jnp.max(work, axis=-1, keepdims=True)
        idx = jnp.min(jnp.where(work == m, lane_f, float(LANES)), axis=-1, keepdims=True)
        sel = lane_f == idx
        vals.append(m)
        sels.append(sel)
        work = jnp.where(sel, neg_inf, work)
    exps = [jnp.exp(v - vals[0]) for v in vals]
    denom = exps[0] + exps[1] + exps[2] + exps[3]

    sel_any = jnp.zeros(logits.shape, F32)
    for sel in sels:
        sel_any = jnp.where(sel, 1.0, sel_any)
    run_old = run_ref[...]
    prior = _dot(tri_ref[...], sel_any.astype(BF16)) + run_old
    run_new = run_old + jnp.sum(sel_any, axis=0, keepdims=True)
    run_ref[...] = run_new
    cnt_ref[...] = run_new
    start_ref[0] = run_old

    slot = jnp.zeros(logits.shape, F32)
    w4 = jnp.zeros(logits.shape, F32)
    for j in range(TOP_K):
        slot = jnp.where(sels[j], float(j), slot)
        w4 = jnp.where(lane == j, exps[j] / denom, w4)
    routed = sel_any > 0.0
    rank_t_ref[...] = jnp.where(routed, prior, -1.0).T[:N_EXPERTS]
    slot_t_ref[...] = jnp.where(routed, slot, -1.0).T[:N_EXPERTS]
    w4_ref[...] = w4


def _post(o_f, o_b, rec, conv, xp, xs, mod3, w_out_bf, rec_g, ln1_g, ln1_b, rw_hi, rw_lo, rb,
          alpha, sample_seq):
    n_p, n_s = xp.shape[0], xs.shape[0]
    n_tok = n_p + n_s
    npt = n_p // TOK_TILE
    tiles_per_sample_seq = sample_seq // TOK_TILE
    tri = jnp.asarray(np.tril(np.ones((TOK_TILE, TOK_TILE), np.float32), -1), BF16)

    def mod_row(i):
        return jnp.where(i < npt, 0, 1 + (i - npt) // tiles_per_sample_seq)

    const = lambda i: (0, 0)
    tok = lambda w: pl.BlockSpec((TOK_TILE, w), lambda i: (i, 0))
    return pl.pallas_call(
        functools.partial(_post_kernel, npt, alpha),
        out_shape=(jax.ShapeDtypeStruct((n_tok, D_MODEL), F32),
                   jax.ShapeDtypeStruct((n_tok * SUBLANES, LANES), F32),
                   jax.ShapeDtypeStruct((N_EXPERTS, n_tok), F32),
                   jax.ShapeDtypeStruct((N_EXPERTS, n_tok), F32),
                   jax.ShapeDtypeStruct((n_tok, LANES), F32),
                   jax.ShapeDtypeStruct((1, LANES), F32),
                   jax.ShapeDtypeStruct((n_tok // TOK_TILE, 1, LANES), F32)),
        grid=(n_tok // TOK_TILE,),
        in_specs=[tok(REC_WIDTH), tok(REC_WIDTH),
                  pl.BlockSpec((TOK_TILE, REC_WIDTH), lambda i: (i, 4)),
                  tok(CONV_WIDTH),
                  pl.BlockSpec((TOK_TILE, D_MODEL), lambda i: (jnp.minimum(i, npt - 1), 0)),
                  pl.BlockSpec((TOK_TILE, D_MODEL), lambda i: (jnp.maximum(i - npt, 0), 0)),
                  pl.BlockSpec((1, 1, 6 * D_MODEL), lambda i: (mod_row(i), 0, 0)),
                  pl.BlockSpec(w_out_bf.shape, const),
                  pl.BlockSpec((1, REC_WIDTH), const),
                  pl.BlockSpec((1, D_MODEL), const), pl.BlockSpec((1, D_MODEL), const),
                  pl.BlockSpec((D_MODEL, LANES), const), pl.BlockSpec((D_MODEL, LANES), const),
                  pl.BlockSpec((1, LANES), const),
                  pl.BlockSpec((TOK_TILE, TOK_TILE), const)],
        out_specs=(tok(D_MODEL), pl.BlockSpec((TOK_TILE * SUBLANES, LANES), lambda i: (i, 0)),
                   pl.BlockSpec((N_EXPERTS, TOK_TILE), lambda i: (0, i)),
                   pl.BlockSpec((N_EXPERTS, TOK_TILE), lambda i: (0, i)),
                   tok(LANES), pl.BlockSpec((1, LANES), const),
                   pl.BlockSpec((1, 1, LANES), lambda i: (i, 0, 0))),
        scratch_shapes=[pltpu.VMEM((1, LANES), F32)],
        compiler_params=pltpu.CompilerParams(dimension_semantics=("arbitrary",),
                                             vmem_limit_bytes=VMEM_LIMIT),
        name="post_mixer_router",
    )(o_f, o_b, rec, conv, xp, xs, mod3, w_out_bf, rec_g, ln1_g, ln1_b, rw_hi, rw_lo, rb, tri)


PLAN_ROWS = 32


def _plan_kernel(n_tok, tile_e, gstart, n_act, cum, starts_ref, ends_ref, rank_t_ref, slot_t_ref,
                 src_ref, dst_ref, acc_ref):
    i = pl.program_id(0)

    @pl.when(i < n_act[0])
    def _():
        e = tile_e[i]
        base = i * EXP_TILE - gstart[e]
        rows_iota = lax.broadcasted_iota(jnp.int32, (PLAN_ROWS, 1), 0)
        lane = lax.broadcasted_iota(jnp.int32, (1, TOK_TILE), 1)
        acc_ref[...] = jnp.zeros_like(acc_ref)

        base_f = base.astype(F32)
        win0 = jnp.sum((ends_ref[pl.ds(e, 1), :] <= base_f).astype(jnp.int32))
        win1 = jnp.sum((starts_ref[pl.ds(e, 1), :] < base_f + EXP_TILE).astype(jnp.int32))

        def window(b, carry):
            tok0 = pl.multiple_of(b * TOK_TILE, TOK_TILE)
            rk = rank_t_ref[pl.ds(e, 1), pl.ds(tok0, TOK_TILE)]
            sl = slot_t_ref[pl.ds(e, 1), pl.ds(tok0, TOK_TILE)]
            code = ((tok0 + lane) * TOP_K).astype(F32) + sl + 1.0
            lo = jnp.maximum(cum[b * N_EXPERTS + e] - base, 0)
            hi = jnp.minimum(cum[(b + 1) * N_EXPERTS + e] - base, EXP_TILE)

            def rows(k, c):
                r0 = pl.multiple_of(k * PLAN_ROWS, PLAN_ROWS)
                want = (base + r0 + rows_iota).astype(F32)
                hit = jnp.where(rk == want, code, 0.0)
                part = hit[:, 0:LANES]
                for c0 in range(LANES, TOK_TILE, LANES):
                    part = part + hit[:, c0:c0 + LANES]
                acc_ref[pl.ds(r0, PLAN_ROWS), :] += part
                return c

            shift = PLAN_ROWS.bit_length() - 1
            lax.fori_loop(lo >> shift, (hi + PLAN_ROWS - 1) >> shift, rows, 0)
            return carry

        lax.fori_loop(win0, win1, window, 0)
        code = jnp.sum(acc_ref[...].T, axis=0, keepdims=True).astype(jnp.int32) - 1
        pos = lax.broadcasted_iota(jnp.int32, (1, EXP_TILE), 1)
        tok = code >> 2
        src = jnp.where(code >= 0, tok, lax.rem(i * EXP_TILE + pos, n_tok))
        spare = TOP_K * n_tok + lax.rem(i, N_YBUF) * EXP_TILE + pos
        dst = jnp.where(code >= 0, (code & (TOP_K - 1)) * n_tok + tok, spare)
        src_ref[0] = src * SUBLANES
        dst_ref[0] = dst * SUBLANES

    @pl.when(i >= n_act[0])
    def _():
        src_ref[...] = jnp.zeros_like(src_ref)
        dst_ref[...] = jnp.zeros_like(dst_ref)


def _plan(rank_t, slot_t, tile_e, gstart, n_act, cum2d):
    n_tok = rank_t.shape[1]
    n_tiles = tile_e.shape[0]
    n_tok_tiles = cum2d.shape[0] - 1
    assert n_tok_tiles <= LANES
    huge = jnp.float32(2.0 ** 30)
    as_rows = lambda a: jnp.pad(a.T.astype(F32), ((0, 0), (0, LANES - n_tok_tiles)),
                                constant_values=huge)
    whole = lambda i, *_: (0, 0)
    per_tile = pl.BlockSpec((1, 1, EXP_TILE), lambda i, *_: (i, 0, 0))
    grid_spec = pltpu.PrefetchScalarGridSpec(
        num_scalar_prefetch=4,
        grid=(n_tiles,),
        in_specs=[pl.BlockSpec((N_EXPERTS, LANES), whole), pl.BlockSpec((N_EXPERTS, LANES), whole),
                  pl.BlockSpec(rank_t.shape, whole), pl.BlockSpec(slot_t.shape, whole)],
        out_specs=(per_tile, per_tile),
        scratch_shapes=[pltpu.VMEM((EXP_TILE, LANES), F32)],
    )
    shape = jax.ShapeDtypeStruct((n_tiles, 1, EXP_TILE), jnp.int32)
    return pl.pallas_call(
        functools.partial(_plan_kernel, n_tok),
        out_shape=(shape, shape),
        grid_spec=grid_spec,
        compiler_params=pltpu.CompilerParams(dimension_semantics=("arbitrary",),
                                             vmem_limit_bytes=VMEM_LIMIT),
        name="moe_plan",
    )(tile_e, gstart, n_act, cum2d.reshape(-1), as_rows(cum2d[:-1]), as_rows(cum2d[1:]),
      rank_t, slot_t)


N_YBUF = 3
DMA_UNROLL = 8


def _expert_kernel(tile_e, n_act, idx_ref, h_ref, wgu_ref, bgu_ref, wd_ref, bd_ref, yk_ref, xbuf,
                   ybuf, gsem, ssem, wgu_bf, wd_bf):
    i = pl.program_id(0)
    n_steps = pl.num_programs(0)
    n_active = n_act[0]
    tile_rows = EXP_TILE * SUBLANES
    n_slot_rows = yk_ref.shape[0] - N_YBUF * tile_rows

    def row_tile(ref, start):
        return ref.at[pl.ds(pl.multiple_of(start, SUBLANES), SUBLANES)]

    def for_each_row(fn):
        def body(g, carry):
            for k in range(DMA_UNROLL):
                fn(g * DMA_UNROLL + k)
            return carry
        lax.fori_loop(0, EXP_TILE // DMA_UNROLL, body, 0)

    def on_slot(slot, n, fn):
        if isinstance(slot, int):
            fn(slot)
            return
        for s in range(n):
            pl.when(slot == s)(functools.partial(fn, s))

    def gather_rows(which, slot):
        def run(s):
            def one(r):
                pltpu.make_async_copy(row_tile(h_ref, idx_ref[0, 0, which * EXP_TILE + r]),
                                      row_tile(xbuf.at[s], r * SUBLANES), gsem.at[s]).start()
            for_each_row(one)
        on_slot(slot, 2, run)

    def scatter_rows(slot):
        def run(s):
            def one(r):
                pltpu.make_async_copy(row_tile(ybuf.at[s], r * SUBLANES),
                                      row_tile(yk_ref, idx_ref[0, 0, 2 * EXP_TILE + r]),
                                      ssem.at[s]).start()
            for_each_row(one)
        on_slot(slot, N_YBUF, run)

    def scatter_wait(slot):
        pltpu.make_async_copy(ybuf.at[slot], yk_ref.at[pl.ds(0, tile_rows)], ssem.at[slot]).wait()

    @pl.when(i == 0)
    def _():
        for s in range(N_YBUF):
            ybuf[s] = jnp.zeros((tile_rows, LANES), F32)
        for s in range(N_YBUF - 1):
            pltpu.make_async_copy(
                ybuf.at[s], yk_ref.at[pl.ds(n_slot_rows + s * tile_rows, tile_rows)],
                ssem.at[s]).start()
        gather_rows(0, 0)

    prev = tile_e[jnp.maximum(i - 1, 0)]
    new_expert = jnp.logical_or(i == 0, tile_e[i] != prev)

    @pl.when(jnp.logical_and(new_expert, i < n_active))
    def _():
        rows = 128

        def cast(r, carry):
            sl = pl.ds(pl.multiple_of(r * rows, rows), rows)
            wgu_bf[sl, :] = wgu_ref[0, sl, :].astype(BF16)
            wd_bf[sl, :] = wd_ref[0, sl, :].astype(BF16)
            return carry

        lax.fori_loop(0, D_MODEL // rows, cast, 0)

    @pl.when(i < n_active)
    def _():
        xslot = i % 2
        yslot = i % N_YBUF
        pltpu.make_async_copy(h_ref.at[pl.ds(0, tile_rows)], xbuf.at[xslot], gsem.at[xslot]).wait()

        scatter_wait(yslot)
        x = _load_row_tiles(xbuf.at[xslot], EXP_TILE).astype(BF16)
        next_x = xbuf.at[(i + 1) % 2]
        next_sem = gsem.at[(i + 1) % 2]
        prev_y = ybuf.at[(i + N_YBUF - 1) % N_YBUF]
        prev_sem = ssem.at[(i + N_YBUF - 1) % N_YBUF]
        gu = _dot(x, wgu_bf[...]) + bgu_ref[0]
        for r in range(EXP_TILE):
            pltpu.make_async_copy(row_tile(h_ref, idx_ref[0, 0, EXP_TILE + r]),
                                  row_tile(next_x, r * SUBLANES), next_sem).start(priority=r % 2)
        gate = jnp.minimum(gu[:, :D_FF], SWIGLU_LIMIT)
        up = jnp.clip(gu[:, D_FF:], -SWIGLU_LIMIT, SWIGLU_LIMIT)
        act = (up + 1.0) * gate * _sigmoid(SWIGLU_ALPHA * gate)
        for r in range(EXP_TILE):
            pltpu.make_async_copy(row_tile(prev_y, r * SUBLANES),
                                  row_tile(yk_ref, idx_ref[0, 0, 3 * EXP_TILE + r]),
                                  prev_sem).start(priority=r % 2)
        _store_row_tiles(ybuf.at[yslot], _dot(act.astype(BF16), wd_bf[...]) + bd_ref[0])

        @pl.when(i == n_active - 1)
        def _():
            scatter_rows(yslot)

    @pl.when(i == n_steps - 1)
    def _():
        for back in range(1, N_YBUF + 1):
            scatter_wait((n_active - back) % N_YBUF)
        last_x = n_active % 2
        pltpu.make_async_copy(h_ref.at[pl.ds(0, tile_rows)], xbuf.at[last_x], gsem.at[last_x]).wait()


def _experts(h2, src3, dst3, tile_e, n_act, w_gate_up, b_gate_up, w_down, b_down):
    n_tok = h2.shape[0] // SUBLANES
    n_tiles = src3.shape[0]
    src_next = jnp.concatenate([src3[1:], src3[-1:]], axis=0)
    spare_rows = TOP_K * n_tok + (N_YBUF - 1) * EXP_TILE + jnp.arange(EXP_TILE, dtype=jnp.int32)
    dst_prev = jnp.concatenate([(spare_rows * SUBLANES).reshape(1, 1, EXP_TILE), dst3[:-1]], axis=0)
    idx3 = jnp.concatenate([src3, src_next, dst3, dst_prev], axis=-1)
    grid_spec = pltpu.PrefetchScalarGridSpec(
        num_scalar_prefetch=2,
        grid=(n_tiles,),
        in_specs=[pl.BlockSpec((1, 1, 4 * EXP_TILE), lambda i, te, na: (i, 0, 0),
                               memory_space=pltpu.SMEM),
                  pl.BlockSpec(memory_space=pl.ANY),
                  pl.BlockSpec((1, D_MODEL, 2 * D_FF), lambda i, te, na: (te[i], 0, 0)),
                  pl.BlockSpec((1, 1, 2 * D_FF), lambda i, te, na: (te[i], 0, 0)),
                  pl.BlockSpec((1, D_FF, D_MODEL), lambda i, te, na: (te[i], 0, 0)),
                  pl.BlockSpec((1, 1, D_MODEL), lambda i, te, na: (te[i], 0, 0))],
        out_specs=pl.BlockSpec(memory_space=pl.ANY),
        scratch_shapes=[pltpu.VMEM((2, EXP_TILE * SUBLANES, LANES), F32),
                        pltpu.VMEM((N_YBUF, EXP_TILE * SUBLANES, LANES), F32),
                        pltpu.SemaphoreType.DMA((2,)), pltpu.SemaphoreType.DMA((N_YBUF,)),
                        pltpu.VMEM((D_MODEL, 2 * D_FF), BF16), pltpu.VMEM((D_FF, D_MODEL), BF16)],
    )
    return pl.pallas_call(
        _expert_kernel,
        out_shape=jax.ShapeDtypeStruct(((TOP_K * n_tok + N_YBUF * EXP_TILE) * SUBLANES, LANES), F32),
        grid_spec=grid_spec,
        compiler_params=pltpu.CompilerParams(dimension_semantics=("arbitrary",),
                                             vmem_limit_bytes=VMEM_LIMIT),
        name="moe_experts",
    )(tile_e, n_act, idx3, h2, w_gate_up, b_gate_up.reshape(N_EXPERTS, 1, 2 * D_FF),
      w_down, b_down.reshape(N_EXPERTS, 1, D_MODEL))


COMBINE_TILE = 512


def _combine_kernel(n_prompt_tiles, alpha, y0_ref, y1_ref, y2_ref, y3_ref, x1_ref, w4_ref, mod_ref,
                    l2g_ref, l2b_ref, yp_ref, ysm_ref):
    i = pl.program_id(0)
    w4 = w4_ref[...]
    ff = _load_row_tiles(y0_ref, COMBINE_TILE) * w4[:, 0:1]
    for j, y_ref in enumerate((y1_ref, y2_ref, y3_ref), start=1):
        ff = ff + _load_row_tiles(y_ref, COMBINE_TILE) * w4[:, j:j + 1]
    g2 = mod_ref[0][:, 5 * D_MODEL:6 * D_MODEL]
    out = _layer_norm(alpha * x1_ref[...] + g2 * ff) * l2g_ref[...] + l2b_ref[...]

    @pl.when(i < n_prompt_tiles)
    def _():
        yp_ref[...] = out

    @pl.when(i >= n_prompt_tiles)
    def _():
        ysm_ref[...] = out


def _combine(yk, x1, w4, mod3, ln2_g, ln2_b, n_p, alpha, sample_seq):
    n_tok = x1.shape[0]
    n_s = n_tok - n_p
    assert n_p % COMBINE_TILE == 0 and sample_seq % COMBINE_TILE == 0
    npt = n_p // COMBINE_TILE
    n_tok_tiles = n_tok // COMBINE_TILE
    tiles_per_sample_seq = sample_seq // COMBINE_TILE

    def mod_row(i):
        return jnp.where(i < npt, 0, 1 + (i - npt) // tiles_per_sample_seq)

    def slot(j):
        return pl.BlockSpec((COMBINE_TILE * SUBLANES, LANES), lambda i: (j * n_tok_tiles + i, 0))

    const = lambda i: (0, 0)
    return pl.pallas_call(
        functools.partial(_combine_kernel, npt, alpha),
        out_shape=(jax.ShapeDtypeStruct((n_p, D_MODEL), F32),
                   jax.ShapeDtypeStruct((n_s, D_MODEL), F32)),
        grid=(n_tok_tiles,),
        in_specs=[slot(0), slot(1), slot(2), slot(3),
                  pl.BlockSpec((COMBINE_TILE, D_MODEL), lambda i: (i, 0)),
                  pl.BlockSpec((COMBINE_TILE, LANES), lambda i: (i, 0)),
                  pl.BlockSpec((1, 1, 6 * D_MODEL), lambda i: (mod_row(i), 0, 0)),
                  pl.BlockSpec((1, D_MODEL), const), pl.BlockSpec((1, D_MODEL), const)],
        out_specs=(pl.BlockSpec((COMBINE_TILE, D_MODEL), lambda i: (jnp.minimum(i, npt - 1), 0)),
                   pl.BlockSpec((COMBINE_TILE, D_MODEL), lambda i: (jnp.maximum(i - npt, 0), 0))),
        compiler_params=pltpu.CompilerParams(dimension_semantics=("arbitrary",),
                                             vmem_limit_bytes=VMEM_LIMIT),
        name="moe_combine",
    )(yk, yk, yk, yk, x1, w4, mod3, ln2_g, ln2_b)


def kernel(x_prompt, x_sample, c, state_fwd, state_bwd, c_ctx, w_ada, b_ada, w_in, conv_w, conv_b,
           conv_ln_g, conv_ln_b, lb_logits, rec_norm_g, w_out, ln1_g, ln1_b, ln2_g, ln2_b,
           router_w, router_b, w_gate_up, b_gate_up, w_down, b_down):
    depth = w_ada.shape[0]
    assert depth == 1
    alpha = (2.0 * depth) ** 0.25
    nb_p, seq_p, _ = x_prompt.shape
    nb_s, seq_s, _ = x_sample.shape
    n_p, n_s = nb_p * seq_p, nb_s * seq_s
    n_tok = n_p + n_s
    row2 = lambda a: a.reshape(1, -1)

    lb = jnp.cumsum(jax.nn.softmax(lb_logits.astype(F32), axis=0), axis=0)[0]

    cond = jnp.concatenate([c_ctx[None, :], c, jnp.zeros((16 - 1 - nb_s, D_MODEL), F32)], axis=0)
    mod = _ada(cond, w_ada[0], b_ada[0])
    mod3 = mod.reshape(16, 1, 6 * D_MODEL)

    xp = x_prompt.reshape(n_p, D_MODEL)
    xs = x_sample.reshape(n_s, D_MODEL)
    conv_w_pad = jnp.concatenate([conv_w[0], jnp.zeros((1, CONV_WIDTH), F32)], axis=0)
    conv_out, rec = _inproj(xp, xs, mod3, w_in[0].astype(BF16), conv_w_pad, row2(conv_b[0]),
                            row2(conv_ln_g[0]), row2(conv_ln_b[0]), seq_p, seq_s)

    zeros_state = jnp.zeros((nb_p, REC_HEADS, REC_DK, REC_DK), F32)
    s0f = jnp.concatenate([zeros_state, jnp.swapaxes(state_fwd[:, 0], -1, -2)], axis=0)
    s0b = jnp.concatenate([zeros_state, jnp.swapaxes(state_bwd[:, 0], -1, -2)], axis=0)
    o_f, o_b, sf_t, sb_t = _scan(rec, lb, s0f, s0b, [seq_p] * nb_p + [seq_s] * nb_s)
    new_f = jnp.swapaxes(sf_t[:nb_p], -1, -2)[:, None]
    new_b = jnp.swapaxes(sb_t[:nb_p], -1, -2)[:, None]

    rw = jnp.pad(router_w[0], ((0, 0), (0, LANES - N_EXPERTS)))
    rw_hi = rw.astype(BF16)
    rw_lo = (rw - rw_hi.astype(F32)).astype(BF16)
    rb = jnp.pad(router_b[0], (0, LANES - N_EXPERTS)).reshape(1, LANES)
    x1, h2, rank_t, slot_t, w4, counts, tile_start = _post(
        o_f, o_b, rec, conv_out, xp, xs, mod3, w_out[0].astype(BF16), row2(rec_norm_g[0]),
        row2(ln1_g[0]), row2(ln1_b[0]), rw_hi, rw_lo, rb, alpha, seq_s)

    cnt = counts[0, :N_EXPERTS].astype(jnp.int32)
    gpad = ((cnt + EXP_TILE - 1) // EXP_TILE) * EXP_TILE
    gend = jnp.cumsum(gpad)
    gstart = gend - gpad
    n_rows = n_tok * TOP_K + N_EXPERTS * EXP_TILE
    n_tiles = n_rows // EXP_TILE
    n_act = gend[-1] // EXP_TILE
    tile_ids = jnp.minimum(jnp.arange(n_tiles, dtype=jnp.int32), n_act - 1)
    tile_e = jnp.sum((gend[None, :] <= tile_ids[:, None] * EXP_TILE).astype(jnp.int32), axis=1)
    n_act1 = n_act.reshape(1).astype(jnp.int32)
    starts = tile_start[:, 0, :N_EXPERTS].astype(jnp.int32)
    cum2d = jnp.concatenate([starts, cnt[None, :]], axis=0)
    src, dst = _plan(rank_t, slot_t, tile_e, gstart.astype(jnp.int32), n_act1, cum2d)
    yk = _experts(h2, src, dst, tile_e, n_act1, w_gate_up[0], b_gate_up[0], w_down[0], b_down[0])
    y_p, y_s = _combine(yk, x1, w4, mod3, row2(ln2_g[0]), row2(ln2_b[0]), n_p, alpha, seq_s)
    return (y_p.reshape(nb_p, seq_p, D_MODEL), y_s.reshape(nb_s, seq_s, D_MODEL), new_f, new_b)
```

```python
import functools

import numpy as np
import jax
import jax.numpy as jnp
from jax import lax
from jax.experimental import pallas as pl
from jax.experimental.pallas import tpu as pltpu

F32 = jnp.float32
BF16 = jnp.bfloat16

D_MODEL = 1024
CONV_WIDTH = 512
CONV_TAPS = 31
REC_HEADS = 4
REC_DK = 128
REC_WIDTH = REC_HEADS * REC_DK
REC_COLS = 5 * REC_WIDTH
CHUNK = 64
N_LEVELS = 6
SCAN_CHUNKS = 4
N_EXPERTS = 32
TOP_K = 4
D_FF = 1024
SWIGLU_LIMIT = 7.0
SWIGLU_ALPHA = 1.702
LN_EPS = 1e-5
RMS_EPS = 1e-6
GRID_W = 64

LANES = 128
TOK_TILE = 512
EXP_TILE = 512
VMEM_LIMIT = 56 * 1024 * 1024


def _sigmoid(x):
    return 1.0 / (1.0 + jnp.exp(-x))


def _layer_norm(x):
    mu = jnp.mean(x, axis=-1, keepdims=True)
    xc = x - mu
    var = jnp.mean(xc * xc, axis=-1, keepdims=True)
    return xc * lax.rsqrt(var + LN_EPS)


def _dot(a, b):
    return jnp.dot(a, b, preferred_element_type=F32)


def _dot_nt(a, b):
    return lax.dot_general(a, b, (((1,), (1,)), ((), ())), preferred_element_type=F32)


SUBLANES = 8
assert D_MODEL == SUBLANES * LANES


def _store_row_tiles(ref, value):
    for s in range(SUBLANES):
        ref[pl.ds(s, value.shape[0], stride=SUBLANES), :] = value[:, s * LANES:(s + 1) * LANES]


def _load_row_tiles(ref, n_rows):
    return jnp.concatenate(
        [ref[pl.ds(s, n_rows, stride=SUBLANES), :] for s in range(SUBLANES)], axis=-1)


def _split3(x):
    hi = x.astype(BF16)
    r1 = x - hi.astype(F32)
    mid = r1.astype(BF16)
    lo = (r1 - mid.astype(F32)).astype(BF16)
    return hi, mid, lo


def _ada_kernel(c_ref, w_ref, b_ref, o_ref):
    c = c_ref[...]
    s = (c * _sigmoid(c)).astype(BF16)
    o_ref[...] = _dot(s, w_ref[...].astype(BF16)) + b_ref[...]


def _ada(cond, w_ada, b_ada):
    rows = cond.shape[0]
    ncol = w_ada.shape[1]
    tn = 1024
    return pl.pallas_call(
        _ada_kernel,
        out_shape=jax.ShapeDtypeStruct((rows, ncol), F32),
        grid=(ncol // tn,),
        in_specs=[pl.BlockSpec((rows, D_MODEL), lambda j: (0, 0)),
                  pl.BlockSpec((D_MODEL, tn), lambda j: (0, j)),
                  pl.BlockSpec((1, tn), lambda j: (0, j))],
        out_specs=pl.BlockSpec((rows, tn), lambda j: (0, j)),
        compiler_params=pltpu.CompilerParams(dimension_semantics=("parallel",),
                                             vmem_limit_bytes=VMEM_LIMIT),
        name="ada_mod",
    )(cond, w_ada, b_ada.reshape(1, ncol))


def _inproj_kernel(n_prompt_tiles, prompt_seg, xp_ref, xs_ref, mod_ref, win_ref, cw_ref, cb_ref,
                   cg_ref, cbeta_ref, conv_ref, rec_ref):
    i = pl.program_id(0)
    is_prompt = i < n_prompt_tiles
    x = jnp.where(is_prompt, xp_ref[...], xs_ref[...])
    mod = mod_ref[0]
    sh1 = mod[:, 0:D_MODEL]
    sc1 = mod[:, D_MODEL:2 * D_MODEL]
    h = (_layer_norm(x) * (1.0 + sc1) + sh1).astype(BF16)

    glu = _dot(h, win_ref[:, 0:2 * CONV_WIDTH])
    u = glu[:, :CONV_WIDTH] * _sigmoid(glu[:, CONV_WIDTH:])

    seg_len = jnp.where(is_prompt, prompt_seg, GRID_W)
    half = CONV_TAPS // 2
    pos = lax.broadcasted_iota(jnp.int32, (TOK_TILE, 1), 0) & (seg_len - 1)

    def masked_shift(d):
        sh = pltpu.roll(u, (-d) % TOK_TILE, axis=0)
        edge_rows = SUBLANES * ((abs(d) + SUBLANES - 1) // SUBLANES)
        pieces = []
        for s0 in range(0, TOK_TILE, GRID_W):
            if d < 0:
                edge = slice(s0, s0 + edge_rows)
                pieces.append(jnp.where(pos[edge] >= -d, sh[edge], 0.0))
                pieces.append(sh[s0 + edge_rows:s0 + GRID_W])
            else:
                edge = slice(s0 + GRID_W - edge_rows, s0 + GRID_W)
                pieces.append(sh[s0:s0 + GRID_W - edge_rows])
                pieces.append(jnp.where(pos[edge] < seg_len - d, sh[edge], 0.0))
        return jnp.concatenate(pieces, axis=0)

    acc = u * cw_ref[half:half + 1, :] + cb_ref[...]
    taps = [j for j in range(CONV_TAPS) if j != half]
    n_groups = REC_COLS // REC_WIDTH
    per_group = len(taps) // n_groups
    anchor = None
    for g in range(n_groups):
        cols = slice(g * REC_WIDTH, (g + 1) * REC_WIDTH)
        rec = _dot(h, win_ref[:, 2 * CONV_WIDTH + g * REC_WIDTH:
                              2 * CONV_WIDTH + (g + 1) * REC_WIDTH])
        rec_ref[:, cols] = rec
        for j in taps[g * per_group:(g + 1) * per_group]:
            w_tap = cw_ref[j:j + 1, :]
            if anchor is not None:
                w_tap = w_tap + anchor
                anchor = None
            acc = acc + masked_shift(j - half) * w_tap
        if g + 1 < n_groups:
            bits = pltpu.bitcast(rec[0:SUBLANES, :], jnp.uint32)
            anchor = pltpu.bitcast((bits >> 16) >> 16, F32)[0:1, :]
    y = _layer_norm(acc) * cg_ref[...] + cbeta_ref[...]
    conv_ref[...] = (y * _sigmoid(y)).astype(BF16)


def _inproj(xp, xs, mod3, w_in_bf, conv_w, conv_b, conv_g, conv_beta, prompt_seq, sample_seq):
    n_p, n_s = xp.shape[0], xs.shape[0]
    n_tok = n_p + n_s
    npt = n_p // TOK_TILE
    tiles_per_sample_seq = sample_seq // TOK_TILE
    assert TOK_TILE % prompt_seq == 0 and n_p % TOK_TILE == 0 and sample_seq % TOK_TILE == 0

    def mod_row(i):
        return jnp.where(i < npt, 0, 1 + (i - npt) // tiles_per_sample_seq)

    const = lambda i: (0, 0)
    return pl.pallas_call(
        functools.partial(_inproj_kernel, npt, prompt_seq),
        out_shape=(jax.ShapeDtypeStruct((n_tok, CONV_WIDTH), BF16),
                   jax.ShapeDtypeStruct((n_tok, REC_COLS), F32)),
        grid=(n_tok // TOK_TILE,),
        in_specs=[pl.BlockSpec((TOK_TILE, D_MODEL), lambda i: (jnp.minimum(i, npt - 1), 0)),
                  pl.BlockSpec((TOK_TILE, D_MODEL), lambda i: (jnp.maximum(i - npt, 0), 0)),
                  pl.BlockSpec((1, 1, 6 * D_MODEL), lambda i: (mod_row(i), 0, 0)),
                  pl.BlockSpec(w_in_bf.shape, const),
                  pl.BlockSpec(conv_w.shape, const),
                  pl.BlockSpec((1, CONV_WIDTH), const),
                  pl.BlockSpec((1, CONV_WIDTH), const),
                  pl.BlockSpec((1, CONV_WIDTH), const)],
        out_specs=(pl.BlockSpec((TOK_TILE, CONV_WIDTH), lambda i: (i, 0)),
                   pl.BlockSpec((TOK_TILE, REC_COLS), lambda i: (i, 0))),
        compiler_params=pltpu.CompilerParams(dimension_semantics=("parallel",),
                                             vmem_limit_bytes=VMEM_LIMIT),
        name="inproj_conv",
    )(xp, xs, mod3, w_in_bf, conv_w, conv_b, conv_g, conv_beta)


def _scan_constants():
    c = CHUNK
    w = np.zeros((N_LEVELS + 2, c, c), np.float32)
    m = np.zeros((N_LEVELS + 1, c, c), np.float32)
    t = np.arange(c)
    for j in range(N_LEVELS):
        half = 1 << j
        for ti in range(c):
            ref = (ti & ~(2 * half - 1)) + half - 1
            if ti & half:
                w[j, ti, ref + 1:ti + 1] = 1.0
            else:
                w[j, ti, ti + 1:ref + 1] = 1.0
        upper = (t[:, None] & half) != 0
        lower = (t[None, :] & half) == 0
        same = (t[:, None] >> (j + 1)) == (t[None, :] >> (j + 1))
        m[j] = (upper & lower & same).astype(np.float32)
    w[N_LEVELS] = (t[None, :] <= t[:, None]).astype(np.float32)
    w[N_LEVELS + 1] = (t[None, :] > t[:, None]).astype(np.float32)
    m[N_LEVELS] = np.eye(c, dtype=np.float32)
    w_f = np.tile(w.reshape(-1, c), (1, 3))
    w_b = np.tile(w[:, ::-1, ::-1].reshape(-1, c), (1, 3))
    m_b = m[:, ::-1, ::-1]
    return (jnp.asarray(w_f, BF16), jnp.asarray(np.ascontiguousarray(w_b), BF16),
            jnp.asarray(m), jnp.asarray(np.ascontiguousarray(m_b)))


def _scan_gates(z, lb, w_ref):
    e = jnp.exp(-jnp.abs(z))
    r = 1.0 / (1.0 + e)
    a = e * r
    sig = jnp.where(z >= 0, r, a)
    sig_neg = jnp.where(z >= 0, a, r)
    one_m_lb = 1.0 - lb
    log_f = jnp.log2(lb + one_m_lb * sig)
    k = one_m_lb * sig_neg
    terms = jnp.concatenate(_split3(log_f), axis=0)
    ex = jnp.exp2(_dot(w_ref[...], terms))
    return k, ex


def _scan_intra(q, k, ex, m_ref, neighbour):
    c = CHUNK
    qb, kb = q.astype(BF16), k.astype(BF16)
    a_mat = m_ref[N_LEVELS] * jnp.sum(q * k, axis=-1, keepdims=True)
    x0 = ex[0:c]
    pair = jnp.sum((q * x0) * pltpu.roll(k * x0, neighbour, axis=0), axis=-1, keepdims=True)
    a_mat = a_mat + m_ref[0] * pair
    for j in range(1, N_LEVELS):
        xj = ex[j * c:(j + 1) * c].astype(BF16)
        a_mat = a_mat + m_ref[j] * _dot_nt(qb * xj, kb * xj)
    return a_mat


def _scan_kernel(fblk, bblk, first, seq, qf_ref, zf_ref, vf_ref, qb_ref, zb_ref, vb_ref, lb_ref,
                 wf_ref, wb_ref, mf_ref, mb_ref, s0f_ref, s0b_ref, of_ref, ob_ref, sf_ref, sb_ref):
    s = pl.program_id(0)

    @pl.when(first[s] == 1)
    def _():
        sf_ref[...] = s0f_ref[...]
        sb_ref[...] = s0b_ref[...]

    c = CHUNK
    dirs = ((qf_ref, zf_ref, vf_ref, 0, wf_ref, mf_ref, sf_ref, of_ref, c - 1,
             tuple(range(SCAN_CHUNKS))),
            (qb_ref, zb_ref, vb_ref, 1, wb_ref, mb_ref, sb_ref, ob_ref, 0,
             tuple(reversed(range(SCAN_CHUNKS)))))
    units = [(d, slice(sub * c, (sub + 1) * c)) for d in dirs for sub in d[-1]]
    gates = [_scan_gates(d[1][rows, :], lb_ref[d[3]:d[3] + 1, :], d[4]) for d, rows in units]
    intra = []
    for (d, rows), (k, ex) in zip(units, gates):
        for h in range(REC_HEADS):
            cols = slice(h * REC_DK, (h + 1) * REC_DK)
            neighbour = 1 if d[8] == c - 1 else c - 1
            intra.append(_scan_intra(d[0][rows, cols], k[:, cols], ex[:, cols], d[5], neighbour))
    idx = 0
    for (d, rows), (k, ex) in zip(units, gates):
        q_ref, _, v_ref, _, _, _, st_ref, o_ref, last, _ = d
        for h in range(REC_HEADS):
            cols = slice(h * REC_DK, (h + 1) * REC_DK)
            q, v = q_ref[rows, cols], v_ref[rows, cols]
            ex_b = ex[N_LEVELS * c:(N_LEVELS + 1) * c, cols]
            ex_l = ex[(N_LEVELS + 1) * c:(N_LEVELS + 2) * c, cols]
            st = st_ref[0, h]
            o_ref[rows, cols] = (_dot(intra[idx].astype(BF16), v.astype(BF16))
                                 + _dot_nt((q * ex_b).astype(BF16), st.astype(BF16)))
            st_ref[0, h] = (st * ex_b[last:last + 1, :]
                            + _dot(v.T.astype(BF16), (k[:, cols] * ex_l).astype(BF16)))
            idx += 1


def _scan(rec, lb, s0f_t, s0b_t, seq_lens):
    n_tok = rec.shape[0]
    fblk, bblk, first, seq = [], [], [], []
    base = 0
    step_rows = CHUNK * SCAN_CHUNKS
    for si, ln in enumerate(seq_lens):
        assert ln % step_rows == 0
        n = ln // step_rows
        for ci in range(n):
            fblk.append(base + ci)
            bblk.append(base + n - 1 - ci)
            first.append(1 if ci == 0 else 0)
            seq.append(si)
        base += n
    steps = len(fblk)
    as_i32 = lambda a: jnp.asarray(np.asarray(a, np.int32))
    w_f, w_b, m_f, m_b = _scan_constants()
    n_seq = len(seq_lens)

    def col(block_of, c):
        return pl.BlockSpec((step_rows, REC_WIDTH), lambda s, fb, bb, fi, sq: (block_of(fb, bb)[s], c))

    fwd = lambda fb, bb: fb
    bwd = lambda fb, bb: bb
    const2 = lambda s, fb, bb, fi, sq: (0, 0)
    const3 = lambda s, fb, bb, fi, sq: (0, 0, 0)
    state_spec = pl.BlockSpec((1, REC_HEADS, REC_DK, REC_DK), lambda s, fb, bb, fi, sq: (sq[s], 0, 0, 0))
    grid_spec = pltpu.PrefetchScalarGridSpec(
        num_scalar_prefetch=4,
        grid=(steps,),
        in_specs=[col(fwd, 0), col(fwd, 1), col(fwd, 3), col(bwd, 0), col(bwd, 2), col(bwd, 3),
                  pl.BlockSpec((2, REC_WIDTH), const2),
                  pl.BlockSpec(w_f.shape, const2), pl.BlockSpec(w_b.shape, const2),
                  pl.BlockSpec(m_f.shape, const3), pl.BlockSpec(m_b.shape, const3),
                  state_spec, state_spec],
        out_specs=(pl.BlockSpec((step_rows, REC_WIDTH), lambda s, fb, bb, fi, sq: (fb[s], 0)),
                   pl.BlockSpec((step_rows, REC_WIDTH), lambda s, fb, bb, fi, sq: (bb[s], 0)),
                   state_spec, state_spec),
    )
    st_shape = jax.ShapeDtypeStruct((n_seq, REC_HEADS, REC_DK, REC_DK), F32)
    return pl.pallas_call(
        _scan_kernel,
        out_shape=(jax.ShapeDtypeStruct((n_tok, REC_WIDTH), F32),
                   jax.ShapeDtypeStruct((n_tok, REC_WIDTH), F32), st_shape, st_shape),
        grid_spec=grid_spec,
        compiler_params=pltpu.CompilerParams(dimension_semantics=("arbitrary",),
                                             vmem_limit_bytes=VMEM_LIMIT),
        name="hgrn2_scan",
    )(as_i32(fblk), as_i32(bblk), as_i32(first), as_i32(seq),
      rec, rec, rec, rec, rec, rec, lb, w_f, w_b, m_f, m_b, s0f_t, s0b_t)


def _post_kernel(n_prompt_tiles, alpha, of_ref, ob_ref, g_ref, conv_ref, xp_ref, xs_ref, mod_ref,
                 wout_ref, rg_ref, l1g_ref, l1b_ref, rwh_ref, rwl_ref, rb_ref, tri_ref,
                 x1_ref, h2_ref, rank_t_ref, slot_t_ref, w4_ref, cnt_ref, start_ref, run_ref):
    i = pl.program_id(0)

    @pl.when(i == 0)
    def _():
        run_ref[...] = jnp.zeros_like(run_ref)

    x = jnp.where(i < n_prompt_tiles, xp_ref[...], xs_ref[...])
    mod = mod_ref[0]
    g1 = mod[:, 2 * D_MODEL:3 * D_MODEL]
    sh2 = mod[:, 3 * D_MODEL:4 * D_MODEL]
    sc2 = mod[:, 4 * D_MODEL:5 * D_MODEL]

    g = g_ref[...]
    silu_g = g * _sigmoid(g)
    mix = _dot(conv_ref[...], wout_ref[0:CONV_WIDTH, :])
    for h in range(REC_HEADS):
        cols = slice(h * REC_DK, (h + 1) * REC_DK)
        o = of_ref[:, cols] + ob_ref[:, cols]
        o = o * lax.rsqrt(jnp.mean(o * o, axis=-1, keepdims=True) + RMS_EPS)
        rec_out = (o * rg_ref[:, cols] * silu_g[:, cols]).astype(BF16)
        mix = mix + _dot(rec_out, wout_ref[CONV_WIDTH + h * REC_DK:CONV_WIDTH + (h + 1) * REC_DK, :])

    x1 = _layer_norm(alpha * x + g1 * mix) * l1g_ref[...] + l1b_ref[...]
    x1_ref[...] = x1
    h2 = _layer_norm(x1) * (1.0 + sc2) + sh2
    _store_row_tiles(h2_ref, h2)

    h_hi = h2.astype(BF16)
    h_lo = (h2 - h_hi.astype(F32)).astype(BF16)
    logits = (_dot(h_hi, rwh_ref[...]) + _dot(h_lo, rwh_ref[...]) + _dot(h_hi, rwl_ref[...])
              + rb_ref[...])
    lane = lax.broadcasted_iota(jnp.int32, logits.shape, 1)
    lane_f = lane.astype(F32)
    neg_inf = jnp.float32(-jnp.inf)
    work = jnp.where(lane < N_EXPERTS, logits, neg_inf)
    vals, sels = [], []
    for _ in range(TOP_K):
        m = jnp.max(work, axis=-1, keepdims=True)
        idx = jnp.min(jnp.where(work == m, lane_f, float(LANES)), axis=-1, keepdims=True)
        sel = lane_f == idx
        vals.append(m)
        sels.append(sel)
        work = jnp.where(sel, neg_inf, work)
    exps = [jnp.exp(v - vals[0]) for v in vals]
    denom = exps[0] + exps[1] + exps[2] + exps[3]

    sel_any = jnp.zeros(logits.shape, F32)
    for sel in sels:
        sel_any = jnp.where(sel, 1.0, sel_any)
    run_old = run_ref[...]
    prior = _dot(tri_ref[...], sel_any.astype(BF16)) + run_old
    run_new = run_old + jnp.sum(sel_any, axis=0, keepdims=True)
    run_ref[...] = run_new
    cnt_ref[...] = run_new
    start_ref[0] = run_old

    slot = jnp.zeros(logits.shape, F32)
    w4 = jnp.zeros(logits.shape, F32)
    for j in range(TOP_K):
        slot = jnp.where(sels[j], float(j), slot)
        w4 = jnp.where(lane == j, exps[j] / denom, w4)
    routed = sel_any > 0.0
    rank_t_ref[...] = jnp.where(routed, prior, -1.0).T[:N_EXPERTS]
    slot_t_ref[...] = jnp.where(routed, slot, -1.0).T[:N_EXPERTS]
    w4_ref[...] = w4


def _post(o_f, o_b, rec, conv, xp, xs, mod3, w_out_bf, rec_g, ln1_g, ln1_b, rw_hi, rw_lo, rb,
          alpha, sample_seq):
    n_p, n_s = xp.shape[0], xs.shape[0]
    n_tok = n_p + n_s
    npt = n_p // TOK_TILE
    tiles_per_sample_seq = sample_seq // TOK_TILE
    tri = jnp.asarray(np.tril(np.ones((TOK_TILE, TOK_TILE), np.float32), -1), BF16)

    def mod_row(i):
        return jnp.where(i < npt, 0, 1 + (i - npt) // tiles_per_sample_seq)

    const = lambda i: (0, 0)
    tok = lambda w: pl.BlockSpec((TOK_TILE, w), lambda i: (i, 0))
    return pl.pallas_call(
        functools.partial(_post_kernel, npt, alpha),
        out_shape=(jax.ShapeDtypeStruct((n_tok, D_MODEL), F32),
                   jax.ShapeDtypeStruct((n_tok * SUBLANES, LANES), F32),
                   jax.ShapeDtypeStruct((N_EXPERTS, n_tok), F32),
                   jax.ShapeDtypeStruct((N_EXPERTS, n_tok), F32),
                   jax.ShapeDtypeStruct((n_tok, LANES), F32),
                   jax.ShapeDtypeStruct((1, LANES), F32),
                   jax.ShapeDtypeStruct((n_tok // TOK_TILE, 1, LANES), F32)),
        grid=(n_tok // TOK_TILE,),
        in_specs=[tok(REC_WIDTH), tok(REC_WIDTH),
                  pl.BlockSpec((TOK_TILE, REC_WIDTH), lambda i: (i, 4)),
                  tok(CONV_WIDTH),
                  pl.BlockSpec((TOK_TILE, D_MODEL), lambda i: (jnp.minimum(i, npt - 1), 0)),
                  pl.BlockSpec((TOK_TILE, D_MODEL), lambda i: (jnp.maximum(i - npt, 0), 0)),
                  pl.BlockSpec((1, 1, 6 * D_MODEL), lambda i: (mod_row(i), 0, 0)),
                  pl.BlockSpec(w_out_bf.shape, const),
                  pl.BlockSpec((1, REC_WIDTH), const),
                  pl.BlockSpec((1, D_MODEL), const), pl.BlockSpec((1, D_MODEL), const),
                  pl.BlockSpec((D_MODEL, LANES), const), pl.BlockSpec((D_MODEL, LANES), const),
                  pl.BlockSpec((1, LANES), const),
                  pl.BlockSpec((TOK_TILE, TOK_TILE), const)],
        out_specs=(tok(D_MODEL), pl.BlockSpec((TOK_TILE * SUBLANES, LANES), lambda i: (i, 0)),
                   pl.BlockSpec((N_EXPERTS, TOK_TILE), lambda i: (0, i)),
                   pl.BlockSpec((N_EXPERTS, TOK_TILE), lambda i: (0, i)),
                   tok(LANES), pl.BlockSpec((1, LANES), const),
                   pl.BlockSpec((1, 1, LANES), lambda i: (i, 0, 0))),
        scratch_shapes=[pltpu.VMEM((1, LANES), F32)],
        compiler_params=pltpu.CompilerParams(dimension_semantics=("arbitrary",),
                                             vmem_limit_bytes=VMEM_LIMIT),
        name="post_mixer_router",
    )(o_f, o_b, rec, conv, xp, xs, mod3, w_out_bf, rec_g, ln1_g, ln1_b, rw_hi, rw_lo, rb, tri)


PLAN_ROWS = 32


def _plan_kernel(n_tok, tile_e, gstart, n_act, cum, starts_ref, ends_ref, rank_t_ref, slot_t_ref,
                 src_ref, dst_ref, acc_ref):
    i = pl.program_id(0)

    @pl.when(i < n_act[0])
    def _():
        e = tile_e[i]
        base = i * EXP_TILE - gstart[e]
        rows_iota = lax.broadcasted_iota(jnp.int32, (PLAN_ROWS, 1), 0)
        lane = lax.broadcasted_iota(jnp.int32, (1, TOK_TILE), 1)
        acc_ref[...] = jnp.zeros_like(acc_ref)

        base_f = base.astype(F32)
        win0 = jnp.sum((ends_ref[pl.ds(e, 1), :] <= base_f).astype(jnp.int32))
        win1 = jnp.sum((starts_ref[pl.ds(e, 1), :] < base_f + EXP_TILE).astype(jnp.int32))

        def window(b, carry):
            tok0 = pl.multiple_of(b * TOK_TILE, TOK_TILE)
            rk = rank_t_ref[pl.ds(e, 1), pl.ds(tok0, TOK_TILE)]
            sl = slot_t_ref[pl.ds(e, 1), pl.ds(tok0, TOK_TILE)]
            code = ((tok0 + lane) * TOP_K).astype(F32) + sl + 1.0
            lo = jnp.maximum(cum[b * N_EXPERTS + e] - base, 0)
            hi = jnp.minimum(cum[(b + 1) * N_EXPERTS + e] - base, EXP_TILE)

            def rows(k, c):
                r0 = pl.multiple_of(k * PLAN_ROWS, PLAN_ROWS)
                want = (base + r0 + rows_iota).astype(F32)
                hit = jnp.where(rk == want, code, 0.0)
                part = hit[:, 0:LANES]
                for c0 in range(LANES, TOK_TILE, LANES):
                    part = part + hit[:, c0:c0 + LANES]
                acc_ref[pl.ds(r0, PLAN_ROWS), :] += part
                return c

            shift = PLAN_ROWS.bit_length() - 1
            lax.fori_loop(lo >> shift, (hi + PLAN_ROWS - 1) >> shift, rows, 0)
            return carry

        lax.fori_loop(win0, win1, window, 0)
        code = jnp.sum(acc_ref[...].T, axis=0, keepdims=True).astype(jnp.int32) - 1
        pos = lax.broadcasted_iota(jnp.int32, (1, EXP_TILE), 1)
        tok = code >> 2
        src = jnp.where(code >= 0, tok, lax.rem(i * EXP_TILE + pos, n_tok))
        spare = TOP_K * n_tok + lax.rem(i, N_YBUF) * EXP_TILE + pos
        dst = jnp.where(code >= 0, (code & (TOP_K - 1)) * n_tok + tok, spare)
        src_ref[0] = src * SUBLANES
        dst_ref[0] = dst * SUBLANES

    @pl.when(i >= n_act[0])
    def _():
        src_ref[...] = jnp.zeros_like(src_ref)
        dst_ref[...] = jnp.zeros_like(dst_ref)


def _plan(rank_t, slot_t, tile_e, gstart, n_act, cum2d):
    n_tok = rank_t.shape[1]
    n_tiles = tile_e.shape[0]
    n_tok_tiles = cum2d.shape[0] - 1
    assert n_tok_tiles <= LANES
    huge = jnp.float32(2.0 ** 30)
    as_rows = lambda a: jnp.pad(a.T.astype(F32), ((0, 0), (0, LANES - n_tok_tiles)),
                                constant_values=huge)
    whole = lambda i, *_: (0, 0)
    per_tile = pl.BlockSpec((1, 1, EXP_TILE), lambda i, *_: (i, 0, 0))
    grid_spec = pltpu.PrefetchScalarGridSpec(
        num_scalar_prefetch=4,
        grid=(n_tiles,),
        in_specs=[pl.BlockSpec((N_EXPERTS, LANES), whole), pl.BlockSpec((N_EXPERTS, LANES), whole),
                  pl.BlockSpec(rank_t.shape, whole), pl.BlockSpec(slot_t.shape, whole)],
        out_specs=(per_tile, per_tile),
        scratch_shapes=[pltpu.VMEM((EXP_TILE, LANES), F32)],
    )
    shape = jax.ShapeDtypeStruct((n_tiles, 1, EXP_TILE), jnp.int32)
    return pl.pallas_call(
        functools.partial(_plan_kernel, n_tok),
        out_shape=(shape, shape),
        grid_spec=grid_spec,
        compiler_params=pltpu.CompilerParams(dimension_semantics=("arbitrary",),
                                             vmem_limit_bytes=VMEM_LIMIT),
        name="moe_plan",
    )(tile_e, gstart, n_act, cum2d.reshape(-1), as_rows(cum2d[:-1]), as_rows(cum2d[1:]),
      rank_t, slot_t)


N_YBUF = 3
DMA_UNROLL = 8


def _expert_kernel(tile_e, n_act, idx_ref, h_ref, wgu_ref, bgu_ref, wd_ref, bd_ref, yk_ref, xbuf,
                   ybuf, gsem, ssem, wgu_bf, wd_bf):
    i = pl.program_id(0)
    n_steps = pl.num_programs(0)
    n_active = n_act[0]
    tile_rows = EXP_TILE * SUBLANES
    n_slot_rows = yk_ref.shape[0] - N_YBUF * tile_rows

    def row_tile(ref, start):
        return ref.at[pl.ds(pl.multiple_of(start, SUBLANES), SUBLANES)]

    def for_each_row(fn):
        def body(g, carry):
            for k in range(DMA_UNROLL):
                fn(g * DMA_UNROLL + k)
            return carry
        lax.fori_loop(0, EXP_TILE // DMA_UNROLL, body, 0)

    def on_slot(slot, n, fn):
        if isinstance(slot, int):
            fn(slot)
            return
        for s in range(n):
            pl.when(slot == s)(functools.partial(fn, s))

    def gather_rows(which, slot):
        def run(s):
            def one(r):
                pltpu.make_async_copy(row_tile(h_ref, idx_ref[0, 0, which * EXP_TILE + r]),
                                      row_tile(xbuf.at[s], r * SUBLANES), gsem.at[s]).start()
            for_each_row(one)
        on_slot(slot, 2, run)

    def scatter_rows(slot):
        def run(s):
            def one(r):
                pltpu.make_async_copy(row_tile(ybuf.at[s], r * SUBLANES),
                                      row_tile(yk_ref, idx_ref[0, 0, 2 * EXP_TILE + r]),
                                      ssem.at[s]).start()
            for_each_row(one)
        on_slot(slot, N_YBUF, run)

    def scatter_wait(slot):
        pltpu.make_async_copy(ybuf.at[slot], yk_ref.at[pl.ds(0, tile_rows)], ssem.at[slot]).wait()

    @pl.when(i == 0)
    def _():
        for s in range(N_YBUF):
            ybuf[s] = jnp.zeros((tile_rows, LANES), F32)
        for s in range(N_YBUF - 1):
            pltpu.make_async_copy(
                ybuf.at[s], yk_ref.at[pl.ds(n_slot_rows + s * tile_rows, tile_rows)],
                ssem.at[s]).start()
        gather_rows(0, 0)

    prev = tile_e[jnp.maximum(i - 1, 0)]
    new_expert = jnp.logical_or(i == 0, tile_e[i] != prev)

    @pl.when(jnp.logical_and(new_expert, i < n_active))
    def _():
        rows = 128

        def cast(r, carry):
            sl = pl.ds(pl.multiple_of(r * rows, rows), rows)
            wgu_bf[sl, :] = wgu_ref[0, sl, :].astype(BF16)
            wd_bf[sl, :] = wd_ref[0, sl, :].astype(BF16)
            return carry

        lax.fori_loop(0, D_MODEL // rows, cast, 0)

    @pl.when(i < n_active)
    def _():
        xslot = i % 2
        yslot = i % N_YBUF
        pltpu.make_async_copy(h_ref.at[pl.ds(0, tile_rows)], xbuf.at[xslot], gsem.at[xslot]).wait()

        scatter_wait(yslot)
        x = _load_row_tiles(xbuf.at[xslot], EXP_TILE).astype(BF16)
        next_x = xbuf.at[(i + 1) % 2]
        next_sem = gsem.at[(i + 1) % 2]
        prev_y = ybuf.at[(i + N_YBUF - 1) % N_YBUF]
        prev_sem = ssem.at[(i + N_YBUF - 1) % N_YBUF]
        gu = _dot(x, wgu_bf[...]) + bgu_ref[0]
        for r in range(EXP_TILE):
            pltpu.make_async_copy(row_tile(h_ref, idx_ref[0, 0, EXP_TILE + r]),
                                  row_tile(next_x, r * SUBLANES), next_sem).start(priority=r % 2)
        gate = jnp.minimum(gu[:, :D_FF], SWIGLU_LIMIT)
        up = jnp.clip(gu[:, D_FF:], -SWIGLU_LIMIT, SWIGLU_LIMIT)
        act = (up + 1.0) * gate * _sigmoid(SWIGLU_ALPHA * gate)
        for r in range(EXP_TILE):
            pltpu.make_async_copy(row_tile(prev_y, r * SUBLANES),
                                  row_tile(yk_ref, idx_ref[0, 0, 3 * EXP_TILE + r]),
                                  prev_sem).start(priority=r % 2)
        _store_row_tiles(ybuf.at[yslot], _dot(act.astype(BF16), wd_bf[...]) + bd_ref[0])

        @pl.when(i == n_active - 1)
        def _():
            scatter_rows(yslot)

    @pl.when(i == n_steps - 1)
    def _():
        for back in range(1, N_YBUF + 1):
            scatter_wait((n_active - back) % N_YBUF)
        last_x = n_active % 2
        pltpu.make_async_copy(h_ref.at[pl.ds(0, tile_rows)], xbuf.at[last_x], gsem.at[last_x]).wait()


def _experts(h2, src3, dst3, tile_e, n_act, w_gate_up, b_gate_up, w_down, b_down):
    n_tok = h2.shape[0] // SUBLANES
    n_tiles = src3.shape[0]
    src_next = jnp.concatenate([src3[1:], src3[-1:]], axis=0)
    spare_rows = TOP_K * n_tok + (N_YBUF - 1) * EXP_TILE + jnp.arange(EXP_TILE, dtype=jnp.int32)
    dst_prev = jnp.concatenate([(spare_rows * SUBLANES).reshape(1, 1, EXP_TILE), dst3[:-1]], axis=0)
    idx3 = jnp.concatenate([src3, src_next, dst3, dst_prev], axis=-1)
    grid_spec = pltpu.PrefetchScalarGridSpec(
        num_scalar_prefetch=2,
        grid=(n_tiles,),
        in_specs=[pl.BlockSpec((1, 1, 4 * EXP_TILE), lambda i, te, na: (i, 0, 0),
                               memory_space=pltpu.SMEM),
                  pl.BlockSpec(memory_space=pl.ANY),
                  pl.BlockSpec((1, D_MODEL, 2 * D_FF), lambda i, te, na: (te[i], 0, 0)),
                  pl.BlockSpec((1, 1, 2 * D_FF), lambda i, te, na: (te[i], 0, 0)),
                  pl.BlockSpec((1, D_FF, D_MODEL), lambda i, te, na: (te[i], 0, 0)),
                  pl.BlockSpec((1, 1, D_MODEL), lambda i, te, na: (te[i], 0, 0))],
        out_specs=pl.BlockSpec(memory_space=pl.ANY),
        scratch_shapes=[pltpu.VMEM((2, EXP_TILE * SUBLANES, LANES), F32),
                        pltpu.VMEM((N_YBUF, EXP_TILE * SUBLANES, LANES), F32),
                        pltpu.SemaphoreType.DMA((2,)), pltpu.SemaphoreType.DMA((N_YBUF,)),
                        pltpu.VMEM((D_MODEL, 2 * D_FF), BF16), pltpu.VMEM((D_FF, D_MODEL), BF16)],
    )
    return pl.pallas_call(
        _expert_kernel,
        out_shape=jax.ShapeDtypeStruct(((TOP_K * n_tok + N_YBUF * EXP_TILE) * SUBLANES, LANES), F32),
        grid_spec=grid_spec,
        compiler_params=pltpu.CompilerParams(dimension_semantics=("arbitrary",),
                                             vmem_limit_bytes=VMEM_LIMIT),
        name="moe_experts",
    )(tile_e, n_act, idx3, h2, w_gate_up, b_gate_up.reshape(N_EXPERTS, 1, 2 * D_FF),
      w_down, b_down.reshape(N_EXPERTS, 1, D_MODEL))


COMBINE_TILE = 512


def _combine_kernel(n_prompt_tiles, alpha, y0_ref, y1_ref, y2_ref, y3_ref, x1_ref, w4_ref, mod_ref,
                    l2g_ref, l2b_ref, yp_ref, ysm_ref):
    i = pl.program_id(0)
    w4 = w4_ref[...]
    ff = _load_row_tiles(y0_ref, COMBINE_TILE) * w4[:, 0:1]
    for j, y_ref in enumerate((y1_ref, y2_ref, y3_ref), start=1):
        ff = ff + _load_row_tiles(y_ref, COMBINE_TILE) * w4[:, j:j + 1]
    g2 = mod_ref[0][:, 5 * D_MODEL:6 * D_MODEL]
    out = _layer_norm(alpha * x1_ref[...] + g2 * ff) * l2g_ref[...] + l2b_ref[...]

    @pl.when(i < n_prompt_tiles)
    def _():
        yp_ref[...] = out

    @pl.when(i >= n_prompt_tiles)
    def _():
        ysm_ref[...] = out


def _combine(yk, x1, w4, mod3, ln2_g, ln2_b, n_p, alpha, sample_seq):
    n_tok = x1.shape[0]
    n_s = n_tok - n_p
    assert n_p % COMBINE_TILE == 0 and sample_seq % COMBINE_TILE == 0
    npt = n_p // COMBINE_TILE
    n_tok_tiles = n_tok // COMBINE_TILE
    tiles_per_sample_seq = sample_seq // COMBINE_TILE

    def mod_row(i):
        return jnp.where(i < npt, 0, 1 + (i - npt) // tiles_per_sample_seq)

    def slot(j):
        return pl.BlockSpec((COMBINE_TILE * SUBLANES, LANES), lambda i: (j * n_tok_tiles + i, 0))

    const = lambda i: (0, 0)
    return pl.pallas_call(
        functools.partial(_combine_kernel, npt, alpha),
        out_shape=(jax.ShapeDtypeStruct((n_p, D_MODEL), F32),
                   jax.ShapeDtypeStruct((n_s, D_MODEL), F32)),
        grid=(n_tok_tiles,),
        in_specs=[slot(0), slot(1), slot(2), slot(3),
                  pl.BlockSpec((COMBINE_TILE, D_MODEL), lambda i: (i, 0)),
                  pl.BlockSpec((COMBINE_TILE, LANES), lambda i: (i, 0)),
                  pl.BlockSpec((1, 1, 6 * D_MODEL), lambda i: (mod_row(i), 0, 0)),
                  pl.BlockSpec((1, D_MODEL), const), pl.BlockSpec((1, D_MODEL), const)],
        out_specs=(pl.BlockSpec((COMBINE_TILE, D_MODEL), lambda i: (jnp.minimum(i, npt - 1), 0)),
                   pl.BlockSpec((COMBINE_TILE, D_MODEL), lambda i: (jnp.maximum(i - npt, 0), 0))),
        compiler_params=pltpu.CompilerParams(dimension_semantics=("arbitrary",),
                                             vmem_limit_bytes=VMEM_LIMIT),
        name="moe_combine",
    )(yk, yk, yk, yk, x1, w4, mod3, ln2_g, ln2_b)


def kernel(x_prompt, x_sample, c, state_fwd, state_bwd, c_ctx, w_ada, b_ada, w_in, conv_w, conv_b,
           conv_ln_g, conv_ln_b, lb_logits, rec_norm_g, w_out, ln1_g, ln1_b, ln2_g, ln2_b,
           router_w, router_b, w_gate_up, b_gate_up, w_down, b_down):
    depth = w_ada.shape[0]
    assert depth == 1
    alpha = (2.0 * depth) ** 0.25
    nb_p, seq_p, _ = x_prompt.shape
    nb_s, seq_s, _ = x_sample.shape
    n_p, n_s = nb_p * seq_p, nb_s * seq_s
    n_tok = n_p + n_s
    row2 = lambda a: a.reshape(1, -1)

    lb = jnp.cumsum(jax.nn.softmax(lb_logits.astype(F32), axis=0), axis=0)[0]

    cond = jnp.concatenate([c_ctx[None, :], c, jnp.zeros((16 - 1 - nb_s, D_MODEL), F32)], axis=0)
    mod = _ada(cond, w_ada[0], b_ada[0])
    mod3 = mod.reshape(16, 1, 6 * D_MODEL)

    xp = x_prompt.reshape(n_p, D_MODEL)
    xs = x_sample.reshape(n_s, D_MODEL)
    conv_w_pad = jnp.concatenate([conv_w[0], jnp.zeros((1, CONV_WIDTH), F32)], axis=0)
    conv_out, rec = _inproj(xp, xs, mod3, w_in[0].astype(BF16), conv_w_pad, row2(conv_b[0]),
                            row2(conv_ln_g[0]), row2(conv_ln_b[0]), seq_p, seq_s)

    zeros_state = jnp.zeros((nb_p, REC_HEADS, REC_DK, REC_DK), F32)
    s0f = jnp.concatenate([zeros_state, jnp.swapaxes(state_fwd[:, 0], -1, -2)], axis=0)
    s0b = jnp.concatenate([zeros_state, jnp.swapaxes(state_bwd[:, 0], -1, -2)], axis=0)
    o_f, o_b, sf_t, sb_t = _scan(rec, lb, s0f, s0b, [seq_p] * nb_p + [seq_s] * nb_s)
    new_f = jnp.swapaxes(sf_t[:nb_p], -1, -2)[:, None]
    new_b = jnp.swapaxes(sb_t[:nb_p], -1, -2)[:, None]

    rw = jnp.pad(router_w[0], ((0, 0), (0, LANES - N_EXPERTS)))
    rw_hi = rw.astype(BF16)
    rw_lo = (rw - rw_hi.astype(F32)).astype(BF16)
    rb = jnp.pad(router_b[0], (0, LANES - N_EXPERTS)).reshape(1, LANES)
    x1, h2, rank_t, slot_t, w4, counts, tile_start = _post(
        o_f, o_b, rec, conv_out, xp, xs, mod3, w_out[0].astype(BF16), row2(rec_norm_g[0]),
        row2(ln1_g[0]), row2(ln1_b[0]), rw_hi, rw_lo, rb, alpha, seq_s)

    cnt = counts[0, :N_EXPERTS].astype(jnp.int32)
    gpad = ((cnt + EXP_TILE - 1) // EXP_TILE) * EXP_TILE
    gend = jnp.cumsum(gpad)
    gstart = gend - gpad
    n_rows = n_tok * TOP_K + N_EXPERTS * EXP_TILE
    n_tiles = n_rows // EXP_TILE
    n_act = gend[-1] // EXP_TILE
    tile_ids = jnp.minimum(jnp.arange(n_tiles, dtype=jnp.int32), n_act - 1)
    tile_e = jnp.sum((gend[None, :] <= tile_ids[:, None] * EXP_TILE).astype(jnp.int32), axis=1)
    n_act1 = n_act.reshape(1).astype(jnp.int32)
    starts = tile_start[:, 0, :N_EXPERTS].astype(jnp.int32)
    cum2d = jnp.concatenate([starts, cnt[None, :]], axis=0)
    src, dst = _plan(rank_t, slot_t, tile_e, gstart.astype(jnp.int32), n_act1, cum2d)
    yk = _experts(h2, src, dst, tile_e, n_act1, w_gate_up[0], b_gate_up[0], w_down[0], b_down[0])
    y_p, y_s = _combine(yk, x1, w4, mod3, row2(ln2_g[0]), row2(ln2_b[0]), n_p, alpha, seq_s)
    return (y_p.reshape(nb_p, seq_p, D_MODEL), y_s.reshape(nb_s, seq_s, D_MODEL), new_f, new_b)
```

```python
import functools

import numpy as np
import jax
import jax.numpy as jnp
from jax import lax
from jax.experimental import pallas as pl
from jax.experimental.pallas import tpu as pltpu

F32 = jnp.float32
BF16 = jnp.bfloat16

D_MODEL = 1024
CONV_WIDTH = 512
CONV_TAPS = 31
REC_HEADS = 4
REC_DK = 128
REC_WIDTH = REC_HEADS * REC_DK
REC_COLS = 5 * REC_WIDTH
CHUNK = 64
N_LEVELS = 6
SCAN_CHUNKS = 4
N_EXPERTS = 32
TOP_K = 4
D_FF = 1024
SWIGLU_LIMIT = 7.0
SWIGLU_ALPHA = 1.702
LN_EPS = 1e-5
RMS_EPS = 1e-6
GRID_W = 64

LANES = 128
TOK_TILE = 512
EXP_TILE = 512
VMEM_LIMIT = 56 * 1024 * 1024


def _sigmoid(x):
    return 1.0 / (1.0 + jnp.exp(-x))


def _layer_norm(x):
    mu = jnp.mean(x, axis=-1, keepdims=True)
    xc = x - mu
    var = jnp.mean(xc * xc, axis=-1, keepdims=True)
    return xc * lax.rsqrt(var + LN_EPS)


def _dot(a, b):
    return jnp.dot(a, b, preferred_element_type=F32)


def _dot_nt(a, b):
    return lax.dot_general(a, b, (((1,), (1,)), ((), ())), preferred_element_type=F32)


SUBLANES = 8
assert D_MODEL == SUBLANES * LANES


def _store_row_tiles(ref, value):
    for s in range(SUBLANES):
        ref[pl.ds(s, value.shape[0], stride=SUBLANES), :] = value[:, s * LANES:(s + 1) * LANES]


def _load_row_tiles(ref, n_rows):
    return jnp.concatenate(
        [ref[pl.ds(s, n_rows, stride=SUBLANES), :] for s in range(SUBLANES)], axis=-1)


def _split3(x):
    hi = x.astype(BF16)
    r1 = x - hi.astype(F32)
    mid = r1.astype(BF16)
    lo = (r1 - mid.astype(F32)).astype(BF16)
    return hi, mid, lo


def _ada_kernel(c_ref, w_ref, b_ref, o_ref):
    c = c_ref[...]
    s = (c * _sigmoid(c)).astype(BF16)
    o_ref[...] = _dot(s, w_ref[...].astype(BF16)) + b_ref[...]


def _ada(cond, w_ada, b_ada):
    rows = cond.shape[0]
    ncol = w_ada.shape[1]
    tn = 1024
    return pl.pallas_call(
        _ada_kernel,
        out_shape=jax.ShapeDtypeStruct((rows, ncol), F32),
        grid=(ncol // tn,),
        in_specs=[pl.BlockSpec((rows, D_MODEL), lambda j: (0, 0)),
                  pl.BlockSpec((D_MODEL, tn), lambda j: (0, j)),
                  pl.BlockSpec((1, tn), lambda j: (0, j))],
        out_specs=pl.BlockSpec((rows, tn), lambda j: (0, j)),
        compiler_params=pltpu.CompilerParams(dimension_semantics=("parallel",),
                                             vmem_limit_bytes=VMEM_LIMIT),
        name="ada_mod",
    )(cond, w_ada, b_ada.reshape(1, ncol))


def _inproj_kernel(n_prompt_tiles, prompt_seg, xp_ref, xs_ref, mod_ref, win_ref, cw_ref, cb_ref,
                   cg_ref, cbeta_ref, conv_ref, rec_ref):
    i = pl.program_id(0)
    is_prompt = i < n_prompt_tiles
    x = jnp.where(is_prompt, xp_ref[...], xs_ref[...])
    mod = mod_ref[0]
    sh1 = mod[:, 0:D_MODEL]
    sc1 = mod[:, D_MODEL:2 * D_MODEL]
    h = (_layer_norm(x) * (1.0 + sc1) + sh1).astype(BF16)

    glu = _dot(h, win_ref[:, 0:2 * CONV_WIDTH])
    u = glu[:, :CONV_WIDTH] * _sigmoid(glu[:, CONV_WIDTH:])

    seg_len = jnp.where(is_prompt, prompt_seg, GRID_W)
    half = CONV_TAPS // 2
    pos = lax.broadcasted_iota(jnp.int32, (TOK_TILE, 1), 0) & (seg_len - 1)

    def masked_shift(d):
        sh = pltpu.roll(u, (-d) % TOK_TILE, axis=0)
        edge_rows = SUBLANES * ((abs(d) + SUBLANES - 1) // SUBLANES)
        pieces = []
        for s0 in range(0, TOK_TILE, GRID_W):
            if d < 0:
                edge = slice(s0, s0 + edge_rows)
                pieces.append(jnp.where(pos[edge] >= -d, sh[edge], 0.0))
                pieces.append(sh[s0 + edge_rows:s0 + GRID_W])
            else:
                edge = slice(s0 + GRID_W - edge_rows, s0 + GRID_W)
                pieces.append(sh[s0:s0 + GRID_W - edge_rows])
                pieces.append(jnp.where(pos[edge] < seg_len - d, sh[edge], 0.0))
        return jnp.concatenate(pieces, axis=0)

    acc = u * cw_ref[half:half + 1, :] + cb_ref[...]
    taps = [j for j in range(CONV_TAPS) if j != half]
    n_groups = REC_COLS // REC_WIDTH
    per_group = len(taps) // n_groups
    anchor = None
    for g in range(n_groups):
        cols = slice(g * REC_WIDTH, (g + 1) * REC_WIDTH)
        rec = _dot(h, win_ref[:, 2 * CONV_WIDTH + g * REC_WIDTH:
                              2 * CONV_WIDTH + (g + 1) * REC_WIDTH])
        rec_ref[:, cols] = rec
        for j in taps[g * per_group:(g + 1) * per_group]:
            w_tap = cw_ref[j:j + 1, :]
            if anchor is not None:
                w_tap = w_tap + anchor
                anchor = None
            acc = acc + masked_shift(j - half) * w_tap
        if g + 1 < n_groups:
            bits = pltpu.bitcast(rec[0:SUBLANES, :], jnp.uint32)
            anchor = pltpu.bitcast((bits >> 16) >> 16, F32)[0:1, :]
    y = _layer_norm(acc) * cg_ref[...] + cbeta_ref[...]
    conv_ref[...] = (y * _sigmoid(y)).astype(BF16)


def _inproj(xp, xs, mod3, w_in_bf, conv_w, conv_b, conv_g, conv_beta, prompt_seq, sample_seq):
    n_p, n_s = xp.shape[0], xs.shape[0]
    n_tok = n_p + n_s
    npt = n_p // TOK_TILE
    tiles_per_sample_seq = sample_seq // TOK_TILE
    assert TOK_TILE % prompt_seq == 0 and n_p % TOK_TILE == 0 and sample_seq % TOK_TILE == 0

    def mod_row(i):
        return jnp.where(i < npt, 0, 1 + (i - npt) // tiles_per_sample_seq)

    const = lambda i: (0, 0)
    return pl.pallas_call(
        functools.partial(_inproj_kernel, npt, prompt_seq),
        out_shape=(jax.ShapeDtypeStruct((n_tok, CONV_WIDTH), BF16),
                   jax.ShapeDtypeStruct((n_tok, REC_COLS), F32)),
        grid=(n_tok // TOK_TILE,),
        in_specs=[pl.BlockSpec((TOK_TILE, D_MODEL), lambda i: (jnp.minimum(i, npt - 1), 0)),
                  pl.BlockSpec((TOK_TILE, D_MODEL), lambda i: (jnp.maximum(i - npt, 0), 0)),
                  pl.BlockSpec((1, 1, 6 * D_MODEL), lambda i: (mod_row(i), 0, 0)),
                  pl.BlockSpec(w_in_bf.shape, const),
                  pl.BlockSpec(conv_w.shape, const),
                  pl.BlockSpec((1, CONV_WIDTH), const),
                  pl.BlockSpec((1, CONV_WIDTH), const),
                  pl.BlockSpec((1, CONV_WIDTH), const)],
        out_specs=(pl.BlockSpec((TOK_TILE, CONV_WIDTH), lambda i: (i, 0)),
                   pl.BlockSpec((TOK_TILE, REC_COLS), lambda i: (i, 0))),
        compiler_params=pltpu.CompilerParams(dimension_semantics=("parallel",),
                                             vmem_limit_bytes=VMEM_LIMIT),
        name="inproj_conv",
    )(xp, xs, mod3, w_in_bf, conv_w, conv_b, conv_g, conv_beta)


def _scan_constants():
    c = CHUNK
    w = np.zeros((N_LEVELS + 2, c, c), np.float32)
    m = np.zeros((N_LEVELS + 1, c, c), np.float32)
    t = np.arange(c)
    for j in range(N_LEVELS):
        half = 1 << j
        for ti in range(c):
            ref = (ti & ~(2 * half - 1)) + half - 1
            if ti & half:
                w[j, ti, ref + 1:ti + 1] = 1.0
            else:
                w[j, ti, ti + 1:ref + 1] = 1.0
        upper = (t[:, None] & half) != 0
        lower = (t[None, :] & half) == 0
        same = (t[:, None] >> (j + 1)) == (t[None, :] >> (j + 1))
        m[j] = (upper & lower & same).astype(np.float32)
    w[N_LEVELS] = (t[None, :] <= t[:, None]).astype(np.float32)
    w[N_LEVELS + 1] = (t[None, :] > t[:, None]).astype(np.float32)
    m[N_LEVELS] = np.eye(c, dtype=np.float32)
    w_f = np.tile(w.reshape(-1, c), (1, 3))
    w_b = np.tile(w[:, ::-1, ::-1].reshape(-1, c), (1, 3))
    m_b = m[:, ::-1, ::-1]
    return (jnp.asarray(w_f, BF16), jnp.asarray(np.ascontiguousarray(w_b), BF16),
            jnp.asarray(m), jnp.asarray(np.ascontiguousarray(m_b)))


def _scan_gates(z, lb, w_ref):
    e = jnp.exp(-jnp.abs(z))
    r = 1.0 / (1.0 + e)
    a = e * r
    sig = jnp.where(z >= 0, r, a)
    sig_neg = jnp.where(z >= 0, a, r)
    one_m_lb = 1.0 - lb
    log_f = jnp.log2(lb + one_m_lb * sig)
    k = one_m_lb * sig_neg
    terms = jnp.concatenate(_split3(log_f), axis=0)
    ex = jnp.exp2(_dot(w_ref[...], terms))
    return k, ex


def _scan_intra(q, k, ex, m_ref, neighbour):
    c = CHUNK
    qb, kb = q.astype(BF16), k.astype(BF16)
    a_mat = m_ref[N_LEVELS] * jnp.sum(q * k, axis=-1, keepdims=True)
    x0 = ex[0:c]
    pair = jnp.sum((q * x0) * pltpu.roll(k * x0, neighbour, axis=0), axis=-1, keepdims=True)
    a_mat = a_mat + m_ref[0] * pair
    for j in range(1, N_LEVELS):
        xj = ex[j * c:(j + 1) * c].astype(BF16)
        a_mat = a_mat + m_ref[j] * _dot_nt(qb * xj, kb * xj)
    return a_mat


def _scan_kernel(fblk, bblk, first, seq, qf_ref, zf_ref, vf_ref, qb_ref, zb_ref, vb_ref, lb_ref,
                 wf_ref, wb_ref, mf_ref, mb_ref, s0f_ref, s0b_ref, of_ref, ob_ref, sf_ref, sb_ref):
    s = pl.program_id(0)

    @pl.when(first[s] == 1)
    def _():
        sf_ref[...] = s0f_ref[...]
        sb_ref[...] = s0b_ref[...]

    c = CHUNK
    dirs = ((qf_ref, zf_ref, vf_ref, 0, wf_ref, mf_ref, sf_ref, of_ref, c - 1,
             tuple(range(SCAN_CHUNKS))),
            (qb_ref, zb_ref, vb_ref, 1, wb_ref, mb_ref, sb_ref, ob_ref, 0,
             tuple(reversed(range(SCAN_CHUNKS)))))
    units = [(d, slice(sub * c, (sub + 1) * c)) for d in dirs for sub in d[-1]]
    gates = [_scan_gates(d[1][rows, :], lb_ref[d[3]:d[3] + 1, :], d[4]) for d, rows in units]
    intra = []
    for (d, rows), (k, ex) in zip(units, gates):
        for h in range(REC_HEADS):
            cols = slice(h * REC_DK, (h + 1) * REC_DK)
            neighbour = 1 if d[8] == c - 1 else c - 1
            intra.append(_scan_intra(d[0][rows, cols], k[:, cols], ex[:, cols], d[5], neighbour))
    idx = 0
    for (d, rows), (k, ex) in zip(units, gates):
        q_ref, _, v_ref, _, _, _, st_ref, o_ref, last, _ = d
        for h in range(REC_HEADS):
            cols = slice(h * REC_DK, (h + 1) * REC_DK)
            q, v = q_ref[rows, cols], v_ref[rows, cols]
            ex_b = ex[N_LEVELS * c:(N_LEVELS + 1) * c, cols]
            ex_l = ex[(N_LEVELS + 1) * c:(N_LEVELS + 2) * c, cols]
            st = st_ref[0, h]
            o_ref[rows, cols] = (_dot(intra[idx].astype(BF16), v.astype(BF16))
                                 + _dot_nt((q * ex_b).astype(BF16), st.astype(BF16)))
            st_ref[0, h] = (st * ex_b[last:last + 1, :]
                            + _dot(v.T.astype(BF16), (k[:, cols] * ex_l).astype(BF16)))
            idx += 1


def _scan(rec, lb, s0f_t, s0b_t, seq_lens):
    n_tok = rec.shape[0]
    fblk, bblk, first, seq = [], [], [], []
    base = 0
    step_rows = CHUNK * SCAN_CHUNKS
    for si, ln in enumerate(seq_lens):
        assert ln % step_rows == 0
        n = ln // step_rows
        for ci in range(n):
            fblk.append(base + ci)
            bblk.append(base + n - 1 - ci)
            first.append(1 if ci == 0 else 0)
            seq.append(si)
        base += n
    steps = len(fblk)
    as_i32 = lambda a: jnp.asarray(np.asarray(a, np.int32))
    w_f, w_b, m_f, m_b = _scan_constants()
    n_seq = len(seq_lens)

    def col(block_of, c):
        return pl.BlockSpec((step_rows, REC_WIDTH), lambda s, fb, bb, fi, sq: (block_of(fb, bb)[s], c))

    fwd = lambda fb, bb: fb
    bwd = lambda fb, bb: bb
    const2 = lambda s, fb, bb, fi, sq: (0, 0)
    const3 = lambda s, fb, bb, fi, sq: (0, 0, 0)
    state_spec = pl.BlockSpec((1, REC_HEADS, REC_DK, REC_DK), lambda s, fb, bb, fi, sq: (sq[s], 0, 0, 0))
    grid_spec = pltpu.PrefetchScalarGridSpec(
        num_scalar_prefetch=4,
        grid=(steps,),
        in_specs=[col(fwd, 0), col(fwd, 1), col(fwd, 3), col(bwd, 0), col(bwd, 2), col(bwd, 3),
                  pl.BlockSpec((2, REC_WIDTH), const2),
                  pl.BlockSpec(w_f.shape, const2), pl.BlockSpec(w_b.shape, const2),
                  pl.BlockSpec(m_f.shape, const3), pl.BlockSpec(m_b.shape, const3),
                  state_spec, state_spec],
        out_specs=(pl.BlockSpec((step_rows, REC_WIDTH), lambda s, fb, bb, fi, sq: (fb[s], 0)),
                   pl.BlockSpec((step_rows, REC_WIDTH), lambda s, fb, bb, fi, sq: (bb[s], 0)),
                   state_spec, state_spec),
    )
    st_shape = jax.ShapeDtypeStruct((n_seq, REC_HEADS, REC_DK, REC_DK), F32)
    return pl.pallas_call(
        _scan_kernel,
        out_shape=(jax.ShapeDtypeStruct((n_tok, REC_WIDTH), F32),
                   jax.ShapeDtypeStruct((n_tok, REC_WIDTH), F32), st_shape, st_shape),
        grid_spec=grid_spec,
        compiler_params=pltpu.CompilerParams(dimension_semantics=("arbitrary",),
                                             vmem_limit_bytes=VMEM_LIMIT),
        name="hgrn2_scan",
    )(as_i32(fblk), as_i32(bblk), as_i32(first), as_i32(seq),
      rec, rec, rec, rec, rec, rec, lb, w_f, w_b, m_f, m_b, s0f_t, s0b_t)


def _post_kernel(n_prompt_tiles, alpha, of_ref, ob_ref, g_ref, conv_ref, xp_ref, xs_ref, mod_ref,
                 wout_ref, rg_ref, l1g_ref, l1b_ref, rwh_ref, rwl_ref, rb_ref, tri_ref,
                 x1_ref, h2_ref, rank_t_ref, slot_t_ref, w4_ref, cnt_ref, start_ref, run_ref):
    i = pl.program_id(0)

    @pl.when(i == 0)
    def _():
        run_ref[...] = jnp.zeros_like(run_ref)

    x = jnp.where(i < n_prompt_tiles, xp_ref[...], xs_ref[...])
    mod = mod_ref[0]
    g1 = mod[:, 2 * D_MODEL:3 * D_MODEL]
    sh2 = mod[:, 3 * D_MODEL:4 * D_MODEL]
    sc2 = mod[:, 4 * D_MODEL:5 * D_MODEL]

    g = g_ref[...]
    silu_g = g * _sigmoid(g)
    mix = _dot(conv_ref[...], wout_ref[0:CONV_WIDTH, :])
    for h in range(REC_HEADS):
        cols = slice(h * REC_DK, (h + 1) * REC_DK)
        o = of_ref[:, cols] + ob_ref[:, cols]
        o = o * lax.rsqrt(jnp.mean(o * o, axis=-1, keepdims=True) + RMS_EPS)
        rec_out = (o * rg_ref[:, cols] * silu_g[:, cols]).astype(BF16)
        mix = mix + _dot(rec_out, wout_ref[CONV_WIDTH + h * REC_DK:CONV_WIDTH + (h + 1) * REC_DK, :])

    x1 = _layer_norm(alpha * x + g1 * mix) * l1g_ref[...] + l1b_ref[...]
    x1_ref[...] = x1
    h2 = _layer_norm(x1) * (1.0 + sc2) + sh2
    _store_row_tiles(h2_ref, h2)

    h_hi = h2.astype(BF16)
    h_lo = (h2 - h_hi.astype(F32)).astype(BF16)
    logits = (_dot(h_hi, rwh_ref[...]) + _dot(h_lo, rwh_ref[...]) + _dot(h_hi, rwl_ref[...])
              + rb_ref[...])
    lane = lax.broadcasted_iota(jnp.int32, logits.shape, 1)
    lane_f = lane.astype(F32)
    neg_inf = jnp.float32(-jnp.inf)
    work = jnp.where(lane < N_EXPERTS, logits, neg_inf)
    vals, sels = [], []
    for _ in range(TOP_K):
        m = jnp.max(work, axis=-1, keepdims=True)
        idx = jnp.min(jnp.where(work == m, lane_f, float(LANES)), axis=-1, keepdims=True)
        sel = lane_f == idx
        vals.append(m)
        sels.append(sel)
        work = jnp.where(sel, neg_inf, work)
    exps = [jnp.exp(v - vals[0]) for v in vals]
    denom = exps[0] + exps[1] + exps[2] + exps[3]

    sel_any = jnp.zeros(logits.shape, F32)
    for sel in sels:
        sel_any = jnp.where(sel, 1.0, sel_any)
    run_old = run_ref[...]
    prior = _dot(tri_ref[...], sel_any.astype(BF16)) + run_old
    run_new = run_old + jnp.sum(sel_any, axis=0, keepdims=True)
    run_ref[...] = run_new
    cnt_ref[...] = run_new
    start_ref[0] = run_old

    slot = jnp.zeros(logits.shape, F32)
    w4 = jnp.zeros(logits.shape, F32)
    for j in range(TOP_K):
        slot = jnp.where(sels[j], float(j), slot)
        w4 = jnp.where(lane == j, exps[j] / denom, w4)
    routed = sel_any > 0.0
    rank_t_ref[...] = jnp.where(routed, prior, -1.0).T[:N_EXPERTS]
    slot_t_ref[...] = jnp.where(routed, slot, -1.0).T[:N_EXPERTS]
    w4_ref[...] = w4


def _post(o_f, o_b, rec, conv, xp, xs, mod3, w_out_bf, rec_g, ln1_g, ln1_b, rw_hi, rw_lo, rb,
          alpha, sample_seq):
    n_p, n_s = xp.shape[0], xs.shape[0]
    n_tok = n_p + n_s
    npt = n_p // TOK_TILE
    tiles_per_sample_seq = sample_seq // TOK_TILE
    tri = jnp.asarray(np.tril(np.ones((TOK_TILE, TOK_TILE), np.float32), -1), BF16)

    def mod_row(i):
        return jnp.where(i < npt, 0, 1 + (i - npt) // tiles_per_sample_seq)

    const = lambda i: (0, 0)
    tok = lambda w: pl.BlockSpec((TOK_TILE, w), lambda i: (i, 0))
    return pl.pallas_call(
        functools.partial(_post_kernel, npt, alpha),
        out_shape=(jax.ShapeDtypeStruct((n_tok, D_MODEL), F32),
                   jax.ShapeDtypeStruct((n_tok * SUBLANES, LANES), F32),
                   jax.ShapeDtypeStruct((N_EXPERTS, n_tok), F32),
                   jax.ShapeDtypeStruct((N_EXPERTS, n_tok), F32),
                   jax.ShapeDtypeStruct((n_tok, LANES), F32),
                   jax.ShapeDtypeStruct((1, LANES), F32),
                   jax.ShapeDtypeStruct((n_tok // TOK_TILE, 1, LANES), F32)),
        grid=(n_tok // TOK_TILE,),
        in_specs=[tok(REC_WIDTH), tok(REC_WIDTH),
                  pl.BlockSpec((TOK_TILE, REC_WIDTH), lambda i: (i, 4)),
                  tok(CONV_WIDTH),
                  pl.BlockSpec((TOK_TILE, D_MODEL), lambda i: (jnp.minimum(i, npt - 1), 0)),
                  pl.BlockSpec((TOK_TILE, D_MODEL), lambda i: (jnp.maximum(i - npt, 0), 0)),
                  pl.BlockSpec((1, 1, 6 * D_MODEL), lambda i: (mod_row(i), 0, 0)),
                  pl.BlockSpec(w_out_bf.shape, const),
                  pl.BlockSpec((1, REC_WIDTH), const),
                  pl.BlockSpec((1, D_MODEL), const), pl.BlockSpec((1, D_MODEL), const),
                  pl.BlockSpec((D_MODEL, LANES), const), pl.BlockSpec((D_MODEL, LANES), const),
                  pl.BlockSpec((1, LANES), const),
                  pl.BlockSpec((TOK_TILE, TOK_TILE), const)],
        out_specs=(tok(D_MODEL), pl.BlockSpec((TOK_TILE * SUBLANES, LANES), lambda i: (i, 0)),
                   pl.BlockSpec((N_EXPERTS, TOK_TILE), lambda i: (0, i)),
                   pl.BlockSpec((N_EXPERTS, TOK_TILE), lambda i: (0, i)),
                   tok(LANES), pl.BlockSpec((1, LANES), const),
                   pl.BlockSpec((1, 1, LANES), lambda i: (i, 0, 0))),
        scratch_shapes=[pltpu.VMEM((1, LANES), F32)],
        compiler_params=pltpu.CompilerParams(dimension_semantics=("arbitrary",),
                                             vmem_limit_bytes=VMEM_LIMIT),
        name="post_mixer_router",
    )(o_f, o_b, rec, conv, xp, xs, mod3, w_out_bf, rec_g, ln1_g, ln1_b, rw_hi, rw_lo, rb, tri)


PLAN_ROWS = 32


def _plan_kernel(n_tok, tile_e, gstart, n_act, cum, starts_ref, ends_ref, rank_t_ref, slot_t_ref,
                 src_ref, dst_ref, acc_ref):
    i = pl.program_id(0)

    @pl.when(i < n_act[0])
    def _():
        e = tile_e[i]
        base = i * EXP_TILE - gstart[e]
        rows_iota = lax.broadcasted_iota(jnp.int32, (PLAN_ROWS, 1), 0)
        lane = lax.broadcasted_iota(jnp.int32, (1, TOK_TILE), 1)
        acc_ref[...] = jnp.zeros_like(acc_ref)

        base_f = base.astype(F32)
        win0 = jnp.sum((ends_ref[pl.ds(e, 1), :] <= base_f).astype(jnp.int32))
        win1 = jnp.sum((starts_ref[pl.ds(e, 1), :] < base_f + EXP_TILE).astype(jnp.int32))

        def window(b, carry):
            tok0 = pl.multiple_of(b * TOK_TILE, TOK_TILE)
            rk = rank_t_ref[pl.ds(e, 1), pl.ds(tok0, TOK_TILE)]
            sl = slot_t_ref[pl.ds(e, 1), pl.ds(tok0, TOK_TILE)]
            code = ((tok0 + lane) * TOP_K).astype(F32) + sl + 1.0
            lo = jnp.maximum(cum[b * N_EXPERTS + e] - base, 0)
            hi = jnp.minimum(cum[(b + 1) * N_EXPERTS + e] - base, EXP_TILE)

            def rows(k, c):
                r0 = pl.multiple_of(k * PLAN_ROWS, PLAN_ROWS)
                want = (base + r0 + rows_iota).astype(F32)
                hit = jnp.where(rk == want, code, 0.0)
                part = hit[:, 0:LANES]
                for c0 in range(LANES, TOK_TILE, LANES):
                    part = part + hit[:, c0:c0 + LANES]
                acc_ref[pl.ds(r0, PLAN_ROWS), :] += part
                return c

            shift = PLAN_ROWS.bit_length() - 1
            lax.fori_loop(lo >> shift, (hi + PLAN_ROWS - 1) >> shift, rows, 0)
            return carry

        lax.fori_loop(win0, win1, window, 0)
        code = jnp.sum(acc_ref[...].T, axis=0, keepdims=True).astype(jnp.int32) - 1
        pos = lax.broadcasted_iota(jnp.int32, (1, EXP_TILE), 1)
        tok = code >> 2
        src = jnp.where(code >= 0, tok, lax.rem(i * EXP_TILE + pos, n_tok))
        spare = TOP_K * n_tok + lax.rem(i, N_YBUF) * EXP_TILE + pos
        dst = jnp.where(code >= 0, (code & (TOP_K - 1)) * n_tok + tok, spare)
        src_ref[0] = src * SUBLANES
        dst_ref[0] = dst * SUBLANES

    @pl.when(i >= n_act[0])
    def _():
        src_ref[...] = jnp.zeros_like(src_ref)
        dst_ref[...] = jnp.zeros_like(dst_ref)


def _plan(rank_t, slot_t, tile_e, gstart, n_act, cum2d):
    n_tok = rank_t.shape[1]
    n_tiles = tile_e.shape[0]
    n_tok_tiles = cum2d.shape[0] - 1
    assert n_tok_tiles <= LANES
    huge = jnp.float32(2.0 ** 30)
    as_rows = lambda a: jnp.pad(a.T.astype(F32), ((0, 0), (0, LANES - n_tok_tiles)),
                                constant_values=huge)
    whole = lambda i, *_: (0, 0)
    per_tile = pl.BlockSpec((1, 1, EXP_TILE), lambda i, *_: (i, 0, 0))
    grid_spec = pltpu.PrefetchScalarGridSpec(
        num_scalar_prefetch=4,
        grid=(n_tiles,),
        in_specs=[pl.BlockSpec((N_EXPERTS, LANES), whole), pl.BlockSpec((N_EXPERTS, LANES), whole),
                  pl.BlockSpec(rank_t.shape, whole), pl.BlockSpec(slot_t.shape, whole)],
        out_specs=(per_tile, per_tile),
        scratch_shapes=[pltpu.VMEM((EXP_TILE, LANES), F32)],
    )
    shape = jax.ShapeDtypeStruct((n_tiles, 1, EXP_TILE), jnp.int32)
    return pl.pallas_call(
        functools.partial(_plan_kernel, n_tok),
        out_shape=(shape, shape),
        grid_spec=grid_spec,
        compiler_params=pltpu.CompilerParams(dimension_semantics=("arbitrary",),
                                             vmem_limit_bytes=VMEM_LIMIT),
        name="moe_plan",
    )(tile_e, gstart, n_act, cum2d.reshape(-1), as_rows(cum2d[:-1]), as_rows(cum2d[1:]),
      rank_t, slot_t)


N_YBUF = 3
DMA_UNROLL = 8


def _expert_kernel(tile_e, n_act, idx_ref, h_ref, wgu_ref, bgu_ref, wd_ref, bd_ref, yk_ref, xbuf,
                   ybuf, gsem, ssem, wgu_bf, wd_bf):
    i = pl.program_id(0)
    n_steps = pl.num_programs(0)
    n_active = n_act[0]
    tile_rows = EXP_TILE * SUBLANES
    n_slot_rows = yk_ref.shape[0] - N_YBUF * tile_rows

    def row_tile(ref, start):
        return ref.at[pl.ds(pl.multiple_of(start, SUBLANES), SUBLANES)]

    def for_each_row(fn):
        def body(g, carry):
            for k in range(DMA_UNROLL):
                fn(g * DMA_UNROLL + k)
            return carry
        lax.fori_loop(0, EXP_TILE // DMA_UNROLL, body, 0)

    def on_slot(slot, n, fn):
        if isinstance(slot, int):
            fn(slot)
            return
        for s in range(n):
            pl.when(slot == s)(functools.partial(fn, s))

    def gather_rows(which, slot):
        def run(s):
            def one(r):
                pltpu.make_async_copy(row_tile(h_ref, idx_ref[0, 0, which * EXP_TILE + r]),
                                      row_tile(xbuf.at[s], r * SUBLANES), gsem.at[s]).start()
            for_each_row(one)
        on_slot(slot, 2, run)

    def scatter_rows(slot):
        def run(s):
            def one(r):
                pltpu.make_async_copy(row_tile(ybuf.at[s], r * SUBLANES),
                                      row_tile(yk_ref, idx_ref[0, 0, 2 * EXP_TILE + r]),
                                      ssem.at[s]).start()
            for_each_row(one)
        on_slot(slot, N_YBUF, run)

    def scatter_wait(slot):
        pltpu.make_async_copy(ybuf.at[slot], yk_ref.at[pl.ds(0, tile_rows)], ssem.at[slot]).wait()

    @pl.when(i == 0)
    def _():
        for s in range(N_YBUF):
            ybuf[s] = jnp.zeros((tile_rows, LANES), F32)
        for s in range(N_YBUF - 1):
            pltpu.make_async_copy(
                ybuf.at[s], yk_ref.at[pl.ds(n_slot_rows + s * tile_rows, tile_rows)],
                ssem.at[s]).start()
        gather_rows(0, 0)

    @pl.when(i + 1 < n_active)
    def _():
        gather_rows(1, (i + 1) % 2)

    prev = tile_e[jnp.maximum(i - 1, 0)]
    new_expert = jnp.logical_or(i == 0, tile_e[i] != prev)

    @pl.when(jnp.logical_and(new_expert, i < n_active))
    def _():
        rows = 128

        def cast(r, carry):
            sl = pl.ds(pl.multiple_of(r * rows, rows), rows)
            wgu_bf[sl, :] = wgu_ref[0, sl, :].astype(BF16)
            wd_bf[sl, :] = wd_ref[0, sl, :].astype(BF16)
            return carry

        lax.fori_loop(0, D_MODEL // rows, cast, 0)

    @pl.when(i < n_active)
    def _():
        xslot = i % 2
        yslot = i % N_YBUF
        pltpu.make_async_copy(h_ref.at[pl.ds(0, tile_rows)], xbuf.at[xslot], gsem.at[xslot]).wait()

        scatter_wait(yslot)
        x = _load_row_tiles(xbuf.at[xslot], EXP_TILE).astype(BF16)
        prev_y = ybuf.at[(i + N_YBUF - 1) % N_YBUF]
        prev_sem = ssem.at[(i + N_YBUF - 1) % N_YBUF]
        gu = _dot(x, wgu_bf[...]) + bgu_ref[0]
        gate = jnp.minimum(gu[:, :D_FF], SWIGLU_LIMIT)
        up = jnp.clip(gu[:, D_FF:], -SWIGLU_LIMIT, SWIGLU_LIMIT)
        act = (up + 1.0) * gate * _sigmoid(SWIGLU_ALPHA * gate)
        for r in range(EXP_TILE):
            pltpu.make_async_copy(row_tile(prev_y, r * SUBLANES),
                                  row_tile(yk_ref, idx_ref[0, 0, 3 * EXP_TILE + r]),
                                  prev_sem).start(priority=r % 2)
        _store_row_tiles(ybuf.at[yslot], _dot(act.astype(BF16), wd_bf[...]) + bd_ref[0])

        @pl.when(i == n_active - 1)
        def _():
            scatter_rows(yslot)

    @pl.when(i == n_steps - 1)
    def _():
        for back in range(1, N_YBUF + 1):
            scatter_wait((n_active - back) % N_YBUF)


def _experts(h2, src3, dst3, tile_e, n_act, w_gate_up, b_gate_up, w_down, b_down):
    n_tok = h2.shape[0] // SUBLANES
    n_tiles = src3.shape[0]
    src_next = jnp.concatenate([src3[1:], src3[-1:]], axis=0)
    spare_rows = TOP_K * n_tok + (N_YBUF - 1) * EXP_TILE + jnp.arange(EXP_TILE, dtype=jnp.int32)
    dst_prev = jnp.concatenate([(spare_rows * SUBLANES).reshape(1, 1, EXP_TILE), dst3[:-1]], axis=0)
    idx3 = jnp.concatenate([src3, src_next, dst3, dst_prev], axis=-1)
    grid_spec = pltpu.PrefetchScalarGridSpec(
        num_scalar_prefetch=2,
        grid=(n_tiles,),
        in_specs=[pl.BlockSpec((1, 1, 4 * EXP_TILE), lambda i, te, na: (i, 0, 0),
                               memory_space=pltpu.SMEM),
                  pl.BlockSpec(memory_space=pl.ANY),
                  pl.BlockSpec((1, D_MODEL, 2 * D_FF), lambda i, te, na: (te[i], 0, 0)),
                  pl.BlockSpec((1, 1, 2 * D_FF), lambda i, te, na: (te[i], 0, 0)),
                  pl.BlockSpec((1, D_FF, D_MODEL), lambda i, te, na: (te[i], 0, 0)),
                  pl.BlockSpec((1, 1, D_MODEL), lambda i, te, na: (te[i], 0, 0))],
        out_specs=pl.BlockSpec(memory_space=pl.ANY),
        scratch_shapes=[pltpu.VMEM((2, EXP_TILE * SUBLANES, LANES), F32),
                        pltpu.VMEM((N_YBUF, EXP_TILE * SUBLANES, LANES), F32),
                        pltpu.SemaphoreType.DMA((2,)), pltpu.SemaphoreType.DMA((N_YBUF,)),
                        pltpu.VMEM((D_MODEL, 2 * D_FF), BF16), pltpu.VMEM((D_FF, D_MODEL), BF16)],
    )
    return pl.pallas_call(
        _expert_kernel,
        out_shape=jax.ShapeDtypeStruct(((TOP_K * n_tok + N_YBUF * EXP_TILE) * SUBLANES, LANES), F32),
        grid_spec=grid_spec,
        compiler_params=pltpu.CompilerParams(dimension_semantics=("arbitrary",),
                                             vmem_limit_bytes=VMEM_LIMIT),
        name="moe_experts",
    )(tile_e, n_act, idx3, h2, w_gate_up, b_gate_up.reshape(N_EXPERTS, 1, 2 * D_FF),
      w_down, b_down.reshape(N_EXPERTS, 1, D_MODEL))


COMBINE_TILE = 512


def _combine_kernel(n_prompt_tiles, alpha, y0_ref, y1_ref, y2_ref, y3_ref, x1_ref, w4_ref, mod_ref,
                    l2g_ref, l2b_ref, yp_ref, ysm_ref):
    i = pl.program_id(0)
    w4 = w4_ref[...]
    ff = _load_row_tiles(y0_ref, COMBINE_TILE) * w4[:, 0:1]
    for j, y_ref in enumerate((y1_ref, y2_ref, y3_ref), start=1):
        ff = ff + _load_row_tiles(y_ref, COMBINE_TILE) * w4[:, j:j + 1]
    g2 = mod_ref[0][:, 5 * D_MODEL:6 * D_MODEL]
    out = _layer_norm(alpha * x1_ref[...] + g2 * ff) * l2g_ref[...] + l2b_ref[...]

    @pl.when(i < n_prompt_tiles)
    def _():
        yp_ref[...] = out

    @pl.when(i >= n_prompt_tiles)
    def _():
        ysm_ref[...] = out


def _combine(yk, x1, w4, mod3, ln2_g, ln2_b, n_p, alpha, sample_seq):
    n_tok = x1.shape[0]
    n_s = n_tok - n_p
    assert n_p % COMBINE_TILE == 0 and sample_seq % COMBINE_TILE == 0
    npt = n_p // COMBINE_TILE
    n_tok_tiles = n_tok // COMBINE_TILE
    tiles_per_sample_seq = sample_seq // COMBINE_TILE

    def mod_row(i):
        return jnp.where(i < npt, 0, 1 + (i - npt) // tiles_per_sample_seq)

    def slot(j):
        return pl.BlockSpec((COMBINE_TILE * SUBLANES, LANES), lambda i: (j * n_tok_tiles + i, 0))

    const = lambda i: (0, 0)
    return pl.pallas_call(
        functools.partial(_combine_kernel, npt, alpha),
        out_shape=(jax.ShapeDtypeStruct((n_p, D_MODEL), F32),
                   jax.ShapeDtypeStruct((n_s, D_MODEL), F32)),
        grid=(n_tok_tiles,),
        in_specs=[slot(0), slot(1), slot(2), slot(3),
                  pl.BlockSpec((COMBINE_TILE, D_MODEL), lambda i: (i, 0)),
                  pl.BlockSpec((COMBINE_TILE, LANES), lambda i: (i, 0)),
                  pl.BlockSpec((1, 1, 6 * D_MODEL), lambda i: (mod_row(i), 0, 0)),
                  pl.BlockSpec((1, D_MODEL), const), pl.BlockSpec((1, D_MODEL), const)],
        out_specs=(pl.BlockSpec((COMBINE_TILE, D_MODEL), lambda i: (jnp.minimum(i, npt - 1), 0)),
                   pl.BlockSpec((COMBINE_TILE, D_MODEL), lambda i: (jnp.maximum(i - npt, 0), 0))),
        compiler_params=pltpu.CompilerParams(dimension_semantics=("arbitrary",),
                                             vmem_limit_bytes=VMEM_LIMIT),
        name="moe_combine",
    )(yk, yk, yk, yk, x1, w4, mod3, ln2_g, ln2_b)


def kernel(x_prompt, x_sample, c, state_fwd, state_bwd, c_ctx, w_ada, b_ada, w_in, conv_w, conv_b,
           conv_ln_g, conv_ln_b, lb_logits, rec_norm_g, w_out, ln1_g, ln1_b, ln2_g, ln2_b,
           router_w, router_b, w_gate_up, b_gate_up, w_down, b_down):
    depth = w_ada.shape[0]
    assert depth == 1
    alpha = (2.0 * depth) ** 0.25
    nb_p, seq_p, _ = x_prompt.shape
    nb_s, seq_s, _ = x_sample.shape
    n_p, n_s = nb_p * seq_p, nb_s * seq_s
    n_tok = n_p + n_s
    row2 = lambda a: a.reshape(1, -1)

    lb = jnp.cumsum(jax.nn.softmax(lb_logits.astype(F32), axis=0), axis=0)[0]

    cond = jnp.concatenate([c_ctx[None, :], c, jnp.zeros((16 - 1 - nb_s, D_MODEL), F32)], axis=0)
    mod = _ada(cond, w_ada[0], b_ada[0])
    mod3 = mod.reshape(16, 1, 6 * D_MODEL)

    xp = x_prompt.reshape(n_p, D_MODEL)
    xs = x_sample.reshape(n_s, D_MODEL)
    conv_w_pad = jnp.concatenate([conv_w[0], jnp.zeros((1, CONV_WIDTH), F32)], axis=0)
    conv_out, rec = _inproj(xp, xs, mod3, w_in[0].astype(BF16), conv_w_pad, row2(conv_b[0]),
                            row2(conv_ln_g[0]), row2(conv_ln_b[0]), seq_p, seq_s)

    zeros_state = jnp.zeros((nb_p, REC_HEADS, REC_DK, REC_DK), F32)
    s0f = jnp.concatenate([zeros_state, jnp.swapaxes(state_fwd[:, 0], -1, -2)], axis=0)
    s0b = jnp.concatenate([zeros_state, jnp.swapaxes(state_bwd[:, 0], -1, -2)], axis=0)
    o_f, o_b, sf_t, sb_t = _scan(rec, lb, s0f, s0b, [seq_p] * nb_p + [seq_s] * nb_s)
    new_f = jnp.swapaxes(sf_t[:nb_p], -1, -2)[:, None]
    new_b = jnp.swapaxes(sb_t[:nb_p], -1, -2)[:, None]

    rw = jnp.pad(router_w[0], ((0, 0), (0, LANES - N_EXPERTS)))
    rw_hi = rw.astype(BF16)
    rw_lo = (rw - rw_hi.astype(F32)).astype(BF16)
    rb = jnp.pad(router_b[0], (0, LANES - N_EXPERTS)).reshape(1, LANES)
    x1, h2, rank_t, slot_t, w4, counts, tile_start = _post(
        o_f, o_b, rec, conv_out, xp, xs, mod3, w_out[0].astype(BF16), row2(rec_norm_g[0]),
        row2(ln1_g[0]), row2(ln1_b[0]), rw_hi, rw_lo, rb, alpha, seq_s)

    cnt = counts[0, :N_EXPERTS].astype(jnp.int32)
    gpad = ((cnt + EXP_TILE - 1) // EXP_TILE) * EXP_TILE
    gend = jnp.cumsum(gpad)
    gstart = gend - gpad
    n_rows = n_tok * TOP_K + N_EXPERTS * EXP_TILE
    n_tiles = n_rows // EXP_TILE
    n_act = gend[-1] // EXP_TILE
    tile_ids = jnp.minimum(jnp.arange(n_tiles, dtype=jnp.int32), n_act - 1)
    tile_e = jnp.sum((gend[None, :] <= tile_ids[:, None] * EXP_TILE).astype(jnp.int32), axis=1)
    n_act1 = n_act.reshape(1).astype(jnp.int32)
    starts = tile_start[:, 0, :N_EXPERTS].astype(jnp.int32)
    cum2d = jnp.concatenate([starts, cnt[None, :]], axis=0)
    src, dst = _plan(rank_t, slot_t, tile_e, gstart.astype(jnp.int32), n_act1, cum2d)
    yk = _experts(h2, src, dst, tile_e, n_act1, w_gate_up[0], b_gate_up[0], w_down[0], b_down[0])
    y_p, y_s = _combine(yk, x1, w4, mod3, row2(ln2_g[0]), row2(ln2_b[0]), n_p, alpha, seq_s)
    return (y_p.reshape(nb_p, seq_p, D_MODEL), y_s.reshape(nb_s, seq_s, D_MODEL), new_f, new_b)
```

```python
import functools

import numpy as np
import jax
import jax.numpy as jnp
from jax import lax
from jax.experimental import pallas as pl
from jax.experimental.pallas import tpu as pltpu

F32 = jnp.float32
BF16 = jnp.bfloat16

D_MODEL = 1024
CONV_WIDTH = 512
CONV_TAPS = 31
REC_HEADS = 4
REC_DK = 128
REC_WIDTH = REC_HEADS * REC_DK
REC_COLS = 5 * REC_WIDTH
CHUNK = 64
N_LEVELS = 6
SCAN_CHUNKS = 4
N_EXPERTS = 32
TOP_K = 4
D_FF = 1024
SWIGLU_LIMIT = 7.0
SWIGLU_ALPHA = 1.702
LN_EPS = 1e-5
RMS_EPS = 1e-6
GRID_W = 64

LANES = 128
TOK_TILE = 512
EXP_TILE = 512
VMEM_LIMIT = 56 * 1024 * 1024


def _sigmoid(x):
    return 1.0 / (1.0 + jnp.exp(-x))


def _layer_norm(x):
    mu = jnp.mean(x, axis=-1, keepdims=True)
    xc = x - mu
    var = jnp.mean(xc * xc, axis=-1, keepdims=True)
    return xc * lax.rsqrt(var + LN_EPS)


def _dot(a, b):
    return jnp.dot(a, b, preferred_element_type=F32)


def _dot_nt(a, b):
    return lax.dot_general(a, b, (((1,), (1,)), ((), ())), preferred_element_type=F32)


SUBLANES = 8
assert D_MODEL == SUBLANES * LANES


def _store_row_tiles(ref, value):
    for s in range(SUBLANES):
        ref[pl.ds(s, value.shape[0], stride=SUBLANES), :] = value[:, s * LANES:(s + 1) * LANES]


def _load_row_tiles(ref, n_rows):
    return jnp.concatenate(
        [ref[pl.ds(s, n_rows, stride=SUBLANES), :] for s in range(SUBLANES)], axis=-1)


def _split3(x):
    hi = x.astype(BF16)
    r1 = x - hi.astype(F32)
    mid = r1.astype(BF16)
    lo = (r1 - mid.astype(F32)).astype(BF16)
    return hi, mid, lo


def _ada_kernel(c_ref, w_ref, b_ref, o_ref):
    c = c_ref[...]
    s = (c * _sigmoid(c)).astype(BF16)
    o_ref[...] = _dot(s, w_ref[...].astype(BF16)) + b_ref[...]


def _ada(cond, w_ada, b_ada):
    rows = cond.shape[0]
    ncol = w_ada.shape[1]
    tn = 1024
    return pl.pallas_call(
        _ada_kernel,
        out_shape=jax.ShapeDtypeStruct((rows, ncol), F32),
        grid=(ncol // tn,),
        in_specs=[pl.BlockSpec((rows, D_MODEL), lambda j: (0, 0)),
                  pl.BlockSpec((D_MODEL, tn), lambda j: (0, j)),
                  pl.BlockSpec((1, tn), lambda j: (0, j))],
        out_specs=pl.BlockSpec((rows, tn), lambda j: (0, j)),
        compiler_params=pltpu.CompilerParams(dimension_semantics=("parallel",),
                                             vmem_limit_bytes=VMEM_LIMIT),
        name="ada_mod",
    )(cond, w_ada, b_ada.reshape(1, ncol))


def _inproj_kernel(n_prompt_tiles, prompt_seg, xp_ref, xs_ref, mod_ref, win_ref, cw_ref, cb_ref,
                   cg_ref, cbeta_ref, conv_ref, rec_ref):
    i = pl.program_id(0)
    is_prompt = i < n_prompt_tiles
    x = jnp.where(is_prompt, xp_ref[...], xs_ref[...])
    mod = mod_ref[0]
    sh1 = mod[:, 0:D_MODEL]
    sc1 = mod[:, D_MODEL:2 * D_MODEL]
    h = (_layer_norm(x) * (1.0 + sc1) + sh1).astype(BF16)

    glu = _dot(h, win_ref[:, 0:2 * CONV_WIDTH])
    u = glu[:, :CONV_WIDTH] * _sigmoid(glu[:, CONV_WIDTH:])

    seg_len = jnp.where(is_prompt, prompt_seg, GRID_W)
    half = CONV_TAPS // 2
    pos = lax.broadcasted_iota(jnp.int32, (TOK_TILE, 1), 0) & (seg_len - 1)

    def masked_shift(d):
        sh = pltpu.roll(u, (-d) % TOK_TILE, axis=0)
        edge_rows = SUBLANES * ((abs(d) + SUBLANES - 1) // SUBLANES)
        pieces = []
        for s0 in range(0, TOK_TILE, GRID_W):
            if d < 0:
                edge = slice(s0, s0 + edge_rows)
                pieces.append(jnp.where(pos[edge] >= -d, sh[edge], 0.0))
                pieces.append(sh[s0 + edge_rows:s0 + GRID_W])
            else:
                edge = slice(s0 + GRID_W - edge_rows, s0 + GRID_W)
                pieces.append(sh[s0:s0 + GRID_W - edge_rows])
                pieces.append(jnp.where(pos[edge] < seg_len - d, sh[edge], 0.0))
        return jnp.concatenate(pieces, axis=0)

    acc = u * cw_ref[half:half + 1, :] + cb_ref[...]
    taps = [j for j in range(CONV_TAPS) if j != half]
    n_groups = REC_COLS // REC_WIDTH
    per_group = len(taps) // n_groups
    anchor = None
    for g in range(n_groups):
        cols = slice(g * REC_WIDTH, (g + 1) * REC_WIDTH)
        rec = _dot(h, win_ref[:, 2 * CONV_WIDTH + g * REC_WIDTH:
                              2 * CONV_WIDTH + (g + 1) * REC_WIDTH])
        rec_ref[:, cols] = rec
        for j in taps[g * per_group:(g + 1) * per_group]:
            w_tap = cw_ref[j:j + 1, :]
            if anchor is not None:
                w_tap = w_tap + anchor
                anchor = None
            acc = acc + masked_shift(j - half) * w_tap
        if g + 1 < n_groups:
            bits = pltpu.bitcast(rec[0:SUBLANES, :], jnp.uint32)
            anchor = pltpu.bitcast((bits >> 16) >> 16, F32)[0:1, :]
    y = _layer_norm(acc) * cg_ref[...] + cbeta_ref[...]
    conv_ref[...] = (y * _sigmoid(y)).astype(BF16)


def _inproj(xp, xs, mod3, w_in_bf, conv_w, conv_b, conv_g, conv_beta, prompt_seq, sample_seq):
    n_p, n_s = xp.shape[0], xs.shape[0]
    n_tok = n_p + n_s
    npt = n_p // TOK_TILE
    tiles_per_sample_seq = sample_seq // TOK_TILE
    assert TOK_TILE % prompt_seq == 0 and n_p % TOK_TILE == 0 and sample_seq % TOK_TILE == 0

    def mod_row(i):
        return jnp.where(i < npt, 0, 1 + (i - npt) // tiles_per_sample_seq)

    const = lambda i: (0, 0)
    return pl.pallas_call(
        functools.partial(_inproj_kernel, npt, prompt_seq),
        out_shape=(jax.ShapeDtypeStruct((n_tok, CONV_WIDTH), BF16),
                   jax.ShapeDtypeStruct((n_tok, REC_COLS), F32)),
        grid=(n_tok // TOK_TILE,),
        in_specs=[pl.BlockSpec((TOK_TILE, D_MODEL), lambda i: (jnp.minimum(i, npt - 1), 0)),
                  pl.BlockSpec((TOK_TILE, D_MODEL), lambda i: (jnp.maximum(i - npt, 0), 0)),
                  pl.BlockSpec((1, 1, 6 * D_MODEL), lambda i: (mod_row(i), 0, 0)),
                  pl.BlockSpec(w_in_bf.shape, const),
                  pl.BlockSpec(conv_w.shape, const),
                  pl.BlockSpec((1, CONV_WIDTH), const),
                  pl.BlockSpec((1, CONV_WIDTH), const),
                  pl.BlockSpec((1, CONV_WIDTH), const)],
        out_specs=(pl.BlockSpec((TOK_TILE, CONV_WIDTH), lambda i: (i, 0)),
                   pl.BlockSpec((TOK_TILE, REC_COLS), lambda i: (i, 0))),
        compiler_params=pltpu.CompilerParams(dimension_semantics=("parallel",),
                                             vmem_limit_bytes=VMEM_LIMIT),
        name="inproj_conv",
    )(xp, xs, mod3, w_in_bf, conv_w, conv_b, conv_g, conv_beta)


def _scan_constants():
    c = CHUNK
    w = np.zeros((N_LEVELS + 2, c, c), np.float32)
    m = np.zeros((N_LEVELS + 1, c, c), np.float32)
    t = np.arange(c)
    for j in range(N_LEVELS):
        half = 1 << j
        for ti in range(c):
            ref = (ti & ~(2 * half - 1)) + half - 1
            if ti & half:
                w[j, ti, ref + 1:ti + 1] = 1.0
            else:
                w[j, ti, ti + 1:ref + 1] = 1.0
        upper = (t[:, None] & half) != 0
        lower = (t[None, :] & half) == 0
        same = (t[:, None] >> (j + 1)) == (t[None, :] >> (j + 1))
        m[j] = (upper & lower & same).astype(np.float32)
    w[N_LEVELS] = (t[None, :] <= t[:, None]).astype(np.float32)
    w[N_LEVELS + 1] = (t[None, :] > t[:, None]).astype(np.float32)
    m[N_LEVELS] = np.eye(c, dtype=np.float32)
    w_f = np.tile(w.reshape(-1, c), (1, 3))
    w_b = np.tile(w[:, ::-1, ::-1].reshape(-1, c), (1, 3))
    m_b = m[:, ::-1, ::-1]
    return (jnp.asarray(w_f, BF16), jnp.asarray(np.ascontiguousarray(w_b), BF16),
            jnp.asarray(m), jnp.asarray(np.ascontiguousarray(m_b)))


def _scan_gates(z, lb, w_ref):
    e = jnp.exp(-jnp.abs(z))
    r = 1.0 / (1.0 + e)
    a = e * r
    sig = jnp.where(z >= 0, r, a)
    sig_neg = jnp.where(z >= 0, a, r)
    one_m_lb = 1.0 - lb
    log_f = jnp.log2(lb + one_m_lb * sig)
    k = one_m_lb * sig_neg
    terms = jnp.concatenate(_split3(log_f), axis=0)
    ex = jnp.exp2(_dot(w_ref[...], terms))
    return k, ex


def _scan_intra(q, k, ex, m_ref, neighbour):
    c = CHUNK
    qb, kb = q.astype(BF16), k.astype(BF16)
    a_mat = m_ref[N_LEVELS] * jnp.sum(q * k, axis=-1, keepdims=True)
    x0 = ex[0:c]
    pair = jnp.sum((q * x0) * pltpu.roll(k * x0, neighbour, axis=0), axis=-1, keepdims=True)
    a_mat = a_mat + m_ref[0] * pair
    for j in range(1, N_LEVELS):
        xj = ex[j * c:(j + 1) * c].astype(BF16)
        a_mat = a_mat + m_ref[j] * _dot_nt(qb * xj, kb * xj)
    return a_mat


def _scan_kernel(fblk, bblk, first, seq, qf_ref, zf_ref, vf_ref, qb_ref, zb_ref, vb_ref, lb_ref,
                 wf_ref, wb_ref, mf_ref, mb_ref, s0f_ref, s0b_ref, of_ref, ob_ref, sf_ref, sb_ref):
    s = pl.program_id(0)

    @pl.when(first[s] == 1)
    def _():
        sf_ref[...] = s0f_ref[...]
        sb_ref[...] = s0b_ref[...]

    c = CHUNK
    dirs = ((qf_ref, zf_ref, vf_ref, 0, wf_ref, mf_ref, sf_ref, of_ref, c - 1,
             tuple(range(SCAN_CHUNKS))),
            (qb_ref, zb_ref, vb_ref, 1, wb_ref, mb_ref, sb_ref, ob_ref, 0,
             tuple(reversed(range(SCAN_CHUNKS)))))
    units = [(d, slice(sub * c, (sub + 1) * c)) for d in dirs for sub in d[-1]]
    gates = [_scan_gates(d[1][rows, :], lb_ref[d[3]:d[3] + 1, :], d[4]) for d, rows in units]
    intra = []
    for (d, rows), (k, ex) in zip(units, gates):
        for h in range(REC_HEADS):
            cols = slice(h * REC_DK, (h + 1) * REC_DK)
            neighbour = 1 if d[8] == c - 1 else c - 1
            intra.append(_scan_intra(d[0][rows, cols], k[:, cols], ex[:, cols], d[5], neighbour))
    idx = 0
    for (d, rows), (k, ex) in zip(units, gates):
        q_ref, _, v_ref, _, _, _, st_ref, o_ref, last, _ = d
        for h in range(REC_HEADS):
            cols = slice(h * REC_DK, (h + 1) * REC_DK)
            q, v = q_ref[rows, cols], v_ref[rows, cols]
            ex_b = ex[N_LEVELS * c:(N_LEVELS + 1) * c, cols]
            ex_l = ex[(N_LEVELS + 1) * c:(N_LEVELS + 2) * c, cols]
            st = st_ref[0, h]
            o_ref[rows, cols] = (_dot(intra[idx].astype(BF16), v.astype(BF16))
                                 + _dot_nt((q * ex_b).astype(BF16), st.astype(BF16)))
            st_ref[0, h] = (st * ex_b[last:last + 1, :]
                            + _dot(v.T.astype(BF16), (k[:, cols] * ex_l).astype(BF16)))
            idx += 1


def _scan(rec, lb, s0f_t, s0b_t, seq_lens):
    n_tok = rec.shape[0]
    fblk, bblk, first, seq = [], [], [], []
    base = 0
    step_rows = CHUNK * SCAN_CHUNKS
    for si, ln in enumerate(seq_lens):
        assert ln % step_rows == 0
        n = ln // step_rows
        for ci in range(n):
            fblk.append(base + ci)
            bblk.append(base + n - 1 - ci)
            first.append(1 if ci == 0 else 0)
            seq.append(si)
        base += n
    steps = len(fblk)
    as_i32 = lambda a: jnp.asarray(np.asarray(a, np.int32))
    w_f, w_b, m_f, m_b = _scan_constants()
    n_seq = len(seq_lens)

    def col(block_of, c):
        return pl.BlockSpec((step_rows, REC_WIDTH), lambda s, fb, bb, fi, sq: (block_of(fb, bb)[s], c))

    fwd = lambda fb, bb: fb
    bwd = lambda fb, bb: bb
    const2 = lambda s, fb, bb, fi, sq: (0, 0)
    const3 = lambda s, fb, bb, fi, sq: (0, 0, 0)
    state_spec = pl.BlockSpec((1, REC_HEADS, REC_DK, REC_DK), lambda s, fb, bb, fi, sq: (sq[s], 0, 0, 0))
    grid_spec = pltpu.PrefetchScalarGridSpec(
        num_scalar_prefetch=4,
        grid=(steps,),
        in_specs=[col(fwd, 0), col(fwd, 1), col(fwd, 3), col(bwd, 0), col(bwd, 2), col(bwd, 3),
                  pl.BlockSpec((2, REC_WIDTH), const2),
                  pl.BlockSpec(w_f.shape, const2), pl.BlockSpec(w_b.shape, const2),
                  pl.BlockSpec(m_f.shape, const3), pl.BlockSpec(m_b.shape, const3),
                  state_spec, state_spec],
        out_specs=(pl.BlockSpec((step_rows, REC_WIDTH), lambda s, fb, bb, fi, sq: (fb[s], 0)),
                   pl.BlockSpec((step_rows, REC_WIDTH), lambda s, fb, bb, fi, sq: (bb[s], 0)),
                   state_spec, state_spec),
    )
    st_shape = jax.ShapeDtypeStruct((n_seq, REC_HEADS, REC_DK, REC_DK), F32)
    return pl.pallas_call(
        _scan_kernel,
        out_shape=(jax.ShapeDtypeStruct((n_tok, REC_WIDTH), F32),
                   jax.ShapeDtypeStruct((n_tok, REC_WIDTH), F32), st_shape, st_shape),
        grid_spec=grid_spec,
        compiler_params=pltpu.CompilerParams(dimension_semantics=("arbitrary",),
                                             vmem_limit_bytes=VMEM_LIMIT),
        name="hgrn2_scan",
    )(as_i32(fblk), as_i32(bblk), as_i32(first), as_i32(seq),
      rec, rec, rec, rec, rec, rec, lb, w_f, w_b, m_f, m_b, s0f_t, s0b_t)


def _post_kernel(n_prompt_tiles, alpha, of_ref, ob_ref, g_ref, conv_ref, xp_ref, xs_ref, mod_ref,
                 wout_ref, rg_ref, l1g_ref, l1b_ref, rwh_ref, rwl_ref, rb_ref, tri_ref,
                 x1_ref, h2_ref, rank_t_ref, slot_t_ref, w4_ref, cnt_ref, start_ref, run_ref):
    i = pl.program_id(0)

    @pl.when(i == 0)
    def _():
        run_ref[...] = jnp.zeros_like(run_ref)

    x = jnp.where(i < n_prompt_tiles, xp_ref[...], xs_ref[...])
    mod = mod_ref[0]
    g1 = mod[:, 2 * D_MODEL:3 * D_MODEL]
    sh2 = mod[:, 3 * D_MODEL:4 * D_MODEL]
    sc2 = mod[:, 4 * D_MODEL:5 * D_MODEL]

    g = g_ref[...]
    silu_g = g * _sigmoid(g)
    mix = _dot(conv_ref[...], wout_ref[0:CONV_WIDTH, :])
    for h in range(REC_HEADS):
        cols = slice(h * REC_DK, (h + 1) * REC_DK)
        o = of_ref[:, cols] + ob_ref[:, cols]
        o = o * lax.rsqrt(jnp.mean(o * o, axis=-1, keepdims=True) + RMS_EPS)
        rec_out = (o * rg_ref[:, cols] * silu_g[:, cols]).astype(BF16)
        mix = mix + _dot(rec_out, wout_ref[CONV_WIDTH + h * REC_DK:CONV_WIDTH + (h + 1) * REC_DK, :])

    x1 = _layer_norm(alpha * x + g1 * mix) * l1g_ref[...] + l1b_ref[...]
    x1_ref[...] = x1
    h2 = _layer_norm(x1) * (1.0 + sc2) + sh2
    _store_row_tiles(h2_ref, h2)

    h_hi = h2.astype(BF16)
    h_lo = (h2 - h_hi.astype(F32)).astype(BF16)
    logits = (_dot(h_hi, rwh_ref[...]) + _dot(h_lo, rwh_ref[...]) + _dot(h_hi, rwl_ref[...])
              + rb_ref[...])
    lane = lax.broadcasted_iota(jnp.int32, logits.shape, 1)
    lane_f = lane.astype(F32)
    neg_inf = jnp.float32(-jnp.inf)
    work = jnp.where(lane < N_EXPERTS, logits, neg_inf)
    vals, sels = [], []
    for _ in range(TOP_K):
        m = jnp.max(work, axis=-1, keepdims=True)
        idx = jnp.min(jnp.where(work == m, lane_f, float(LANES)), axis=-1, keepdims=True)
        sel = lane_f == idx
        vals.append(m)
        sels.append(sel)
        work = jnp.where(sel, neg_inf, work)
    exps = [jnp.exp(v - vals[0]) for v in vals]
    denom = exps[0] + exps[1] + exps[2] + exps[3]

    sel_any = jnp.zeros(logits.shape, F32)
    for sel in sels:
        sel_any = jnp.where(sel, 1.0, sel_any)
    run_old = run_ref[...]
    prior = _dot(tri_ref[...], sel_any.astype(BF16)) + run_old
    run_new = run_old + jnp.sum(sel_any, axis=0, keepdims=True)
    run_ref[...] = run_new
    cnt_ref[...] = run_new
    start_ref[0] = run_old

    slot = jnp.zeros(logits.shape, F32)
    w4 = jnp.zeros(logits.shape, F32)
    for j in range(TOP_K):
        slot = jnp.where(sels[j], float(j), slot)
        w4 = jnp.where(lane == j, exps[j] / denom, w4)
    routed = sel_any > 0.0
    rank_t_ref[...] = jnp.where(routed, prior, -1.0).T[:N_EXPERTS]
    slot_t_ref[...] = jnp.where(routed, slot, -1.0).T[:N_EXPERTS]
    w4_ref[...] = w4


def _post(o_f, o_b, rec, conv, xp, xs, mod3, w_out_bf, rec_g, ln1_g, ln1_b, rw_hi, rw_lo, rb,
          alpha, sample_seq):
    n_p, n_s = xp.shape[0], xs.shape[0]
    n_tok = n_p + n_s
    npt = n_p // TOK_TILE
    tiles_per_sample_seq = sample_seq // TOK_TILE
    tri = jnp.asarray(np.tril(np.ones((TOK_TILE, TOK_TILE), np.float32), -1), BF16)

    def mod_row(i):
        return jnp.where(i < npt, 0, 1 + (i - npt) // tiles_per_sample_seq)

    const = lambda i: (0, 0)
    tok = lambda w: pl.BlockSpec((TOK_TILE, w), lambda i: (i, 0))
    return pl.pallas_call(
        functools.partial(_post_kernel, npt, alpha),
        out_shape=(jax.ShapeDtypeStruct((n_tok, D_MODEL), F32),
                   jax.ShapeDtypeStruct((n_tok * SUBLANES, LANES), F32),
                   jax.ShapeDtypeStruct((N_EXPERTS, n_tok), F32),
                   jax.ShapeDtypeStruct((N_EXPERTS, n_tok), F32),
                   jax.ShapeDtypeStruct((n_tok, LANES), F32),
                   jax.ShapeDtypeStruct((1, LANES), F32),
                   jax.ShapeDtypeStruct((n_tok // TOK_TILE, 1, LANES), F32)),
        grid=(n_tok // TOK_TILE,),
        in_specs=[tok(REC_WIDTH), tok(REC_WIDTH),
                  pl.BlockSpec((TOK_TILE, REC_WIDTH), lambda i: (i, 4)),
                  tok(CONV_WIDTH),
                  pl.BlockSpec((TOK_TILE, D_MODEL), lambda i: (jnp.minimum(i, npt - 1), 0)),
                  pl.BlockSpec((TOK_TILE, D_MODEL), lambda i: (jnp.maximum(i - npt, 0), 0)),
                  pl.BlockSpec((1, 1, 6 * D_MODEL), lambda i: (mod_row(i), 0, 0)),
                  pl.BlockSpec(w_out_bf.shape, const),
                  pl.BlockSpec((1, REC_WIDTH), const),
                  pl.BlockSpec((1, D_MODEL), const), pl.BlockSpec((1, D_MODEL), const),
                  pl.BlockSpec((D_MODEL, LANES), const), pl.BlockSpec((D_MODEL, LANES), const),
                  pl.BlockSpec((1, LANES), const),
                  pl.BlockSpec((TOK_TILE, TOK_TILE), const)],
        out_specs=(tok(D_MODEL), pl.BlockSpec((TOK_TILE * SUBLANES, LANES), lambda i: (i, 0)),
                   pl.BlockSpec((N_EXPERTS, TOK_TILE), lambda i: (0, i)),
                   pl.BlockSpec((N_EXPERTS, TOK_TILE), lambda i: (0, i)),
                   tok(LANES), pl.BlockSpec((1, LANES), const),
                   pl.BlockSpec((1, 1, LANES), lambda i: (i, 0, 0))),
        scratch_shapes=[pltpu.VMEM((1, LANES), F32)],
        compiler_params=pltpu.CompilerParams(dimension_semantics=("arbitrary",),
                                             vmem_limit_bytes=VMEM_LIMIT),
        name="post_mixer_router",
    )(o_f, o_b, rec, conv, xp, xs, mod3, w_out_bf, rec_g, ln1_g, ln1_b, rw_hi, rw_lo, rb, tri)


PLAN_ROWS = 32


def _plan_kernel(n_tok, tile_e, gstart, n_act, cum, starts_ref, ends_ref, rank_t_ref, slot_t_ref,
                 src_ref, dst_ref, acc_ref):
    i = pl.program_id(0)

    @pl.when(i < n_act[0])
    def _():
        e = tile_e[i]
        base = i * EXP_TILE - gstart[e]
        rows_iota = lax.broadcasted_iota(jnp.int32, (PLAN_ROWS, 1), 0)
        lane = lax.broadcasted_iota(jnp.int32, (1, TOK_TILE), 1)
        acc_ref[...] = jnp.zeros_like(acc_ref)

        base_f = base.astype(F32)
        win0 = jnp.sum((ends_ref[pl.ds(e, 1), :] <= base_f).astype(jnp.int32))
        win1 = jnp.sum((starts_ref[pl.ds(e, 1), :] < base_f + EXP_TILE).astype(jnp.int32))

        def window(b, carry):
            tok0 = pl.multiple_of(b * TOK_TILE, TOK_TILE)
            rk = rank_t_ref[pl.ds(e, 1), pl.ds(tok0, TOK_TILE)]
            sl = slot_t_ref[pl.ds(e, 1), pl.ds(tok0, TOK_TILE)]
            code = ((tok0 + lane) * TOP_K).astype(F32) + sl + 1.0
            lo = jnp.maximum(cum[b * N_EXPERTS + e] - base, 0)
            hi = jnp.minimum(cum[(b + 1) * N_EXPERTS + e] - base, EXP_TILE)

            def rows(k, c):
                r0 = pl.multiple_of(k * PLAN_ROWS, PLAN_ROWS)
                want = (base + r0 + rows_iota).astype(F32)
                hit = jnp.where(rk == want, code, 0.0)
                part = hit[:, 0:LANES]
                for c0 in range(LANES, TOK_TILE, LANES):
                    part = part + hit[:, c0:c0 + LANES]
                acc_ref[pl.ds(r0, PLAN_ROWS), :] += part
                return c

            shift = PLAN_ROWS.bit_length() - 1
            lax.fori_loop(lo >> shift, (hi + PLAN_ROWS - 1) >> shift, rows, 0)
            return carry

        lax.fori_loop(win0, win1, window, 0)
        code = jnp.sum(acc_ref[...].T, axis=0, keepdims=True).astype(jnp.int32) - 1
        pos = lax.broadcasted_iota(jnp.int32, (1, EXP_TILE), 1)
        tok = code >> 2
        src = jnp.where(code >= 0, tok, lax.rem(i * EXP_TILE + pos, n_tok))
        spare = TOP_K * n_tok + lax.rem(i, N_YBUF) * EXP_TILE + pos
        dst = jnp.where(code >= 0, (code & (TOP_K - 1)) * n_tok + tok, spare)
        src_ref[0] = src * SUBLANES
        dst_ref[0] = dst * SUBLANES

    @pl.when(i >= n_act[0])
    def _():
        src_ref[...] = jnp.zeros_like(src_ref)
        dst_ref[...] = jnp.zeros_like(dst_ref)


def _plan(rank_t, slot_t, tile_e, gstart, n_act, cum2d):
    n_tok = rank_t.shape[1]
    n_tiles = tile_e.shape[0]
    n_tok_tiles = cum2d.shape[0] - 1
    assert n_tok_tiles <= LANES
    huge = jnp.float32(2.0 ** 30)
    as_rows = lambda a: jnp.pad(a.T.astype(F32), ((0, 0), (0, LANES - n_tok_tiles)),
                                constant_values=huge)
    whole = lambda i, *_: (0, 0)
    per_tile = pl.BlockSpec((1, 1, EXP_TILE), lambda i, *_: (i, 0, 0))
    grid_spec = pltpu.PrefetchScalarGridSpec(
        num_scalar_prefetch=4,
        grid=(n_tiles,),
        in_specs=[pl.BlockSpec((N_EXPERTS, LANES), whole), pl.BlockSpec((N_EXPERTS, LANES), whole),
                  pl.BlockSpec(rank_t.shape, whole), pl.BlockSpec(slot_t.shape, whole)],
        out_specs=(per_tile, per_tile),
        scratch_shapes=[pltpu.VMEM((EXP_TILE, LANES), F32)],
    )
    shape = jax.ShapeDtypeStruct((n_tiles, 1, EXP_TILE), jnp.int32)
    return pl.pallas_call(
        functools.partial(_plan_kernel, n_tok),
        out_shape=(shape, shape),
        grid_spec=grid_spec,
        compiler_params=pltpu.CompilerParams(dimension_semantics=("arbitrary",),
                                             vmem_limit_bytes=VMEM_LIMIT),
        name="moe_plan",
    )(tile_e, gstart, n_act, cum2d.reshape(-1), as_rows(cum2d[:-1]), as_rows(cum2d[1:]),
      rank_t, slot_t)


N_YBUF = 3
DMA_UNROLL = 8


def _expert_kernel(tile_e, n_act, idx_ref, h_ref, wgu_ref, bgu_ref, wd_ref, bd_ref, yk_ref, xbuf,
                   ybuf, gsem, ssem, wgu_bf, wd_bf, gu_ref):
    i = pl.program_id(0)
    n_steps = pl.num_programs(0)
    n_active = n_act[0]
    tile_rows = EXP_TILE * SUBLANES
    n_slot_rows = yk_ref.shape[0] - N_YBUF * tile_rows

    def row_tile(ref, start):
        return ref.at[pl.ds(pl.multiple_of(start, SUBLANES), SUBLANES)]

    def for_each_row(fn):
        def body(g, carry):
            for k in range(DMA_UNROLL):
                fn(g * DMA_UNROLL + k)
            return carry
        lax.fori_loop(0, EXP_TILE // DMA_UNROLL, body, 0)

    def on_slot(slot, n, fn):
        if isinstance(slot, int):
            fn(slot)
            return
        for s in range(n):
            pl.when(slot == s)(functools.partial(fn, s))

    def gather_rows(which, slot):
        def run(s):
            def one(r):
                pltpu.make_async_copy(row_tile(h_ref, idx_ref[0, 0, which * EXP_TILE + r]),
                                      row_tile(xbuf.at[s], r * SUBLANES), gsem.at[s]).start()
            for_each_row(one)
        on_slot(slot, 2, run)

    def scatter_rows(slot):
        def run(s):
            def one(r):
                pltpu.make_async_copy(row_tile(ybuf.at[s], r * SUBLANES),
                                      row_tile(yk_ref, idx_ref[0, 0, 2 * EXP_TILE + r]),
                                      ssem.at[s]).start()
            for_each_row(one)
        on_slot(slot, N_YBUF, run)

    def scatter_wait(slot):
        pltpu.make_async_copy(ybuf.at[slot], yk_ref.at[pl.ds(0, tile_rows)], ssem.at[slot]).wait()

    @pl.when(i == 0)
    def _():
        for s in range(N_YBUF):
            ybuf[s] = jnp.zeros((tile_rows, LANES), F32)
        for s in range(N_YBUF - 1):
            pltpu.make_async_copy(
                ybuf.at[s], yk_ref.at[pl.ds(n_slot_rows + s * tile_rows, tile_rows)],
                ssem.at[s]).start()
        gather_rows(0, 0)

    prev = tile_e[jnp.maximum(i - 1, 0)]
    new_expert = jnp.logical_or(i == 0, tile_e[i] != prev)

    @pl.when(jnp.logical_and(new_expert, i < n_active))
    def _():
        rows = 128

        def cast(r, carry):
            sl = pl.ds(pl.multiple_of(r * rows, rows), rows)
            wgu_bf[sl, :] = wgu_ref[0, sl, :].astype(BF16)
            wd_bf[sl, :] = wd_ref[0, sl, :].astype(BF16)
            return carry

        lax.fori_loop(0, D_MODEL // rows, cast, 0)

    @pl.when(i < n_active)
    def _():
        xslot = i % 2
        yslot = i % N_YBUF
        pltpu.make_async_copy(h_ref.at[pl.ds(0, tile_rows)], xbuf.at[xslot], gsem.at[xslot]).wait()

        scatter_wait(yslot)
        next_x = xbuf.at[(i + 1) % 2]
        next_sem = gsem.at[(i + 1) % 2]
        prev_y = ybuf.at[(i + N_YBUF - 1) % N_YBUF]
        prev_sem = ssem.at[(i + N_YBUF - 1) % N_YBUF]

        def up_projection(_, carry):
            x = _load_row_tiles(xbuf.at[xslot], EXP_TILE).astype(BF16)
            gu_ref[...] = _dot(x, wgu_bf[...]) + bgu_ref[0]
            for r in range(EXP_TILE):
                pltpu.make_async_copy(row_tile(h_ref, idx_ref[0, 0, EXP_TILE + r]),
                                      row_tile(next_x, r * SUBLANES),
                                      next_sem).start(priority=r % 2)
            return carry

        lax.fori_loop(0, jnp.minimum(n_active - i, 1), up_projection, 0)
        gu = gu_ref[...]
        gate = jnp.minimum(gu[:, :D_FF], SWIGLU_LIMIT)
        up = jnp.clip(gu[:, D_FF:], -SWIGLU_LIMIT, SWIGLU_LIMIT)
        act = (up + 1.0) * gate * _sigmoid(SWIGLU_ALPHA * gate)
        for r in range(EXP_TILE):
            pltpu.make_async_copy(row_tile(prev_y, r * SUBLANES),
                                  row_tile(yk_ref, idx_ref[0, 0, 3 * EXP_TILE + r]),
                                  prev_sem).start(priority=r % 2)
        _store_row_tiles(ybuf.at[yslot], _dot(act.astype(BF16), wd_bf[...]) + bd_ref[0])

        @pl.when(i == n_active - 1)
        def _():
            scatter_rows(yslot)

    @pl.when(i == n_steps - 1)
    def _():
        for back in range(1, N_YBUF + 1):
            scatter_wait((n_active - back) % N_YBUF)
        last_x = n_active % 2
        pltpu.make_async_copy(h_ref.at[pl.ds(0, tile_rows)], xbuf.at[last_x], gsem.at[last_x]).wait()


def _experts(h2, src3, dst3, tile_e, n_act, w_gate_up, b_gate_up, w_down, b_down):
    n_tok = h2.shape[0] // SUBLANES
    n_tiles = src3.shape[0]
    src_next = jnp.concatenate([src3[1:], src3[-1:]], axis=0)
    spare_rows = TOP_K * n_tok + (N_YBUF - 1) * EXP_TILE + jnp.arange(EXP_TILE, dtype=jnp.int32)
    dst_prev = jnp.concatenate([(spare_rows * SUBLANES).reshape(1, 1, EXP_TILE), dst3[:-1]], axis=0)
    idx3 = jnp.concatenate([src3, src_next, dst3, dst_prev], axis=-1)
    grid_spec = pltpu.PrefetchScalarGridSpec(
        num_scalar_prefetch=2,
        grid=(n_tiles,),
        in_specs=[pl.BlockSpec((1, 1, 4 * EXP_TILE), lambda i, te, na: (i, 0, 0),
                               memory_space=pltpu.SMEM),
                  pl.BlockSpec(memory_space=pl.ANY),
                  pl.BlockSpec((1, D_MODEL, 2 * D_FF), lambda i, te, na: (te[i], 0, 0)),
                  pl.BlockSpec((1, 1, 2 * D_FF), lambda i, te, na: (te[i], 0, 0)),
                  pl.BlockSpec((1, D_FF, D_MODEL), lambda i, te, na: (te[i], 0, 0)),
                  pl.BlockSpec((1, 1, D_MODEL), lambda i, te, na: (te[i], 0, 0))],
        out_specs=pl.BlockSpec(memory_space=pl.ANY),
        scratch_shapes=[pltpu.VMEM((2, EXP_TILE * SUBLANES, LANES), F32),
                        pltpu.VMEM((N_YBUF, EXP_TILE * SUBLANES, LANES), F32),
                        pltpu.SemaphoreType.DMA((2,)), pltpu.SemaphoreType.DMA((N_YBUF,)),
                        pltpu.VMEM((D_MODEL, 2 * D_FF), BF16), pltpu.VMEM((D_FF, D_MODEL), BF16),
                        pltpu.VMEM((EXP_TILE, 2 * D_FF), F32)],
    )
    return pl.pallas_call(
        _expert_kernel,
        out_shape=jax.ShapeDtypeStruct(((TOP_K * n_tok + N_YBUF * EXP_TILE) * SUBLANES, LANES), F32),
        grid_spec=grid_spec,
        compiler_params=pltpu.CompilerParams(dimension_semantics=("arbitrary",),
                                             vmem_limit_bytes=VMEM_LIMIT),
        name="moe_experts",
    )(tile_e, n_act, idx3, h2, w_gate_up, b_gate_up.reshape(N_EXPERTS, 1, 2 * D_FF),
      w_down, b_down.reshape(N_EXPERTS, 1, D_MODEL))


COMBINE_TILE = 512


def _combine_kernel(n_prompt_tiles, alpha, y0_ref, y1_ref, y2_ref, y3_ref, x1_ref, w4_ref, mod_ref,
                    l2g_ref, l2b_ref, yp_ref, ysm_ref):
    i = pl.program_id(0)
    w4 = w4_ref[...]
    ff = _load_row_tiles(y0_ref, COMBINE_TILE) * w4[:, 0:1]
    for j, y_ref in enumerate((y1_ref, y2_ref, y3_ref), start=1):
        ff = ff + _load_row_tiles(y_ref, COMBINE_TILE) * w4[:, j:j + 1]
    g2 = mod_ref[0][:, 5 * D_MODEL:6 * D_MODEL]
    out = _layer_norm(alpha * x1_ref[...] + g2 * ff) * l2g_ref[...] + l2b_ref[...]

    @pl.when(i < n_prompt_tiles)
    def _():
        yp_ref[...] = out

    @pl.when(i >= n_prompt_tiles)
    def _():
        ysm_ref[...] = out


def _combine(yk, x1, w4, mod3, ln2_g, ln2_b, n_p, alpha, sample_seq):
    n_tok = x1.shape[0]
    n_s = n_tok - n_p
    assert n_p % COMBINE_TILE == 0 and sample_seq % COMBINE_TILE == 0
    npt = n_p // COMBINE_TILE
    n_tok_tiles = n_tok // COMBINE_TILE
    tiles_per_sample_seq = sample_seq // COMBINE_TILE

    def mod_row(i):
        return jnp.where(i < npt, 0, 1 + (i - npt) // tiles_per_sample_seq)

    def slot(j):
        return pl.BlockSpec((COMBINE_TILE * SUBLANES, LANES), lambda i: (j * n_tok_tiles + i, 0))

    const = lambda i: (0, 0)
    return pl.pallas_call(
        functools.partial(_combine_kernel, npt, alpha),
        out_shape=(jax.ShapeDtypeStruct((n_p, D_MODEL), F32),
                   jax.ShapeDtypeStruct((n_s, D_MODEL), F32)),
        grid=(n_tok_tiles,),
        in_specs=[slot(0), slot(1), slot(2), slot(3),
                  pl.BlockSpec((COMBINE_TILE, D_MODEL), lambda i: (i, 0)),
                  pl.BlockSpec((COMBINE_TILE, LANES), lambda i: (i, 0)),
                  pl.BlockSpec((1, 1, 6 * D_MODEL), lambda i: (mod_row(i), 0, 0)),
                  pl.BlockSpec((1, D_MODEL), const), pl.BlockSpec((1, D_MODEL), const)],
        out_specs=(pl.BlockSpec((COMBINE_TILE, D_MODEL), lambda i: (jnp.minimum(i, npt - 1), 0)),
                   pl.BlockSpec((COMBINE_TILE, D_MODEL), lambda i: (jnp.maximum(i - npt, 0), 0))),
        compiler_params=pltpu.CompilerParams(dimension_semantics=("arbitrary",),
                                             vmem_limit_bytes=VMEM_LIMIT),
        name="moe_combine",
    )(yk, yk, yk, yk, x1, w4, mod3, ln2_g, ln2_b)


def kernel(x_prompt, x_sample, c, state_fwd, state_bwd, c_ctx, w_ada, b_ada, w_in, conv_w, conv_b,
           conv_ln_g, conv_ln_b, lb_logits, rec_norm_g, w_out, ln1_g, ln1_b, ln2_g, ln2_b,
           router_w, router_b, w_gate_up, b_gate_up, w_down, b_down):
    depth = w_ada.shape[0]
    assert depth == 1
    alpha = (2.0 * depth) ** 0.25
    nb_p, seq_p, _ = x_prompt.shape
    nb_s, seq_s, _ = x_sample.shape
    n_p, n_s = nb_p * seq_p, nb_s * seq_s
    n_tok = n_p + n_s
    row2 = lambda a: a.reshape(1, -1)

    lb = jnp.cumsum(jax.nn.softmax(lb_logits.astype(F32), axis=0), axis=0)[0]

    cond = jnp.concatenate([c_ctx[None, :], c, jnp.zeros((16 - 1 - nb_s, D_MODEL), F32)], axis=0)
    mod = _ada(cond, w_ada[0], b_ada[0])
    mod3 = mod.reshape(16, 1, 6 * D_MODEL)

    xp = x_prompt.reshape(n_p, D_MODEL)
    xs = x_sample.reshape(n_s, D_MODEL)
    conv_w_pad = jnp.concatenate([conv_w[0], jnp.zeros((1, CONV_WIDTH), F32)], axis=0)
    conv_out, rec = _inproj(xp, xs, mod3, w_in[0].astype(BF16), conv_w_pad, row2(conv_b[0]),
                            row2(conv_ln_g[0]), row2(conv_ln_b[0]), seq_p, seq_s)

    zeros_state = jnp.zeros((nb_p, REC_HEADS, REC_DK, REC_DK), F32)
    s0f = jnp.concatenate([zeros_state, jnp.swapaxes(state_fwd[:, 0], -1, -2)], axis=0)
    s0b = jnp.concatenate([zeros_state, jnp.swapaxes(state_bwd[:, 0], -1, -2)], axis=0)
    o_f, o_b, sf_t, sb_t = _scan(rec, lb, s0f, s0b, [seq_p] * nb_p + [seq_s] * nb_s)
    new_f = jnp.swapaxes(sf_t[:nb_p], -1, -2)[:, None]
    new_b = jnp.swapaxes(sb_t[:nb_p], -1, -2)[:, None]

    rw = jnp.pad(router_w[0], ((0, 0), (0, LANES - N_EXPERTS)))
    rw_hi = rw.astype(BF16)
    rw_lo = (rw - rw_hi.astype(F32)).astype(BF16)
    rb = jnp.pad(router_b[0], (0, LANES - N_EXPERTS)).reshape(1, LANES)
    x1, h2, rank_t, slot_t, w4, counts, tile_start = _post(
        o_f, o_b, rec, conv_out, xp, xs, mod3, w_out[0].astype(BF16), row2(rec_norm_g[0]),
        row2(ln1_g[0]), row2(ln1_b[0]), rw_hi, rw_lo, rb, alpha, seq_s)

    cnt = counts[0, :N_EXPERTS].astype(jnp.int32)
    gpad = ((cnt + EXP_TILE - 1) // EXP_TILE) * EXP_TILE
    gend = jnp.cumsum(gpad)
    gstart = gend - gpad
    n_rows = n_tok * TOP_K + N_EXPERTS * EXP_TILE
    n_tiles = n_rows // EXP_TILE
    n_act = gend[-1] // EXP_TILE
    tile_ids = jnp.minimum(jnp.arange(n_tiles, dtype=jnp.int32), n_act - 1)
    tile_e = jnp.sum((gend[None, :] <= tile_ids[:, None] * EXP_TILE).astype(jnp.int32), axis=1)
    n_act1 = n_act.reshape(1).astype(jnp.int32)
    starts = tile_start[:, 0, :N_EXPERTS].astype(jnp.int32)
    cum2d = jnp.concatenate([starts, cnt[None, :]], axis=0)
    src, dst = _plan(rank_t, slot_t, tile_e, gstart.astype(jnp.int32), n_act1, cum2d)
    yk = _experts(h2, src, dst, tile_e, n_act1, w_gate_up[0], b_gate_up[0], w_down[0], b_down[0])
    y_p, y_s = _combine(yk, x1, w4, mod3, row2(ln2_g[0]), row2(ln2_b[0]), n_p, alpha, seq_s)
    return (y_p.reshape(nb_p, seq_p, D_MODEL), y_s.reshape(nb_s, seq_s, D_MODEL), new_f, new_b)
```

```python
import functools

import numpy as np
import jax
import jax.numpy as jnp
from jax import lax
from jax.experimental import pallas as pl
from jax.experimental.pallas import tpu as pltpu

F32 = jnp.float32
BF16 = jnp.bfloat16

D_MODEL = 1024
CONV_WIDTH = 512
CONV_TAPS = 31
REC_HEADS = 4
REC_DK = 128
REC_WIDTH = REC_HEADS * REC_DK
REC_COLS = 5 * REC_WIDTH
CHUNK = 64
N_LEVELS = 6
SCAN_CHUNKS = 4
N_EXPERTS = 32
TOP_K = 4
D_FF = 1024
SWIGLU_LIMIT = 7.0
SWIGLU_ALPHA = 1.702
LN_EPS = 1e-5
RMS_EPS = 1e-6
GRID_W = 64

LANES = 128
TOK_TILE = 512
EXP_TILE = 512
VMEM_LIMIT = 56 * 1024 * 1024


def _sigmoid(x):
    return 1.0 / (1.0 + jnp.exp(-x))


def _layer_norm(x):
    mu = jnp.mean(x, axis=-1, keepdims=True)
    xc = x - mu
    var = jnp.mean(xc * xc, axis=-1, keepdims=True)
    return xc * lax.rsqrt(var + LN_EPS)


def _dot(a, b):
    return jnp.dot(a, b, preferred_element_type=F32)


def _dot_nt(a, b):
    return lax.dot_general(a, b, (((1,), (1,)), ((), ())), preferred_element_type=F32)


SUBLANES = 8
assert D_MODEL == SUBLANES * LANES


def _store_row_tiles(ref, value):
    for s in range(SUBLANES):
        ref[pl.ds(s, value.shape[0], stride=SUBLANES), :] = value[:, s * LANES:(s + 1) * LANES]


def _load_row_tiles(ref, n_rows):
    return jnp.concatenate(
        [ref[pl.ds(s, n_rows, stride=SUBLANES), :] for s in range(SUBLANES)], axis=-1)


def _split3(x):
    hi = x.astype(BF16)
    r1 = x - hi.astype(F32)
    mid = r1.astype(BF16)
    lo = (r1 - mid.astype(F32)).astype(BF16)
    return hi, mid, lo


def _ada_kernel(c_ref, w_ref, b_ref, o_ref):
    c = c_ref[...]
    s = (c * _sigmoid(c)).astype(BF16)
    o_ref[...] = _dot(s, w_ref[...].astype(BF16)) + b_ref[...]


def _ada(cond, w_ada, b_ada):
    rows = cond.shape[0]
    ncol = w_ada.shape[1]
    tn = 1024
    return pl.pallas_call(
        _ada_kernel,
        out_shape=jax.ShapeDtypeStruct((rows, ncol), F32),
        grid=(ncol // tn,),
        in_specs=[pl.BlockSpec((rows, D_MODEL), lambda j: (0, 0)),
                  pl.BlockSpec((D_MODEL, tn), lambda j: (0, j)),
                  pl.BlockSpec((1, tn), lambda j: (0, j))],
        out_specs=pl.BlockSpec((rows, tn), lambda j: (0, j)),
        compiler_params=pltpu.CompilerParams(dimension_semantics=("parallel",),
                                             vmem_limit_bytes=VMEM_LIMIT),
        name="ada_mod",
    )(cond, w_ada, b_ada.reshape(1, ncol))


def _inproj_kernel(n_prompt_tiles, prompt_seg, xp_ref, xs_ref, mod_ref, win_ref, cw_ref, cb_ref,
                   cg_ref, cbeta_ref, conv_ref, rec_ref):
    i = pl.program_id(0)
    is_prompt = i < n_prompt_tiles
    x = jnp.where(is_prompt, xp_ref[...], xs_ref[...])
    mod = mod_ref[0]
    sh1 = mod[:, 0:D_MODEL]
    sc1 = mod[:, D_MODEL:2 * D_MODEL]
    h = (_layer_norm(x) * (1.0 + sc1) + sh1).astype(BF16)

    glu = _dot(h, win_ref[:, 0:2 * CONV_WIDTH])
    u = glu[:, :CONV_WIDTH] * _sigmoid(glu[:, CONV_WIDTH:])

    seg_len = jnp.where(is_prompt, prompt_seg, GRID_W)
    half = CONV_TAPS // 2
    pos = lax.broadcasted_iota(jnp.int32, (TOK_TILE, 1), 0) & (seg_len - 1)

    def masked_shift(d):
        sh = pltpu.roll(u, (-d) % TOK_TILE, axis=0)
        edge_rows = SUBLANES * ((abs(d) + SUBLANES - 1) // SUBLANES)
        pieces = []
        for s0 in range(0, TOK_TILE, GRID_W):
            if d < 0:
                edge = slice(s0, s0 + edge_rows)
                pieces.append(jnp.where(pos[edge] >= -d, sh[edge], 0.0))
                pieces.append(sh[s0 + edge_rows:s0 + GRID_W])
            else:
                edge = slice(s0 + GRID_W - edge_rows, s0 + GRID_W)
                pieces.append(sh[s0:s0 + GRID_W - edge_rows])
                pieces.append(jnp.where(pos[edge] < seg_len - d, sh[edge], 0.0))
        return jnp.concatenate(pieces, axis=0)

    acc = u * cw_ref[half:half + 1, :] + cb_ref[...]
    taps = [j for j in range(CONV_TAPS) if j != half]
    n_groups = REC_COLS // REC_WIDTH
    per_group = len(taps) // n_groups
    anchor = None
    for g in range(n_groups):
        cols = slice(g * REC_WIDTH, (g + 1) * REC_WIDTH)
        rec = _dot(h, win_ref[:, 2 * CONV_WIDTH + g * REC_WIDTH:
                              2 * CONV_WIDTH + (g + 1) * REC_WIDTH])
        rec_ref[:, cols] = rec
        for j in taps[g * per_group:(g + 1) * per_group]:
            w_tap = cw_ref[j:j + 1, :]
            if anchor is not None:
                w_tap = w_tap + anchor
                anchor = None
            acc = acc + masked_shift(j - half) * w_tap
        if g + 1 < n_groups:
            bits = pltpu.bitcast(rec[0:SUBLANES, :], jnp.uint32)
            anchor = pltpu.bitcast((bits >> 16) >> 16, F32)[0:1, :]
    y = _layer_norm(acc) * cg_ref[...] + cbeta_ref[...]
    conv_ref[...] = (y * _sigmoid(y)).astype(BF16)


def _inproj(xp, xs, mod3, w_in_bf, conv_w, conv_b, conv_g, conv_beta, prompt_seq, sample_seq):
    n_p, n_s = xp.shape[0], xs.shape[0]
    n_tok = n_p + n_s
    npt = n_p // TOK_TILE
    tiles_per_sample_seq = sample_seq // TOK_TILE
    assert TOK_TILE % prompt_seq == 0 and n_p % TOK_TILE == 0 and sample_seq % TOK_TILE == 0

    def mod_row(i):
        return jnp.where(i < npt, 0, 1 + (i - npt) // tiles_per_sample_seq)

    const = lambda i: (0, 0)
    return pl.pallas_call(
        functools.partial(_inproj_kernel, npt, prompt_seq),
        out_shape=(jax.ShapeDtypeStruct((n_tok, CONV_WIDTH), BF16),
                   jax.ShapeDtypeStruct((n_tok, REC_COLS), F32)),
        grid=(n_tok // TOK_TILE,),
        in_specs=[pl.BlockSpec((TOK_TILE, D_MODEL), lambda i: (jnp.minimum(i, npt - 1), 0)),
                  pl.BlockSpec((TOK_TILE, D_MODEL), lambda i: (jnp.maximum(i - npt, 0), 0)),
                  pl.BlockSpec((1, 1, 6 * D_MODEL), lambda i: (mod_row(i), 0, 0)),
                  pl.BlockSpec(w_in_bf.shape, const),
                  pl.BlockSpec(conv_w.shape, const),
                  pl.BlockSpec((1, CONV_WIDTH), const),
                  pl.BlockSpec((1, CONV_WIDTH), const),
                  pl.BlockSpec((1, CONV_WIDTH), const)],
        out_specs=(pl.BlockSpec((TOK_TILE, CONV_WIDTH), lambda i: (i, 0)),
                   pl.BlockSpec((TOK_TILE, REC_COLS), lambda i: (i, 0))),
        compiler_params=pltpu.CompilerParams(dimension_semantics=("parallel",),
                                             vmem_limit_bytes=VMEM_LIMIT),
        name="inproj_conv",
    )(xp, xs, mod3, w_in_bf, conv_w, conv_b, conv_g, conv_beta)


def _scan_constants():
    c = CHUNK
    w = np.zeros((N_LEVELS + 2, c, c), np.float32)
    m = np.zeros((N_LEVELS + 1, c, c), np.float32)
    t = np.arange(c)
    for j in range(N_LEVELS):
        half = 1 << j
        for ti in range(c):
            ref = (ti & ~(2 * half - 1)) + half - 1
            if ti & half:
                w[j, ti, ref + 1:ti + 1] = 1.0
            else:
                w[j, ti, ti + 1:ref + 1] = 1.0
        upper = (t[:, None] & half) != 0
        lower = (t[None, :] & half) == 0
        same = (t[:, None] >> (j + 1)) == (t[None, :] >> (j + 1))
        m[j] = (upper & lower & same).astype(np.float32)
    w[N_LEVELS] = (t[None, :] <= t[:, None]).astype(np.float32)
    w[N_LEVELS + 1] = (t[None, :] > t[:, None]).astype(np.float32)
    m[N_LEVELS] = np.eye(c, dtype=np.float32)
    w_f = np.tile(w.reshape(-1, c), (1, 3))
    w_b = np.tile(w[:, ::-1, ::-1].reshape(-1, c), (1, 3))
    m_b = m[:, ::-1, ::-1]
    return (jnp.asarray(w_f, BF16), jnp.asarray(np.ascontiguousarray(w_b), BF16),
            jnp.asarray(m), jnp.asarray(np.ascontiguousarray(m_b)))


def _scan_gates(z, lb, w_ref):
    e = jnp.exp(-jnp.abs(z))
    r = 1.0 / (1.0 + e)
    a = e * r
    sig = jnp.where(z >= 0, r, a)
    sig_neg = jnp.where(z >= 0, a, r)
    one_m_lb = 1.0 - lb
    log_f = jnp.log2(lb + one_m_lb * sig)
    k = one_m_lb * sig_neg
    terms = jnp.concatenate(_split3(log_f), axis=0)
    ex = jnp.exp2(_dot(w_ref[...], terms))
    return k, ex


def _scan_intra(q, k, ex, m_ref, neighbour):
    c = CHUNK
    qb, kb = q.astype(BF16), k.astype(BF16)
    a_mat = m_ref[N_LEVELS] * jnp.sum(q * k, axis=-1, keepdims=True)
    x0 = ex[0:c]
    pair = jnp.sum((q * x0) * pltpu.roll(k * x0, neighbour, axis=0), axis=-1, keepdims=True)
    a_mat = a_mat + m_ref[0] * pair
    for j in range(1, N_LEVELS):
        xj = ex[j * c:(j + 1) * c].astype(BF16)
        a_mat = a_mat + m_ref[j] * _dot_nt(qb * xj, kb * xj)
    return a_mat


def _scan_kernel(fblk, bblk, first, seq, qf_ref, zf_ref, vf_ref, qb_ref, zb_ref, vb_ref, lb_ref,
                 wf_ref, wb_ref, mf_ref, mb_ref, s0f_ref, s0b_ref, of_ref, ob_ref, sf_ref, sb_ref):
    s = pl.program_id(0)

    @pl.when(first[s] == 1)
    def _():
        sf_ref[...] = s0f_ref[...]
        sb_ref[...] = s0b_ref[...]

    c = CHUNK
    dirs = ((qf_ref, zf_ref, vf_ref, 0, wf_ref, mf_ref, sf_ref, of_ref, c - 1,
             tuple(range(SCAN_CHUNKS))),
            (qb_ref, zb_ref, vb_ref, 1, wb_ref, mb_ref, sb_ref, ob_ref, 0,
             tuple(reversed(range(SCAN_CHUNKS)))))
    units = [(d, slice(sub * c, (sub + 1) * c)) for d in dirs for sub in d[-1]]
    gates = [_scan_gates(d[1][rows, :], lb_ref[d[3]:d[3] + 1, :], d[4]) for d, rows in units]
    intra = []
    for (d, rows), (k, ex) in zip(units, gates):
        for h in range(REC_HEADS):
            cols = slice(h * REC_DK, (h + 1) * REC_DK)
            neighbour = 1 if d[8] == c - 1 else c - 1
            intra.append(_scan_intra(d[0][rows, cols], k[:, cols], ex[:, cols], d[5], neighbour))
    idx = 0
    for (d, rows), (k, ex) in zip(units, gates):
        q_ref, _, v_ref, _, _, _, st_ref, o_ref, last, _ = d
        for h in range(REC_HEADS):
            cols = slice(h * REC_DK, (h + 1) * REC_DK)
            q, v = q_ref[rows, cols], v_ref[rows, cols]
            ex_b = ex[N_LEVELS * c:(N_LEVELS + 1) * c, cols]
            ex_l = ex[(N_LEVELS + 1) * c:(N_LEVELS + 2) * c, cols]
            st = st_ref[0, h]
            o_ref[rows, cols] = (_dot(intra[idx].astype(BF16), v.astype(BF16))
                                 + _dot_nt((q * ex_b).astype(BF16), st.astype(BF16)))
            st_ref[0, h] = (st * ex_b[last:last + 1, :]
                            + _dot(v.T.astype(BF16), (k[:, cols] * ex_l).astype(BF16)))
            idx += 1


def _scan(rec, lb, s0f_t, s0b_t, seq_lens):
    n_tok = rec.shape[0]
    fblk, bblk, first, seq = [], [], [], []
    base = 0
    step_rows = CHUNK * SCAN_CHUNKS
    for si, ln in enumerate(seq_lens):
        assert ln % step_rows == 0
        n = ln // step_rows
        for ci in range(n):
            fblk.append(base + ci)
            bblk.append(base + n - 1 - ci)
            first.append(1 if ci == 0 else 0)
            seq.append(si)
        base += n
    steps = len(fblk)
    as_i32 = lambda a: jnp.asarray(np.asarray(a, np.int32))
    w_f, w_b, m_f, m_b = _scan_constants()
    n_seq = len(seq_lens)

    def col(block_of, c):
        return pl.BlockSpec((step_rows, REC_WIDTH), lambda s, fb, bb, fi, sq: (block_of(fb, bb)[s], c))

    fwd = lambda fb, bb: fb
    bwd = lambda fb, bb: bb
    const2 = lambda s, fb, bb, fi, sq: (0, 0)
    const3 = lambda s, fb, bb, fi, sq: (0, 0, 0)
    state_spec = pl.BlockSpec((1, REC_HEADS, REC_DK, REC_DK), lambda s, fb, bb, fi, sq: (sq[s], 0, 0, 0))
    grid_spec = pltpu.PrefetchScalarGridSpec(
        num_scalar_prefetch=4,
        grid=(steps,),
        in_specs=[col(fwd, 0), col(fwd, 1), col(fwd, 3), col(bwd, 0), col(bwd, 2), col(bwd, 3),
                  pl.BlockSpec((2, REC_WIDTH), const2),
                  pl.BlockSpec(w_f.shape, const2), pl.BlockSpec(w_b.shape, const2),
                  pl.BlockSpec(m_f.shape, const3), pl.BlockSpec(m_b.shape, const3),
                  state_spec, state_spec],
        out_specs=(pl.BlockSpec((step_rows, REC_WIDTH), lambda s, fb, bb, fi, sq: (fb[s], 0)),
                   pl.BlockSpec((step_rows, REC_WIDTH), lambda s, fb, bb, fi, sq: (bb[s], 0)),
                   state_spec, state_spec),
    )
    st_shape = jax.ShapeDtypeStruct((n_seq, REC_HEADS, REC_DK, REC_DK), F32)
    return pl.pallas_call(
        _scan_kernel,
        out_shape=(jax.ShapeDtypeStruct((n_tok, REC_WIDTH), F32),
                   jax.ShapeDtypeStruct((n_tok, REC_WIDTH), F32), st_shape, st_shape),
        grid_spec=grid_spec,
        compiler_params=pltpu.CompilerParams(dimension_semantics=("arbitrary",),
                                             vmem_limit_bytes=VMEM_LIMIT),
        name="hgrn2_scan",
    )(as_i32(fblk), as_i32(bblk), as_i32(first), as_i32(seq),
      rec, rec, rec, rec, rec, rec, lb, w_f, w_b, m_f, m_b, s0f_t, s0b_t)


def _post_kernel(n_prompt_tiles, alpha, of_ref, ob_ref, g_ref, conv_ref, xp_ref, xs_ref, mod_ref,
                 wout_ref, rg_ref, l1g_ref, l1b_ref, rwh_ref, rwl_ref, rb_ref, tri_ref,
                 x1_ref, h2_ref, rank_t_ref, slot_t_ref, w4_ref, cnt_ref, start_ref, run_ref):
    i = pl.program_id(0)

    @pl.when(i == 0)
    def _():
        run_ref[...] = jnp.zeros_like(run_ref)

    x = jnp.where(i < n_prompt_tiles, xp_ref[...], xs_ref[...])
    mod = mod_ref[0]
    g1 = mod[:, 2 * D_MODEL:3 * D_MODEL]
    sh2 = mod[:, 3 * D_MODEL:4 * D_MODEL]
    sc2 = mod[:, 4 * D_MODEL:5 * D_MODEL]

    g = g_ref[...]
    silu_g = g * _sigmoid(g)
    mix = _dot(conv_ref[...], wout_ref[0:CONV_WIDTH, :])
    for h in range(REC_HEADS):
        cols = slice(h * REC_DK, (h + 1) * REC_DK)
        o = of_ref[:, cols] + ob_ref[:, cols]
        o = o * lax.rsqrt(jnp.mean(o * o, axis=-1, keepdims=True) + RMS_EPS)
        rec_out = (o * rg_ref[:, cols] * silu_g[:, cols]).astype(BF16)
        mix = mix + _dot(rec_out, wout_ref[CONV_WIDTH + h * REC_DK:CONV_WIDTH + (h + 1) * REC_DK, :])

    x1 = _layer_norm(alpha * x + g1 * mix) * l1g_ref[...] + l1b_ref[...]
    x1_ref[...] = x1
    h2 = _layer_norm(x1) * (1.0 + sc2) + sh2
    _store_row_tiles(h2_ref, h2)

    h_hi = h2.astype(BF16)
    h_lo = (h2 - h_hi.astype(F32)).astype(BF16)
    logits = (_dot(h_hi, rwh_ref[...]) + _dot(h_lo, rwh_ref[...]) + _dot(h_hi, rwl_ref[...])
              + rb_ref[...])
    lane = lax.broadcasted_iota(jnp.int32, logits.shape, 1)
    lane_f = lane.astype(F32)
    neg_inf = jnp.float32(-jnp.inf)
    work = jnp.where(lane < N_EXPERTS, logits, neg_inf)
    vals, sels = [], []
    for _ in range(TOP_K):
        m = jnp.max(work, axis=-1, keepdims=True)
        idx = jnp.min(jnp.where(work == m, lane_f, float(LANES)), axis=-1, keepdims=True)
        sel = lane_f == idx
        vals.append(m)
        sels.append(sel)
        work = jnp.where(sel, neg_inf, work)
    exps = [jnp.exp(v - vals[0]) for v in vals]
    denom = exps[0] + exps[1] + exps[2] + exps[3]

    sel_any = jnp.zeros(logits.shape, F32)
    for sel in sels:
        sel_any = jnp.where(sel, 1.0, sel_any)
    run_old = run_ref[...]
    prior = _dot(tri_ref[...], sel_any.astype(BF16)) + run_old
    run_new = run_old + jnp.sum(sel_any, axis=0, keepdims=True)
    run_ref[...] = run_new
    cnt_ref[...] = run_new
    start_ref[0] = run_old

    slot = jnp.zeros(logits.shape, F32)
    w4 = jnp.zeros(logits.shape, F32)
    for j in range(TOP_K):
        slot = jnp.where(sels[j], float(j), slot)
        w4 = jnp.where(lane == j, exps[j] / denom, w4)
    routed = sel_any > 0.0
    rank_t_ref[...] = jnp.where(routed, prior, -1.0).T[:N_EXPERTS]
    slot_t_ref[...] = jnp.where(routed, slot, -1.0).T[:N_EXPERTS]
    w4_ref[...] = w4


def _post(o_f, o_b, rec, conv, xp, xs, mod3, w_out_bf, rec_g, ln1_g, ln1_b, rw_hi, rw_lo, rb,
          alpha, sample_seq):
    n_p, n_s = xp.shape[0], xs.shape[0]
    n_tok = n_p + n_s
    npt = n_p // TOK_TILE
    tiles_per_sample_seq = sample_seq // TOK_TILE
    tri = jnp.asarray(np.tril(np.ones((TOK_TILE, TOK_TILE), np.float32), -1), BF16)

    def mod_row(i):
        return jnp.where(i < npt, 0, 1 + (i - npt) // tiles_per_sample_seq)

    const = lambda i: (0, 0)
    tok = lambda w: pl.BlockSpec((TOK_TILE, w), lambda i: (i, 0))
    return pl.pallas_call(
        functools.partial(_post_kernel, npt, alpha),
        out_shape=(jax.ShapeDtypeStruct((n_tok, D_MODEL), F32),
                   jax.ShapeDtypeStruct((n_tok * SUBLANES, LANES), F32),
                   jax.ShapeDtypeStruct((N_EXPERTS, n_tok), F32),
                   jax.ShapeDtypeStruct((N_EXPERTS, n_tok), F32),
                   jax.ShapeDtypeStruct((n_tok, LANES), F32),
                   jax.ShapeDtypeStruct((1, LANES), F32),
                   jax.ShapeDtypeStruct((n_tok // TOK_TILE, 1, LANES), F32)),
        grid=(n_tok // TOK_TILE,),
        in_specs=[tok(REC_WIDTH), tok(REC_WIDTH),
                  pl.BlockSpec((TOK_TILE, REC_WIDTH), lambda i: (i, 4)),
                  tok(CONV_WIDTH),
                  pl.BlockSpec((TOK_TILE, D_MODEL), lambda i: (jnp.minimum(i, npt - 1), 0)),
                  pl.BlockSpec((TOK_TILE, D_MODEL), lambda i: (jnp.maximum(i - npt, 0), 0)),
                  pl.BlockSpec((1, 1, 6 * D_MODEL), lambda i: (mod_row(i), 0, 0)),
                  pl.BlockSpec(w_out_bf.shape, const),
                  pl.BlockSpec((1, REC_WIDTH), const),
                  pl.BlockSpec((1, D_MODEL), const), pl.BlockSpec((1, D_MODEL), const),
                  pl.BlockSpec((D_MODEL, LANES), const), pl.BlockSpec((D_MODEL, LANES), const),
                  pl.BlockSpec((1, LANES), const),
                  pl.BlockSpec((TOK_TILE, TOK_TILE), const)],
        out_specs=(tok(D_MODEL), pl.BlockSpec((TOK_TILE * SUBLANES, LANES), lambda i: (i, 0)),
                   pl.BlockSpec((N_EXPERTS, TOK_TILE), lambda i: (0, i)),
                   pl.BlockSpec((N_EXPERTS, TOK_TILE), lambda i: (0, i)),
                   tok(LANES), pl.BlockSpec((1, LANES), const),
                   pl.BlockSpec((1, 1, LANES), lambda i: (i, 0, 0))),
        scratch_shapes=[pltpu.VMEM((1, LANES), F32)],
        compiler_params=pltpu.CompilerParams(dimension_semantics=("arbitrary",),
                                             vmem_limit_bytes=VMEM_LIMIT),
        name="post_mixer_router",
    )(o_f, o_b, rec, conv, xp, xs, mod3, w_out_bf, rec_g, ln1_g, ln1_b, rw_hi, rw_lo, rb, tri)


PLAN_ROWS = 32


def _plan_kernel(n_tok, tile_e, gstart, n_act, cum, starts_ref, ends_ref, rank_t_ref, slot_t_ref,
                 src_ref, dst_ref, acc_ref):
    i = pl.program_id(0)

    @pl.when(i < n_act[0])
    def _():
        e = tile_e[i]
        base = i * EXP_TILE - gstart[e]
        rows_iota = lax.broadcasted_iota(jnp.int32, (PLAN_ROWS, 1), 0)
        lane = lax.broadcasted_iota(jnp.int32, (1, TOK_TILE), 1)
        acc_ref[...] = jnp.zeros_like(acc_ref)

        base_f = base.astype(F32)
        win0 = jnp.sum((ends_ref[pl.ds(e, 1), :] <= base_f).astype(jnp.int32))
        win1 = jnp.sum((starts_ref[pl.ds(e, 1), :] < base_f + EXP_TILE).astype(jnp.int32))

        def window(b, carry):
            tok0 = pl.multiple_of(b * TOK_TILE, TOK_TILE)
            rk = rank_t_ref[pl.ds(e, 1), pl.ds(tok0, TOK_TILE)]
            sl = slot_t_ref[pl.ds(e, 1), pl.ds(tok0, TOK_TILE)]
            code = ((tok0 + lane) * TOP_K).astype(F32) + sl + 1.0
            lo = jnp.maximum(cum[b * N_EXPERTS + e] - base, 0)
            hi = jnp.minimum(cum[(b + 1) * N_EXPERTS + e] - base, EXP_TILE)

            def rows(k, c):
                r0 = pl.multiple_of(k * PLAN_ROWS, PLAN_ROWS)
                want = (base + r0 + rows_iota).astype(F32)
                hit = jnp.where(rk == want, code, 0.0)
                part = hit[:, 0:LANES]
                for c0 in range(LANES, TOK_TILE, LANES):
                    part = part + hit[:, c0:c0 + LANES]
                acc_ref[pl.ds(r0, PLAN_ROWS), :] += part
                return c

            shift = PLAN_ROWS.bit_length() - 1
            lax.fori_loop(lo >> shift, (hi + PLAN_ROWS - 1) >> shift, rows, 0)
            return carry

        lax.fori_loop(win0, win1, window, 0)
        code = jnp.sum(acc_ref[...].T, axis=0, keepdims=True).astype(jnp.int32) - 1
        pos = lax.broadcasted_iota(jnp.int32, (1, EXP_TILE), 1)
        tok = code >> 2
        src = jnp.where(code >= 0, tok, lax.rem(i * EXP_TILE + pos, n_tok))
        spare = TOP_K * n_tok + lax.rem(i, N_YBUF) * EXP_TILE + pos
        dst = jnp.where(code >= 0, (code & (TOP_K - 1)) * n_tok + tok, spare)
        src_ref[0] = src * SUBLANES
        dst_ref[0] = dst * SUBLANES

    @pl.when(i >= n_act[0])
    def _():
        src_ref[...] = jnp.zeros_like(src_ref)
        dst_ref[...] = jnp.zeros_like(dst_ref)


def _plan(rank_t, slot_t, tile_e, gstart, n_act, cum2d):
    n_tok = rank_t.shape[1]
    n_tiles = tile_e.shape[0]
    n_tok_tiles = cum2d.shape[0] - 1
    assert n_tok_tiles <= LANES
    huge = jnp.float32(2.0 ** 30)
    as_rows = lambda a: jnp.pad(a.T.astype(F32), ((0, 0), (0, LANES - n_tok_tiles)),
                                constant_values=huge)
    whole = lambda i, *_: (0, 0)
    per_tile = pl.BlockSpec((1, 1, EXP_TILE), lambda i, *_: (i, 0, 0))
    grid_spec = pltpu.PrefetchScalarGridSpec(
        num_scalar_prefetch=4,
        grid=(n_tiles,),
        in_specs=[pl.BlockSpec((N_EXPERTS, LANES), whole), pl.BlockSpec((N_EXPERTS, LANES), whole),
                  pl.BlockSpec(rank_t.shape, whole), pl.BlockSpec(slot_t.shape, whole)],
        out_specs=(per_tile, per_tile),
        scratch_shapes=[pltpu.VMEM((EXP_TILE, LANES), F32)],
    )
    shape = jax.ShapeDtypeStruct((n_tiles, 1, EXP_TILE), jnp.int32)
    return pl.pallas_call(
        functools.partial(_plan_kernel, n_tok),
        out_shape=(shape, shape),
        grid_spec=grid_spec,
        compiler_params=pltpu.CompilerParams(dimension_semantics=("arbitrary",),
                                             vmem_limit_bytes=VMEM_LIMIT),
        name="moe_plan",
    )(tile_e, gstart, n_act, cum2d.reshape(-1), as_rows(cum2d[:-1]), as_rows(cum2d[1:]),
      rank_t, slot_t)


N_YBUF = 3
DMA_UNROLL = 8
FF_CHUNK = 256


def _expert_kernel(tile_e, n_act, idx_ref, h_ref, wgu_ref, bgu_ref, wd_ref, bd_ref, yk_ref, xbuf,
                   ybuf, gsem, ssem, wgu_bf, wd_bf, gu_ref):
    i = pl.program_id(0)
    n_steps = pl.num_programs(0)
    n_active = n_act[0]
    tile_rows = EXP_TILE * SUBLANES
    n_slot_rows = yk_ref.shape[0] - N_YBUF * tile_rows

    def row_tile(ref, start):
        return ref.at[pl.ds(pl.multiple_of(start, SUBLANES), SUBLANES)]

    def for_each_row(fn):
        def body(g, carry):
            for k in range(DMA_UNROLL):
                fn(g * DMA_UNROLL + k)
            return carry
        lax.fori_loop(0, EXP_TILE // DMA_UNROLL, body, 0)

    def on_slot(slot, n, fn):
        if isinstance(slot, int):
            fn(slot)
            return
        for s in range(n):
            pl.when(slot == s)(functools.partial(fn, s))

    def gather_rows(which, slot):
        def run(s):
            def one(r):
                pltpu.make_async_copy(row_tile(h_ref, idx_ref[0, 0, which * EXP_TILE + r]),
                                      row_tile(xbuf.at[s], r * SUBLANES), gsem.at[s]).start()
            for_each_row(one)
        on_slot(slot, 2, run)

    def scatter_rows(slot):
        def run(s):
            def one(r):
                pltpu.make_async_copy(row_tile(ybuf.at[s], r * SUBLANES),
                                      row_tile(yk_ref, idx_ref[0, 0, 2 * EXP_TILE + r]),
                                      ssem.at[s]).start()
            for_each_row(one)
        on_slot(slot, N_YBUF, run)

    def scatter_wait(slot):
        pltpu.make_async_copy(ybuf.at[slot], yk_ref.at[pl.ds(0, tile_rows)], ssem.at[slot]).wait()

    @pl.when(i == 0)
    def _():
        for s in range(N_YBUF):
            ybuf[s] = jnp.zeros((tile_rows, LANES), F32)
        for s in range(N_YBUF - 1):
            pltpu.make_async_copy(
                ybuf.at[s], yk_ref.at[pl.ds(n_slot_rows + s * tile_rows, tile_rows)],
                ssem.at[s]).start()
        gather_rows(0, 0)

    prev = tile_e[jnp.maximum(i - 1, 0)]
    new_expert = jnp.logical_or(i == 0, tile_e[i] != prev)

    @pl.when(jnp.logical_and(new_expert, i < n_active))
    def _():
        rows = 128

        def cast(r, carry):
            sl = pl.ds(pl.multiple_of(r * rows, rows), rows)
            wgu_bf[sl, :] = wgu_ref[0, sl, :].astype(BF16)
            wd_bf[sl, :] = wd_ref[0, sl, :].astype(BF16)
            return carry

        lax.fori_loop(0, D_MODEL // rows, cast, 0)

    @pl.when(i < n_active)
    def _():
        xslot = i % 2
        yslot = i % N_YBUF
        pltpu.make_async_copy(h_ref.at[pl.ds(0, tile_rows)], xbuf.at[xslot], gsem.at[xslot]).wait()

        scatter_wait(yslot)
        next_x = xbuf.at[(i + 1) % 2]
        next_sem = gsem.at[(i + 1) % 2]
        prev_y = ybuf.at[(i + N_YBUF - 1) % N_YBUF]
        prev_sem = ssem.at[(i + N_YBUF - 1) % N_YBUF]

        def up_projection(_, carry):
            x = _load_row_tiles(xbuf.at[xslot], EXP_TILE).astype(BF16)
            gu_ref[...] = _dot(x, wgu_bf[...]) + bgu_ref[0]
            for r in range(EXP_TILE):
                pltpu.make_async_copy(row_tile(h_ref, idx_ref[0, 0, EXP_TILE + r]),
                                      row_tile(next_x, r * SUBLANES),
                                      next_sem).start(priority=r % 2)
            return carry

        lax.fori_loop(0, jnp.minimum(n_active - i, 1), up_projection, 0)
        for r in range(EXP_TILE):
            pltpu.make_async_copy(row_tile(prev_y, r * SUBLANES),
                                  row_tile(yk_ref, idx_ref[0, 0, 3 * EXP_TILE + r]),
                                  prev_sem).start(priority=r % 2)
        y = None
        for c0 in range(0, D_FF, FF_CHUNK):
            gate = jnp.minimum(gu_ref[:, c0:c0 + FF_CHUNK], SWIGLU_LIMIT)
            up = jnp.clip(gu_ref[:, D_FF + c0:D_FF + c0 + FF_CHUNK], -SWIGLU_LIMIT, SWIGLU_LIMIT)
            act = (up + 1.0) * gate * _sigmoid(SWIGLU_ALPHA * gate)
            part = _dot(act.astype(BF16), wd_bf[c0:c0 + FF_CHUNK, :])
            y = part if y is None else y + part
        _store_row_tiles(ybuf.at[yslot], y + bd_ref[0])

        @pl.when(i == n_active - 1)
        def _():
            scatter_rows(yslot)

    @pl.when(i == n_steps - 1)
    def _():
        for back in range(1, N_YBUF + 1):
            scatter_wait((n_active - back) % N_YBUF)
        last_x = n_active % 2
        pltpu.make_async_copy(h_ref.at[pl.ds(0, tile_rows)], xbuf.at[last_x], gsem.at[last_x]).wait()


def _experts(h2, src3, dst3, tile_e, n_act, w_gate_up, b_gate_up, w_down, b_down):
    n_tok = h2.shape[0] // SUBLANES
    n_tiles = src3.shape[0]
    src_next = jnp.concatenate([src3[1:], src3[-1:]], axis=0)
    spare_rows = TOP_K * n_tok + (N_YBUF - 1) * EXP_TILE + jnp.arange(EXP_TILE, dtype=jnp.int32)
    dst_prev = jnp.concatenate([(spare_rows * SUBLANES).reshape(1, 1, EXP_TILE), dst3[:-1]], axis=0)
    idx3 = jnp.concatenate([src3, src_next, dst3, dst_prev], axis=-1)
    grid_spec = pltpu.PrefetchScalarGridSpec(
        num_scalar_prefetch=2,
        grid=(n_tiles,),
        in_specs=[pl.BlockSpec((1, 1, 4 * EXP_TILE), lambda i, te, na: (i, 0, 0),
                               memory_space=pltpu.SMEM),
                  pl.BlockSpec(memory_space=pl.ANY),
                  pl.BlockSpec((1, D_MODEL, 2 * D_FF), lambda i, te, na: (te[i], 0, 0)),
                  pl.BlockSpec((1, 1, 2 * D_FF), lambda i, te, na: (te[i], 0, 0)),
                  pl.BlockSpec((1, D_FF, D_MODEL), lambda i, te, na: (te[i], 0, 0)),
                  pl.BlockSpec((1, 1, D_MODEL), lambda i, te, na: (te[i], 0, 0))],
        out_specs=pl.BlockSpec(memory_space=pl.ANY),
        scratch_shapes=[pltpu.VMEM((2, EXP_TILE * SUBLANES, LANES), F32),
                        pltpu.VMEM((N_YBUF, EXP_TILE * SUBLANES, LANES), F32),
                        pltpu.SemaphoreType.DMA((2,)), pltpu.SemaphoreType.DMA((N_YBUF,)),
                        pltpu.VMEM((D_MODEL, 2 * D_FF), BF16), pltpu.VMEM((D_FF, D_MODEL), BF16),
                        pltpu.VMEM((EXP_TILE, 2 * D_FF), F32)],
    )
    return pl.pallas_call(
        _expert_kernel,
        out_shape=jax.ShapeDtypeStruct(((TOP_K * n_tok + N_YBUF * EXP_TILE) * SUBLANES, LANES), F32),
        grid_spec=grid_spec,
        compiler_params=pltpu.CompilerParams(dimension_semantics=("arbitrary",),
                                             vmem_limit_bytes=VMEM_LIMIT),
        name="moe_experts",
    )(tile_e, n_act, idx3, h2, w_gate_up, b_gate_up.reshape(N_EXPERTS, 1, 2 * D_FF),
      w_down, b_down.reshape(N_EXPERTS, 1, D_MODEL))


COMBINE_TILE = 512


def _combine_kernel(n_prompt_tiles, alpha, y0_ref, y1_ref, y2_ref, y3_ref, x1_ref, w4_ref, mod_ref,
                    l2g_ref, l2b_ref, yp_ref, ysm_ref):
    i = pl.program_id(0)
    w4 = w4_ref[...]
    ff = _load_row_tiles(y0_ref, COMBINE_TILE) * w4[:, 0:1]
    for j, y_ref in enumerate((y1_ref, y2_ref, y3_ref), start=1):
        ff = ff + _load_row_tiles(y_ref, COMBINE_TILE) * w4[:, j:j + 1]
    g2 = mod_ref[0][:, 5 * D_MODEL:6 * D_MODEL]
    out = _layer_norm(alpha * x1_ref[...] + g2 * ff) * l2g_ref[...] + l2b_ref[...]

    @pl.when(i < n_prompt_tiles)
    def _():
        yp_ref[...] = out

    @pl.when(i >= n_prompt_tiles)
    def _():
        ysm_ref[...] = out


def _combine(yk, x1, w4, mod3, ln2_g, ln2_b, n_p, alpha, sample_seq):
    n_tok = x1.shape[0]
    n_s = n_tok - n_p
    assert n_p % COMBINE_TILE == 0 and sample_seq % COMBINE_TILE == 0
    npt = n_p // COMBINE_TILE
    n_tok_tiles = n_tok // COMBINE_TILE
    tiles_per_sample_seq = sample_seq // COMBINE_TILE

    def mod_row(i):
        return jnp.where(i < npt, 0, 1 + (i - npt) // tiles_per_sample_seq)

    def slot(j):
        return pl.BlockSpec((COMBINE_TILE * SUBLANES, LANES), lambda i: (j * n_tok_tiles + i, 0))

    const = lambda i: (0, 0)
    return pl.pallas_call(
        functools.partial(_combine_kernel, npt, alpha),
        out_shape=(jax.ShapeDtypeStruct((n_p, D_MODEL), F32),
                   jax.ShapeDtypeStruct((n_s, D_MODEL), F32)),
        grid=(n_tok_tiles,),
        in_specs=[slot(0), slot(1), slot(2), slot(3),
                  pl.BlockSpec((COMBINE_TILE, D_MODEL), lambda i: (i, 0)),
                  pl.BlockSpec((COMBINE_TILE, LANES), lambda i: (i, 0)),
                  pl.BlockSpec((1, 1, 6 * D_MODEL), lambda i: (mod_row(i), 0, 0)),
                  pl.BlockSpec((1, D_MODEL), const), pl.BlockSpec((1, D_MODEL), const)],
        out_specs=(pl.BlockSpec((COMBINE_TILE, D_MODEL), lambda i: (jnp.minimum(i, npt - 1), 0)),
                   pl.BlockSpec((COMBINE_TILE, D_MODEL), lambda i: (jnp.maximum(i - npt, 0), 0))),
        compiler_params=pltpu.CompilerParams(dimension_semantics=("arbitrary",),
                                             vmem_limit_bytes=VMEM_LIMIT),
        name="moe_combine",
    )(yk, yk, yk, yk, x1, w4, mod3, ln2_g, ln2_b)


def kernel(x_prompt, x_sample, c, state_fwd, state_bwd, c_ctx, w_ada, b_ada, w_in, conv_w, conv_b,
           conv_ln_g, conv_ln_b, lb_logits, rec_norm_g, w_out, ln1_g, ln1_b, ln2_g, ln2_b,
           router_w, router_b, w_gate_up, b_gate_up, w_down, b_down):
    depth = w_ada.shape[0]
    assert depth == 1
    alpha = (2.0 * depth) ** 0.25
    nb_p, seq_p, _ = x_prompt.shape
    nb_s, seq_s, _ = x_sample.shape
    n_p, n_s = nb_p * seq_p, nb_s * seq_s
    n_tok = n_p + n_s
    row2 = lambda a: a.reshape(1, -1)

    lb = jnp.cumsum(jax.nn.softmax(lb_logits.astype(F32), axis=0), axis=0)[0]

    cond = jnp.concatenate([c_ctx[None, :], c, jnp.zeros((16 - 1 - nb_s, D_MODEL), F32)], axis=0)
    mod = _ada(cond, w_ada[0], b_ada[0])
    mod3 = mod.reshape(16, 1, 6 * D_MODEL)

    xp = x_prompt.reshape(n_p, D_MODEL)
    xs = x_sample.reshape(n_s, D_MODEL)
    conv_w_pad = jnp.concatenate([conv_w[0], jnp.zeros((1, CONV_WIDTH), F32)], axis=0)
    conv_out, rec = _inproj(xp, xs, mod3, w_in[0].astype(BF16), conv_w_pad, row2(conv_b[0]),
                            row2(conv_ln_g[0]), row2(conv_ln_b[0]), seq_p, seq_s)

    zeros_state = jnp.zeros((nb_p, REC_HEADS, REC_DK, REC_DK), F32)
    s0f = jnp.concatenate([zeros_state, jnp.swapaxes(state_fwd[:, 0], -1, -2)], axis=0)
    s0b = jnp.concatenate([zeros_state, jnp.swapaxes(state_bwd[:, 0], -1, -2)], axis=0)
    o_f, o_b, sf_t, sb_t = _scan(rec, lb, s0f, s0b, [seq_p] * nb_p + [seq_s] * nb_s)
    new_f = jnp.swapaxes(sf_t[:nb_p], -1, -2)[:, None]
    new_b = jnp.swapaxes(sb_t[:nb_p], -1, -2)[:, None]

    rw = jnp.pad(router_w[0], ((0, 0), (0, LANES - N_EXPERTS)))
    rw_hi = rw.astype(BF16)
    rw_lo = (rw - rw_hi.astype(F32)).astype(BF16)
    rb = jnp.pad(router_b[0], (0, LANES - N_EXPERTS)).reshape(1, LANES)
    x1, h2, rank_t, slot_t, w4, counts, tile_start = _post(
        o_f, o_b, rec, conv_out, xp, xs, mod3, w_out[0].astype(BF16), row2(rec_norm_g[0]),
        row2(ln1_g[0]), row2(ln1_b[0]), rw_hi, rw_lo, rb, alpha, seq_s)

    cnt = counts[0, :N_EXPERTS].astype(jnp.int32)
    gpad = ((cnt + EXP_TILE - 1) // EXP_TILE) * EXP_TILE
    gend = jnp.cumsum(gpad)
    gstart = gend - gpad
    n_rows = n_tok * TOP_K + N_EXPERTS * EXP_TILE
    n_tiles = n_rows // EXP_TILE
    n_act = gend[-1] // EXP_TILE
    tile_ids = jnp.minimum(jnp.arange(n_tiles, dtype=jnp.int32), n_act - 1)
    tile_e = jnp.sum((gend[None, :] <= tile_ids[:, None] * EXP_TILE).astype(jnp.int32), axis=1)
    n_act1 = n_act.reshape(1).astype(jnp.int32)
    starts = tile_start[:, 0, :N_EXPERTS].astype(jnp.int32)
    cum2d = jnp.concatenate([starts, cnt[None, :]], axis=0)
    src, dst = _plan(rank_t, slot_t, tile_e, gstart.astype(jnp.int32), n_act1, cum2d)
    yk = _experts(h2, src, dst, tile_e, n_act1, w_gate_up[0], b_gate_up[0], w_down[0], b_down[0])
    y_p, y_s = _combine(yk, x1, w4, mod3, row2(ln2_g[0]), row2(ln2_b[0]), n_p, alpha, seq_s)
    return (y_p.reshape(nb_p, seq_p, D_MODEL), y_s.reshape(nb_s, seq_s, D_MODEL), new_f, new_b)
```

```python
import functools

import numpy as np
import jax
import jax.numpy as jnp
from jax import lax
from jax.experimental import pallas as pl
from jax.experimental.pallas import tpu as pltpu

F32 = jnp.float32
BF16 = jnp.bfloat16

D_MODEL = 1024
CONV_WIDTH = 512
CONV_TAPS = 31
REC_HEADS = 4
REC_DK = 128
REC_WIDTH = REC_HEADS * REC_DK
REC_COLS = 5 * REC_WIDTH
CHUNK = 64
N_LEVELS = 6
SCAN_CHUNKS = 4
N_EXPERTS = 32
TOP_K = 4
D_FF = 1024
SWIGLU_LIMIT = 7.0
SWIGLU_ALPHA = 1.702
LN_EPS = 1e-5
RMS_EPS = 1e-6
GRID_W = 64

LANES = 128
TOK_TILE = 512
EXP_TILE = 512
VMEM_LIMIT = 56 * 1024 * 1024


def _sigmoid(x):
    return 1.0 / (1.0 + jnp.exp(-x))


def _layer_norm(x):
    mu = jnp.mean(x, axis=-1, keepdims=True)
    xc = x - mu
    var = jnp.mean(xc * xc, axis=-1, keepdims=True)
    return xc * lax.rsqrt(var + LN_EPS)


def _dot(a, b):
    return jnp.dot(a, b, preferred_element_type=F32)


def _dot_nt(a, b):
    return lax.dot_general(a, b, (((1,), (1,)), ((), ())), preferred_element_type=F32)


SUBLANES = 8
assert D_MODEL == SUBLANES * LANES


def _store_row_tiles(ref, value):
    for s in range(SUBLANES):
        ref[pl.ds(s, value.shape[0], stride=SUBLANES), :] = value[:, s * LANES:(s + 1) * LANES]


def _load_row_tiles(ref, n_rows):
    return jnp.concatenate(
        [ref[pl.ds(s, n_rows, stride=SUBLANES), :] for s in range(SUBLANES)], axis=-1)


def _split3(x):
    hi = x.astype(BF16)
    r1 = x - hi.astype(F32)
    mid = r1.astype(BF16)
    lo = (r1 - mid.astype(F32)).astype(BF16)
    return hi, mid, lo


def _ada_kernel(c_ref, w_ref, b_ref, o_ref):
    c = c_ref[...]
    s = (c * _sigmoid(c)).astype(BF16)
    o_ref[...] = _dot(s, w_ref[...].astype(BF16)) + b_ref[...]


def _ada(cond, w_ada, b_ada):
    rows = cond.shape[0]
    ncol = w_ada.shape[1]
    tn = 1024
    return pl.pallas_call(
        _ada_kernel,
        out_shape=jax.ShapeDtypeStruct((rows, ncol), F32),
        grid=(ncol // tn,),
        in_specs=[pl.BlockSpec((rows, D_MODEL), lambda j: (0, 0)),
                  pl.BlockSpec((D_MODEL, tn), lambda j: (0, j)),
                  pl.BlockSpec((1, tn), lambda j: (0, j))],
        out_specs=pl.BlockSpec((rows, tn), lambda j: (0, j)),
        compiler_params=pltpu.CompilerParams(dimension_semantics=("parallel",),
                                             vmem_limit_bytes=VMEM_LIMIT),
        name="ada_mod",
    )(cond, w_ada, b_ada.reshape(1, ncol))


def _inproj_kernel(n_prompt_tiles, prompt_seg, xp_ref, xs_ref, mod_ref, win_ref, cw_ref, cb_ref,
                   cg_ref, cbeta_ref, conv_ref, rec_ref):
    i = pl.program_id(0)
    is_prompt = i < n_prompt_tiles
    mod = mod_ref[0]
    sh1 = mod[:, 0:D_MODEL]
    sc1 = mod[:, D_MODEL:2 * D_MODEL]
    hs, us = [], []
    for r0 in range(0, TOK_TILE, TOK_TILE // 2):
        rows = slice(r0, r0 + TOK_TILE // 2)
        x = jnp.where(is_prompt, xp_ref[rows, :], xs_ref[rows, :])
        h_part = (_layer_norm(x) * (1.0 + sc1) + sh1).astype(BF16)
        glu = _dot(h_part, win_ref[:, 0:2 * CONV_WIDTH])
        hs.append(h_part)
        us.append(glu[:, :CONV_WIDTH] * _sigmoid(glu[:, CONV_WIDTH:]))
    h = jnp.concatenate(hs, axis=0)
    u = jnp.concatenate(us, axis=0)

    seg_len = jnp.where(is_prompt, prompt_seg, GRID_W)
    half = CONV_TAPS // 2
    pos = lax.broadcasted_iota(jnp.int32, (TOK_TILE, 1), 0) & (seg_len - 1)

    def masked_shift(d):
        sh = pltpu.roll(u, (-d) % TOK_TILE, axis=0)
        edge_rows = SUBLANES * ((abs(d) + SUBLANES - 1) // SUBLANES)
        pieces = []
        for s0 in range(0, TOK_TILE, GRID_W):
            if d < 0:
                edge = slice(s0, s0 + edge_rows)
                pieces.append(jnp.where(pos[edge] >= -d, sh[edge], 0.0))
                pieces.append(sh[s0 + edge_rows:s0 + GRID_W])
            else:
                edge = slice(s0 + GRID_W - edge_rows, s0 + GRID_W)
                pieces.append(sh[s0:s0 + GRID_W - edge_rows])
                pieces.append(jnp.where(pos[edge] < seg_len - d, sh[edge], 0.0))
        return jnp.concatenate(pieces, axis=0)

    acc = u * cw_ref[half:half + 1, :] + cb_ref[...]
    taps = [j for j in range(CONV_TAPS) if j != half]
    n_groups = REC_COLS // REC_WIDTH
    per_group = len(taps) // n_groups
    anchor = None
    for g in range(n_groups):
        cols = slice(g * REC_WIDTH, (g + 1) * REC_WIDTH)
        rec = _dot(h, win_ref[:, 2 * CONV_WIDTH + g * REC_WIDTH:
                              2 * CONV_WIDTH + (g + 1) * REC_WIDTH])
        rec_ref[:, cols] = rec
        for j in taps[g * per_group:(g + 1) * per_group]:
            w_tap = cw_ref[j:j + 1, :]
            if anchor is not None:
                w_tap = w_tap + anchor
                anchor = None
            acc = acc + masked_shift(j - half) * w_tap
        if g + 1 < n_groups:
            bits = pltpu.bitcast(rec[0:SUBLANES, :], jnp.uint32)
            anchor = pltpu.bitcast((bits >> 16) >> 16, F32)[0:1, :]
    y = _layer_norm(acc) * cg_ref[...] + cbeta_ref[...]
    conv_ref[...] = (y * _sigmoid(y)).astype(BF16)


def _inproj(xp, xs, mod3, w_in_bf, conv_w, conv_b, conv_g, conv_beta, prompt_seq, sample_seq):
    n_p, n_s = xp.shape[0], xs.shape[0]
    n_tok = n_p + n_s
    npt = n_p // TOK_TILE
    tiles_per_sample_seq = sample_seq // TOK_TILE
    assert TOK_TILE % prompt_seq == 0 and n_p % TOK_TILE == 0 and sample_seq % TOK_TILE == 0

    def mod_row(i):
        return jnp.where(i < npt, 0, 1 + (i - npt) // tiles_per_sample_seq)

    const = lambda i: (0, 0)
    return pl.pallas_call(
        functools.partial(_inproj_kernel, npt, prompt_seq),
        out_shape=(jax.ShapeDtypeStruct((n_tok, CONV_WIDTH), BF16),
                   jax.ShapeDtypeStruct((n_tok, REC_COLS), F32)),
        grid=(n_tok // TOK_TILE,),
        in_specs=[pl.BlockSpec((TOK_TILE, D_MODEL), lambda i: (jnp.minimum(i, npt - 1), 0)),
                  pl.BlockSpec((TOK_TILE, D_MODEL), lambda i: (jnp.maximum(i - npt, 0), 0)),
                  pl.BlockSpec((1, 1, 6 * D_MODEL), lambda i: (mod_row(i), 0, 0)),
                  pl.BlockSpec(w_in_bf.shape, const),
                  pl.BlockSpec(conv_w.shape, const),
                  pl.BlockSpec((1, CONV_WIDTH), const),
                  pl.BlockSpec((1, CONV_WIDTH), const),
                  pl.BlockSpec((1, CONV_WIDTH), const)],
        out_specs=(pl.BlockSpec((TOK_TILE, CONV_WIDTH), lambda i: (i, 0)),
                   pl.BlockSpec((TOK_TILE, REC_COLS), lambda i: (i, 0))),
        compiler_params=pltpu.CompilerParams(dimension_semantics=("parallel",),
                                             vmem_limit_bytes=VMEM_LIMIT),
        name="inproj_conv",
    )(xp, xs, mod3, w_in_bf, conv_w, conv_b, conv_g, conv_beta)


def _scan_constants():
    c = CHUNK
    w = np.zeros((N_LEVELS + 2, c, c), np.float32)
    m = np.zeros((N_LEVELS + 1, c, c), np.float32)
    t = np.arange(c)
    for j in range(N_LEVELS):
        half = 1 << j
        for ti in range(c):
            ref = (ti & ~(2 * half - 1)) + half - 1
            if ti & half:
                w[j, ti, ref + 1:ti + 1] = 1.0
            else:
                w[j, ti, ti + 1:ref + 1] = 1.0
        upper = (t[:, None] & half) != 0
        lower = (t[None, :] & half) == 0
        same = (t[:, None] >> (j + 1)) == (t[None, :] >> (j + 1))
        m[j] = (upper & lower & same).astype(np.float32)
    w[N_LEVELS] = (t[None, :] <= t[:, None]).astype(np.float32)
    w[N_LEVELS + 1] = (t[None, :] > t[:, None]).astype(np.float32)
    m[N_LEVELS] = np.eye(c, dtype=np.float32)
    w_f = np.tile(w.reshape(-1, c), (1, 3))
    w_b = np.tile(w[:, ::-1, ::-1].reshape(-1, c), (1, 3))
    m_b = m[:, ::-1, ::-1]
    return (jnp.asarray(w_f, BF16), jnp.asarray(np.ascontiguousarray(w_b), BF16),
            jnp.asarray(m), jnp.asarray(np.ascontiguousarray(m_b)))


def _scan_gates(z, lb, w_ref):
    e = jnp.exp(-jnp.abs(z))
    r = 1.0 / (1.0 + e)
    a = e * r
    sig = jnp.where(z >= 0, r, a)
    sig_neg = jnp.where(z >= 0, a, r)
    one_m_lb = 1.0 - lb
    log_f = jnp.log2(lb + one_m_lb * sig)
    k = one_m_lb * sig_neg
    terms = jnp.concatenate(_split3(log_f), axis=0)
    ex = jnp.exp2(_dot(w_ref[...], terms))
    return k, ex


def _scan_intra(q, k, ex, m_ref, neighbour):
    c = CHUNK
    qb, kb = q.astype(BF16), k.astype(BF16)
    a_mat = m_ref[N_LEVELS] * jnp.sum(q * k, axis=-1, keepdims=True)
    x0 = ex[0:c]
    pair = jnp.sum((q * x0) * pltpu.roll(k * x0, neighbour, axis=0), axis=-1, keepdims=True)
    a_mat = a_mat + m_ref[0] * pair
    for j in range(1, N_LEVELS):
        xj = ex[j * c:(j + 1) * c].astype(BF16)
        a_mat = a_mat + m_ref[j] * _dot_nt(qb * xj, kb * xj)
    return a_mat


def _scan_kernel(fblk, bblk, first, seq, qf_ref, zf_ref, vf_ref, qb_ref, zb_ref, vb_ref, lb_ref,
                 wf_ref, wb_ref, mf_ref, mb_ref, s0f_ref, s0b_ref, of_ref, ob_ref, sf_ref, sb_ref):
    s = pl.program_id(0)

    @pl.when(first[s] == 1)
    def _():
        sf_ref[...] = s0f_ref[...]
        sb_ref[...] = s0b_ref[...]

    c = CHUNK
    dirs = ((qf_ref, zf_ref, vf_ref, 0, wf_ref, mf_ref, sf_ref, of_ref, c - 1,
             tuple(range(SCAN_CHUNKS))),
            (qb_ref, zb_ref, vb_ref, 1, wb_ref, mb_ref, sb_ref, ob_ref, 0,
             tuple(reversed(range(SCAN_CHUNKS)))))
    units = [(d, slice(sub * c, (sub + 1) * c)) for d in dirs for sub in d[-1]]
    gates = [_scan_gates(d[1][rows, :], lb_ref[d[3]:d[3] + 1, :], d[4]) for d, rows in units]
    intra = []
    for (d, rows), (k, ex) in zip(units, gates):
        for h in range(REC_HEADS):
            cols = slice(h * REC_DK, (h + 1) * REC_DK)
            neighbour = 1 if d[8] == c - 1 else c - 1
            intra.append(_scan_intra(d[0][rows, cols], k[:, cols], ex[:, cols], d[5], neighbour))
    idx = 0
    for (d, rows), (k, ex) in zip(units, gates):
        q_ref, _, v_ref, _, _, _, st_ref, o_ref, last, _ = d
        for h in range(REC_HEADS):
            cols = slice(h * REC_DK, (h + 1) * REC_DK)
            q, v = q_ref[rows, cols], v_ref[rows, cols]
            ex_b = ex[N_LEVELS * c:(N_LEVELS + 1) * c, cols]
            ex_l = ex[(N_LEVELS + 1) * c:(N_LEVELS + 2) * c, cols]
            st = st_ref[0, h]
            o_ref[rows, cols] = (_dot(intra[idx].astype(BF16), v.astype(BF16))
                                 + _dot_nt((q * ex_b).astype(BF16), st.astype(BF16)))
            st_ref[0, h] = (st * ex_b[last:last + 1, :]
                            + _dot(v.T.astype(BF16), (k[:, cols] * ex_l).astype(BF16)))
            idx += 1


def _scan(rec, lb, s0f_t, s0b_t, seq_lens):
    n_tok = rec.shape[0]
    fblk, bblk, first, seq = [], [], [], []
    base = 0
    step_rows = CHUNK * SCAN_CHUNKS
    for si, ln in enumerate(seq_lens):
        assert ln % step_rows == 0
        n = ln // step_rows
        for ci in range(n):
            fblk.append(base + ci)
            bblk.append(base + n - 1 - ci)
            first.append(1 if ci == 0 else 0)
            seq.append(si)
        base += n
    steps = len(fblk)
    as_i32 = lambda a: jnp.asarray(np.asarray(a, np.int32))
    w_f, w_b, m_f, m_b = _scan_constants()
    n_seq = len(seq_lens)

    def col(block_of, c):
        return pl.BlockSpec((step_rows, REC_WIDTH), lambda s, fb, bb, fi, sq: (block_of(fb, bb)[s], c))

    fwd = lambda fb, bb: fb
    bwd = lambda fb, bb: bb
    const2 = lambda s, fb, bb, fi, sq: (0, 0)
    const3 = lambda s, fb, bb, fi, sq: (0, 0, 0)
    state_spec = pl.BlockSpec((1, REC_HEADS, REC_DK, REC_DK), lambda s, fb, bb, fi, sq: (sq[s], 0, 0, 0))
    grid_spec = pltpu.PrefetchScalarGridSpec(
        num_scalar_prefetch=4,
        grid=(steps,),
        in_specs=[col(fwd, 0), col(fwd, 1), col(fwd, 3), col(bwd, 0), col(bwd, 2), col(bwd, 3),
                  pl.BlockSpec((2, REC_WIDTH), const2),
                  pl.BlockSpec(w_f.shape, const2), pl.BlockSpec(w_b.shape, const2),
                  pl.BlockSpec(m_f.shape, const3), pl.BlockSpec(m_b.shape, const3),
                  state_spec, state_spec],
        out_specs=(pl.BlockSpec((step_rows, REC_WIDTH), lambda s, fb, bb, fi, sq: (fb[s], 0)),
                   pl.BlockSpec((step_rows, REC_WIDTH), lambda s, fb, bb, fi, sq: (bb[s], 0)),
                   state_spec, state_spec),
    )
    st_shape = jax.ShapeDtypeStruct((n_seq, REC_HEADS, REC_DK, REC_DK), F32)
    return pl.pallas_call(
        _scan_kernel,
        out_shape=(jax.ShapeDtypeStruct((n_tok, REC_WIDTH), F32),
                   jax.ShapeDtypeStruct((n_tok, REC_WIDTH), F32), st_shape, st_shape),
        grid_spec=grid_spec,
        compiler_params=pltpu.CompilerParams(dimension_semantics=("arbitrary",),
                                             vmem_limit_bytes=VMEM_LIMIT),
        name="hgrn2_scan",
    )(as_i32(fblk), as_i32(bblk), as_i32(first), as_i32(seq),
      rec, rec, rec, rec, rec, rec, lb, w_f, w_b, m_f, m_b, s0f_t, s0b_t)


def _post_kernel(n_prompt_tiles, alpha, of_ref, ob_ref, g_ref, conv_ref, xp_ref, xs_ref, mod_ref,
                 wout_ref, rg_ref, l1g_ref, l1b_ref, rwh_ref, rwl_ref, rb_ref, tri_ref,
                 x1_ref, h2_ref, rank_t_ref, slot_t_ref, w4_ref, cnt_ref, start_ref, run_ref):
    i = pl.program_id(0)

    @pl.when(i == 0)
    def _():
        run_ref[...] = jnp.zeros_like(run_ref)

    x = jnp.where(i < n_prompt_tiles, xp_ref[...], xs_ref[...])
    mod = mod_ref[0]
    g1 = mod[:, 2 * D_MODEL:3 * D_MODEL]
    sh2 = mod[:, 3 * D_MODEL:4 * D_MODEL]
    sc2 = mod[:, 4 * D_MODEL:5 * D_MODEL]

    g = g_ref[...]
    silu_g = g * _sigmoid(g)
    mix = _dot(conv_ref[...], wout_ref[0:CONV_WIDTH, :])
    for h in range(REC_HEADS):
        cols = slice(h * REC_DK, (h + 1) * REC_DK)
        o = of_ref[:, cols] + ob_ref[:, cols]
        o = o * lax.rsqrt(jnp.mean(o * o, axis=-1, keepdims=True) + RMS_EPS)
        rec_out = (o * rg_ref[:, cols] * silu_g[:, cols]).astype(BF16)
        mix = mix + _dot(rec_out, wout_ref[CONV_WIDTH + h * REC_DK:CONV_WIDTH + (h + 1) * REC_DK, :])

    x1 = _layer_norm(alpha * x + g1 * mix) * l1g_ref[...] + l1b_ref[...]
    x1_ref[...] = x1
    h2 = _layer_norm(x1) * (1.0 + sc2) + sh2
    _store_row_tiles(h2_ref, h2)

    h_hi = h2.astype(BF16)
    h_lo = (h2 - h_hi.astype(F32)).astype(BF16)
    logits = (_dot(h_hi, rwh_ref[...]) + _dot(h_lo, rwh_ref[...]) + _dot(h_hi, rwl_ref[...])
              + rb_ref[...])
    lane = lax.broadcasted_iota(jnp.int32, logits.shape, 1)
    lane_f = lane.astype(F32)
    neg_inf = jnp.float32(-jnp.inf)
    work = jnp.where(lane < N_EXPERTS, logits, neg_inf)
    vals, sels = [], []
    for _ in range(TOP_K):
        m = jnp.max(work, axis=-1, keepdims=True)
        idx = jnp.min(jnp.where(work == m, lane_f, float(LANES)), axis=-1, keepdims=True)
        sel = lane_f == idx
        vals.append(m)
        sels.append(sel)
        work = jnp.where(sel, neg_inf, work)
    exps = [jnp.exp(v - vals[0]) for v in vals]
    denom = exps[0] + exps[1] + exps[2] + exps[3]

    sel_any = jnp.zeros(logits.shape, F32)
    for sel in sels:
        sel_any = jnp.where(sel, 1.0, sel_any)
    run_old = run_ref[...]
    prior = _dot(tri_ref[...], sel_any.astype(BF16)) + run_old
    run_new = run_old + jnp.sum(sel_any, axis=0, keepdims=True)
    run_ref[...] = run_new
    cnt_ref[...] = run_new
    start_ref[0] = run_old

    slot = jnp.zeros(logits.shape, F32)
    w4 = jnp.zeros(logits.shape, F32)
    for j in range(TOP_K):
        slot = jnp.where(sels[j], float(j), slot)
        w4 = jnp.where(lane == j, exps[j] / denom, w4)
    routed = sel_any > 0.0
    rank_t_ref[...] = jnp.where(routed, prior, -1.0).T[:N_EXPERTS]
    slot_t_ref[...] = jnp.where(routed, slot, -1.0).T[:N_EXPERTS]
    w4_ref[...] = w4


def _post(o_f, o_b, rec, conv, xp, xs, mod3, w_out_bf, rec_g, ln1_g, ln1_b, rw_hi, rw_lo, rb,
          alpha, sample_seq):
    n_p, n_s = xp.shape[0], xs.shape[0]
    n_tok = n_p + n_s
    npt = n_p // TOK_TILE
    tiles_per_sample_seq = sample_seq // TOK_TILE
    tri = jnp.asarray(np.tril(np.ones((TOK_TILE, TOK_TILE), np.float32), -1), BF16)

    def mod_row(i):
        return jnp.where(i < npt, 0, 1 + (i - npt) // tiles_per_sample_seq)

    const = lambda i: (0, 0)
    tok = lambda w: pl.BlockSpec((TOK_TILE, w), lambda i: (i, 0))
    return pl.pallas_call(
        functools.partial(_post_kernel, npt, alpha),
        out_shape=(jax.ShapeDtypeStruct((n_tok, D_MODEL), F32),
                   jax.ShapeDtypeStruct((n_tok * SUBLANES, LANES), F32),
                   jax.ShapeDtypeStruct((N_EXPERTS, n_tok), F32),
                   jax.ShapeDtypeStruct((N_EXPERTS, n_tok), F32),
                   jax.ShapeDtypeStruct((n_tok, LANES), F32),
                   jax.ShapeDtypeStruct((1, LANES), F32),
                   jax.ShapeDtypeStruct((n_tok // TOK_TILE, 1, LANES), F32)),
        grid=(n_tok // TOK_TILE,),
        in_specs=[tok(REC_WIDTH), tok(REC_WIDTH),
                  pl.BlockSpec((TOK_TILE, REC_WIDTH), lambda i: (i, 4)),
                  tok(CONV_WIDTH),
                  pl.BlockSpec((TOK_TILE, D_MODEL), lambda i: (jnp.minimum(i, npt - 1), 0)),
                  pl.BlockSpec((TOK_TILE, D_MODEL), lambda i: (jnp.maximum(i - npt, 0), 0)),
                  pl.BlockSpec((1, 1, 6 * D_MODEL), lambda i: (mod_row(i), 0, 0)),
                  pl.BlockSpec(w_out_bf.shape, const),
                  pl.BlockSpec((1, REC_WIDTH), const),
                  pl.BlockSpec((1, D_MODEL), const), pl.BlockSpec((1, D_MODEL), const),
                  pl.BlockSpec((D_MODEL, LANES), const), pl.BlockSpec((D_MODEL, LANES), const),
                  pl.BlockSpec((1, LANES), const),
                  pl.BlockSpec((TOK_TILE, TOK_TILE), const)],
        out_specs=(tok(D_MODEL), pl.BlockSpec((TOK_TILE * SUBLANES, LANES), lambda i: (i, 0)),
                   pl.BlockSpec((N_EXPERTS, TOK_TILE), lambda i: (0, i)),
                   pl.BlockSpec((N_EXPERTS, TOK_TILE), lambda i: (0, i)),
                   tok(LANES), pl.BlockSpec((1, LANES), const),
                   pl.BlockSpec((1, 1, LANES), lambda i: (i, 0, 0))),
        scratch_shapes=[pltpu.VMEM((1, LANES), F32)],
        compiler_params=pltpu.CompilerParams(dimension_semantics=("arbitrary",),
                                             vmem_limit_bytes=VMEM_LIMIT),
        name="post_mixer_router",
    )(o_f, o_b, rec, conv, xp, xs, mod3, w_out_bf, rec_g, ln1_g, ln1_b, rw_hi, rw_lo, rb, tri)


PLAN_ROWS = 64


def _plan_kernel(n_tok, tile_e, gstart, n_act, cum, starts_ref, ends_ref, rank_t_ref, slot_t_ref,
                 src_ref, dst_ref, acc_ref):
    i = pl.program_id(0)

    @pl.when(i < n_act[0])
    def _():
        e = tile_e[i]
        base = i * EXP_TILE - gstart[e]
        rows_iota = lax.broadcasted_iota(jnp.int32, (PLAN_ROWS, 1), 0)
        lane = lax.broadcasted_iota(jnp.int32, (1, TOK_TILE), 1)
        acc_ref[...] = jnp.zeros_like(acc_ref)

        base_f = base.astype(F32)
        win0 = jnp.sum((ends_ref[pl.ds(e, 1), :] <= base_f).astype(jnp.int32))
        win1 = jnp.sum((starts_ref[pl.ds(e, 1), :] < base_f + EXP_TILE).astype(jnp.int32))

        def window(b, carry):
            tok0 = pl.multiple_of(b * TOK_TILE, TOK_TILE)
            rk = rank_t_ref[pl.ds(e, 1), pl.ds(tok0, TOK_TILE)]
            sl = slot_t_ref[pl.ds(e, 1), pl.ds(tok0, TOK_TILE)]
            code = ((tok0 + lane) * TOP_K).astype(F32) + sl + 1.0
            lo = jnp.maximum(cum[b * N_EXPERTS + e] - base, 0)
            hi = jnp.minimum(cum[(b + 1) * N_EXPERTS + e] - base, EXP_TILE)

            def rows(k, c):
                r0 = pl.multiple_of(k * PLAN_ROWS, PLAN_ROWS)
                want = (base + r0 + rows_iota).astype(F32)
                hit = jnp.where(rk == want, code, 0.0)
                part = hit[:, 0:LANES]
                for c0 in range(LANES, TOK_TILE, LANES):
                    part = part + hit[:, c0:c0 + LANES]
                acc_ref[pl.ds(r0, PLAN_ROWS), :] += part
                return c

            shift = PLAN_ROWS.bit_length() - 1
            lax.fori_loop(lo >> shift, (hi + PLAN_ROWS - 1) >> shift, rows, 0)
            return carry

        lax.fori_loop(win0, win1, window, 0)
        code = jnp.sum(acc_ref[...].T, axis=0, keepdims=True).astype(jnp.int32) - 1
        pos = lax.broadcasted_iota(jnp.int32, (1, EXP_TILE), 1)
        tok = code >> 2
        src = jnp.where(code >= 0, tok, lax.rem(i * EXP_TILE + pos, n_tok))
        spare = TOP_K * n_tok + lax.rem(i, N_YBUF) * EXP_TILE + pos
        dst = jnp.where(code >= 0, (code & (TOP_K - 1)) * n_tok + tok, spare)
        src_ref[0] = src * SUBLANES
        dst_ref[0] = dst * SUBLANES

    @pl.when(i >= n_act[0])
    def _():
        src_ref[...] = jnp.zeros_like(src_ref)
        dst_ref[...] = jnp.zeros_like(dst_ref)


def _plan(rank_t, slot_t, tile_e, gstart, n_act, cum2d):
    n_tok = rank_t.shape[1]
    n_tiles = tile_e.shape[0]
    n_tok_tiles = cum2d.shape[0] - 1
    assert n_tok_tiles <= LANES
    huge = jnp.float32(2.0 ** 30)
    as_rows = lambda a: jnp.pad(a.T.astype(F32), ((0, 0), (0, LANES - n_tok_tiles)),
                                constant_values=huge)
    whole = lambda i, *_: (0, 0)
    per_tile = pl.BlockSpec((1, 1, EXP_TILE), lambda i, *_: (i, 0, 0))
    grid_spec = pltpu.PrefetchScalarGridSpec(
        num_scalar_prefetch=4,
        grid=(n_tiles,),
        in_specs=[pl.BlockSpec((N_EXPERTS, LANES), whole), pl.BlockSpec((N_EXPERTS, LANES), whole),
                  pl.BlockSpec(rank_t.shape, whole), pl.BlockSpec(slot_t.shape, whole)],
        out_specs=(per_tile, per_tile),
        scratch_shapes=[pltpu.VMEM((EXP_TILE, LANES), F32)],
    )
    shape = jax.ShapeDtypeStruct((n_tiles, 1, EXP_TILE), jnp.int32)
    return pl.pallas_call(
        functools.partial(_plan_kernel, n_tok),
        out_shape=(shape, shape),
        grid_spec=grid_spec,
        compiler_params=pltpu.CompilerParams(dimension_semantics=("arbitrary",),
                                             vmem_limit_bytes=VMEM_LIMIT),
        name="moe_plan",
    )(tile_e, gstart, n_act, cum2d.reshape(-1), as_rows(cum2d[:-1]), as_rows(cum2d[1:]),
      rank_t, slot_t)


N_YBUF = 3
DMA_UNROLL = 8
FF_CHUNK = 256


def _expert_kernel(tile_e, n_act, idx_ref, h_ref, wgu_ref, bgu_ref, wd_ref, bd_ref, yk_ref, xbuf,
                   ybuf, gsem, ssem, wgu_bf, wd_bf, gu_ref):
    i = pl.program_id(0)
    n_steps = pl.num_programs(0)
    n_active = n_act[0]
    tile_rows = EXP_TILE * SUBLANES
    n_slot_rows = yk_ref.shape[0] - N_YBUF * tile_rows

    def row_tile(ref, start):
        return ref.at[pl.ds(pl.multiple_of(start, SUBLANES), SUBLANES)]

    def for_each_row(fn):
        def body(g, carry):
            for k in range(DMA_UNROLL):
                fn(g * DMA_UNROLL + k)
            return carry
        lax.fori_loop(0, EXP_TILE // DMA_UNROLL, body, 0)

    def on_slot(slot, n, fn):
        if isinstance(slot, int):
            fn(slot)
            return
        for s in range(n):
            pl.when(slot == s)(functools.partial(fn, s))

    def gather_rows(which, slot):
        def run(s):
            def one(r):
                pltpu.make_async_copy(row_tile(h_ref, idx_ref[0, 0, which * EXP_TILE + r]),
                                      row_tile(xbuf.at[s], r * SUBLANES), gsem.at[s]).start()
            for_each_row(one)
        on_slot(slot, 2, run)

    def scatter_rows(slot):
        def run(s):
            def one(r):
                pltpu.make_async_copy(row_tile(ybuf.at[s], r * SUBLANES),
                                      row_tile(yk_ref, idx_ref[0, 0, 2 * EXP_TILE + r]),
                                      ssem.at[s]).start()
            for_each_row(one)
        on_slot(slot, N_YBUF, run)

    def scatter_wait(slot):
        pltpu.make_async_copy(ybuf.at[slot], yk_ref.at[pl.ds(0, tile_rows)], ssem.at[slot]).wait()

    @pl.when(i == 0)
    def _():
        for s in range(N_YBUF):
            ybuf[s] = jnp.zeros((tile_rows, LANES), F32)
        for s in range(N_YBUF - 1):
            pltpu.make_async_copy(
                ybuf.at[s], yk_ref.at[pl.ds(n_slot_rows + s * tile_rows, tile_rows)],
                ssem.at[s]).start()
        gather_rows(0, 0)

    prev = tile_e[jnp.maximum(i - 1, 0)]
    new_expert = jnp.logical_or(i == 0, tile_e[i] != prev)

    @pl.when(jnp.logical_and(new_expert, i < n_active))
    def _():
        rows = 128

        def cast(r, carry):
            sl = pl.ds(pl.multiple_of(r * rows, rows), rows)
            wgu_bf[sl, :] = wgu_ref[0, sl, :].astype(BF16)
            wd_bf[sl, :] = wd_ref[0, sl, :].astype(BF16)
            return carry

        lax.fori_loop(0, D_MODEL // rows, cast, 0)

    @pl.when(i < n_active)
    def _():
        xslot = i % 2
        yslot = i % N_YBUF
        pltpu.make_async_copy(h_ref.at[pl.ds(0, tile_rows)], xbuf.at[xslot], gsem.at[xslot]).wait()

        scatter_wait(yslot)
        next_x = xbuf.at[(i + 1) % 2]
        next_sem = gsem.at[(i + 1) % 2]
        prev_y = ybuf.at[(i + N_YBUF - 1) % N_YBUF]
        prev_sem = ssem.at[(i + N_YBUF - 1) % N_YBUF]

        def up_projection(_, carry):
            x = _load_row_tiles(xbuf.at[xslot], EXP_TILE).astype(BF16)
            gu_ref[...] = _dot(x, wgu_bf[...]) + bgu_ref[0]
            for r in range(EXP_TILE):
                pltpu.make_async_copy(row_tile(h_ref, idx_ref[0, 0, EXP_TILE + r]),
                                      row_tile(next_x, r * SUBLANES),
                                      next_sem).start(priority=r % 2)
            return carry

        lax.fori_loop(0, jnp.minimum(n_active - i, 1), up_projection, 0)
        for r in range(EXP_TILE):
            pltpu.make_async_copy(row_tile(prev_y, r * SUBLANES),
                                  row_tile(yk_ref, idx_ref[0, 0, 3 * EXP_TILE + r]),
                                  prev_sem).start(priority=r % 2)
        y = None
        for c0 in range(0, D_FF, FF_CHUNK):
            gate = jnp.minimum(gu_ref[:, c0:c0 + FF_CHUNK], SWIGLU_LIMIT)
            up = jnp.clip(gu_ref[:, D_FF + c0:D_FF + c0 + FF_CHUNK], -SWIGLU_LIMIT, SWIGLU_LIMIT)
            act = (up + 1.0) * gate * _sigmoid(SWIGLU_ALPHA * gate)
            part = _dot(act.astype(BF16), wd_bf[c0:c0 + FF_CHUNK, :])
            y = part if y is None else y + part
        _store_row_tiles(ybuf.at[yslot], y + bd_ref[0])

        @pl.when(i == n_active - 1)
        def _():
            scatter_rows(yslot)

    @pl.when(i == n_steps - 1)
    def _():
        for back in range(1, N_YBUF + 1):
            scatter_wait((n_active - back) % N_YBUF)
        last_x = n_active % 2
        pltpu.make_async_copy(h_ref.at[pl.ds(0, tile_rows)], xbuf.at[last_x], gsem.at[last_x]).wait()


def _experts(h2, src3, dst3, tile_e, n_act, w_gate_up, b_gate_up, w_down, b_down):
    n_tok = h2.shape[0] // SUBLANES
    n_tiles = src3.shape[0]
    src_next = jnp.concatenate([src3[1:], src3[-1:]], axis=0)
    spare_rows = TOP_K * n_tok + (N_YBUF - 1) * EXP_TILE + jnp.arange(EXP_TILE, dtype=jnp.int32)
    dst_prev = jnp.concatenate([(spare_rows * SUBLANES).reshape(1, 1, EXP_TILE), dst3[:-1]], axis=0)
    idx3 = jnp.concatenate([src3, src_next, dst3, dst_prev], axis=-1)
    grid_spec = pltpu.PrefetchScalarGridSpec(
        num_scalar_prefetch=2,
        grid=(n_tiles,),
        in_specs=[pl.BlockSpec((1, 1, 4 * EXP_TILE), lambda i, te, na: (i, 0, 0),
                               memory_space=pltpu.SMEM),
                  pl.BlockSpec(memory_space=pl.ANY),
                  pl.BlockSpec((1, D_MODEL, 2 * D_FF), lambda i, te, na: (te[i], 0, 0)),
                  pl.BlockSpec((1, 1, 2 * D_FF), lambda i, te, na: (te[i], 0, 0)),
                  pl.BlockSpec((1, D_FF, D_MODEL), lambda i, te, na: (te[i], 0, 0)),
                  pl.BlockSpec((1, 1, D_MODEL), lambda i, te, na: (te[i], 0, 0))],
        out_specs=pl.BlockSpec(memory_space=pl.ANY),
        scratch_shapes=[pltpu.VMEM((2, EXP_TILE * SUBLANES, LANES), F32),
                        pltpu.VMEM((N_YBUF, EXP_TILE * SUBLANES, LANES), F32),
                        pltpu.SemaphoreType.DMA((2,)), pltpu.SemaphoreType.DMA((N_YBUF,)),
                        pltpu.VMEM((D_MODEL, 2 * D_FF), BF16), pltpu.VMEM((D_FF, D_MODEL), BF16),
                        pltpu.VMEM((EXP_TILE, 2 * D_FF), F32)],
    )
    return pl.pallas_call(
        _expert_kernel,
        out_shape=jax.ShapeDtypeStruct(((TOP_K * n_tok + N_YBUF * EXP_TILE) * SUBLANES, LANES), F32),
        grid_spec=grid_spec,
        compiler_params=pltpu.CompilerParams(dimension_semantics=("arbitrary",),
                                             vmem_limit_bytes=VMEM_LIMIT),
        name="moe_experts",
    )(tile_e, n_act, idx3, h2, w_gate_up, b_gate_up.reshape(N_EXPERTS, 1, 2 * D_FF),
      w_down, b_down.reshape(N_EXPERTS, 1, D_MODEL))


COMBINE_TILE = 512


def _combine_kernel(n_prompt_tiles, alpha, y0_ref, y1_ref, y2_ref, y3_ref, x1_ref, w4_ref, mod_ref,
                    l2g_ref, l2b_ref, yp_ref, ysm_ref):
    i = pl.program_id(0)
    w4 = w4_ref[...]
    ff = _load_row_tiles(y0_ref, COMBINE_TILE) * w4[:, 0:1]
    for j, y_ref in enumerate((y1_ref, y2_ref, y3_ref), start=1):
        ff = ff + _load_row_tiles(y_ref, COMBINE_TILE) * w4[:, j:j + 1]
    g2 = mod_ref[0][:, 5 * D_MODEL:6 * D_MODEL]
    out = _layer_norm(alpha * x1_ref[...] + g2 * ff) * l2g_ref[...] + l2b_ref[...]

    @pl.when(i < n_prompt_tiles)
    def _():
        yp_ref[...] = out

    @pl.when(i >= n_prompt_tiles)
    def _():
        ysm_ref[...] = out


def _combine(yk, x1, w4, mod3, ln2_g, ln2_b, n_p, alpha, sample_seq):
    n_tok = x1.shape[0]
    n_s = n_tok - n_p
    assert n_p % COMBINE_TILE == 0 and sample_seq % COMBINE_TILE == 0
    npt = n_p // COMBINE_TILE
    n_tok_tiles = n_tok // COMBINE_TILE
    tiles_per_sample_seq = sample_seq // COMBINE_TILE

    def mod_row(i):
        return jnp.where(i < npt, 0, 1 + (i - npt) // tiles_per_sample_seq)

    def slot(j):
        return pl.BlockSpec((COMBINE_TILE * SUBLANES, LANES), lambda i: (j * n_tok_tiles + i, 0))

    const = lambda i: (0, 0)
    return pl.pallas_call(
        functools.partial(_combine_kernel, npt, alpha),
        out_shape=(jax.ShapeDtypeStruct((n_p, D_MODEL), F32),
                   jax.ShapeDtypeStruct((n_s, D_MODEL), F32)),
        grid=(n_tok_tiles,),
        in_specs=[slot(0), slot(1), slot(2), slot(3),
                  pl.BlockSpec((COMBINE_TILE, D_MODEL), lambda i: (i, 0)),
                  pl.BlockSpec((COMBINE_TILE, LANES), lambda i: (i, 0)),
                  pl.BlockSpec((1, 1, 6 * D_MODEL), lambda i: (mod_row(i), 0, 0)),
                  pl.BlockSpec((1, D_MODEL), const), pl.BlockSpec((1, D_MODEL), const)],
        out_specs=(pl.BlockSpec((COMBINE_TILE, D_MODEL), lambda i: (jnp.minimum(i, npt - 1), 0)),
                   pl.BlockSpec((COMBINE_TILE, D_MODEL), lambda i: (jnp.maximum(i - npt, 0), 0))),
        compiler_params=pltpu.CompilerParams(dimension_semantics=("arbitrary",),
                                             vmem_limit_bytes=VMEM_LIMIT),
        name="moe_combine",
    )(yk, yk, yk, yk, x1, w4, mod3, ln2_g, ln2_b)


def kernel(x_prompt, x_sample, c, state_fwd, state_bwd, c_ctx, w_ada, b_ada, w_in, conv_w, conv_b,
           conv_ln_g, conv_ln_b, lb_logits, rec_norm_g, w_out, ln1_g, ln1_b, ln2_g, ln2_b,
           router_w, router_b, w_gate_up, b_gate_up, w_down, b_down):
    depth = w_ada.shape[0]
    assert depth == 1
    alpha = (2.0 * depth) ** 0.25
    nb_p, seq_p, _ = x_prompt.shape
    nb_s, seq_s, _ = x_sample.shape
    n_p, n_s = nb_p * seq_p, nb_s * seq_s
    n_tok = n_p + n_s
    row2 = lambda a: a.reshape(1, -1)

    lb = jnp.cumsum(jax.nn.softmax(lb_logits.astype(F32), axis=0), axis=0)[0]

    cond = jnp.concatenate([c_ctx[None, :], c, jnp.zeros((16 - 1 - nb_s, D_MODEL), F32)], axis=0)
    mod = _ada(cond, w_ada[0], b_ada[0])
    mod3 = mod.reshape(16, 1, 6 * D_MODEL)

    xp = x_prompt.reshape(n_p, D_MODEL)
    xs = x_sample.reshape(n_s, D_MODEL)
    conv_w_pad = jnp.concatenate([conv_w[0], jnp.zeros((1, CONV_WIDTH), F32)], axis=0)
    conv_out, rec = _inproj(xp, xs, mod3, w_in[0].astype(BF16), conv_w_pad, row2(conv_b[0]),
                            row2(conv_ln_g[0]), row2(conv_ln_b[0]), seq_p, seq_s)

    zeros_state = jnp.zeros((nb_p, REC_HEADS, REC_DK, REC_DK), F32)
    s0f = jnp.concatenate([zeros_state, jnp.swapaxes(state_fwd[:, 0], -1, -2)], axis=0)
    s0b = jnp.concatenate([zeros_state, jnp.swapaxes(state_bwd[:, 0], -1, -2)], axis=0)
    o_f, o_b, sf_t, sb_t = _scan(rec, lb, s0f, s0b, [seq_p] * nb_p + [seq_s] * nb_s)
    new_f = jnp.swapaxes(sf_t[:nb_p], -1, -2)[:, None]
    new_b = jnp.swapaxes(sb_t[:nb_p], -1, -2)[:, None]

    rw = jnp.pad(router_w[0], ((0, 0), (0, LANES - N_EXPERTS)))
    rw_hi = rw.astype(BF16)
    rw_lo = (rw - rw_hi.astype(F32)).astype(BF16)
    rb = jnp.pad(router_b[0], (0, LANES - N_EXPERTS)).reshape(1, LANES)
    x1, h2, rank_t, slot_t, w4, counts, tile_start = _post(
        o_f, o_b, rec, conv_out, xp, xs, mod3, w_out[0].astype(BF16), row2(rec_norm_g[0]),
        row2(ln1_g[0]), row2(ln1_b[0]), rw_hi, rw_lo, rb, alpha, seq_s)

    cnt = counts[0, :N_EXPERTS].astype(jnp.int32)
    gpad = ((cnt + EXP_TILE - 1) // EXP_TILE) * EXP_TILE
    gend = jnp.cumsum(gpad)
    gstart = gend - gpad
    n_rows = n_tok * TOP_K + N_EXPERTS * EXP_TILE
    n_tiles = n_rows // EXP_TILE
    n_act = gend[-1] // EXP_TILE
    tile_ids = jnp.minimum(jnp.arange(n_tiles, dtype=jnp.int32), n_act - 1)
    tile_e = jnp.sum((gend[None, :] <= tile_ids[:, None] * EXP_TILE).astype(jnp.int32), axis=1)
    n_act1 = n_act.reshape(1).astype(jnp.int32)
    starts = tile_start[:, 0, :N_EXPERTS].astype(jnp.int32)
    cum2d = jnp.concatenate([starts, cnt[None, :]], axis=0)
    src, dst = _plan(rank_t, slot_t, tile_e, gstart.astype(jnp.int32), n_act1, cum2d)
    yk = _experts(h2, src, dst, tile_e, n_act1, w_gate_up[0], b_gate_up[0], w_down[0], b_down[0])
    y_p, y_s = _combine(yk, x1, w4, mod3, row2(ln2_g[0]), row2(ln2_b[0]), n_p, alpha, seq_s)
    return (y_p.reshape(nb_p, seq_p, D_MODEL), y_s.reshape(nb_s, seq_s, D_MODEL), new_f, new_b)
```
